```python
import math
import jax
import jax.numpy as jnp
from jax import lax
import numpy as np

D_MODEL = 2048
BATCH = 1
SEQ = 8192
DEPTH = 2

F32 = jnp.float32
CHUNK = 64
Q_BLOCK = 128
FOX_HEADS = 6
FOX_HD = 128
FOX_W = FOX_HEADS * FOX_HD
LRU_W = 768
LRU_BLOCKS = 6
LRU_BW = LRU_W // LRU_BLOCKS
CONV_W = 4
LRU_C = 8.0
DIFF_HEADS = 4
DIFF_HD = 64
DIFF_VD = 2 * DIFF_HD
DIFF_W = DIFF_HEADS * DIFF_VD
ROT_DIM = DIFF_HD // 4
ROPE_THETA = 500000.0
D_MIX = FOX_W + LRU_W + DIFF_W
N_IN = 3 * FOX_W + FOX_HEADS + 2 * LRU_W + 3 * DIFF_W
N_GROUPS = 4
EXPERTS_PER_GROUP = 8
TOP_K = 2
D_FF_EXPERT = 512
ALPHA = (2.0 * DEPTH) ** 0.25
BETA = (8.0 * DEPTH) ** -0.25
LN_EPS = 1e-5

kernel_name = 'hybrid_fox_rglru_diffattn_hmoe_deepnorm'


def _layer_norm(x, g, b):
    xf = x.astype(F32)
    mu = jnp.mean(xf, -1, keepdims=True)
    xc = xf - mu
    var = jnp.mean(xc * xc, -1, keepdims=True)
    return (xc * lax.rsqrt(var + LN_EPS) * g.astype(F32) + b.astype(F32)).astype(x.dtype)


def _rms_norm(x, g):
    xf = x.astype(F32)
    return (xf * lax.rsqrt(jnp.mean(xf * xf, -1, keepdims=True) + LN_EPS) * g.astype(F32)).astype(x.dtype)


def _rope_tables(positions):
    half = ROT_DIM // 2
    inv_freq = ROPE_THETA ** (-jnp.arange(half, dtype=F32) * 2.0 / ROT_DIM)
    ang = positions.astype(F32)[..., None] * inv_freq
    return jnp.cos(ang)[:, :, None, :], jnp.sin(ang)[:, :, None, :]


def _apply_partial_rope(t, cos, sin):
    half = ROT_DIM // 2
    t1 = t[..., :half].astype(F32)
    t2 = t[..., half:ROT_DIM].astype(F32)
    rot = jnp.concatenate([t1 * cos - t2 * sin, t2 * cos + t1 * sin], -1).astype(t.dtype)
    return jnp.concatenate([rot, t[..., ROT_DIM:]], -1)


def _to_blocks(t):
    b, s = t.shape[:2]
    return jnp.moveaxis(t.reshape((b, s // Q_BLOCK, Q_BLOCK) + t.shape[2:]), 1, 0)


def _from_blocks(t):
    nb, b, q = t.shape[:3]
    return jnp.moveaxis(t, 0, 1).reshape((b, nb * q) + t.shape[3:])


def _forgetting_attention(q, k, v, log_f):
    s_len = q.shape[1]
    cum = jnp.moveaxis(jnp.cumsum(log_f, axis=1), -1, 1)
    k_pos = jnp.arange(s_len)
    q_pos = k_pos.reshape(-1, Q_BLOCK)
    cum_q = jnp.moveaxis(cum.reshape(cum.shape[0], cum.shape[1], -1, Q_BLOCK), 2, 0)
    scale = FOX_HD ** -0.5

    def block(args):
        qb, cq, pq = args
        logits = jnp.einsum('bqhd,bkhd->bhqk', qb, k, preferred_element_type=F32) * scale
        logits = logits + cq[..., None] - cum[:, :, None, :]
        logits = jnp.where(k_pos[None, :] <= pq[:, None], logits, -jnp.inf)
        p = jax.nn.softmax(logits, axis=-1)
        return jnp.einsum('bhqk,bkhd->bqhd', p.astype(v.dtype), v)

    return _from_blocks(lax.map(block, (_to_blocks(q), cum_q, q_pos)))


def _differential_attention(q, k, v, lam):
    s_len = q.shape[1]
    chunk = jnp.arange(s_len) // CHUNK
    chunk_q = chunk.reshape(-1, Q_BLOCK)
    scale = DIFF_HD ** -0.5

    def block(args):
        qb, cq = args
        logits = jnp.einsum('bqhmd,bkhmd->bhmqk', qb, k, preferred_element_type=F32) * scale
        logits = jnp.where(chunk[None, :] <= cq[:, None], logits, -jnp.inf)
        p = jax.nn.softmax(logits, axis=-1)
        w = p[:, :, 0] - lam * p[:, :, 1]
        return jnp.einsum('bhqk,bkhd->bqhd', w.astype(v.dtype), v)

    return _from_blocks(lax.map(block, (_to_blocks(q), chunk_q)))


def _causal_depthwise_conv(x, w, b):
    y = lax.conv_general_dilated(x, w[:, None, :], (1,), [(CONV_W - 1, 0)],
                                 dimension_numbers=('NWC', 'WIO', 'NWC'),
                                 feature_group_count=x.shape[-1])
    return y + b


def _rg_lru(x, w_a, b_a, w_i, b_i, lam):
    bsz, s_len, _ = x.shape
    xb = x.reshape(bsz, s_len, LRU_BLOCKS, LRU_BW)
    r = jax.nn.sigmoid((jnp.einsum('bsnc,ncd->bsnd', xb, w_a).reshape(bsz, s_len, LRU_W) + b_a).astype(F32))
    i = jax.nn.sigmoid((jnp.einsum('bsnc,ncd->bsnd', xb, w_i).reshape(bsz, s_len, LRU_W) + b_i).astype(F32))
    log_a = LRU_C * r * jax.nn.log_sigmoid(lam.astype(F32))
    a = jnp.exp(log_a)
    u = jnp.sqrt(-jnp.expm1(2.0 * log_a)) * (i * x.astype(F32))

    def combine(left, right):
        a_l, u_l = left
        a_r, u_r = right
        return a_l * a_r, a_r * u_l + u_r

    _, h = lax.associative_scan(combine, (a, u), axis=1)
    return h.astype(x.dtype)


def _hierarchical_moe(x, w_group, b_group, w_router, b_router, w_gate, w_up, w_down):
    bsz, s_len, d = x.shape
    t = x.reshape(-1, d)
    g_logits = jnp.matmul(t, w_group, preferred_element_type=F32) + b_group.astype(F32)
    g_prob = jax.nn.softmax(g_logits, axis=-1)
    g_onehot = jax.nn.one_hot(jnp.argmax(g_logits, axis=-1), N_GROUPS, dtype=F32)
    g_weight = jnp.sum(g_prob * g_onehot, -1, keepdims=True)
    e_logits = (jnp.matmul(t, w_router, preferred_element_type=F32) + b_router.astype(F32)
                ).reshape(-1, N_GROUPS, EXPERTS_PER_GROUP)
    e_logits = jnp.einsum('tg,tge->te', g_onehot, e_logits)
    e_prob = jax.nn.softmax(e_logits, axis=-1)
    top_w, top_i = lax.top_k(e_prob, TOP_K)
    top_w = top_w / jnp.sum(top_w, -1, keepdims=True)
    e_gate = jnp.einsum('tk,tke->te', top_w, jax.nn.one_hot(top_i, EXPERTS_PER_GROUP, dtype=F32))
    gate = g_onehot[:, :, None] * (g_weight * e_gate)[:, None, :]
    out = jnp.zeros((t.shape[0], d), F32)
    for g in range(N_GROUPS):
        hg = jnp.einsum('td,edf->tef', t, w_gate[g])
        hu = jnp.einsum('td,edf->tef', t, w_up[g])
        h = jax.nn.silu(hg) * hu * gate[:, g, :, None].astype(t.dtype)
        out = out + jnp.einsum('tef,efd->td', h, w_down[g], preferred_element_type=F32)
    return out.reshape(bsz, s_len, d).astype(x.dtype)


def _mixer(x, cos, sin, layer, w_in, b_f, conv_w, conv_b, w_a, b_a, w_i, b_i, lru_lambda,
           lam_q1, lam_k1, lam_q2, lam_k2, subln_g, w_o):
    bsz, s_len, _ = x.shape
    proj = x @ w_in
    sizes = (FOX_W, FOX_W, FOX_W, FOX_HEADS, LRU_W, LRU_W, DIFF_W, DIFF_W, DIFF_W)
    offs = [sum(sizes[:j + 1]) for j in range(len(sizes) - 1)]
    qa, ka, va, fa, xr, gr, qc, kc, vc = jnp.split(proj, offs, axis=-1)

    def heads(t, h):
        return t.reshape(bsz, s_len, h, -1)

    log_f = jax.nn.log_sigmoid(fa.astype(F32) + b_f.astype(F32))
    out_a = _forgetting_attention(heads(qa, FOX_HEADS), heads(ka, FOX_HEADS), heads(va, FOX_HEADS),
                                  log_f).reshape(bsz, s_len, FOX_W)
    xr = _causal_depthwise_conv(xr, conv_w, conv_b)
    out_b = jax.nn.gelu(gr, approximate=True) * _rg_lru(xr, w_a, b_a, w_i, b_i, lru_lambda)
    qc = _apply_partial_rope(heads(qc, 2 * DIFF_HEADS), cos, sin).reshape(bsz, s_len, DIFF_HEADS, 2, DIFF_HD)
    kc = _apply_partial_rope(heads(kc, 2 * DIFF_HEADS), cos, sin).reshape(bsz, s_len, DIFF_HEADS, 2, DIFF_HD)
    lam_init = 0.8 - 0.6 * math.exp(-0.3 * layer)
    lam = (jnp.exp(jnp.sum(lam_q1.astype(F32) * lam_k1.astype(F32)))
           - jnp.exp(jnp.sum(lam_q2.astype(F32) * lam_k2.astype(F32))) + lam_init)
    oc = _differential_attention(qc, kc, heads(vc, DIFF_HEADS), lam)
    out_c = (_rms_norm(oc, subln_g) * (1.0 - lam_init)).reshape(bsz, s_len, DIFF_W)
    return jnp.concatenate([out_a, out_b, out_c], axis=-1) @ w_o


def setup_inputs(seed: int = 0) -> dict:
    key = jax.random.key(seed)
    ks = jax.random.split(key, 32)

    def nrm(k, shape, scale):
        return jax.random.normal(k, shape, F32) * scale

    G, E = N_GROUPS, EXPERTS_PER_GROUP
    x = nrm(ks[0], (BATCH, SEQ, D_MODEL), 1.0)
    offset = jax.random.randint(ks[1], (BATCH, 1), 0, 64) * CHUNK
    positions = (offset + jnp.arange(SEQ)[None, :]).astype(jnp.int32)
    a_init = jax.random.uniform(ks[10], (DEPTH, LRU_W), F32, 0.9, 0.999) ** (1.0 / LRU_C)
    lru_lambda = jnp.log(a_init) - jnp.log1p(-a_init)
    return {
        'x': x,
        'positions': positions,
        'w_in': nrm(ks[2], (DEPTH, D_MODEL, N_IN), D_MODEL ** -0.5),
        'b_f': 1.0 + nrm(ks[3], (DEPTH, FOX_HEADS), 0.1),
        'conv_w': nrm(ks[4], (DEPTH, CONV_W, LRU_W), CONV_W ** -0.5),
        'conv_b': nrm(ks[5], (DEPTH, LRU_W), 0.02),
        'w_a': nrm(ks[6], (DEPTH, LRU_BLOCKS, LRU_BW, LRU_BW), LRU_BW ** -0.5),
        'b_a': nrm(ks[7], (DEPTH, LRU_W), 0.1),
        'w_i': nrm(ks[8], (DEPTH, LRU_BLOCKS, LRU_BW, LRU_BW), LRU_BW ** -0.5),
        'b_i': nrm(ks[9], (DEPTH, LRU_W), 0.1),
        'lru_lambda': lru_lambda,
        'lam_q1': nrm(ks[11], (DEPTH, DIFF_HD), 0.1),
        'lam_k1': nrm(ks[12], (DEPTH, DIFF_HD), 0.1),
        'lam_q2': nrm(ks[13], (DEPTH, DIFF_HD), 0.1),
        'lam_k2': nrm(ks[14], (DEPTH, DIFF_HD), 0.1),
        'subln_g': 1.0 + nrm(ks[15], (DEPTH, DIFF_VD), 0.02),
        'w_o': nrm(ks[16], (DEPTH, D_MIX, D_MODEL), BETA * D_MIX ** -0.5),
        'ln1_g': 1.0 + nrm(ks[17], (DEPTH, D_MODEL), 0.02),
        'ln1_b': nrm(ks[18], (DEPTH, D_MODEL), 0.02),
        'w_group': nrm(ks[19], (DEPTH, D_MODEL, G), D_MODEL ** -0.5),
        'b_group': nrm(ks[20], (DEPTH, G), 0.01),
        'w_router': nrm(ks[21], (DEPTH, D_MODEL, G * E), D_MODEL ** -0.5),
        'b_router': nrm(ks[22], (DEPTH, G * E), 0.01),
        'w_gate': nrm(ks[23], (DEPTH, G, E, D_MODEL, D_FF_EXPERT), D_MODEL ** -0.5),
        'w_up': nrm(ks[24], (DEPTH, G, E, D_MODEL, D_FF_EXPERT), D_MODEL ** -0.5),
        'w_down': nrm(ks[25], (DEPTH, G, E, D_FF_EXPERT, D_MODEL), BETA * D_FF_EXPERT ** -0.5),
        'ln2_g': 1.0 + nrm(ks[26], (DEPTH, D_MODEL), 0.02),
        'ln2_b': nrm(ks[27], (DEPTH, D_MODEL), 0.02),
    }


def reference(x, positions, w_in, b_f, conv_w, conv_b, w_a, b_a, w_i, b_i, lru_lambda,
              lam_q1, lam_k1, lam_q2, lam_k2, subln_g, w_o, ln1_g, ln1_b,
              w_group, b_group, w_router, b_router, w_gate, w_up, w_down, ln2_g, ln2_b):
    cos, sin = _rope_tables(positions)
    h = x
    for l in range(DEPTH):
        mix = _mixer(h, cos, sin, l, w_in[l], b_f[l], conv_w[l], conv_b[l], w_a[l], b_a[l],
                     w_i[l], b_i[l], lru_lambda[l], lam_q1[l], lam_k1[l], lam_q2[l], lam_k2[l],
                     subln_g[l], w_o[l])
        h = _layer_norm(ALPHA * h + mix, ln1_g[l], ln1_b[l])
        ffn = _hierarchical_moe(h, w_group[l], b_group[l], w_router[l], b_router[l],
                                w_gate[l], w_up[l], w_down[l])
        h = _layer_norm(ALPHA * h + ffn, ln2_g[l], ln2_b[l])
    return h
```

```python
import functools
import math

import jax
import jax.numpy as jnp
from jax import lax
from jax.experimental import pallas as pl
from jax.experimental.pallas import tpu as pltpu

F32 = jnp.float32
BF16 = jnp.bfloat16

D_MODEL = 2048
DEPTH = 2
CHUNK = 64
FOX_HEADS = 6
FOX_HD = 128
FOX_W = FOX_HEADS * FOX_HD
LRU_W = 768
LRU_BLOCKS = 6
LRU_BW = LRU_W // LRU_BLOCKS
CONV_W = 4
LRU_C = 8.0
DIFF_HEADS = 4
DIFF_HD = 64
DIFF_VD = 2 * DIFF_HD
DIFF_W = DIFF_HEADS * DIFF_VD
ROT_DIM = DIFF_HD // 4
ROPE_THETA = 500000.0
N_GROUPS = 4
EXPERTS_PER_GROUP = 8
N_EXPERTS = N_GROUPS * EXPERTS_PER_GROUP
D_FF_EXPERT = 512
ALPHA = (2.0 * DEPTH) ** 0.25
LN_EPS = 1e-5

LANES = 128
NEG_BIG = -1e30
VMEM_LIMIT = 56 * 1024 * 1024

G16_W = 3 * FOX_W + DIFF_W
G32_W = 2 * DIFF_W + 2 * LRU_W + 2 * LANES
G16_QA, G16_KA, G16_VA, G16_VC = 0, 6, 12, 18
G32_XR, G32_GR, G32_FA = 8, 14, 20


def _cparams(sem, vmem=VMEM_LIMIT):
    return pltpu.CompilerParams(dimension_semantics=sem, vmem_limit_bytes=vmem)


def _mm_kernel(x_ref, w_ref, s_ref, o_ref):
    acc = jnp.dot(x_ref[...], w_ref[...], preferred_element_type=F32)
    o_ref[...] = (acc * s_ref[...]).astype(o_ref.dtype)


def _matmul(x, w, s, out_dtype, tm, tn, name):
    m, k = x.shape
    n = w.shape[1]
    return pl.pallas_call(
        _mm_kernel,
        grid=(m // tm, n // tn),
        in_specs=[pl.BlockSpec((tm, k), lambda i, j: (i, 0)),
                  pl.BlockSpec((k, tn), lambda i, j: (0, j)),
                  pl.BlockSpec((1, tn), lambda i, j: (0, j))],
        out_specs=pl.BlockSpec((tm, tn), lambda i, j: (i, j)),
        out_shape=jax.ShapeDtypeStruct((m, n), out_dtype),
        compiler_params=_cparams(("parallel", "parallel")),
        name=name,
    )(x, w, s)


def _split3(x):
    hi = x.astype(BF16)
    r1 = x - hi.astype(F32)
    mid = r1.astype(BF16)
    lo = (r1 - mid.astype(F32)).astype(BF16)
    return hi, mid, lo


def _cum_kernel(fa_ref, bf_ref, ckb_ref, crow_ref, carry_ref):
    @pl.when(pl.program_id(0) == 0)
    def _():
        carry_ref[...] = jnp.zeros_like(carry_ref)

    z = fa_ref[...] + bf_ref[...]
    lf = jnp.minimum(z, 0.0) - jnp.log1p(jnp.exp(-jnp.abs(z)))
    t = z.shape[0]
    row = lax.broadcasted_iota(jnp.int32, (t, t), 0)
    col = lax.broadcasted_iota(jnp.int32, (t, t), 1)
    tri = jnp.where(row >= col, 1.0, 0.0).astype(BF16)
    hi, mid, lo = _split3(lf)
    cs = (jnp.dot(tri, hi, preferred_element_type=F32)
          + jnp.dot(tri, mid, preferred_element_type=F32)
          + jnp.dot(tri, lo, preferred_element_type=F32)) + carry_ref[...]
    carry_ref[...] = cs[t - 1:t, :]
    lane = lax.broadcasted_iota(jnp.int32, (t, LANES), 1)
    for h in range(FOX_HEADS):
        col_h = jnp.sum(jnp.where(lane == h, cs, 0.0), axis=1, keepdims=True)
        ckb_ref[h] = jnp.broadcast_to(col_h, (t, LANES))
    cst = cs.T
    for h in range(8):
        crow_ref[h] = cst[h:h + 1, :]


def _fox_cumsum(g32, bf_pad, tb):
    s = g32.shape[0]
    return pl.pallas_call(
        _cum_kernel,
        grid=(s // tb,),
        in_specs=[pl.BlockSpec((tb, LANES), lambda i: (i, G32_FA)),
                  pl.BlockSpec((1, LANES), lambda i: (0, 0))],
        out_specs=[pl.BlockSpec((FOX_HEADS, tb, LANES), lambda i: (0, i, 0)),
                   pl.BlockSpec((8, 1, tb), lambda i: (0, 0, i))],
        out_shape=[jax.ShapeDtypeStruct((FOX_HEADS, s, LANES), F32),
                   jax.ShapeDtypeStruct((8, 1, s), F32)],
        scratch_shapes=[pltpu.VMEM((1, LANES), F32)],
        compiler_params=_cparams(("arbitrary",)),
        name="fox_cumsum",
    )(g32, bf_pad)


def _rope_kernel(x_ref, pos_ref, invf_ref, o_ref):
    ang = pos_ref[...] * invf_ref[...]
    c = jnp.cos(ang)
    s = jnp.sin(ang)
    half = ROT_DIM // 2
    d = lax.broadcasted_iota(jnp.int32, (1, LANES), 1) % DIFF_HD
    sa = jnp.where(d >= half, s, 0.0)
    sb = jnp.where(d < half, -s, 0.0)
    nblk = x_ref.shape[1] // LANES
    for j in range(nblk):
        t = x_ref[:, j * LANES:(j + 1) * LANES]
        r = t * c + pltpu.roll(t, half, 1) * sa + pltpu.roll(t, LANES - half, 1) * sb
        if j < nblk // 2:
            r = r * (DIFF_HD ** -0.5)
        o_ref[:, j * LANES:(j + 1) * LANES] = r.astype(o_ref.dtype)


def _rope(g32, pos_f, invf, tb):
    s = g32.shape[0]
    w = 2 * DIFF_W
    return pl.pallas_call(
        _rope_kernel,
        grid=(s // tb,),
        in_specs=[pl.BlockSpec((tb, w), lambda i: (i, 0)),
                  pl.BlockSpec((tb, LANES), lambda i: (i, 0)),
                  pl.BlockSpec((1, LANES), lambda i: (0, 0))],
        out_specs=pl.BlockSpec((tb, w), lambda i: (i, 0)),
        out_shape=jax.ShapeDtypeStruct((s, w), BF16),
        compiler_params=_cparams(("parallel",)),
        name="diff_rope",
    )(g32, pos_f, invf)


def _flash_step(k, v, s, m_ref, l_ref, acc_ref):
    m_prev = m_ref[...]
    m_new = jnp.maximum(m_prev, jnp.max(s, axis=0, keepdims=True))
    a = jnp.exp(m_prev - m_new)
    p = jnp.exp(s - m_new)
    l_ref[...] = a * l_ref[...] + jnp.sum(p, axis=0, keepdims=True)
    pv = lax.dot_general(v, p.astype(v.dtype), (((0,), (0,)), ((), ())),
                         preferred_element_type=F32)
    acc_ref[...] = a * acc_ref[...] + pv
    m_ref[...] = m_new


def _init_state(m_ref, l_ref, acc_ref):
    m_ref[...] = jnp.full(m_ref.shape, NEG_BIG, F32)
    l_ref[...] = jnp.zeros(l_ref.shape, F32)
    acc_ref[...] = jnp.zeros(acc_ref.shape, F32)


def _fox_kernel(q_ref, k_ref, v_ref, cq_ref, ckb_ref, o_ref, m_ref, l_ref, acc_ref, *, tb):
    i = pl.program_id(1)
    q = q_ref[...]
    cq = cq_ref[0]
    _init_state(m_ref, l_ref, acc_ref)

    def step(k0, masked):
        k = k_ref[pl.ds(k0, tb), :]
        v = v_ref[pl.ds(k0, tb), :]
        ck = ckb_ref[0, pl.ds(k0, tb), :]
        s = lax.dot_general(k, q, (((1,), (1,)), ((), ())), preferred_element_type=F32)
        s = s + cq - jnp.concatenate([ck] * (tb // LANES), axis=1)
        if masked:
            kk = lax.broadcasted_iota(jnp.int32, (tb, tb), 0)
            qq = lax.broadcasted_iota(jnp.int32, (tb, tb), 1)
            s = jnp.where(kk <= qq, s, NEG_BIG)
        _flash_step(k, v, s, m_ref, l_ref, acc_ref)

    def body(j, carry):
        step(pl.multiple_of(j * tb, tb), False)
        return carry

    lax.fori_loop(0, i, body, 0)
    step(pl.multiple_of(i * tb, tb), True)
    o = acc_ref[...] / l_ref[...]
    o_ref[...] = o.T.astype(o_ref.dtype)


def _fox_attention(g16, ckb, crow, tb):
    s = g16.shape[0]
    return pl.pallas_call(
        functools.partial(_fox_kernel, tb=tb),
        grid=(FOX_HEADS, s // tb),
        in_specs=[pl.BlockSpec((tb, LANES), lambda h, i: (i, G16_QA + h)),
                  pl.BlockSpec((s, LANES), lambda h, i: (0, G16_KA + h)),
                  pl.BlockSpec((s, LANES), lambda h, i: (0, G16_VA + h)),
                  pl.BlockSpec((1, 1, tb), lambda h, i: (h, 0, i)),
                  pl.BlockSpec((1, s, LANES), lambda h, i: (h, 0, 0))],
        out_specs=pl.BlockSpec((tb, LANES), lambda h, i: (i, h)),
        out_shape=jax.ShapeDtypeStruct((s, FOX_W), BF16),
        scratch_shapes=[pltpu.VMEM((1, tb), F32), pltpu.VMEM((1, tb), F32),
                        pltpu.VMEM((FOX_HD, tb), F32)],
        compiler_params=_cparams(("parallel", "parallel")),
        name="fox_attention",
    )(g16, g16, g16, crow, ckb)


def _diff_kernel(q_ref, k_ref, v_ref, lamv_ref, g_ref, o_ref, m_ref, l_ref, acc_ref,
                 *, tb, lam_init):
    i = pl.program_id(1)
    q = q_ref[...].astype(F32)
    lane = lax.broadcasted_iota(jnp.int32, (tb, LANES), 1)
    qq2 = jnp.concatenate([jnp.where(lane < DIFF_HD, q, 0.0),
                           jnp.where(lane >= DIFF_HD, q, 0.0)], axis=0).astype(BF16)
    _init_state(m_ref, l_ref, acc_ref)

    def step(k0, masked):
        k = k_ref[pl.ds(k0, tb), :]
        v = v_ref[pl.ds(k0, tb), :]
        s = lax.dot_general(k, qq2, (((1,), (1,)), ((), ())), preferred_element_type=F32)
        if masked:
            kc = lax.broadcasted_iota(jnp.int32, (tb, 2 * tb), 0) // CHUNK
            qc = (lax.broadcasted_iota(jnp.int32, (tb, 2 * tb), 1) % tb) // CHUNK
            s = jnp.where(kc <= qc, s, NEG_BIG)
        _flash_step(k, v, s, m_ref, l_ref, acc_ref)

    def body(j, carry):
        step(pl.multiple_of(j * tb, tb), False)
        return carry

    lax.fori_loop(0, i, body, 0)
    step(pl.multiple_of(i * tb, tb), True)

    lv = lamv_ref[...]
    lam = (jnp.exp(jnp.sum(lv[0:1] * lv[1:2], axis=1, keepdims=True))
           - jnp.exp(jnp.sum(lv[2:3] * lv[3:4], axis=1, keepdims=True)) + lam_init)
    on = acc_ref[...] / l_ref[...]
    o = (on[:, :tb] - lam * on[:, tb:]).T
    ms = jnp.mean(o * o, axis=-1, keepdims=True)
    o = o * lax.rsqrt(ms + LN_EPS) * g_ref[...] * (1.0 - lam_init)
    o_ref[...] = o.astype(o_ref.dtype)


def _diff_attention(qk16, g16, lamv, subg, lam_init, tb):
    s = g16.shape[0]
    return pl.pallas_call(
        functools.partial(_diff_kernel, tb=tb, lam_init=lam_init),
        grid=(DIFF_HEADS, s // tb),
        in_specs=[pl.BlockSpec((tb, LANES), lambda h, i: (i, h)),
                  pl.BlockSpec((s, LANES), lambda h, i: (0, DIFF_HEADS + h)),
                  pl.BlockSpec((s, LANES), lambda h, i: (0, G16_VC + h)),
                  pl.BlockSpec((4, LANES), lambda h, i: (0, 0)),
                  pl.BlockSpec((1, LANES), lambda h, i: (0, 0))],
        out_specs=pl.BlockSpec((tb, LANES), lambda h, i: (i, h)),
        out_shape=jax.ShapeDtypeStruct((s, DIFF_W), BF16),
        scratch_shapes=[pltpu.VMEM((1, 2 * tb), F32), pltpu.VMEM((1, 2 * tb), F32),
                        pltpu.VMEM((DIFF_VD, 2 * tb), F32)],
        compiler_params=_cparams(("parallel", "parallel")),
        name="diff_attention",
    )(qk16, qk16, g16, lamv, subg)


def _shift_rows(x, d, fill):
    rows = lax.broadcasted_iota(jnp.int32, x.shape, 0)
    return jnp.where(rows >= d, pltpu.roll(x, d, 0), fill)


def _lru_kernel(xr_ref, gr_ref, cw_ref, cb_ref, wa_ref, ba_ref, wi_ref, bi_ref, lam_ref,
                o_ref, halo_ref, h_ref, *, tb, sub):
    @pl.when(pl.program_id(1) == 0)
    def _():
        halo_ref[...] = jnp.zeros_like(halo_ref)
        h_ref[...] = jnp.zeros_like(h_ref)

    x = xr_ref[...]
    xe = jnp.concatenate([halo_ref[...], x], axis=0)
    halo_ref[...] = x[tb - 8:tb, :]
    cw = cw_ref[...]
    xc = cb_ref[...] + cw[CONV_W - 1:CONV_W, :] * x
    for j in range(CONV_W - 1):
        sh = CONV_W - 1 - j
        xc = xc + cw[j:j + 1, :] * pltpu.roll(xe, sh, 0)[8:8 + tb, :]

    xcb = xc.astype(BF16)
    r = jax.nn.sigmoid(jnp.dot(xcb, wa_ref[0], preferred_element_type=F32) + ba_ref[...])
    ig = jax.nn.sigmoid(jnp.dot(xcb, wi_ref[0], preferred_element_type=F32) + bi_ref[...])
    lam = lam_ref[...]
    ls = jnp.minimum(lam, 0.0) - jnp.log1p(jnp.exp(-jnp.abs(lam)))
    log_a = LRU_C * r * ls
    a = jnp.exp(log_a)
    z2 = 2.0 * log_a
    e2 = jnp.exp(z2)
    small = jnp.where(e2 == 1.0, -z2, (1.0 - e2) * z2 / jnp.log(e2))
    neg_expm1 = jnp.where(z2 < -1.0, 1.0 - e2, small)
    u = jnp.sqrt(neg_expm1) * (ig * xc)

    h = h_ref[...]
    for c in range(tb // sub):
        ac = a[c * sub:(c + 1) * sub, :]
        uc = u[c * sub:(c + 1) * sub, :]
        d = 1
        while d < sub:
            uc = ac * _shift_rows(uc, d, 0.0) + uc
            ac = ac * _shift_rows(ac, d, 1.0)
            d *= 2
        hc = uc + ac * h
        h = hc[sub - 1:sub, :]
        g = gr_ref[c * sub:(c + 1) * sub, :]
        gelu = 0.5 * g * (1.0 + jnp.tanh(math.sqrt(2.0 / math.pi) * (g + 0.044715 * (g * g * g))))
        o_ref[c * sub:(c + 1) * sub, :] = (gelu * hc).astype(o_ref.dtype)
    h_ref[...] = h


def _lru_branch(g32, cw, cb, wa, ba, wi, bi, lam, tb, sub):
    s = g32.shape[0]
    vec = lambda c, i: (0, c)
    return pl.pallas_call(
        functools.partial(_lru_kernel, tb=tb, sub=sub),
        grid=(LRU_BLOCKS, s // tb),
        in_specs=[pl.BlockSpec((tb, LANES), lambda c, i: (i, G32_XR + c)),
                  pl.BlockSpec((tb, LANES), lambda c, i: (i, G32_GR + c)),
                  pl.BlockSpec((CONV_W, LANES), vec),
                  pl.BlockSpec((1, LANES), vec),
                  pl.BlockSpec((1, LRU_BW, LRU_BW), lambda c, i: (c, 0, 0)),
                  pl.BlockSpec((1, LANES), vec),
                  pl.BlockSpec((1, LRU_BW, LRU_BW), lambda c, i: (c, 0, 0)),
                  pl.BlockSpec((1, LANES), vec),
                  pl.BlockSpec((1, LANES), vec)],
        out_specs=pl.BlockSpec((tb, LANES), lambda c, i: (i, c)),
        out_shape=jax.ShapeDtypeStruct((s, LRU_W), BF16),
        scratch_shapes=[pltpu.VMEM((8, LANES), F32), pltpu.VMEM((1, LANES), F32)],
        compiler_params=_cparams(("parallel", "arbitrary")),
        name="rg_lru",
    )(g32, g32, cw, cb, wa, ba, wi, bi, lam)


def _layer_norm_rows(y, g, b):
    mu = jnp.mean(y, axis=-1, keepdims=True)
    yc = y - mu
    var = jnp.mean(yc * yc, axis=-1, keepdims=True)
    return yc * lax.rsqrt(var + LN_EPS) * g + b


def _oproj_ln_kernel(xa_ref, xb_ref, xc_ref, wa_ref, wb_ref, wc_ref, h_ref, g_ref, b_ref,
                     o32_ref, o16_ref):
    mix = (jnp.dot(xa_ref[...], wa_ref[...], preferred_element_type=F32)
           + jnp.dot(xb_ref[...], wb_ref[...], preferred_element_type=F32)
           + jnp.dot(xc_ref[...], wc_ref[...], preferred_element_type=F32))
    o = _layer_norm_rows(ALPHA * h_ref[...] + mix, g_ref[...], b_ref[...])
    o32_ref[...] = o
    o16_ref[...] = o.astype(BF16)


def _oproj_ln(xa, xb, xc, wa, wb, wc, h, g, b, tm):
    s = h.shape[0]
    row = lambda i: (i, 0)
    fixed = lambda i: (0, 0)
    return pl.pallas_call(
        _oproj_ln_kernel,
        grid=(s // tm,),
        in_specs=[pl.BlockSpec((tm, FOX_W), row), pl.BlockSpec((tm, LRU_W), row),
                  pl.BlockSpec((tm, DIFF_W), row),
                  pl.BlockSpec((FOX_W, D_MODEL), fixed), pl.BlockSpec((LRU_W, D_MODEL), fixed),
                  pl.BlockSpec((DIFF_W, D_MODEL), fixed),
                  pl.BlockSpec((tm, D_MODEL), row),
                  pl.BlockSpec((1, D_MODEL), fixed), pl.BlockSpec((1, D_MODEL), fixed)],
        out_specs=[pl.BlockSpec((tm, D_MODEL), row), pl.BlockSpec((tm, D_MODEL), row)],
        out_shape=[jax.ShapeDtypeStruct((s, D_MODEL), F32),
                   jax.ShapeDtypeStruct((s, D_MODEL), BF16)],
        compiler_params=_cparams(("parallel",)),
        name="oproj_ln1",
    )(xa, xb, xc, wa, wb, wc, h, g, b)


def _router_kernel(h_ref, whi_ref, wlo_ref, b_ref, o_ref):
    h = h_ref[...]
    hi = h.astype(BF16)
    lo = (h - hi.astype(F32)).astype(BF16)
    whi = whi_ref[...]
    logits = (jnp.dot(hi, whi, preferred_element_type=F32)
              + jnp.dot(hi, wlo_ref[...], preferred_element_type=F32)
              + jnp.dot(lo, whi, preferred_element_type=F32)) + b_ref[...]
    lane = lax.broadcasted_iota(jnp.int32, logits.shape, 1)
    big = jnp.int32(1 << 20)

    def first_lane(cond):
        return jnp.min(jnp.where(cond, lane, big), axis=1, keepdims=True)

    gm = lane < N_GROUPS
    gl = jnp.where(gm, logits, -jnp.inf)
    gmax = jnp.max(gl, axis=1, keepdims=True)
    gexp = jnp.where(gm, jnp.exp(logits - gmax), 0.0)
    gprob = gexp / jnp.sum(gexp, axis=1, keepdims=True)
    gidx = first_lane(gl == gmax)
    g_weight = jnp.sum(jnp.where(lane == gidx, gprob, 0.0), axis=1, keepdims=True)

    e0 = N_GROUPS + EXPERTS_PER_GROUP * gidx
    em = jnp.logical_and(lane >= e0, lane < e0 + EXPERTS_PER_GROUP)
    el = jnp.where(em, logits, -jnp.inf)
    emax = jnp.max(el, axis=1, keepdims=True)
    eexp = jnp.where(em, jnp.exp(logits - emax), 0.0)
    eprob = jnp.where(em, eexp / jnp.sum(eexp, axis=1, keepdims=True), -1.0)
    p1 = jnp.max(eprob, axis=1, keepdims=True)
    i1 = first_lane(eprob == p1)
    eprob2 = jnp.where(lane == i1, -1.0, eprob)
    p2 = jnp.max(eprob2, axis=1, keepdims=True)
    i2 = first_lane(eprob2 == p2)
    den = p1 + p2
    gate1 = g_weight * (p1 / den)
    gate2 = g_weight * (p2 / den)
    id1 = (i1 - N_GROUPS).astype(F32)
    id2 = (i2 - N_GROUPS).astype(F32)
    o_ref[...] = jnp.where(lane == 0, id1,
                           jnp.where(lane == 1, id2,
                                     jnp.where(lane == 2, gate1,
                                               jnp.where(lane == 3, gate2, 0.0))))


def _router(h, whi, wlo, bias, tm):
    s = h.shape[0]
    fixed = lambda i: (0, 0)
    return pl.pallas_call(
        _router_kernel,
        grid=(s // tm,),
        in_specs=[pl.BlockSpec((tm, D_MODEL), lambda i: (i, 0)),
                  pl.BlockSpec((D_MODEL, LANES), fixed), pl.BlockSpec((D_MODEL, LANES), fixed),
                  pl.BlockSpec((1, LANES), fixed)],
        out_specs=pl.BlockSpec((tm, LANES), lambda i: (i, 0)),
        out_shape=jax.ShapeDtypeStruct((s, LANES), F32),
        compiler_params=_cparams(("parallel",)),
        name="moe_router",
    )(h, whi, wlo, bias)


def _issue_row_gather(idx_ref, base, n, src_hbm, dst_ref, sem):
    def body(r, carry):
        row = idx_ref[base + r]
        pltpu.make_async_copy(src_hbm.at[pl.ds(row, 1), :], dst_ref.at[pl.ds(r, 1), :], sem).start()
        return carry
    lax.fori_loop(0, n, body, 0)


def _moe_kernel(te_ref, nt_ref, tok_ref, h_hbm, gate_ref, wg_ref, wu_ref, wd_ref, y_ref,
                xbuf, sem, *, tm):
    del te_ref
    t = pl.program_id(0)
    nt = nt_ref[0]
    slot = t % 2

    @pl.when(t == 0)
    def _():
        _issue_row_gather(tok_ref, 0, tm, h_hbm, xbuf.at[0], sem.at[0])

    @pl.when(t + 1 < nt)
    def _():
        _issue_row_gather(tok_ref, (t + 1) * tm, tm, h_hbm, xbuf.at[1 - slot], sem.at[1 - slot])

    @pl.when(t < nt)
    def _():
        pltpu.make_async_copy(xbuf.at[slot], xbuf.at[slot], sem.at[slot]).wait()
        x = xbuf[slot].astype(BF16)
        hg = jnp.dot(x, wg_ref[0].astype(BF16), preferred_element_type=F32)
        hu = jnp.dot(x, wu_ref[0].astype(BF16), preferred_element_type=F32)
        act = (hg * jax.nn.sigmoid(hg)) * hu * gate_ref[...]
        y_ref[...] = jnp.dot(act.astype(BF16), wd_ref[0].astype(BF16),
                             preferred_element_type=F32)

    @pl.when(t >= nt)
    def _():
        y_ref[...] = jnp.zeros_like(y_ref)


def _moe_grouped(tile_expert, ntiles, tok_of_row, h, gate_rows, wg, wu, wd, tm):
    p_max = tok_of_row.shape[0]
    nt_max = p_max // tm
    grid_spec = pltpu.PrefetchScalarGridSpec(
        num_scalar_prefetch=3,
        grid=(nt_max,),
        in_specs=[pl.BlockSpec(memory_space=pl.ANY),
                  pl.BlockSpec((tm, 1), lambda t, te, nt, tok: (t, 0)),
                  pl.BlockSpec((1, D_MODEL, D_FF_EXPERT), lambda t, te, nt, tok: (te[t], 0, 0)),
                  pl.BlockSpec((1, D_MODEL, D_FF_EXPERT), lambda t, te, nt, tok: (te[t], 0, 0)),
                  pl.BlockSpec((1, D_FF_EXPERT, D_MODEL), lambda t, te, nt, tok: (te[t], 0, 0))],
        out_specs=pl.BlockSpec((tm, D_MODEL), lambda t, te, nt, tok: (t, 0)),
        scratch_shapes=[pltpu.VMEM((2, tm, D_MODEL), F32), pltpu.SemaphoreType.DMA((2,))],
    )
    return pl.pallas_call(
        functools.partial(_moe_kernel, tm=tm),
        grid_spec=grid_spec,
        out_shape=jax.ShapeDtypeStruct((p_max, D_MODEL), F32),
        compiler_params=_cparams(("arbitrary",)),
        name="moe_experts",
    )(tile_expert, ntiles, tok_of_row, h, gate_rows, wg, wu, wd)


def _combine_ln_kernel(pos_ref, y_hbm, h_ref, g_ref, b_ref, o32_ref, o16_ref, ybuf, sem, *, tm):
    t = pl.program_id(0)
    nt = pl.num_programs(0)
    slot = t % 2

    def issue(tile, sl):
        for j in range(2):
            def body(r, carry, j=j):
                row = pos_ref[(tile * tm + r) * 2 + j]
                pltpu.make_async_copy(y_hbm.at[pl.ds(row, 1), :],
                                      ybuf.at[sl, j, pl.ds(r, 1), :], sem.at[sl]).start()
                return carry
            lax.fori_loop(0, tm, body, 0)

    @pl.when(t == 0)
    def _():
        issue(0, 0)

    @pl.when(t + 1 < nt)
    def _():
        issue(t + 1, 1 - slot)

    pltpu.make_async_copy(ybuf.at[slot], ybuf.at[slot], sem.at[slot]).wait()
    y = ALPHA * h_ref[...] + (ybuf[slot, 0] + ybuf[slot, 1])
    o = _layer_norm_rows(y, g_ref[...], b_ref[...])
    o32_ref[...] = o
    o16_ref[...] = o.astype(BF16)


def _combine_ln(pos, ys, h, g, b, tm):
    s = h.shape[0]
    row = lambda t, pos: (t, 0)
    fixed = lambda t, pos: (0, 0)
    grid_spec = pltpu.PrefetchScalarGridSpec(
        num_scalar_prefetch=1,
        grid=(s // tm,),
        in_specs=[pl.BlockSpec(memory_space=pl.ANY),
                  pl.BlockSpec((tm, D_MODEL), row),
                  pl.BlockSpec((1, D_MODEL), fixed), pl.BlockSpec((1, D_MODEL), fixed)],
        out_specs=[pl.BlockSpec((tm, D_MODEL), row), pl.BlockSpec((tm, D_MODEL), row)],
        scratch_shapes=[pltpu.VMEM((2, 2, tm, D_MODEL), F32), pltpu.SemaphoreType.DMA((2,))],
    )
    return pl.pallas_call(
        functools.partial(_combine_ln_kernel, tm=tm),
        grid_spec=grid_spec,
        out_shape=[jax.ShapeDtypeStruct((s, D_MODEL), F32),
                   jax.ShapeDtypeStruct((s, D_MODEL), BF16)],
        compiler_params=_cparams(("arbitrary",)),
        name="combine_ln2",
    )(pos, ys, h, g, b)


def _dispatch_plan(meta, tm):
    t = meta.shape[0]
    eid = meta[:, 0:2].astype(jnp.int32).reshape(-1)
    gate = meta[:, 2:4].reshape(-1)
    onehot = (eid[:, None] == jnp.arange(N_EXPERTS, dtype=jnp.int32)[None, :]).astype(jnp.int32)
    csum = jnp.cumsum(onehot, axis=0)
    rank = jnp.sum(csum * onehot, axis=1) - 1
    counts = csum[-1]
    padded = ((counts + tm - 1) // tm) * tm
    pend = jnp.cumsum(padded)
    poff = pend - padded
    pos = poff[eid] + rank
    p_max = 2 * t + N_EXPERTS * tm
    nt_max = p_max // tm
    ntiles = (pend[-1] // tm).astype(jnp.int32)
    tile_ids = jnp.arange(nt_max, dtype=jnp.int32)
    tile_expert = jnp.searchsorted(pend // tm, jnp.minimum(tile_ids, ntiles - 1), side="right")
    tile_expert = jnp.minimum(tile_expert, N_EXPERTS - 1).astype(jnp.int32)
    pair_tok = jnp.arange(2 * t, dtype=jnp.int32) // 2
    tok_of_row = jnp.zeros((p_max,), jnp.int32).at[pos].set(pair_tok)
    gate_rows = jnp.zeros((p_max,), F32).at[pos].set(gate).reshape(p_max, 1)
    return tile_expert, ntiles.reshape(1), tok_of_row, gate_rows, pos.astype(jnp.int32)


def _pad_lanes(v, width=LANES):
    v = v.reshape(1, -1).astype(F32)
    return jnp.pad(v, ((0, 0), (0, width - v.shape[1])))


def _layer(l, h32, h16, pos_f, invf, p, tb, tm_moe):
    s = h32.shape[0]
    w_in = p["w_in"]
    o = 0
    cols = {}
    for name, width in (("qa", FOX_W), ("ka", FOX_W), ("va", FOX_W), ("fa", FOX_HEADS),
                        ("xr", LRU_W), ("gr", LRU_W), ("qc", DIFF_W), ("kc", DIFF_W),
                        ("vc", DIFF_W)):
        cols[name] = w_in[:, o:o + width]
        o += width
    w16 = jnp.concatenate([cols["qa"], cols["ka"], cols["va"], cols["vc"]], axis=1).astype(BF16)
    w32 = jnp.concatenate([cols["qc"], cols["kc"], cols["xr"], cols["gr"], cols["fa"],
                           jnp.zeros((D_MODEL, 2 * LANES - FOX_HEADS), F32)], axis=1).astype(BF16)
    s16 = jnp.concatenate([jnp.full((1, FOX_W), FOX_HD ** -0.5, F32),
                           jnp.ones((1, G16_W - FOX_W), F32)], axis=1)
    s32 = jnp.ones((1, G32_W), F32)
    tm_proj = min(s, 2048)
    g16 = _matmul(h16, w16, s16, BF16, tm_proj, 256, "in_proj_bf16")
    g32 = _matmul(h16, w32, s32, F32, tm_proj, 256, "in_proj_f32")

    ckb, crow = _fox_cumsum(g32, _pad_lanes(p["b_f"]), min(s, 256))
    out_a = _fox_attention(g16, ckb, crow, tb)

    out_b = _lru_branch(g32, p["conv_w"], p["conv_b"].reshape(1, -1),
                        p["w_a"].astype(BF16), p["b_a"].reshape(1, -1),
                        p["w_i"].astype(BF16), p["b_i"].reshape(1, -1),
                        p["lru_lambda"].reshape(1, -1), min(s, 1024), min(s, 256))

    qk16 = _rope(g32, pos_f, invf, min(s, 512))
    lamv = jnp.concatenate([_pad_lanes(p["lam_q1"]), _pad_lanes(p["lam_k1"]),
                            _pad_lanes(p["lam_q2"]), _pad_lanes(p["lam_k2"])], axis=0)
    lam_init = 0.8 - 0.6 * math.exp(-0.3 * l)
    out_c = _diff_attention(qk16, g16, lamv, p["subln_g"].reshape(1, -1), lam_init, tb)

    w_o = p["w_o"].astype(BF16)
    h1_32, h1_16 = _oproj_ln(out_a, out_b, out_c, w_o[:FOX_W], w_o[FOX_W:FOX_W + LRU_W],
                             w_o[FOX_W + LRU_W:], h32, p["ln1_g"].reshape(1, -1),
                             p["ln1_b"].reshape(1, -1), min(s, 256))
    del h1_16

    w_rt = jnp.concatenate([p["w_group"], p["w_router"],
                            jnp.zeros((D_MODEL, LANES - N_GROUPS - N_EXPERTS), F32)], axis=1)
    w_rt_hi = w_rt.astype(BF16)
    w_rt_lo = (w_rt - w_rt_hi.astype(F32)).astype(BF16)
    b_rt = _pad_lanes(jnp.concatenate([p["b_group"], p["b_router"]]))
    meta = _router(h1_32, w_rt_hi, w_rt_lo, b_rt, min(s, 512))

    tile_expert, ntiles, tok_of_row, gate_rows, pos = _dispatch_plan(meta, tm_moe)
    wg = p["w_gate"].reshape(N_EXPERTS, D_MODEL, D_FF_EXPERT)
    wu = p["w_up"].reshape(N_EXPERTS, D_MODEL, D_FF_EXPERT)
    wd = p["w_down"].reshape(N_EXPERTS, D_FF_EXPERT, D_MODEL)
    ys = _moe_grouped(tile_expert, ntiles, tok_of_row, h1_32, gate_rows, wg, wu, wd, tm_moe)
    return _combine_ln(pos, ys, h1_32, p["ln2_g"].reshape(1, -1), p["ln2_b"].reshape(1, -1),
                       min(s, 256))


def _forward(x, positions, params, tb, tm_moe):
    bsz, s, d = x.shape
    assert bsz == 1 and d == D_MODEL
    h32 = x.reshape(s, d)
    h16 = h32.astype(BF16)
    half = ROT_DIM // 2
    inv_freq = ROPE_THETA ** (-jnp.arange(half, dtype=F32) * 2.0 / ROT_DIM)
    dlane = jnp.arange(LANES) % DIFF_HD
    invf = jnp.where(dlane < ROT_DIM, inv_freq[dlane % half], 0.0).reshape(1, LANES).astype(F32)
    pos_f = jnp.broadcast_to(positions.astype(F32).reshape(s, 1), (s, LANES))
    for l in range(DEPTH):
        p = {k: v[l] for k, v in params.items()}
        h32, h16 = _layer(l, h32, h16, pos_f, invf, p, tb, tm_moe)
    return h32.reshape(bsz, s, d)


def kernel(x, positions, w_in, b_f, conv_w, conv_b, w_a, b_a, w_i, b_i, lru_lambda, lam_q1, lam_k1, lam_q2, lam_k2, subln_g, w_o, ln1_g, ln1_b, w_group, b_group, w_router, b_router, w_gate, w_up, w_down, ln2_g, ln2_b):
    params = dict(w_in=w_in, b_f=b_f, conv_w=conv_w, conv_b=conv_b, w_a=w_a, b_a=b_a, w_i=w_i,
                  b_i=b_i, lru_lambda=lru_lambda, lam_q1=lam_q1, lam_k1=lam_k1, lam_q2=lam_q2,
                  lam_k2=lam_k2, subln_g=subln_g, w_o=w_o, ln1_g=ln1_g, ln1_b=ln1_b,
                  w_group=w_group, b_group=b_group, w_router=w_router, b_router=b_router,
                  w_gate=w_gate, w_up=w_up, w_down=w_down, ln2_g=ln2_g, ln2_b=ln2_b)
    s = x.shape[1]
    return _forward(x, positions, params, tb=min(s, 512), tm_moe=min(s, 256))
```

```python
import functools
import math

import jax
import jax.numpy as jnp
from jax import lax
from jax.experimental import pallas as pl
from jax.experimental.pallas import tpu as pltpu

F32 = jnp.float32
BF16 = jnp.bfloat16

D_MODEL = 2048
DEPTH = 2
CHUNK = 64
FOX_HEADS = 6
FOX_HD = 128
FOX_W = FOX_HEADS * FOX_HD
LRU_W = 768
LRU_BLOCKS = 6
LRU_BW = LRU_W // LRU_BLOCKS
CONV_W = 4
LRU_C = 8.0
DIFF_HEADS = 4
DIFF_HD = 64
DIFF_VD = 2 * DIFF_HD
DIFF_W = DIFF_HEADS * DIFF_VD
ROT_DIM = DIFF_HD // 4
ROPE_THETA = 500000.0
N_GROUPS = 4
EXPERTS_PER_GROUP = 8
N_EXPERTS = N_GROUPS * EXPERTS_PER_GROUP
D_FF_EXPERT = 512
ALPHA = (2.0 * DEPTH) ** 0.25
LN_EPS = 1e-5

LANES = 128
NEG_BIG = -1e30
VMEM_LIMIT = 56 * 1024 * 1024

G16_W = 3 * FOX_W + DIFF_W
G32_W = 2 * DIFF_W + 2 * LRU_W + 2 * LANES
G16_QA, G16_KA, G16_VA, G16_VC = 0, 6, 12, 18
G32_XR, G32_GR, G32_FA = 8, 14, 20


def _cparams(sem, vmem=VMEM_LIMIT):
    return pltpu.CompilerParams(dimension_semantics=sem, vmem_limit_bytes=vmem)


def _mm_kernel(x_ref, w_ref, s_ref, o_ref):
    acc = jnp.dot(x_ref[...], w_ref[...], preferred_element_type=F32)
    o_ref[...] = (acc * s_ref[...]).astype(o_ref.dtype)


def _matmul(x, w, s, out_dtype, tm, tn, name):
    m, k = x.shape
    n = w.shape[1]
    return pl.pallas_call(
        _mm_kernel,
        grid=(m // tm, n // tn),
        in_specs=[pl.BlockSpec((tm, k), lambda i, j: (i, 0)),
                  pl.BlockSpec((k, tn), lambda i, j: (0, j)),
                  pl.BlockSpec((1, tn), lambda i, j: (0, j))],
        out_specs=pl.BlockSpec((tm, tn), lambda i, j: (i, j)),
        out_shape=jax.ShapeDtypeStruct((m, n), out_dtype),
        compiler_params=_cparams(("parallel", "parallel")),
        name=name,
    )(x, w, s)


_IN_QA = 0
_IN_KA = _IN_QA + FOX_W
_IN_VA = _IN_KA + FOX_W
_IN_FA = _IN_VA + FOX_W
_IN_XR = _IN_FA + FOX_HEADS
_IN_GR = _IN_XR + LRU_W
_IN_QC = _IN_GR + LRU_W
_IN_KC = _IN_QC + DIFF_W
_IN_VC = _IN_KC + DIFF_W
N_IN = _IN_VC + DIFF_W


def _wprep_kernel(w_ref, o16_ref, o32_ref):
    def put(o_ref, dst, src, width):
        o_ref[:, dst:dst + width] = w_ref[0, :, src:src + width].astype(BF16)

    put(o16_ref, 0, _IN_QA, FOX_W)
    put(o16_ref, FOX_W, _IN_KA, FOX_W)
    put(o16_ref, 2 * FOX_W, _IN_VA, FOX_W)
    put(o16_ref, 3 * FOX_W, _IN_VC, DIFF_W)
    put(o32_ref, 0, _IN_QC, DIFF_W)
    put(o32_ref, DIFF_W, _IN_KC, DIFF_W)
    put(o32_ref, G32_XR * LANES, _IN_XR, LRU_W)
    put(o32_ref, G32_GR * LANES, _IN_GR, LRU_W)
    fa0 = G32_FA * LANES
    o32_ref[:, fa0:fa0 + 2 * LANES] = jnp.zeros((o32_ref.shape[0], 2 * LANES), BF16)
    put(o32_ref, fa0, _IN_FA, FOX_HEADS)


def _prep_w_in(w_in, layer, tk=256):
    return pl.pallas_call(
        _wprep_kernel,
        grid=(D_MODEL // tk,),
        in_specs=[pl.BlockSpec((1, tk, N_IN), lambda i: (layer, i, 0))],
        out_specs=[pl.BlockSpec((tk, G16_W), lambda i: (i, 0)),
                   pl.BlockSpec((tk, G32_W), lambda i: (i, 0))],
        out_shape=[jax.ShapeDtypeStruct((D_MODEL, G16_W), BF16),
                   jax.ShapeDtypeStruct((D_MODEL, G32_W), BF16)],
        compiler_params=_cparams(("parallel",)),
        name="w_in_regroup",
    )(w_in)


def _split3(x):
    hi = x.astype(BF16)
    r1 = x - hi.astype(F32)
    mid = r1.astype(BF16)
    lo = (r1 - mid.astype(F32)).astype(BF16)
    return hi, mid, lo


def _cum_kernel(fa_ref, bf_ref, ckb_ref, crow_ref, carry_ref):
    @pl.when(pl.program_id(0) == 0)
    def _():
        carry_ref[...] = jnp.zeros_like(carry_ref)

    z = fa_ref[...] + bf_ref[...]
    lf = jnp.minimum(z, 0.0) - jnp.log1p(jnp.exp(-jnp.abs(z)))
    t = z.shape[0]
    row = lax.broadcasted_iota(jnp.int32, (t, t), 0)
    col = lax.broadcasted_iota(jnp.int32, (t, t), 1)
    tri = jnp.where(row >= col, 1.0, 0.0).astype(BF16)
    hi, mid, lo = _split3(lf)
    cs = (jnp.dot(tri, hi, preferred_element_type=F32)
          + jnp.dot(tri, mid, preferred_element_type=F32)
          + jnp.dot(tri, lo, preferred_element_type=F32)) + carry_ref[...]
    carry_ref[...] = cs[t - 1:t, :]
    lane = lax.broadcasted_iota(jnp.int32, (t, LANES), 1)
    for h in range(FOX_HEADS):
        col_h = jnp.sum(jnp.where(lane == h, cs, 0.0), axis=1, keepdims=True)
        ckb_ref[h] = jnp.broadcast_to(col_h, (t, LANES))
    cst = cs.T
    for h in range(8):
        crow_ref[h] = cst[h:h + 1, :]


def _fox_cumsum(g32, bf_pad, tb):
    s = g32.shape[0]
    return pl.pallas_call(
        _cum_kernel,
        grid=(s // tb,),
        in_specs=[pl.BlockSpec((tb, LANES), lambda i: (i, G32_FA)),
                  pl.BlockSpec((1, LANES), lambda i: (0, 0))],
        out_specs=[pl.BlockSpec((FOX_HEADS, tb, LANES), lambda i: (0, i, 0)),
                   pl.BlockSpec((8, 1, tb), lambda i: (0, 0, i))],
        out_shape=[jax.ShapeDtypeStruct((FOX_HEADS, s, LANES), F32),
                   jax.ShapeDtypeStruct((8, 1, s), F32)],
        scratch_shapes=[pltpu.VMEM((1, LANES), F32)],
        compiler_params=_cparams(("arbitrary",)),
        name="fox_cumsum",
    )(g32, bf_pad)


def _rope_kernel(x_ref, pos_ref, invf_ref, o_ref):
    ang = pos_ref[...] * invf_ref[...]
    c = jnp.cos(ang)
    s = jnp.sin(ang)
    half = ROT_DIM // 2
    d = lax.broadcasted_iota(jnp.int32, (1, LANES), 1) % DIFF_HD
    sa = jnp.where(d >= half, s, 0.0)
    sb = jnp.where(d < half, -s, 0.0)
    nblk = x_ref.shape[1] // LANES
    for j in range(nblk):
        t = x_ref[:, j * LANES:(j + 1) * LANES]
        r = t * c + pltpu.roll(t, half, 1) * sa + pltpu.roll(t, LANES - half, 1) * sb
        if j < nblk // 2:
            r = r * (DIFF_HD ** -0.5)
        o_ref[:, j * LANES:(j + 1) * LANES] = r.astype(o_ref.dtype)


def _rope(g32, pos_f, invf, tb):
    s = g32.shape[0]
    w = 2 * DIFF_W
    return pl.pallas_call(
        _rope_kernel,
        grid=(s // tb,),
        in_specs=[pl.BlockSpec((tb, w), lambda i: (i, 0)),
                  pl.BlockSpec((tb, LANES), lambda i: (i, 0)),
                  pl.BlockSpec((1, LANES), lambda i: (0, 0))],
        out_specs=pl.BlockSpec((tb, w), lambda i: (i, 0)),
        out_shape=jax.ShapeDtypeStruct((s, w), BF16),
        compiler_params=_cparams(("parallel",)),
        name="diff_rope",
    )(g32, pos_f, invf)


def _flash_step(k, v, s, m_ref, l_ref, acc_ref):
    m_prev = m_ref[...]
    m_new = jnp.maximum(m_prev, jnp.max(s, axis=0, keepdims=True))
    a = jnp.exp(m_prev - m_new)
    p = jnp.exp(s - m_new)
    l_ref[...] = a * l_ref[...] + jnp.sum(p, axis=0, keepdims=True)
    pv = lax.dot_general(v, p.astype(v.dtype), (((0,), (0,)), ((), ())),
                         preferred_element_type=F32)
    acc_ref[...] = a * acc_ref[...] + pv
    m_ref[...] = m_new


def _init_state(m_ref, l_ref, acc_ref):
    m_ref[...] = jnp.full(m_ref.shape, NEG_BIG, F32)
    l_ref[...] = jnp.zeros(l_ref.shape, F32)
    acc_ref[...] = jnp.zeros(acc_ref.shape, F32)


def _fox_kernel(q_ref, k_ref, v_ref, cq_ref, ckb_ref, o_ref, m_ref, l_ref, acc_ref, *, tb):
    i = pl.program_id(1)
    q = q_ref[...]
    cq = cq_ref[0]
    _init_state(m_ref, l_ref, acc_ref)

    def step(k0, masked):
        k = k_ref[pl.ds(k0, tb), :]
        v = v_ref[pl.ds(k0, tb), :]
        ck = ckb_ref[0, pl.ds(k0, tb), :]
        s = lax.dot_general(k, q, (((1,), (1,)), ((), ())), preferred_element_type=F32)
        s = s + cq - jnp.concatenate([ck] * (tb // LANES), axis=1)
        if masked:
            kk = lax.broadcasted_iota(jnp.int32, (tb, tb), 0)
            qq = lax.broadcasted_iota(jnp.int32, (tb, tb), 1)
            s = jnp.where(kk <= qq, s, NEG_BIG)
        _flash_step(k, v, s, m_ref, l_ref, acc_ref)

    def body(j, carry):
        step(pl.multiple_of(j * tb, tb), False)
        return carry

    lax.fori_loop(0, i, body, 0)
    step(pl.multiple_of(i * tb, tb), True)
    o = acc_ref[...] / l_ref[...]
    o_ref[...] = o.T.astype(o_ref.dtype)


def _fox_attention(g16, ckb, crow, tb):
    s = g16.shape[0]
    return pl.pallas_call(
        functools.partial(_fox_kernel, tb=tb),
        grid=(FOX_HEADS, s // tb),
        in_specs=[pl.BlockSpec((tb, LANES), lambda h, i: (i, G16_QA + h)),
                  pl.BlockSpec((s, LANES), lambda h, i: (0, G16_KA + h)),
                  pl.BlockSpec((s, LANES), lambda h, i: (0, G16_VA + h)),
                  pl.BlockSpec((1, 1, tb), lambda h, i: (h, 0, i)),
                  pl.BlockSpec((1, s, LANES), lambda h, i: (h, 0, 0))],
        out_specs=pl.BlockSpec((tb, LANES), lambda h, i: (i, h)),
        out_shape=jax.ShapeDtypeStruct((s, FOX_W), BF16),
        scratch_shapes=[pltpu.VMEM((1, tb), F32), pltpu.VMEM((1, tb), F32),
                        pltpu.VMEM((FOX_HD, tb), F32)],
        compiler_params=_cparams(("parallel", "parallel")),
        name="fox_attention",
    )(g16, g16, g16, crow, ckb)


def _diff_kernel(q_ref, k_ref, v_ref, lamv_ref, g_ref, o_ref, m_ref, l_ref, acc_ref,
                 *, tb, lam_init):
    i = pl.program_id(1)
    q = q_ref[...].astype(F32)
    lane = lax.broadcasted_iota(jnp.int32, (tb, LANES), 1)
    qq2 = jnp.concatenate([jnp.where(lane < DIFF_HD, q, 0.0),
                           jnp.where(lane >= DIFF_HD, q, 0.0)], axis=0).astype(BF16)
    _init_state(m_ref, l_ref, acc_ref)

    def step(k0, masked):
        k = k_ref[pl.ds(k0, tb), :]
        v = v_ref[pl.ds(k0, tb), :]
        s = lax.dot_general(k, qq2, (((1,), (1,)), ((), ())), preferred_element_type=F32)
        if masked:
            kc = lax.broadcasted_iota(jnp.int32, (tb, 2 * tb), 0) // CHUNK
            qc = (lax.broadcasted_iota(jnp.int32, (tb, 2 * tb), 1) % tb) // CHUNK
            s = jnp.where(kc <= qc, s, NEG_BIG)
        _flash_step(k, v, s, m_ref, l_ref, acc_ref)

    def body(j, carry):
        step(pl.multiple_of(j * tb, tb), False)
        return carry

    lax.fori_loop(0, i, body, 0)
    step(pl.multiple_of(i * tb, tb), True)

    lv = lamv_ref[...]
    lam = (jnp.exp(jnp.sum(lv[0:1] * lv[1:2], axis=1, keepdims=True))
           - jnp.exp(jnp.sum(lv[2:3] * lv[3:4], axis=1, keepdims=True)) + lam_init)
    on = acc_ref[...] / l_ref[...]
    o = (on[:, :tb] - lam * on[:, tb:]).T
    ms = jnp.mean(o * o, axis=-1, keepdims=True)
    o = o * lax.rsqrt(ms + LN_EPS) * g_ref[...] * (1.0 - lam_init)
    o_ref[...] = o.astype(o_ref.dtype)


def _diff_attention(qk16, g16, lamv, subg, lam_init, tb):
    s = g16.shape[0]
    return pl.pallas_call(
        functools.partial(_diff_kernel, tb=tb, lam_init=lam_init),
        grid=(DIFF_HEADS, s // tb),
        in_specs=[pl.BlockSpec((tb, LANES), lambda h, i: (i, h)),
                  pl.BlockSpec((s, LANES), lambda h, i: (0, DIFF_HEADS + h)),
                  pl.BlockSpec((s, LANES), lambda h, i: (0, G16_VC + h)),
                  pl.BlockSpec((4, LANES), lambda h, i: (0, 0)),
                  pl.BlockSpec((1, LANES), lambda h, i: (0, 0))],
        out_specs=pl.BlockSpec((tb, LANES), lambda h, i: (i, h)),
        out_shape=jax.ShapeDtypeStruct((s, DIFF_W), BF16),
        scratch_shapes=[pltpu.VMEM((1, 2 * tb), F32), pltpu.VMEM((1, 2 * tb), F32),
                        pltpu.VMEM((DIFF_VD, 2 * tb), F32)],
        compiler_params=_cparams(("parallel", "parallel")),
        name="diff_attention",
    )(qk16, qk16, g16, lamv, subg)


def _shift_rows(x, d, fill):
    rows = lax.broadcasted_iota(jnp.int32, x.shape, 0)
    return jnp.where(rows >= d, pltpu.roll(x, d, 0), fill)


def _lru_kernel(xr_ref, gr_ref, cw_ref, cb_ref, wa_ref, ba_ref, wi_ref, bi_ref, lam_ref,
                o_ref, halo_ref, h_ref, *, tb, sub):
    @pl.when(pl.program_id(1) == 0)
    def _():
        halo_ref[...] = jnp.zeros_like(halo_ref)
        h_ref[...] = jnp.zeros_like(h_ref)

    x = xr_ref[...]
    xe = jnp.concatenate([halo_ref[...], x], axis=0)
    halo_ref[...] = x[tb - 8:tb, :]
    cw = cw_ref[...]
    xc = cb_ref[...] + cw[CONV_W - 1:CONV_W, :] * x
    for j in range(CONV_W - 1):
        sh = CONV_W - 1 - j
        xc = xc + cw[j:j + 1, :] * pltpu.roll(xe, sh, 0)[8:8 + tb, :]

    xcb = xc.astype(BF16)
    r = jax.nn.sigmoid(jnp.dot(xcb, wa_ref[0], preferred_element_type=F32) + ba_ref[...])
    ig = jax.nn.sigmoid(jnp.dot(xcb, wi_ref[0], preferred_element_type=F32) + bi_ref[...])
    lam = lam_ref[...]
    ls = jnp.minimum(lam, 0.0) - jnp.log1p(jnp.exp(-jnp.abs(lam)))
    log_a = LRU_C * r * ls
    a = jnp.exp(log_a)
    z2 = 2.0 * log_a
    e2 = jnp.exp(z2)
    small = jnp.where(e2 == 1.0, -z2, (1.0 - e2) * z2 / jnp.log(e2))
    neg_expm1 = jnp.where(z2 < -1.0, 1.0 - e2, small)
    u = jnp.sqrt(neg_expm1) * (ig * xc)

    h = h_ref[...]
    for c in range(tb // sub):
        ac = a[c * sub:(c + 1) * sub, :]
        uc = u[c * sub:(c + 1) * sub, :]
        d = 1
        while d < sub:
            uc = ac * _shift_rows(uc, d, 0.0) + uc
            ac = ac * _shift_rows(ac, d, 1.0)
            d *= 2
        hc = uc + ac * h
        h = hc[sub - 1:sub, :]
        g = gr_ref[c * sub:(c + 1) * sub, :]
        gelu = 0.5 * g * (1.0 + jnp.tanh(math.sqrt(2.0 / math.pi) * (g + 0.044715 * (g * g * g))))
        o_ref[c * sub:(c + 1) * sub, :] = (gelu * hc).astype(o_ref.dtype)
    h_ref[...] = h


def _lru_branch(g32, cw, cb, wa, ba, wi, bi, lam, tb, sub):
    s = g32.shape[0]
    vec = lambda c, i: (0, c)
    return pl.pallas_call(
        functools.partial(_lru_kernel, tb=tb, sub=sub),
        grid=(LRU_BLOCKS, s // tb),
        in_specs=[pl.BlockSpec((tb, LANES), lambda c, i: (i, G32_XR + c)),
                  pl.BlockSpec((tb, LANES), lambda c, i: (i, G32_GR + c)),
                  pl.BlockSpec((CONV_W, LANES), vec),
                  pl.BlockSpec((1, LANES), vec),
                  pl.BlockSpec((1, LRU_BW, LRU_BW), lambda c, i: (c, 0, 0)),
                  pl.BlockSpec((1, LANES), vec),
                  pl.BlockSpec((1, LRU_BW, LRU_BW), lambda c, i: (c, 0, 0)),
                  pl.BlockSpec((1, LANES), vec),
                  pl.BlockSpec((1, LANES), vec)],
        out_specs=pl.BlockSpec((tb, LANES), lambda c, i: (i, c)),
        out_shape=jax.ShapeDtypeStruct((s, LRU_W), BF16),
        scratch_shapes=[pltpu.VMEM((8, LANES), F32), pltpu.VMEM((1, LANES), F32)],
        compiler_params=_cparams(("parallel", "arbitrary")),
        name="rg_lru",
    )(g32, g32, cw, cb, wa, ba, wi, bi, lam)


def _layer_norm_rows(y, g, b):
    mu = jnp.mean(y, axis=-1, keepdims=True)
    yc = y - mu
    var = jnp.mean(yc * yc, axis=-1, keepdims=True)
    return yc * lax.rsqrt(var + LN_EPS) * g + b


def _oproj_ln_kernel(xa_ref, xb_ref, xc_ref, wa_ref, wb_ref, wc_ref, h_ref, g_ref, b_ref,
                     o32_ref, o16_ref):
    mix = (jnp.dot(xa_ref[...], wa_ref[...], preferred_element_type=F32)
           + jnp.dot(xb_ref[...], wb_ref[...], preferred_element_type=F32)
           + jnp.dot(xc_ref[...], wc_ref[...], preferred_element_type=F32))
    o = _layer_norm_rows(ALPHA * h_ref[...] + mix, g_ref[...], b_ref[...])
    o32_ref[...] = o
    o16_ref[...] = o.astype(BF16)


def _oproj_ln(xa, xb, xc, wa, wb, wc, h, g, b, tm):
    s = h.shape[0]
    row = lambda i: (i, 0)
    fixed = lambda i: (0, 0)
    return pl.pallas_call(
        _oproj_ln_kernel,
        grid=(s // tm,),
        in_specs=[pl.BlockSpec((tm, FOX_W), row), pl.BlockSpec((tm, LRU_W), row),
                  pl.BlockSpec((tm, DIFF_W), row),
                  pl.BlockSpec((FOX_W, D_MODEL), fixed), pl.BlockSpec((LRU_W, D_MODEL), fixed),
                  pl.BlockSpec((DIFF_W, D_MODEL), fixed),
                  pl.BlockSpec((tm, D_MODEL), row),
                  pl.BlockSpec((1, D_MODEL), fixed), pl.BlockSpec((1, D_MODEL), fixed)],
        out_specs=[pl.BlockSpec((tm, D_MODEL), row), pl.BlockSpec((tm, D_MODEL), row)],
        out_shape=[jax.ShapeDtypeStruct((s, D_MODEL), F32),
                   jax.ShapeDtypeStruct((s, D_MODEL), BF16)],
        compiler_params=_cparams(("parallel",)),
        name="oproj_ln1",
    )(xa, xb, xc, wa, wb, wc, h, g, b)


def _router_kernel(h_ref, whi_ref, wlo_ref, b_ref, o_ref):
    h = h_ref[...]
    hi = h.astype(BF16)
    lo = (h - hi.astype(F32)).astype(BF16)
    whi = whi_ref[...]
    logits = (jnp.dot(hi, whi, preferred_element_type=F32)
              + jnp.dot(hi, wlo_ref[...], preferred_element_type=F32)
              + jnp.dot(lo, whi, preferred_element_type=F32)) + b_ref[...]
    lane = lax.broadcasted_iota(jnp.int32, logits.shape, 1)
    big = jnp.int32(1 << 20)

    def first_lane(cond):
        return jnp.min(jnp.where(cond, lane, big), axis=1, keepdims=True)

    gm = lane < N_GROUPS
    gl = jnp.where(gm, logits, -jnp.inf)
    gmax = jnp.max(gl, axis=1, keepdims=True)
    gexp = jnp.where(gm, jnp.exp(logits - gmax), 0.0)
    gprob = gexp / jnp.sum(gexp, axis=1, keepdims=True)
    gidx = first_lane(gl == gmax)
    g_weight = jnp.sum(jnp.where(lane == gidx, gprob, 0.0), axis=1, keepdims=True)

    e0 = N_GROUPS + EXPERTS_PER_GROUP * gidx
    em = jnp.logical_and(lane >= e0, lane < e0 + EXPERTS_PER_GROUP)
    el = jnp.where(em, logits, -jnp.inf)
    emax = jnp.max(el, axis=1, keepdims=True)
    eexp = jnp.where(em, jnp.exp(logits - emax), 0.0)
    eprob = jnp.where(em, eexp / jnp.sum(eexp, axis=1, keepdims=True), -1.0)
    p1 = jnp.max(eprob, axis=1, keepdims=True)
    i1 = first_lane(eprob == p1)
    eprob2 = jnp.where(lane == i1, -1.0, eprob)
    p2 = jnp.max(eprob2, axis=1, keepdims=True)
    i2 = first_lane(eprob2 == p2)
    den = p1 + p2
    gate1 = g_weight * (p1 / den)
    gate2 = g_weight * (p2 / den)
    id1 = (i1 - N_GROUPS).astype(F32)
    id2 = (i2 - N_GROUPS).astype(F32)
    o_ref[...] = jnp.where(lane == 0, id1,
                           jnp.where(lane == 1, id2,
                                     jnp.where(lane == 2, gate1,
                                               jnp.where(lane == 3, gate2, 0.0))))


def _router(h, whi, wlo, bias, tm):
    s = h.shape[0]
    fixed = lambda i: (0, 0)
    return pl.pallas_call(
        _router_kernel,
        grid=(s // tm,),
        in_specs=[pl.BlockSpec((tm, D_MODEL), lambda i: (i, 0)),
                  pl.BlockSpec((D_MODEL, LANES), fixed), pl.BlockSpec((D_MODEL, LANES), fixed),
                  pl.BlockSpec((1, LANES), fixed)],
        out_specs=pl.BlockSpec((tm, LANES), lambda i: (i, 0)),
        out_shape=jax.ShapeDtypeStruct((s, LANES), F32),
        compiler_params=_cparams(("parallel",)),
        name="moe_router",
    )(h, whi, wlo, bias)


def _issue_row_gather(idx_ref, base, n, src_hbm, dst_ref, sem):
    def body(r, carry):
        row = idx_ref[base + r]
        pltpu.make_async_copy(src_hbm.at[pl.ds(row, 1), :], dst_ref.at[pl.ds(r, 1), :], sem).start()
        return carry
    lax.fori_loop(0, n, body, 0, unroll=8)


def _moe_kernel(te_ref, nt_ref, tok_ref, h_hbm, gate_ref, wg_ref, wu_ref, wd_ref, y_ref,
                xbuf, sem, *, tm):
    del te_ref
    t = pl.program_id(0)
    nt = nt_ref[0]
    slot = t % 2

    @pl.when(t == 0)
    def _():
        _issue_row_gather(tok_ref, 0, tm, h_hbm, xbuf.at[0], sem.at[0])

    @pl.when(t + 1 < nt)
    def _():
        _issue_row_gather(tok_ref, (t + 1) * tm, tm, h_hbm, xbuf.at[1 - slot], sem.at[1 - slot])

    @pl.when(t < nt)
    def _():
        pltpu.make_async_copy(xbuf.at[slot], xbuf.at[slot], sem.at[slot]).wait()
        x = xbuf[slot].astype(BF16)
        hg = jnp.dot(x, wg_ref[0].astype(BF16), preferred_element_type=F32)
        hu = jnp.dot(x, wu_ref[0].astype(BF16), preferred_element_type=F32)
        act = (hg * jax.nn.sigmoid(hg)) * hu * gate_ref[...]
        y_ref[...] = jnp.dot(act.astype(BF16), wd_ref[0].astype(BF16),
                             preferred_element_type=F32)

    @pl.when(t >= nt)
    def _():
        y_ref[...] = jnp.zeros_like(y_ref)


def _moe_grouped(tile_expert, ntiles, tok_of_row, h, gate_rows, wg, wu, wd, tm):
    p_max = tok_of_row.shape[0]
    nt_max = p_max // tm
    grid_spec = pltpu.PrefetchScalarGridSpec(
        num_scalar_prefetch=3,
        grid=(nt_max,),
        in_specs=[pl.BlockSpec(memory_space=pl.ANY),
                  pl.BlockSpec((tm, 1), lambda t, te, nt, tok: (t, 0)),
                  pl.BlockSpec((1, D_MODEL, D_FF_EXPERT), lambda t, te, nt, tok: (te[t], 0, 0)),
                  pl.BlockSpec((1, D_MODEL, D_FF_EXPERT), lambda t, te, nt, tok: (te[t], 0, 0)),
                  pl.BlockSpec((1, D_FF_EXPERT, D_MODEL), lambda t, te, nt, tok: (te[t], 0, 0))],
        out_specs=pl.BlockSpec((tm, D_MODEL), lambda t, te, nt, tok: (t, 0)),
        scratch_shapes=[pltpu.VMEM((2, tm, D_MODEL), F32), pltpu.SemaphoreType.DMA((2,))],
    )
    return pl.pallas_call(
        functools.partial(_moe_kernel, tm=tm),
        grid_spec=grid_spec,
        out_shape=jax.ShapeDtypeStruct((p_max, D_MODEL), F32),
        compiler_params=_cparams(("arbitrary",)),
        name="moe_experts",
    )(tile_expert, ntiles, tok_of_row, h, gate_rows, wg, wu, wd)


def _combine_ln_kernel(pos_ref, y_hbm, h_ref, g_ref, b_ref, o32_ref, o16_ref, ybuf, sem, *, tm):
    t = pl.program_id(0)
    nt = pl.num_programs(0)
    slot = t % 2

    def issue(tile, sl):
        for j in range(2):
            def body(r, carry, j=j):
                row = pos_ref[(tile * tm + r) * 2 + j]
                pltpu.make_async_copy(y_hbm.at[pl.ds(row, 1), :],
                                      ybuf.at[sl, j, pl.ds(r, 1), :], sem.at[sl]).start()
                return carry
            lax.fori_loop(0, tm, body, 0, unroll=8)

    @pl.when(t == 0)
    def _():
        issue(0, 0)

    @pl.when(t + 1 < nt)
    def _():
        issue(t + 1, 1 - slot)

    pltpu.make_async_copy(ybuf.at[slot], ybuf.at[slot], sem.at[slot]).wait()
    y = ALPHA * h_ref[...] + (ybuf[slot, 0] + ybuf[slot, 1])
    o = _layer_norm_rows(y, g_ref[...], b_ref[...])
    o32_ref[...] = o
    o16_ref[...] = o.astype(BF16)


def _combine_ln(pos, ys, h, g, b, tm):
    s = h.shape[0]
    row = lambda t, pos: (t, 0)
    fixed = lambda t, pos: (0, 0)
    grid_spec = pltpu.PrefetchScalarGridSpec(
        num_scalar_prefetch=1,
        grid=(s // tm,),
        in_specs=[pl.BlockSpec(memory_space=pl.ANY),
                  pl.BlockSpec((tm, D_MODEL), row),
                  pl.BlockSpec((1, D_MODEL), fixed), pl.BlockSpec((1, D_MODEL), fixed)],
        out_specs=[pl.BlockSpec((tm, D_MODEL), row), pl.BlockSpec((tm, D_MODEL), row)],
        scratch_shapes=[pltpu.VMEM((2, 2, tm, D_MODEL), F32), pltpu.SemaphoreType.DMA((2,))],
    )
    return pl.pallas_call(
        functools.partial(_combine_ln_kernel, tm=tm),
        grid_spec=grid_spec,
        out_shape=[jax.ShapeDtypeStruct((s, D_MODEL), F32),
                   jax.ShapeDtypeStruct((s, D_MODEL), BF16)],
        compiler_params=_cparams(("arbitrary",)),
        name="combine_ln2",
    )(pos, ys, h, g, b)


def _dispatch_plan(meta, tm):
    t = meta.shape[0]
    eid = meta[:, 0:2].astype(jnp.int32).reshape(-1)
    gate = meta[:, 2:4].reshape(-1)
    onehot = (eid[:, None] == jnp.arange(N_EXPERTS, dtype=jnp.int32)[None, :]).astype(jnp.int32)
    csum = jnp.cumsum(onehot, axis=0)
    rank = jnp.sum(csum * onehot, axis=1) - 1
    counts = csum[-1]
    padded = ((counts + tm - 1) // tm) * tm
    pend = jnp.cumsum(padded)
    poff = pend - padded
    off = jnp.cumsum(counts) - counts
    pos = poff[eid] + rank
    p_max = 2 * t + N_EXPERTS * tm
    nt_max = p_max // tm
    ntiles = (pend[-1] // tm).astype(jnp.int32)
    tile_ids = jnp.minimum(jnp.arange(nt_max, dtype=jnp.int32), ntiles - 1)
    tile_expert = jnp.sum((pend[None, :] // tm <= tile_ids[:, None]).astype(jnp.int32), axis=1)
    tile_expert = jnp.minimum(tile_expert, N_EXPERTS - 1).astype(jnp.int32)
    order = jnp.argsort(eid, stable=True).astype(jnp.int32)
    row_expert = jnp.repeat(tile_expert, tm)
    k = jnp.arange(p_max, dtype=jnp.int32) - poff[row_expert]
    valid = jnp.logical_and(k >= 0, k < counts[row_expert])
    pair = order[jnp.clip(off[row_expert] + k, 0, 2 * t - 1)]
    tok_of_row = jnp.where(valid, pair // 2, 0).astype(jnp.int32)
    gate_rows = jnp.where(valid, gate[pair], 0.0).reshape(p_max, 1)
    return tile_expert, ntiles.reshape(1), tok_of_row, gate_rows, pos.astype(jnp.int32)


def _pad_lanes(v, width=LANES):
    v = v.reshape(1, -1).astype(F32)
    return jnp.pad(v, ((0, 0), (0, width - v.shape[1])))


def _layer(l, h32, h16, pos_f, invf, p, stacked, tb, tm_moe):
    s = h32.shape[0]
    w16, w32 = _prep_w_in(stacked["w_in"], l)
    s16 = jnp.concatenate([jnp.full((1, FOX_W), FOX_HD ** -0.5, F32),
                           jnp.ones((1, G16_W - FOX_W), F32)], axis=1)
    s32 = jnp.ones((1, G32_W), F32)
    tm_proj = min(s, 2048)
    g16 = _matmul(h16, w16, s16, BF16, tm_proj, 256, "in_proj_bf16")
    g32 = _matmul(h16, w32, s32, F32, tm_proj, 256, "in_proj_f32")

    ckb, crow = _fox_cumsum(g32, _pad_lanes(p["b_f"]), min(s, 256))
    out_a = _fox_attention(g16, ckb, crow, tb)

    out_b = _lru_branch(g32, p["conv_w"], p["conv_b"].reshape(1, -1),
                        p["w_a"].astype(BF16), p["b_a"].reshape(1, -1),
                        p["w_i"].astype(BF16), p["b_i"].reshape(1, -1),
                        p["lru_lambda"].reshape(1, -1), min(s, 1024), min(s, 256))

    qk16 = _rope(g32, pos_f, invf, min(s, 512))
    lamv = jnp.concatenate([_pad_lanes(p["lam_q1"]), _pad_lanes(p["lam_k1"]),
                            _pad_lanes(p["lam_q2"]), _pad_lanes(p["lam_k2"])], axis=0)
    lam_init = 0.8 - 0.6 * math.exp(-0.3 * l)
    out_c = _diff_attention(qk16, g16, lamv, p["subln_g"].reshape(1, -1), lam_init, tb)

    w_o = p["w_o"].astype(BF16)
    h1_32, h1_16 = _oproj_ln(out_a, out_b, out_c, w_o[:FOX_W], w_o[FOX_W:FOX_W + LRU_W],
                             w_o[FOX_W + LRU_W:], h32, p["ln1_g"].reshape(1, -1),
                             p["ln1_b"].reshape(1, -1), min(s, 256))
    del h1_16

    w_rt = jnp.concatenate([p["w_group"], p["w_router"],
                            jnp.zeros((D_MODEL, LANES - N_GROUPS - N_EXPERTS), F32)], axis=1)
    w_rt_hi = w_rt.astype(BF16)
    w_rt_lo = (w_rt - w_rt_hi.astype(F32)).astype(BF16)
    b_rt = _pad_lanes(jnp.concatenate([p["b_group"], p["b_router"]]))
    meta = _router(h1_32, w_rt_hi, w_rt_lo, b_rt, min(s, 512))

    tile_expert, ntiles, tok_of_row, gate_rows, pos = _dispatch_plan(meta, tm_moe)
    ys = _moe_grouped(tile_expert + l * N_EXPERTS, ntiles, tok_of_row, h1_32, gate_rows,
                      stacked["w_gate"], stacked["w_up"], stacked["w_down"], tm_moe)
    return _combine_ln(pos, ys, h1_32, p["ln2_g"].reshape(1, -1), p["ln2_b"].reshape(1, -1),
                       min(s, 256))


def _forward(x, positions, params, tb, tm_moe):
    bsz, s, d = x.shape
    assert bsz == 1 and d == D_MODEL
    h32 = x.reshape(s, d)
    h16 = h32.astype(BF16)
    half = ROT_DIM // 2
    inv_freq = ROPE_THETA ** (-jnp.arange(half, dtype=F32) * 2.0 / ROT_DIM)
    dlane = jnp.arange(LANES) % DIFF_HD
    invf = jnp.where(dlane < ROT_DIM, inv_freq[dlane % half], 0.0).reshape(1, LANES).astype(F32)
    pos_f = jnp.broadcast_to(positions.astype(F32).reshape(s, 1), (s, LANES))
    big = ("w_in", "w_gate", "w_up", "w_down")
    stacked = {
        "w_in": params["w_in"],
        "w_gate": params["w_gate"].reshape(DEPTH * N_EXPERTS, D_MODEL, D_FF_EXPERT),
        "w_up": params["w_up"].reshape(DEPTH * N_EXPERTS, D_MODEL, D_FF_EXPERT),
        "w_down": params["w_down"].reshape(DEPTH * N_EXPERTS, D_FF_EXPERT, D_MODEL),
    }
    for l in range(DEPTH):
        p = {k: v[l] for k, v in params.items() if k not in big}
        h32, h16 = _layer(l, h32, h16, pos_f, invf, p, stacked, tb, tm_moe)
    return h32.reshape(bsz, s, d)


def kernel(x, positions, w_in, b_f, conv_w, conv_b, w_a, b_a, w_i, b_i, lru_lambda, lam_q1, lam_k1, lam_q2, lam_k2, subln_g, w_o, ln1_g, ln1_b, w_group, b_group, w_router, b_router, w_gate, w_up, w_down, ln2_g, ln2_b):
    params = dict(w_in=w_in, b_f=b_f, conv_w=conv_w, conv_b=conv_b, w_a=w_a, b_a=b_a, w_i=w_i,
                  b_i=b_i, lru_lambda=lru_lambda, lam_q1=lam_q1, lam_k1=lam_k1, lam_q2=lam_q2,
                  lam_k2=lam_k2, subln_g=subln_g, w_o=w_o, ln1_g=ln1_g, ln1_b=ln1_b,
                  w_group=w_group, b_group=b_group, w_router=w_router, b_router=b_router,
                  w_gate=w_gate, w_up=w_up, w_down=w_down, ln2_g=ln2_g, ln2_b=ln2_b)
    s = x.shape[1]
    return _forward(x, positions, params, tb=min(s, 512), tm_moe=min(s, 256))
```

```python
import functools
import math

import jax
import jax.numpy as jnp
from jax import lax
from jax.experimental import pallas as pl
from jax.experimental.pallas import tpu as pltpu

F32 = jnp.float32
BF16 = jnp.bfloat16

D_MODEL = 2048
DEPTH = 2
CHUNK = 64
FOX_HEADS = 6
FOX_HD = 128
FOX_W = FOX_HEADS * FOX_HD
LRU_W = 768
LRU_BLOCKS = 6
LRU_BW = LRU_W // LRU_BLOCKS
CONV_W = 4
LRU_C = 8.0
DIFF_HEADS = 4
DIFF_HD = 64
DIFF_VD = 2 * DIFF_HD
DIFF_W = DIFF_HEADS * DIFF_VD
ROT_DIM = DIFF_HD // 4
ROPE_THETA = 500000.0
N_GROUPS = 4
EXPERTS_PER_GROUP = 8
N_EXPERTS = N_GROUPS * EXPERTS_PER_GROUP
D_FF_EXPERT = 512
ALPHA = (2.0 * DEPTH) ** 0.25
LN_EPS = 1e-5

LANES = 128
NEG_BIG = -1e30
LOG2E = math.log2(math.e)
SKIP_BITS = 72.0
VMEM_LIMIT = 56 * 1024 * 1024

G16_W = 3 * FOX_W + DIFF_W
G32_W = 2 * DIFF_W + 2 * LRU_W + 2 * LANES
G16_QA, G16_KA, G16_VA, G16_VC = 0, 6, 12, 18
G32_XR, G32_GR, G32_FA = 8, 14, 20


def _cparams(sem, vmem=VMEM_LIMIT):
    return pltpu.CompilerParams(dimension_semantics=sem, vmem_limit_bytes=vmem)


def _mm_kernel(x_ref, w_ref, s_ref, o_ref):
    acc = jnp.dot(x_ref[...], w_ref[...], preferred_element_type=F32)
    o_ref[...] = (acc * s_ref[...]).astype(o_ref.dtype)


def _matmul(x, w, s, out_dtype, tm, tn, name):
    m, k = x.shape
    n = w.shape[1]
    return pl.pallas_call(
        _mm_kernel,
        grid=(m // tm, n // tn),
        in_specs=[pl.BlockSpec((tm, k), lambda i, j: (i, 0)),
                  pl.BlockSpec((k, tn), lambda i, j: (0, j)),
                  pl.BlockSpec((1, tn), lambda i, j: (0, j))],
        out_specs=pl.BlockSpec((tm, tn), lambda i, j: (i, j)),
        out_shape=jax.ShapeDtypeStruct((m, n), out_dtype),
        compiler_params=_cparams(("parallel", "parallel")),
        name=name,
    )(x, w, s)


_IN_QA = 0
_IN_KA = _IN_QA + FOX_W
_IN_VA = _IN_KA + FOX_W
_IN_FA = _IN_VA + FOX_W
_IN_XR = _IN_FA + FOX_HEADS
_IN_GR = _IN_XR + LRU_W
_IN_QC = _IN_GR + LRU_W
_IN_KC = _IN_QC + DIFF_W
_IN_VC = _IN_KC + DIFF_W
N_IN = _IN_VC + DIFF_W


def _wprep_kernel(w_ref, o16_ref, o32_ref):
    def put(o_ref, dst, src, width):
        o_ref[:, dst:dst + width] = w_ref[0, :, src:src + width].astype(BF16)

    put(o16_ref, 0, _IN_QA, FOX_W)
    put(o16_ref, FOX_W, _IN_KA, FOX_W)
    put(o16_ref, 2 * FOX_W, _IN_VA, FOX_W)
    put(o16_ref, 3 * FOX_W, _IN_VC, DIFF_W)
    put(o32_ref, 0, _IN_QC, DIFF_W)
    put(o32_ref, DIFF_W, _IN_KC, DIFF_W)
    put(o32_ref, G32_XR * LANES, _IN_XR, LRU_W)
    put(o32_ref, G32_GR * LANES, _IN_GR, LRU_W)
    fa0 = G32_FA * LANES
    o32_ref[:, fa0:fa0 + 2 * LANES] = jnp.zeros((o32_ref.shape[0], 2 * LANES), BF16)
    put(o32_ref, fa0, _IN_FA, FOX_HEADS)


def _prep_w_in(w_in, layer, tk=256):
    return pl.pallas_call(
        _wprep_kernel,
        grid=(D_MODEL // tk,),
        in_specs=[pl.BlockSpec((1, tk, N_IN), lambda i: (layer, i, 0))],
        out_specs=[pl.BlockSpec((tk, G16_W), lambda i: (i, 0)),
                   pl.BlockSpec((tk, G32_W), lambda i: (i, 0))],
        out_shape=[jax.ShapeDtypeStruct((D_MODEL, G16_W), BF16),
                   jax.ShapeDtypeStruct((D_MODEL, G32_W), BF16)],
        compiler_params=_cparams(("parallel",)),
        name="w_in_regroup",
    )(w_in)


def _split3(x):
    hi = x.astype(BF16)
    r1 = x - hi.astype(F32)
    mid = r1.astype(BF16)
    lo = (r1 - mid.astype(F32)).astype(BF16)
    return hi, mid, lo


def _fox_prep_kernel(fa_ref, bf_ref, q_ref, k_ref, qaux_ref, kaux_ref, stat_ref, carry_ref):
    @pl.when(pl.program_id(0) == 0)
    def _():
        carry_ref[...] = jnp.zeros_like(carry_ref)

    z = fa_ref[...] + bf_ref[...]
    lf = (jnp.minimum(z, 0.0) - jnp.log1p(jnp.exp(-jnp.abs(z)))) * LOG2E
    t = z.shape[0]
    row = lax.broadcasted_iota(jnp.int32, (t, t), 0)
    col = lax.broadcasted_iota(jnp.int32, (t, t), 1)
    tri = jnp.where(row >= col, 1.0, 0.0).astype(BF16)
    hi, mid, lo = _split3(lf)
    cs = (jnp.dot(tri, hi, preferred_element_type=F32)
          + jnp.dot(tri, mid, preferred_element_type=F32)
          + jnp.dot(tri, lo, preferred_element_type=F32)) + carry_ref[...]
    carry_ref[...] = cs[t - 1:t, :]

    lane = lax.broadcasted_iota(jnp.int32, (t, LANES), 1)
    lane1 = lax.broadcasted_iota(jnp.int32, (1, LANES), 1)
    qn = jnp.zeros((1, LANES), F32)
    kn = jnp.zeros((1, LANES), F32)
    for h in range(FOX_HEADS):
        c = jnp.broadcast_to(jnp.sum(jnp.where(lane == h, cs, 0.0), axis=1, keepdims=True),
                             (t, LANES))
        c_hi = c.astype(BF16).astype(F32)
        r1 = c - c_hi
        c_mid = r1.astype(BF16).astype(F32)
        c_lo = (r1 - c_mid).astype(BF16).astype(F32)
        kaux = jnp.where(lane == 0, c_hi, jnp.where(lane == 1, c_mid, jnp.where(
            lane == 2, c_lo, jnp.where(lane < 6, 1.0, 0.0))))
        qaux = jnp.where(lane < 3, -1.0, jnp.where(lane == 3, c_hi, jnp.where(
            lane == 4, c_mid, jnp.where(lane == 5, c_lo, 0.0))))
        kaux_ref[h] = kaux.astype(BF16)
        qaux_ref[h] = qaux.astype(BF16)
        qh = q_ref[:, h * FOX_HD:(h + 1) * FOX_HD].astype(F32)
        kh = k_ref[:, h * FOX_HD:(h + 1) * FOX_HD].astype(F32)
        q2 = jnp.max(jnp.sum(qh * qh, axis=1, keepdims=True), axis=0, keepdims=True)
        k2 = jnp.max(jnp.sum(kh * kh, axis=1, keepdims=True), axis=0, keepdims=True)
        qn = jnp.where(lane1 == h, q2, qn)
        kn = jnp.where(lane1 == h, k2, kn)
    stat_ref[0] = jnp.concatenate([qn, kn, cs[0:1, :], cs[t - 1:t, :],
                                   jnp.zeros((4, LANES), F32)], axis=0)


def _fox_prep(g32, g16, bf_pad, tb):
    s = g32.shape[0]
    qk_blocks = FOX_W // LANES
    return pl.pallas_call(
        _fox_prep_kernel,
        grid=(s // tb,),
        in_specs=[pl.BlockSpec((tb, LANES), lambda i: (i, G32_FA)),
                  pl.BlockSpec((1, LANES), lambda i: (0, 0)),
                  pl.BlockSpec((tb, FOX_W), lambda i: (i, G16_QA // qk_blocks)),
                  pl.BlockSpec((tb, FOX_W), lambda i: (i, G16_KA // qk_blocks))],
        out_specs=[pl.BlockSpec((FOX_HEADS, tb, LANES), lambda i: (0, i, 0)),
                   pl.BlockSpec((FOX_HEADS, tb, LANES), lambda i: (0, i, 0)),
                   pl.BlockSpec((1, 8, LANES), lambda i: (i, 0, 0))],
        out_shape=[jax.ShapeDtypeStruct((FOX_HEADS, s, LANES), BF16),
                   jax.ShapeDtypeStruct((FOX_HEADS, s, LANES), BF16),
                   jax.ShapeDtypeStruct((s // tb, 8, LANES), F32)],
        scratch_shapes=[pltpu.VMEM((1, LANES), F32)],
        compiler_params=_cparams(("arbitrary",)),
        name="fox_prep",
    )(g32, bf_pad, g16, g16)


def _fox_skip_table(stat, per_block):
    nb = stat.shape[0] // per_block
    st = stat.reshape(nb, per_block, 8, LANES)
    qn = jnp.sqrt(jnp.max(st[:, :, 0, :FOX_HEADS], axis=1))
    kn = jnp.sqrt(jnp.max(st[:, :, 1, :FOX_HEADS], axis=1))
    c_first = st[:, 0, 2, :FOX_HEADS]
    c_last = st[:, per_block - 1, 3, :FOX_HEADS]
    kmax = jnp.max(kn, axis=0, keepdims=True)
    bound = 1.01 * qn * (kmax + kn) + c_first
    skip = (bound[:, None, :] - c_last[None, :, :]) < -SKIP_BITS
    jj = jnp.arange(nb)
    skip = jnp.logical_and(skip, (jj[None, :] < jj[:, None])[:, :, None])
    lead = jnp.cumprod(skip.astype(jnp.int32), axis=1)
    return jnp.sum(lead, axis=1).T.reshape(-1)


def _rope_kernel(x_ref, pos_ref, invf_ref, o_ref):
    ang = pos_ref[...] * invf_ref[...]
    c = jnp.cos(ang)
    s = jnp.sin(ang)
    half = ROT_DIM // 2
    d = lax.broadcasted_iota(jnp.int32, (1, LANES), 1) % DIFF_HD
    sa = jnp.where(d >= half, s, 0.0)
    sb = jnp.where(d < half, -s, 0.0)
    nblk = x_ref.shape[1] // LANES
    for j in range(nblk):
        t = x_ref[:, j * LANES:(j + 1) * LANES]
        r = t * c + pltpu.roll(t, half, 1) * sa + pltpu.roll(t, LANES - half, 1) * sb
        if j < nblk // 2:
            r = r * (DIFF_HD ** -0.5 * LOG2E)
        o_ref[:, j * LANES:(j + 1) * LANES] = r.astype(o_ref.dtype)


def _rope(g32, pos_f, invf, tb):
    s = g32.shape[0]
    w = 2 * DIFF_W
    return pl.pallas_call(
        _rope_kernel,
        grid=(s // tb,),
        in_specs=[pl.BlockSpec((tb, w), lambda i: (i, 0)),
                  pl.BlockSpec((tb, LANES), lambda i: (i, 0)),
                  pl.BlockSpec((1, LANES), lambda i: (0, 0))],
        out_specs=pl.BlockSpec((tb, w), lambda i: (i, 0)),
        out_shape=jax.ShapeDtypeStruct((s, w), BF16),
        compiler_params=_cparams(("parallel",)),
        name="diff_rope",
    )(g32, pos_f, invf)


def _flash_step(v, s, m_ref, l_ref, acc_ref):
    m_prev = m_ref[...]
    m_new = jnp.maximum(m_prev, jnp.max(s, axis=0, keepdims=True))
    a = jnp.exp2(m_prev - m_new)
    p = jnp.exp2(s - m_new)
    l_ref[...] = a * l_ref[...] + jnp.sum(p, axis=0, keepdims=True)
    pv = lax.dot_general(v, p.astype(v.dtype), (((0,), (0,)), ((), ())),
                         preferred_element_type=F32)
    acc_ref[...] = a * acc_ref[...] + pv
    m_ref[...] = m_new


def _flash_pipeline(j0, i, qk, sm):
    npairs = (i - j0) // 2
    qk(j0, 0)

    def body(p, carry):
        j = j0 + 2 * p
        qk(j + 1, 1)
        sm(j, 0, False)
        qk(j + 2, 0)
        sm(j + 1, 1, False)
        return carry

    lax.fori_loop(0, npairs, body, 0)
    jn = j0 + 2 * npairs

    @pl.when(jn == i)
    def _():
        sm(i, 0, True)

    @pl.when(jn != i)
    def _():
        qk(i, 1)
        sm(jn, 0, False)
        sm(i, 1, True)


def _init_state(m_ref, l_ref, acc_ref):
    m_ref[...] = jnp.full(m_ref.shape, NEG_BIG, F32)
    l_ref[...] = jnp.zeros(l_ref.shape, F32)
    acc_ref[...] = jnp.zeros(acc_ref.shape, F32)


def _fox_kernel(jmin_ref, q_ref, qaux_ref, k_ref, kaux_ref, v_ref, o_ref,
                m_ref, l_ref, acc_ref, sa_ref, sb_ref, *, tb):
    h = pl.program_id(0)
    i = pl.program_id(1)
    qf = jnp.concatenate([q_ref[...], qaux_ref[0]], axis=1)
    _init_state(m_ref, l_ref, acc_ref)
    s_refs = (sa_ref, sb_ref)

    def qk(j, slot):
        k0 = pl.multiple_of(j * tb, tb)
        kf = jnp.concatenate([k_ref[pl.ds(k0, tb), :], kaux_ref[0, pl.ds(k0, tb), :]], axis=1)
        s_refs[slot][...] = lax.dot_general(kf, qf, (((1,), (1,)), ((), ())),
                                            preferred_element_type=F32)

    def sm(j, slot, masked):
        s = s_refs[slot][...]
        if masked:
            kk = lax.broadcasted_iota(jnp.int32, (tb, tb), 0)
            qq = lax.broadcasted_iota(jnp.int32, (tb, tb), 1)
            s = jnp.where(kk <= qq, s, NEG_BIG)
        v = v_ref[pl.ds(pl.multiple_of(j * tb, tb), tb), :]
        _flash_step(v, s, m_ref, l_ref, acc_ref)

    _flash_pipeline(jmin_ref[h * pl.num_programs(1) + i], i, qk, sm)
    o = acc_ref[...] / l_ref[...]
    o_ref[...] = o.T.astype(o_ref.dtype)


def _fox_attention(jmin, g16, qaux, kaux, tb):
    s = g16.shape[0]
    grid_spec = pltpu.PrefetchScalarGridSpec(
        num_scalar_prefetch=1,
        grid=(FOX_HEADS, s // tb),
        in_specs=[pl.BlockSpec((tb, LANES), lambda h, i, jm: (i, G16_QA + h)),
                  pl.BlockSpec((1, tb, LANES), lambda h, i, jm: (h, i, 0)),
                  pl.BlockSpec((s, LANES), lambda h, i, jm: (0, G16_KA + h)),
                  pl.BlockSpec((1, s, LANES), lambda h, i, jm: (h, 0, 0)),
                  pl.BlockSpec((s, LANES), lambda h, i, jm: (0, G16_VA + h))],
        out_specs=pl.BlockSpec((tb, LANES), lambda h, i, jm: (i, h)),
        scratch_shapes=[pltpu.VMEM((1, tb), F32), pltpu.VMEM((1, tb), F32),
                        pltpu.VMEM((FOX_HD, tb), F32),
                        pltpu.VMEM((tb, tb), F32), pltpu.VMEM((tb, tb), F32)],
    )
    return pl.pallas_call(
        functools.partial(_fox_kernel, tb=tb),
        grid_spec=grid_spec,
        out_shape=jax.ShapeDtypeStruct((s, FOX_W), BF16),
        compiler_params=_cparams(("parallel", "parallel")),
        name="fox_attention",
    )(jmin, g16, qaux, g16, kaux, g16)


def _diff_kernel(q_ref, k_ref, v_ref, lamv_ref, g_ref, o_ref, m_ref, l_ref, acc_ref,
                 sa_ref, sb_ref, *, tb, lam_init):
    i = pl.program_id(1)
    q = q_ref[...].astype(F32)
    lane = lax.broadcasted_iota(jnp.int32, (tb, LANES), 1)
    qq2 = jnp.concatenate([jnp.where(lane < DIFF_HD, q, 0.0),
                           jnp.where(lane >= DIFF_HD, q, 0.0)], axis=0).astype(BF16)
    _init_state(m_ref, l_ref, acc_ref)
    s_refs = (sa_ref, sb_ref)

    def qk(j, slot):
        k = k_ref[pl.ds(pl.multiple_of(j * tb, tb), tb), :]
        s_refs[slot][...] = lax.dot_general(k, qq2, (((1,), (1,)), ((), ())),
                                            preferred_element_type=F32)

    def sm(j, slot, masked):
        s = s_refs[slot][...]
        if masked:
            kc = lax.broadcasted_iota(jnp.int32, (tb, 2 * tb), 0) // CHUNK
            qc = (lax.broadcasted_iota(jnp.int32, (tb, 2 * tb), 1) % tb) // CHUNK
            s = jnp.where(kc <= qc, s, NEG_BIG)
        v = v_ref[pl.ds(pl.multiple_of(j * tb, tb), tb), :]
        _flash_step(v, s, m_ref, l_ref, acc_ref)

    _flash_pipeline(0, i, qk, sm)

    lv = lamv_ref[...]
    lam = (jnp.exp(jnp.sum(lv[0:1] * lv[1:2], axis=1, keepdims=True))
           - jnp.exp(jnp.sum(lv[2:3] * lv[3:4], axis=1, keepdims=True)) + lam_init)
    on = acc_ref[...] / l_ref[...]
    o = (on[:, :tb] - lam * on[:, tb:]).T
    ms = jnp.mean(o * o, axis=-1, keepdims=True)
    o = o * lax.rsqrt(ms + LN_EPS) * g_ref[...] * (1.0 - lam_init)
    o_ref[...] = o.astype(o_ref.dtype)


def _diff_attention(qk16, g16, lamv, subg, lam_init, tb):
    s = g16.shape[0]
    return pl.pallas_call(
        functools.partial(_diff_kernel, tb=tb, lam_init=lam_init),
        grid=(DIFF_HEADS, s // tb),
        in_specs=[pl.BlockSpec((tb, LANES), lambda h, i: (i, h)),
                  pl.BlockSpec((s, LANES), lambda h, i: (0, DIFF_HEADS + h)),
                  pl.BlockSpec((s, LANES), lambda h, i: (0, G16_VC + h)),
                  pl.BlockSpec((4, LANES), lambda h, i: (0, 0)),
                  pl.BlockSpec((1, LANES), lambda h, i: (0, 0))],
        out_specs=pl.BlockSpec((tb, LANES), lambda h, i: (i, h)),
        out_shape=jax.ShapeDtypeStruct((s, DIFF_W), BF16),
        scratch_shapes=[pltpu.VMEM((1, 2 * tb), F32), pltpu.VMEM((1, 2 * tb), F32),
                        pltpu.VMEM((DIFF_VD, 2 * tb), F32),
                        pltpu.VMEM((tb, 2 * tb), F32), pltpu.VMEM((tb, 2 * tb), F32)],
        compiler_params=_cparams(("parallel", "parallel")),
        name="diff_attention",
    )(qk16, qk16, g16, lamv, subg)


def _shift_rows(x, d, fill):
    rows = lax.broadcasted_iota(jnp.int32, x.shape, 0)
    return jnp.where(rows >= d, pltpu.roll(x, d, 0), fill)


def _lru_kernel(xr_ref, gr_ref, cw_ref, cb_ref, wa_ref, ba_ref, wi_ref, bi_ref, lam_ref,
                o_ref, halo_ref, h_ref, *, tb, sub):
    @pl.when(pl.program_id(1) == 0)
    def _():
        halo_ref[...] = jnp.zeros_like(halo_ref)
        h_ref[...] = jnp.zeros_like(h_ref)

    x = xr_ref[...]
    xe = jnp.concatenate([halo_ref[...], x], axis=0)
    halo_ref[...] = x[tb - 8:tb, :]
    cw = cw_ref[...]
    xc = cb_ref[...] + cw[CONV_W - 1:CONV_W, :] * x
    for j in range(CONV_W - 1):
        sh = CONV_W - 1 - j
        xc = xc + cw[j:j + 1, :] * pltpu.roll(xe, sh, 0)[8:8 + tb, :]

    xcb = xc.astype(BF16)
    r = jax.nn.sigmoid(jnp.dot(xcb, wa_ref[0], preferred_element_type=F32) + ba_ref[...])
    ig = jax.nn.sigmoid(jnp.dot(xcb, wi_ref[0], preferred_element_type=F32) + bi_ref[...])
    lam = lam_ref[...]
    ls = jnp.minimum(lam, 0.0) - jnp.log1p(jnp.exp(-jnp.abs(lam)))
    log_a = LRU_C * r * ls
    a = jnp.exp(log_a)
    z2 = 2.0 * log_a
    e2 = jnp.exp(z2)
    small = jnp.where(e2 == 1.0, -z2, (1.0 - e2) * z2 / jnp.log(e2))
    neg_expm1 = jnp.where(z2 < -1.0, 1.0 - e2, small)
    u = jnp.sqrt(neg_expm1) * (ig * xc)

    h = h_ref[...]
    for c in range(tb // sub):
        ac = a[c * sub:(c + 1) * sub, :]
        uc = u[c * sub:(c + 1) * sub, :]
        d = 1
        while d < sub:
            uc = ac * _shift_rows(uc, d, 0.0) + uc
            ac = ac * _shift_rows(ac, d, 1.0)
            d *= 2
        hc = uc + ac * h
        h = hc[sub - 1:sub, :]
        g = gr_ref[c * sub:(c + 1) * sub, :]
        gelu = 0.5 * g * (1.0 + jnp.tanh(math.sqrt(2.0 / math.pi) * (g + 0.044715 * (g * g * g))))
        o_ref[c * sub:(c + 1) * sub, :] = (gelu * hc).astype(o_ref.dtype)
    h_ref[...] = h


def _lru_branch(g32, cw, cb, wa, ba, wi, bi, lam, tb, sub):
    s = g32.shape[0]
    vec = lambda c, i: (0, c)
    return pl.pallas_call(
        functools.partial(_lru_kernel, tb=tb, sub=sub),
        grid=(LRU_BLOCKS, s // tb),
        in_specs=[pl.BlockSpec((tb, LANES), lambda c, i: (i, G32_XR + c)),
                  pl.BlockSpec((tb, LANES), lambda c, i: (i, G32_GR + c)),
                  pl.BlockSpec((CONV_W, LANES), vec),
                  pl.BlockSpec((1, LANES), vec),
                  pl.BlockSpec((1, LRU_BW, LRU_BW), lambda c, i: (c, 0, 0)),
                  pl.BlockSpec((1, LANES), vec),
                  pl.BlockSpec((1, LRU_BW, LRU_BW), lambda c, i: (c, 0, 0)),
                  pl.BlockSpec((1, LANES), vec),
                  pl.BlockSpec((1, LANES), vec)],
        out_specs=pl.BlockSpec((tb, LANES), lambda c, i: (i, c)),
        out_shape=jax.ShapeDtypeStruct((s, LRU_W), BF16),
        scratch_shapes=[pltpu.VMEM((8, LANES), F32), pltpu.VMEM((1, LANES), F32)],
        compiler_params=_cparams(("parallel", "arbitrary")),
        name="rg_lru",
    )(g32, g32, cw, cb, wa, ba, wi, bi, lam)


def _layer_norm_rows(y, g, b):
    mu = jnp.mean(y, axis=-1, keepdims=True)
    yc = y - mu
    var = jnp.mean(yc * yc, axis=-1, keepdims=True)
    return yc * lax.rsqrt(var + LN_EPS) * g + b


def _oproj_ln_kernel(xa_ref, xb_ref, xc_ref, wa_ref, wb_ref, wc_ref, h_ref, g_ref, b_ref,
                     o32_ref, o16_ref):
    mix = (jnp.dot(xa_ref[...], wa_ref[...], preferred_element_type=F32)
           + jnp.dot(xb_ref[...], wb_ref[...], preferred_element_type=F32)
           + jnp.dot(xc_ref[...], wc_ref[...], preferred_element_type=F32))
    o = _layer_norm_rows(ALPHA * h_ref[...] + mix, g_ref[...], b_ref[...])
    o32_ref[...] = o
    o16_ref[...] = o.astype(BF16)


def _oproj_ln(xa, xb, xc, wa, wb, wc, h, g, b, tm):
    s = h.shape[0]
    row = lambda i: (i, 0)
    fixed = lambda i: (0, 0)
    return pl.pallas_call(
        _oproj_ln_kernel,
        grid=(s // tm,),
        in_specs=[pl.BlockSpec((tm, FOX_W), row), pl.BlockSpec((tm, LRU_W), row),
                  pl.BlockSpec((tm, DIFF_W), row),
                  pl.BlockSpec((FOX_W, D_MODEL), fixed), pl.BlockSpec((LRU_W, D_MODEL), fixed),
                  pl.BlockSpec((DIFF_W, D_MODEL), fixed),
                  pl.BlockSpec((tm, D_MODEL), row),
                  pl.BlockSpec((1, D_MODEL), fixed), pl.BlockSpec((1, D_MODEL), fixed)],
        out_specs=[pl.BlockSpec((tm, D_MODEL), row), pl.BlockSpec((tm, D_MODEL), row)],
        out_shape=[jax.ShapeDtypeStruct((s, D_MODEL), F32),
                   jax.ShapeDtypeStruct((s, D_MODEL), BF16)],
        compiler_params=_cparams(("parallel",)),
        name="oproj_ln1",
    )(xa, xb, xc, wa, wb, wc, h, g, b)


def _router_kernel(h_ref, whi_ref, wlo_ref, b_ref, o_ref):
    h = h_ref[...]
    hi = h.astype(BF16)
    lo = (h - hi.astype(F32)).astype(BF16)
    whi = whi_ref[...]
    logits = (jnp.dot(hi, whi, preferred_element_type=F32)
              + jnp.dot(hi, wlo_ref[...], preferred_element_type=F32)
              + jnp.dot(lo, whi, preferred_element_type=F32)) + b_ref[...]
    lane = lax.broadcasted_iota(jnp.int32, logits.shape, 1)
    big = jnp.int32(1 << 20)

    def first_lane(cond):
        return jnp.min(jnp.where(cond, lane, big), axis=1, keepdims=True)

    gm = lane < N_GROUPS
    gl = jnp.where(gm, logits, -jnp.inf)
    gmax = jnp.max(gl, axis=1, keepdims=True)
    gexp = jnp.where(gm, jnp.exp(logits - gmax), 0.0)
    gprob = gexp / jnp.sum(gexp, axis=1, keepdims=True)
    gidx = first_lane(gl == gmax)
    g_weight = jnp.sum(jnp.where(lane == gidx, gprob, 0.0), axis=1, keepdims=True)

    e0 = N_GROUPS + EXPERTS_PER_GROUP * gidx
    em = jnp.logical_and(lane >= e0, lane < e0 + EXPERTS_PER_GROUP)
    el = jnp.where(em, logits, -jnp.inf)
    emax = jnp.max(el, axis=1, keepdims=True)
    eexp = jnp.where(em, jnp.exp(logits - emax), 0.0)
    eprob = jnp.where(em, eexp / jnp.sum(eexp, axis=1, keepdims=True), -1.0)
    p1 = jnp.max(eprob, axis=1, keepdims=True)
    i1 = first_lane(eprob == p1)
    eprob2 = jnp.where(lane == i1, -1.0, eprob)
    p2 = jnp.max(eprob2, axis=1, keepdims=True)
    i2 = first_lane(eprob2 == p2)
    den = p1 + p2
    gate1 = g_weight * (p1 / den)
    gate2 = g_weight * (p2 / den)
    id1 = (i1 - N_GROUPS).astype(F32)
    id2 = (i2 - N_GROUPS).astype(F32)
    o_ref[...] = jnp.where(lane == 0, id1,
                           jnp.where(lane == 1, id2,
                                     jnp.where(lane == 2, gate1,
                                               jnp.where(lane == 3, gate2, 0.0))))


def _router(h, whi, wlo, bias, tm):
    s = h.shape[0]
    fixed = lambda i: (0, 0)
    return pl.pallas_call(
        _router_kernel,
        grid=(s // tm,),
        in_specs=[pl.BlockSpec((tm, D_MODEL), lambda i: (i, 0)),
                  pl.BlockSpec((D_MODEL, LANES), fixed), pl.BlockSpec((D_MODEL, LANES), fixed),
                  pl.BlockSpec((1, LANES), fixed)],
        out_specs=pl.BlockSpec((tm, LANES), lambda i: (i, 0)),
        out_shape=jax.ShapeDtypeStruct((s, LANES), F32),
        compiler_params=_cparams(("parallel",)),
        name="moe_router",
    )(h, whi, wlo, bias)


def _issue_row_gather(idx_ref, base, n, src_hbm, dst_ref, sem):
    def body(r, carry):
        row = idx_ref[base + r]
        pltpu.make_async_copy(src_hbm.at[pl.ds(row, 1), :], dst_ref.at[pl.ds(r, 1), :], sem).start()
        return carry
    lax.fori_loop(0, n, body, 0, unroll=8)


def _moe_kernel(te_ref, nt_ref, tok_ref, h_hbm, gate_ref, wg_ref, wu_ref, wd_ref, y_ref,
                xbuf, sem, *, tm):
    del te_ref
    t = pl.program_id(0)
    nt = nt_ref[0]
    slot = t % 2

    @pl.when(t == 0)
    def _():
        _issue_row_gather(tok_ref, 0, tm, h_hbm, xbuf.at[0], sem.at[0])

    @pl.when(t + 1 < nt)
    def _():
        _issue_row_gather(tok_ref, (t + 1) * tm, tm, h_hbm, xbuf.at[1 - slot], sem.at[1 - slot])

    @pl.when(t < nt)
    def _():
        pltpu.make_async_copy(xbuf.at[slot], xbuf.at[slot], sem.at[slot]).wait()
        x = xbuf[slot].astype(BF16)
        hg = jnp.dot(x, wg_ref[0].astype(BF16), preferred_element_type=F32)
        hu = jnp.dot(x, wu_ref[0].astype(BF16), preferred_element_type=F32)
        act = (hg * jax.nn.sigmoid(hg)) * hu * gate_ref[...]
        y_ref[...] = jnp.dot(act.astype(BF16), wd_ref[0].astype(BF16),
                             preferred_element_type=F32)

    @pl.when(t >= nt)
    def _():
        y_ref[...] = jnp.zeros_like(y_ref)


def _moe_grouped(tile_expert, ntiles, tok_of_row, h, gate_rows, wg, wu, wd, tm):
    p_max = tok_of_row.shape[0]
    nt_max = p_max // tm
    grid_spec = pltpu.PrefetchScalarGridSpec(
        num_scalar_prefetch=3,
        grid=(nt_max,),
        in_specs=[pl.BlockSpec(memory_space=pl.ANY),
                  pl.BlockSpec((tm, 1), lambda t, te, nt, tok: (t, 0)),
                  pl.BlockSpec((1, D_MODEL, D_FF_EXPERT), lambda t, te, nt, tok: (te[t], 0, 0)),
                  pl.BlockSpec((1, D_MODEL, D_FF_EXPERT), lambda t, te, nt, tok: (te[t], 0, 0)),
                  pl.BlockSpec((1, D_FF_EXPERT, D_MODEL), lambda t, te, nt, tok: (te[t], 0, 0))],
        out_specs=pl.BlockSpec((tm, D_MODEL), lambda t, te, nt, tok: (t, 0)),
        scratch_shapes=[pltpu.VMEM((2, tm, D_MODEL), F32), pltpu.SemaphoreType.DMA((2,))],
    )
    return pl.pallas_call(
        functools.partial(_moe_kernel, tm=tm),
        grid_spec=grid_spec,
        out_shape=jax.ShapeDtypeStruct((p_max, D_MODEL), F32),
        compiler_params=_cparams(("arbitrary",)),
        name="moe_experts",
    )(tile_expert, ntiles, tok_of_row, h, gate_rows, wg, wu, wd)


def _combine_ln_kernel(pos_ref, y_hbm, h_ref, g_ref, b_ref, o32_ref, o16_ref, ybuf, sem, *, tm):
    t = pl.program_id(0)
    nt = pl.num_programs(0)
    slot = t % 2

    def issue(tile, sl):
        for j in range(2):
            def body(r, carry, j=j):
                row = pos_ref[(tile * tm + r) * 2 + j]
                pltpu.make_async_copy(y_hbm.at[pl.ds(row, 1), :],
                                      ybuf.at[sl, j, pl.ds(r, 1), :], sem.at[sl]).start()
                return carry
            lax.fori_loop(0, tm, body, 0, unroll=8)

    @pl.when(t == 0)
    def _():
        issue(0, 0)

    @pl.when(t + 1 < nt)
    def _():
        issue(t + 1, 1 - slot)

    pltpu.make_async_copy(ybuf.at[slot], ybuf.at[slot], sem.at[slot]).wait()
    y = ALPHA * h_ref[...] + (ybuf[slot, 0] + ybuf[slot, 1])
    o = _layer_norm_rows(y, g_ref[...], b_ref[...])
    o32_ref[...] = o
    o16_ref[...] = o.astype(BF16)


def _combine_ln(pos, ys, h, g, b, tm):
    s = h.shape[0]
    row = lambda t, pos: (t, 0)
    fixed = lambda t, pos: (0, 0)
    grid_spec = pltpu.PrefetchScalarGridSpec(
        num_scalar_prefetch=1,
        grid=(s // tm,),
        in_specs=[pl.BlockSpec(memory_space=pl.ANY),
                  pl.BlockSpec((tm, D_MODEL), row),
                  pl.BlockSpec((1, D_MODEL), fixed), pl.BlockSpec((1, D_MODEL), fixed)],
        out_specs=[pl.BlockSpec((tm, D_MODEL), row), pl.BlockSpec((tm, D_MODEL), row)],
        scratch_shapes=[pltpu.VMEM((2, 2, tm, D_MODEL), F32), pltpu.SemaphoreType.DMA((2,))],
    )
    return pl.pallas_call(
        functools.partial(_combine_ln_kernel, tm=tm),
        grid_spec=grid_spec,
        out_shape=[jax.ShapeDtypeStruct((s, D_MODEL), F32),
                   jax.ShapeDtypeStruct((s, D_MODEL), BF16)],
        compiler_params=_cparams(("arbitrary",)),
        name="combine_ln2",
    )(pos, ys, h, g, b)


def _dispatch_plan(meta, tm):
    t = meta.shape[0]
    eid = meta[:, 0:2].astype(jnp.int32).reshape(-1)
    gate = meta[:, 2:4].reshape(-1)
    onehot = (eid[:, None] == jnp.arange(N_EXPERTS, dtype=jnp.int32)[None, :]).astype(jnp.int32)
    csum = jnp.cumsum(onehot, axis=0)
    counts = csum[-1]
    padded = ((counts + tm - 1) // tm) * tm
    pend = jnp.cumsum(padded)
    poff = pend - padded
    off = jnp.cumsum(counts) - counts
    pos = jnp.sum(onehot * (csum + poff[None, :]), axis=1) - 1
    p_max = 2 * t + N_EXPERTS * tm
    nt_max = p_max // tm
    ntiles = (pend[-1] // tm).astype(jnp.int32)
    tile_ids = jnp.minimum(jnp.arange(nt_max, dtype=jnp.int32), ntiles - 1)
    tile_expert = jnp.sum((pend[None, :] // tm <= tile_ids[:, None]).astype(jnp.int32), axis=1)
    tile_expert = jnp.minimum(tile_expert, N_EXPERTS - 1).astype(jnp.int32)
    order = jnp.argsort(eid, stable=True).astype(jnp.int32)
    k = (jnp.arange(p_max, dtype=jnp.int32).reshape(nt_max, tm)
         - poff[tile_expert][:, None])
    valid = jnp.logical_and(k >= 0, k < counts[tile_expert][:, None]).reshape(-1)
    src = jnp.clip(off[tile_expert][:, None] + k, 0, 2 * t - 1).reshape(-1)
    pair = order[src]
    tok_of_row = jnp.where(valid, pair // 2, 0).astype(jnp.int32)
    gate_rows = jnp.where(valid, gate[pair], 0.0).reshape(p_max, 1)
    return tile_expert, ntiles.reshape(1), tok_of_row, gate_rows, pos.astype(jnp.int32)


def _pad_lanes(v, width=LANES):
    v = v.reshape(1, -1).astype(F32)
    return jnp.pad(v, ((0, 0), (0, width - v.shape[1])))


def _layer(l, h32, h16, pos_f, invf, p, stacked, tb, tm_moe):
    s = h32.shape[0]
    w16, w32 = _prep_w_in(stacked["w_in"], l)
    s16 = jnp.concatenate([jnp.full((1, FOX_W), FOX_HD ** -0.5 * LOG2E, F32),
                           jnp.ones((1, G16_W - FOX_W), F32)], axis=1)
    s32 = jnp.ones((1, G32_W), F32)
    tm_proj = min(s, 2048)
    g16 = _matmul(h16, w16, s16, BF16, tm_proj, 256, "in_proj_bf16")
    g32 = _matmul(h16, w32, s32, F32, tm_proj, 256, "in_proj_f32")

    tb_prep = min(s, 256)
    qaux, kaux, stat = _fox_prep(g32, g16, _pad_lanes(p["b_f"]), tb_prep)
    jmin = _fox_skip_table(stat, tb // tb_prep)
    out_a = _fox_attention(jmin, g16, qaux, kaux, tb)

    out_b = _lru_branch(g32, p["conv_w"], p["conv_b"].reshape(1, -1),
                        p["w_a"].astype(BF16), p["b_a"].reshape(1, -1),
                        p["w_i"].astype(BF16), p["b_i"].reshape(1, -1),
                        p["lru_lambda"].reshape(1, -1), min(s, 1024), min(s, 256))

    qk16 = _rope(g32, pos_f, invf, min(s, 512))
    lamv = jnp.concatenate([_pad_lanes(p["lam_q1"]), _pad_lanes(p["lam_k1"]),
                            _pad_lanes(p["lam_q2"]), _pad_lanes(p["lam_k2"])], axis=0)
    lam_init = 0.8 - 0.6 * math.exp(-0.3 * l)
    out_c = _diff_attention(qk16, g16, lamv, p["subln_g"].reshape(1, -1), lam_init, tb)

    w_o = p["w_o"].astype(BF16)
    h1_32, h1_16 = _oproj_ln(out_a, out_b, out_c, w_o[:FOX_W], w_o[FOX_W:FOX_W + LRU_W],
                             w_o[FOX_W + LRU_W:], h32, p["ln1_g"].reshape(1, -1),
                             p["ln1_b"].reshape(1, -1), min(s, 256))
    del h1_16

    w_rt = jnp.concatenate([p["w_group"], p["w_router"],
                            jnp.zeros((D_MODEL, LANES - N_GROUPS - N_EXPERTS), F32)], axis=1)
    w_rt_hi = w_rt.astype(BF16)
    w_rt_lo = (w_rt - w_rt_hi.astype(F32)).astype(BF16)
    b_rt = _pad_lanes(jnp.concatenate([p["b_group"], p["b_router"]]))
    meta = _router(h1_32, w_rt_hi, w_rt_lo, b_rt, min(s, 512))

    tile_expert, ntiles, tok_of_row, gate_rows, pos = _dispatch_plan(meta, tm_moe)
    ys = _moe_grouped(tile_expert + l * N_EXPERTS, ntiles, tok_of_row, h1_32, gate_rows,
                      stacked["w_gate"], stacked["w_up"], stacked["w_down"], tm_moe)
    return _combine_ln(pos, ys, h1_32, p["ln2_g"].reshape(1, -1), p["ln2_b"].reshape(1, -1),
                       min(s, 256))


def _forward(x, positions, params, tb, tm_moe):
    bsz, s, d = x.shape
    assert bsz == 1 and d == D_MODEL
    h32 = x.reshape(s, d)
    h16 = h32.astype(BF16)
    half = ROT_DIM // 2
    inv_freq = ROPE_THETA ** (-jnp.arange(half, dtype=F32) * 2.0 / ROT_DIM)
    dlane = jnp.arange(LANES) % DIFF_HD
    invf = jnp.where(dlane < ROT_DIM, inv_freq[dlane % half], 0.0).reshape(1, LANES).astype(F32)
    pos_f = jnp.broadcast_to(positions.astype(F32).reshape(s, 1), (s, LANES))
    big = ("w_in", "w_gate", "w_up", "w_down")
    stacked = {
        "w_in": params["w_in"],
        "w_gate": params["w_gate"].reshape(DEPTH * N_EXPERTS, D_MODEL, D_FF_EXPERT),
        "w_up": params["w_up"].reshape(DEPTH * N_EXPERTS, D_MODEL, D_FF_EXPERT),
        "w_down": params["w_down"].reshape(DEPTH * N_EXPERTS, D_FF_EXPERT, D_MODEL),
    }
    for l in range(DEPTH):
        p = {k: v[l] for k, v in params.items() if k not in big}
        h32, h16 = _layer(l, h32, h16, pos_f, invf, p, stacked, tb, tm_moe)
    return h32.reshape(bsz, s, d)


def kernel(x, positions, w_in, b_f, conv_w, conv_b, w_a, b_a, w_i, b_i, lru_lambda, lam_q1, lam_k1, lam_q2, lam_k2, subln_g, w_o, ln1_g, ln1_b, w_group, b_group, w_router, b_router, w_gate, w_up, w_down, ln2_g, ln2_b):
    params = dict(w_in=w_in, b_f=b_f, conv_w=conv_w, conv_b=conv_b, w_a=w_a, b_a=b_a, w_i=w_i,
                  b_i=b_i, lru_lambda=lru_lambda, lam_q1=lam_q1, lam_k1=lam_k1, lam_q2=lam_q2,
                  lam_k2=lam_k2, subln_g=subln_g, w_o=w_o, ln1_g=ln1_g, ln1_b=ln1_b,
                  w_group=w_group, b_group=b_group, w_router=w_router, b_router=b_router,
                  w_gate=w_gate, w_up=w_up, w_down=w_down, ln2_g=ln2_g, ln2_b=ln2_b)
    s = x.shape[1]
    return _forward(x, positions, params, tb=min(s, 512), tm_moe=min(s, 256))
```

```python
import functools
import math

import jax
import jax.numpy as jnp
from jax import lax
from jax.experimental import pallas as pl
from jax.experimental.pallas import tpu as pltpu

F32 = jnp.float32
BF16 = jnp.bfloat16

D_MODEL = 2048
DEPTH = 2
CHUNK = 64
FOX_HEADS = 6
FOX_HD = 128
FOX_W = FOX_HEADS * FOX_HD
LRU_W = 768
LRU_BLOCKS = 6
LRU_BW = LRU_W // LRU_BLOCKS
CONV_W = 4
LRU_C = 8.0
DIFF_HEADS = 4
DIFF_HD = 64
DIFF_VD = 2 * DIFF_HD
DIFF_W = DIFF_HEADS * DIFF_VD
ROT_DIM = DIFF_HD // 4
ROPE_THETA = 500000.0
N_GROUPS = 4
EXPERTS_PER_GROUP = 8
N_EXPERTS = N_GROUPS * EXPERTS_PER_GROUP
D_FF_EXPERT = 512
ALPHA = (2.0 * DEPTH) ** 0.25
LN_EPS = 1e-5

LANES = 128
NEG_BIG = -1e30
LOG2E = math.log2(math.e)
SKIP_BITS = 72.0
VMEM_LIMIT = 56 * 1024 * 1024

G16_W = 3 * FOX_W + DIFF_W
G32_W = 2 * DIFF_W + 2 * LRU_W + 2 * LANES
G16_QA, G16_KA, G16_VA, G16_VC = 0, 6, 12, 18
G32_XR, G32_GR, G32_FA = 8, 14, 20


def _cparams(sem, vmem=VMEM_LIMIT):
    return pltpu.CompilerParams(dimension_semantics=sem, vmem_limit_bytes=vmem)


def _mm_kernel(x_ref, w_ref, s_ref, o_ref):
    acc = jnp.dot(x_ref[...], w_ref[...], preferred_element_type=F32)
    o_ref[...] = (acc * s_ref[...]).astype(o_ref.dtype)


def _matmul(x, w, s, out_dtype, tm, tn, name):
    m, k = x.shape
    n = w.shape[1]
    return pl.pallas_call(
        _mm_kernel,
        grid=(m // tm, n // tn),
        in_specs=[pl.BlockSpec((tm, k), lambda i, j: (i, 0)),
                  pl.BlockSpec((k, tn), lambda i, j: (0, j)),
                  pl.BlockSpec((1, tn), lambda i, j: (0, j))],
        out_specs=pl.BlockSpec((tm, tn), lambda i, j: (i, j)),
        out_shape=jax.ShapeDtypeStruct((m, n), out_dtype),
        compiler_params=_cparams(("parallel", "parallel")),
        name=name,
    )(x, w, s)


_IN_QA = 0
_IN_KA = _IN_QA + FOX_W
_IN_VA = _IN_KA + FOX_W
_IN_FA = _IN_VA + FOX_W
_IN_XR = _IN_FA + FOX_HEADS
_IN_GR = _IN_XR + LRU_W
_IN_QC = _IN_GR + LRU_W
_IN_KC = _IN_QC + DIFF_W
_IN_VC = _IN_KC + DIFF_W
N_IN = _IN_VC + DIFF_W


def _wprep_kernel(w_ref, o16_ref, o32_ref):
    def put(o_ref, dst, src, width):
        o_ref[:, dst:dst + width] = w_ref[0, :, src:src + width].astype(BF16)

    put(o16_ref, 0, _IN_QA, FOX_W)
    put(o16_ref, FOX_W, _IN_KA, FOX_W)
    put(o16_ref, 2 * FOX_W, _IN_VA, FOX_W)
    put(o16_ref, 3 * FOX_W, _IN_VC, DIFF_W)
    put(o32_ref, 0, _IN_QC, DIFF_W)
    put(o32_ref, DIFF_W, _IN_KC, DIFF_W)
    put(o32_ref, G32_XR * LANES, _IN_XR, LRU_W)
    put(o32_ref, G32_GR * LANES, _IN_GR, LRU_W)
    fa0 = G32_FA * LANES
    o32_ref[:, fa0:fa0 + 2 * LANES] = jnp.zeros((o32_ref.shape[0], 2 * LANES), BF16)
    put(o32_ref, fa0, _IN_FA, FOX_HEADS)


def _prep_w_in(w_in, layer, tk=256):
    return pl.pallas_call(
        _wprep_kernel,
        grid=(D_MODEL // tk,),
        in_specs=[pl.BlockSpec((1, tk, N_IN), lambda i: (layer, i, 0))],
        out_specs=[pl.BlockSpec((tk, G16_W), lambda i: (i, 0)),
                   pl.BlockSpec((tk, G32_W), lambda i: (i, 0))],
        out_shape=[jax.ShapeDtypeStruct((D_MODEL, G16_W), BF16),
                   jax.ShapeDtypeStruct((D_MODEL, G32_W), BF16)],
        compiler_params=_cparams(("parallel",)),
        name="w_in_regroup",
    )(w_in)


def _split3(x):
    hi = x.astype(BF16)
    r1 = x - hi.astype(F32)
    mid = r1.astype(BF16)
    lo = (r1 - mid.astype(F32)).astype(BF16)
    return hi, mid, lo


def _fox_prep_kernel(fa_ref, bf_ref, q_ref, k_ref, qaux_ref, kaux_ref, stat_ref, carry_ref):
    @pl.when(pl.program_id(0) == 0)
    def _():
        carry_ref[...] = jnp.zeros_like(carry_ref)

    z = fa_ref[...] + bf_ref[...]
    lf = (jnp.minimum(z, 0.0) - jnp.log1p(jnp.exp(-jnp.abs(z)))) * LOG2E
    t = z.shape[0]
    row = lax.broadcasted_iota(jnp.int32, (t, t), 0)
    col = lax.broadcasted_iota(jnp.int32, (t, t), 1)
    tri = jnp.where(row >= col, 1.0, 0.0).astype(BF16)
    hi, mid, lo = _split3(lf)
    cs = (jnp.dot(tri, hi, preferred_element_type=F32)
          + jnp.dot(tri, mid, preferred_element_type=F32)
          + jnp.dot(tri, lo, preferred_element_type=F32)) + carry_ref[...]
    carry_ref[...] = cs[t - 1:t, :]

    lane = lax.broadcasted_iota(jnp.int32, (t, LANES), 1)
    lane1 = lax.broadcasted_iota(jnp.int32, (1, LANES), 1)
    qn = jnp.zeros((1, LANES), F32)
    kn = jnp.zeros((1, LANES), F32)
    for h in range(FOX_HEADS):
        c = jnp.broadcast_to(jnp.sum(jnp.where(lane == h, cs, 0.0), axis=1, keepdims=True),
                             (t, LANES))
        c_hi = c.astype(BF16).astype(F32)
        r1 = c - c_hi
        c_mid = r1.astype(BF16).astype(F32)
        c_lo = (r1 - c_mid).astype(BF16).astype(F32)
        kaux = jnp.where(lane == 0, c_hi, jnp.where(lane == 1, c_mid, jnp.where(
            lane == 2, c_lo, jnp.where(lane < 6, 1.0, 0.0))))
        qaux = jnp.where(lane < 3, -1.0, jnp.where(lane == 3, c_hi, jnp.where(
            lane == 4, c_mid, jnp.where(lane == 5, c_lo, 0.0))))
        kaux_ref[h] = kaux.astype(BF16)
        qaux_ref[h] = qaux.astype(BF16)
        qh = q_ref[:, h * FOX_HD:(h + 1) * FOX_HD].astype(F32)
        kh = k_ref[:, h * FOX_HD:(h + 1) * FOX_HD].astype(F32)
        q2 = jnp.max(jnp.sum(qh * qh, axis=1, keepdims=True), axis=0, keepdims=True)
        k2 = jnp.max(jnp.sum(kh * kh, axis=1, keepdims=True), axis=0, keepdims=True)
        qn = jnp.where(lane1 == h, q2, qn)
        kn = jnp.where(lane1 == h, k2, kn)
    stat_ref[0] = jnp.concatenate([qn, kn, cs[0:1, :], cs[t - 1:t, :],
                                   jnp.zeros((4, LANES), F32)], axis=0)


def _fox_prep(g32, g16, bf_pad, tb):
    s = g32.shape[0]
    qk_blocks = FOX_W // LANES
    return pl.pallas_call(
        _fox_prep_kernel,
        grid=(s // tb,),
        in_specs=[pl.BlockSpec((tb, LANES), lambda i: (i, G32_FA)),
                  pl.BlockSpec((1, LANES), lambda i: (0, 0)),
                  pl.BlockSpec((tb, FOX_W), lambda i: (i, G16_QA // qk_blocks)),
                  pl.BlockSpec((tb, FOX_W), lambda i: (i, G16_KA // qk_blocks))],
        out_specs=[pl.BlockSpec((FOX_HEADS, tb, LANES), lambda i: (0, i, 0)),
                   pl.BlockSpec((FOX_HEADS, tb, LANES), lambda i: (0, i, 0)),
                   pl.BlockSpec((1, 8, LANES), lambda i: (i, 0, 0))],
        out_shape=[jax.ShapeDtypeStruct((FOX_HEADS, s, LANES), BF16),
                   jax.ShapeDtypeStruct((FOX_HEADS, s, LANES), BF16),
                   jax.ShapeDtypeStruct((s // tb, 8, LANES), F32)],
        scratch_shapes=[pltpu.VMEM((1, LANES), F32)],
        compiler_params=_cparams(("arbitrary",)),
        name="fox_prep",
    )(g32, bf_pad, g16, g16)


def _fox_skip_table(stat, per_block):
    nb = stat.shape[0] // per_block
    st = stat.reshape(nb, per_block, 8, LANES)
    qn = jnp.sqrt(jnp.max(st[:, :, 0, :FOX_HEADS], axis=1))
    kn = jnp.sqrt(jnp.max(st[:, :, 1, :FOX_HEADS], axis=1))
    c_first = st[:, 0, 2, :FOX_HEADS]
    c_last = st[:, per_block - 1, 3, :FOX_HEADS]
    kmax = jnp.max(kn, axis=0, keepdims=True)
    bound = 1.01 * qn * (kmax + kn) + c_first
    skip = (bound[:, None, :] - c_last[None, :, :]) < -SKIP_BITS
    jj = jnp.arange(nb)
    skip = jnp.logical_and(skip, (jj[None, :] < jj[:, None])[:, :, None])
    lead = jnp.cumsum(1 - skip.astype(jnp.int32), axis=1) == 0
    return jnp.sum(lead.astype(jnp.int32), axis=1).T.reshape(-1)


def _rope_kernel(x_ref, pos_ref, invf_ref, o_ref):
    ang = pos_ref[...] * invf_ref[...]
    c = jnp.cos(ang)
    s = jnp.sin(ang)
    half = ROT_DIM // 2
    d = lax.broadcasted_iota(jnp.int32, (1, LANES), 1) % DIFF_HD
    sa = jnp.where(d >= half, s, 0.0)
    sb = jnp.where(d < half, -s, 0.0)
    nblk = x_ref.shape[1] // LANES
    for j in range(nblk):
        t = x_ref[:, j * LANES:(j + 1) * LANES]
        r = t * c + pltpu.roll(t, half, 1) * sa + pltpu.roll(t, LANES - half, 1) * sb
        if j < nblk // 2:
            r = r * (DIFF_HD ** -0.5 * LOG2E)
        o_ref[:, j * LANES:(j + 1) * LANES] = r.astype(o_ref.dtype)


def _rope(g32, pos_f, invf, tb):
    s = g32.shape[0]
    w = 2 * DIFF_W
    return pl.pallas_call(
        _rope_kernel,
        grid=(s // tb,),
        in_specs=[pl.BlockSpec((tb, w), lambda i: (i, 0)),
                  pl.BlockSpec((tb, LANES), lambda i: (i, 0)),
                  pl.BlockSpec((1, LANES), lambda i: (0, 0))],
        out_specs=pl.BlockSpec((tb, w), lambda i: (i, 0)),
        out_shape=jax.ShapeDtypeStruct((s, w), BF16),
        compiler_params=_cparams(("parallel",)),
        name="diff_rope",
    )(g32, pos_f, invf)


def _flash_step(v, s, m_ref, l_ref, acc_ref):
    m_prev = m_ref[...]
    m_new = jnp.maximum(m_prev, jnp.max(s, axis=0, keepdims=True))
    a = jnp.exp2(m_prev - m_new)
    p = jnp.exp2(s - m_new)
    l_ref[...] = a * l_ref[...] + jnp.sum(p, axis=0, keepdims=True)
    pv = lax.dot_general(v, p.astype(v.dtype), (((0,), (0,)), ((), ())),
                         preferred_element_type=F32)
    acc_ref[...] = a * acc_ref[...] + pv
    m_ref[...] = m_new


def _flash_pipeline(j0, i, qk, sm):
    npairs = (i - j0) // 2
    qk(j0, 0)

    def body(p, carry):
        j = j0 + 2 * p
        qk(j + 1, 1)
        sm(j, 0, False)
        qk(j + 2, 0)
        sm(j + 1, 1, False)
        return carry

    lax.fori_loop(0, npairs, body, 0)
    jn = j0 + 2 * npairs

    @pl.when(jn == i)
    def _():
        sm(i, 0, True)

    @pl.when(jn != i)
    def _():
        qk(i, 1)
        sm(jn, 0, False)
        sm(i, 1, True)


def _init_state(m_ref, l_ref, acc_ref):
    m_ref[...] = jnp.full(m_ref.shape, NEG_BIG, F32)
    l_ref[...] = jnp.zeros(l_ref.shape, F32)
    acc_ref[...] = jnp.zeros(acc_ref.shape, F32)


def _fox_kernel(jmin_ref, q_ref, qaux_ref, k_ref, kaux_ref, v_ref, o_ref,
                m_ref, l_ref, acc_ref, sa_ref, sb_ref, *, tb):
    h = pl.program_id(0)
    i = pl.program_id(1)
    qf = jnp.concatenate([q_ref[...], qaux_ref[0]], axis=1)
    _init_state(m_ref, l_ref, acc_ref)
    s_refs = (sa_ref, sb_ref)

    def qk(j, slot):
        k0 = pl.multiple_of(j * tb, tb)
        kf = jnp.concatenate([k_ref[pl.ds(k0, tb), :], kaux_ref[0, pl.ds(k0, tb), :]], axis=1)
        s_refs[slot][...] = lax.dot_general(kf, qf, (((1,), (1,)), ((), ())),
                                            preferred_element_type=F32)

    def sm(j, slot, masked):
        s = s_refs[slot][...]
        if masked:
            kk = lax.broadcasted_iota(jnp.int32, (tb, tb), 0)
            qq = lax.broadcasted_iota(jnp.int32, (tb, tb), 1)
            s = jnp.where(kk <= qq, s, NEG_BIG)
        v = v_ref[pl.ds(pl.multiple_of(j * tb, tb), tb), :]
        _flash_step(v, s, m_ref, l_ref, acc_ref)

    _flash_pipeline(jmin_ref[h * pl.num_programs(1) + i], i, qk, sm)
    o = acc_ref[...] / l_ref[...]
    o_ref[...] = o.T.astype(o_ref.dtype)


def _fox_attention(jmin, g16, qaux, kaux, tb):
    s = g16.shape[0]
    grid_spec = pltpu.PrefetchScalarGridSpec(
        num_scalar_prefetch=1,
        grid=(FOX_HEADS, s // tb),
        in_specs=[pl.BlockSpec((tb, LANES), lambda h, i, jm: (i, G16_QA + h)),
                  pl.BlockSpec((1, tb, LANES), lambda h, i, jm: (h, i, 0)),
                  pl.BlockSpec((s, LANES), lambda h, i, jm: (0, G16_KA + h)),
                  pl.BlockSpec((1, s, LANES), lambda h, i, jm: (h, 0, 0)),
                  pl.BlockSpec((s, LANES), lambda h, i, jm: (0, G16_VA + h))],
        out_specs=pl.BlockSpec((tb, LANES), lambda h, i, jm: (i, h)),
        scratch_shapes=[pltpu.VMEM((1, tb), F32), pltpu.VMEM((1, tb), F32),
                        pltpu.VMEM((FOX_HD, tb), F32),
                        pltpu.VMEM((tb, tb), F32), pltpu.VMEM((tb, tb), F32)],
    )
    return pl.pallas_call(
        functools.partial(_fox_kernel, tb=tb),
        grid_spec=grid_spec,
        out_shape=jax.ShapeDtypeStruct((s, FOX_W), BF16),
        compiler_params=_cparams(("parallel", "parallel")),
        name="fox_attention",
    )(jmin, g16, qaux, g16, kaux, g16)


def _diff_kernel(q_ref, k_ref, v_ref, lamv_ref, g_ref, o_ref, m_ref, l_ref, acc_ref,
                 sa_ref, sb_ref, *, tb, lam_init):
    i = pl.program_id(1)
    q = q_ref[...].astype(F32)
    lane = lax.broadcasted_iota(jnp.int32, (tb, LANES), 1)
    qq2 = jnp.concatenate([jnp.where(lane < DIFF_HD, q, 0.0),
                           jnp.where(lane >= DIFF_HD, q, 0.0)], axis=0).astype(BF16)
    _init_state(m_ref, l_ref, acc_ref)
    s_refs = (sa_ref, sb_ref)

    def qk(j, slot):
        k = k_ref[pl.ds(pl.multiple_of(j * tb, tb), tb), :]
        s_refs[slot][...] = lax.dot_general(k, qq2, (((1,), (1,)), ((), ())),
                                            preferred_element_type=F32)

    def sm(j, slot, masked):
        s = s_refs[slot][...]
        if masked:
            kc = lax.broadcasted_iota(jnp.int32, (tb, 2 * tb), 0) // CHUNK
            qc = (lax.broadcasted_iota(jnp.int32, (tb, 2 * tb), 1) % tb) // CHUNK
            s = jnp.where(kc <= qc, s, NEG_BIG)
        v = v_ref[pl.ds(pl.multiple_of(j * tb, tb), tb), :]
        _flash_step(v, s, m_ref, l_ref, acc_ref)

    _flash_pipeline(0, i, qk, sm)

    lv = lamv_ref[...]
    lam = (jnp.exp(jnp.sum(lv[0:1] * lv[1:2], axis=1, keepdims=True))
           - jnp.exp(jnp.sum(lv[2:3] * lv[3:4], axis=1, keepdims=True)) + lam_init)
    on = acc_ref[...] / l_ref[...]
    o = (on[:, :tb] - lam * on[:, tb:]).T
    ms = jnp.mean(o * o, axis=-1, keepdims=True)
    o = o * lax.rsqrt(ms + LN_EPS) * g_ref[...] * (1.0 - lam_init)
    o_ref[...] = o.astype(o_ref.dtype)


def _diff_attention(qk16, g16, lamv, subg, lam_init, tb):
    s = g16.shape[0]
    return pl.pallas_call(
        functools.partial(_diff_kernel, tb=tb, lam_init=lam_init),
        grid=(DIFF_HEADS, s // tb),
        in_specs=[pl.BlockSpec((tb, LANES), lambda h, i: (i, h)),
                  pl.BlockSpec((s, LANES), lambda h, i: (0, DIFF_HEADS + h)),
                  pl.BlockSpec((s, LANES), lambda h, i: (0, G16_VC + h)),
                  pl.BlockSpec((4, LANES), lambda h, i: (0, 0)),
                  pl.BlockSpec((1, LANES), lambda h, i: (0, 0))],
        out_specs=pl.BlockSpec((tb, LANES), lambda h, i: (i, h)),
        out_shape=jax.ShapeDtypeStruct((s, DIFF_W), BF16),
        scratch_shapes=[pltpu.VMEM((1, 2 * tb), F32), pltpu.VMEM((1, 2 * tb), F32),
                        pltpu.VMEM((DIFF_VD, 2 * tb), F32),
                        pltpu.VMEM((tb, 2 * tb), F32), pltpu.VMEM((tb, 2 * tb), F32)],
        compiler_params=_cparams(("parallel", "parallel")),
        name="diff_attention",
    )(qk16, qk16, g16, lamv, subg)


def _shift_rows(x, d, fill):
    rows = lax.broadcasted_iota(jnp.int32, x.shape, 0)
    return jnp.where(rows >= d, pltpu.roll(x, d, 0), fill)


def _lru_kernel(xr_ref, gr_ref, cw_ref, cb_ref, wa_ref, ba_ref, wi_ref, bi_ref, lam_ref,
                o_ref, halo_ref, h_ref, *, tb, sub):
    @pl.when(pl.program_id(1) == 0)
    def _():
        halo_ref[...] = jnp.zeros_like(halo_ref)
        h_ref[...] = jnp.zeros_like(h_ref)

    x = xr_ref[...]
    xe = jnp.concatenate([halo_ref[...], x], axis=0)
    halo_ref[...] = x[tb - 8:tb, :]
    cw = cw_ref[...]
    xc = cb_ref[...] + cw[CONV_W - 1:CONV_W, :] * x
    for j in range(CONV_W - 1):
        sh = CONV_W - 1 - j
        xc = xc + cw[j:j + 1, :] * pltpu.roll(xe, sh, 0)[8:8 + tb, :]

    xcb = xc.astype(BF16)
    r = jax.nn.sigmoid(jnp.dot(xcb, wa_ref[0], preferred_element_type=F32) + ba_ref[...])
    ig = jax.nn.sigmoid(jnp.dot(xcb, wi_ref[0], preferred_element_type=F32) + bi_ref[...])
    lam = lam_ref[...]
    ls = jnp.minimum(lam, 0.0) - jnp.log1p(jnp.exp(-jnp.abs(lam)))
    log_a = LRU_C * r * ls
    a = jnp.exp(log_a)
    z2 = 2.0 * log_a
    e2 = jnp.exp(z2)
    small = jnp.where(e2 == 1.0, -z2, (1.0 - e2) * z2 / jnp.log(e2))
    neg_expm1 = jnp.where(z2 < -1.0, 1.0 - e2, small)
    u = jnp.sqrt(neg_expm1) * (ig * xc)

    h = h_ref[...]
    for c in range(tb // sub):
        ac = a[c * sub:(c + 1) * sub, :]
        uc = u[c * sub:(c + 1) * sub, :]
        d = 1
        while d < sub:
            uc = ac * _shift_rows(uc, d, 0.0) + uc
            ac = ac * _shift_rows(ac, d, 1.0)
            d *= 2
        hc = uc + ac * h
        h = hc[sub - 1:sub, :]
        g = gr_ref[c * sub:(c + 1) * sub, :]
        gelu = 0.5 * g * (1.0 + jnp.tanh(math.sqrt(2.0 / math.pi) * (g + 0.044715 * (g * g * g))))
        o_ref[c * sub:(c + 1) * sub, :] = (gelu * hc).astype(o_ref.dtype)
    h_ref[...] = h


def _lru_branch(g32, cw, cb, wa, ba, wi, bi, lam, tb, sub):
    s = g32.shape[0]
    vec = lambda c, i: (0, c)
    return pl.pallas_call(
        functools.partial(_lru_kernel, tb=tb, sub=sub),
        grid=(LRU_BLOCKS, s // tb),
        in_specs=[pl.BlockSpec((tb, LANES), lambda c, i: (i, G32_XR + c)),
                  pl.BlockSpec((tb, LANES), lambda c, i: (i, G32_GR + c)),
                  pl.BlockSpec((CONV_W, LANES), vec),
                  pl.BlockSpec((1, LANES), vec),
                  pl.BlockSpec((1, LRU_BW, LRU_BW), lambda c, i: (c, 0, 0)),
                  pl.BlockSpec((1, LANES), vec),
                  pl.BlockSpec((1, LRU_BW, LRU_BW), lambda c, i: (c, 0, 0)),
                  pl.BlockSpec((1, LANES), vec),
                  pl.BlockSpec((1, LANES), vec)],
        out_specs=pl.BlockSpec((tb, LANES), lambda c, i: (i, c)),
        out_shape=jax.ShapeDtypeStruct((s, LRU_W), BF16),
        scratch_shapes=[pltpu.VMEM((8, LANES), F32), pltpu.VMEM((1, LANES), F32)],
        compiler_params=_cparams(("parallel", "arbitrary")),
        name="rg_lru",
    )(g32, g32, cw, cb, wa, ba, wi, bi, lam)


def _layer_norm_rows(y, g, b):
    mu = jnp.mean(y, axis=-1, keepdims=True)
    yc = y - mu
    var = jnp.mean(yc * yc, axis=-1, keepdims=True)
    return yc * lax.rsqrt(var + LN_EPS) * g + b


def _oproj_ln_kernel(xa_ref, xb_ref, xc_ref, wa_ref, wb_ref, wc_ref, h_ref, g_ref, b_ref,
                     o32_ref, o16_ref):
    mix = (jnp.dot(xa_ref[...], wa_ref[...], preferred_element_type=F32)
           + jnp.dot(xb_ref[...], wb_ref[...], preferred_element_type=F32)
           + jnp.dot(xc_ref[...], wc_ref[...], preferred_element_type=F32))
    o = _layer_norm_rows(ALPHA * h_ref[...] + mix, g_ref[...], b_ref[...])
    o32_ref[...] = o
    o16_ref[...] = o.astype(BF16)


def _oproj_ln(xa, xb, xc, wa, wb, wc, h, g, b, tm):
    s = h.shape[0]
    row = lambda i: (i, 0)
    fixed = lambda i: (0, 0)
    return pl.pallas_call(
        _oproj_ln_kernel,
        grid=(s // tm,),
        in_specs=[pl.BlockSpec((tm, FOX_W), row), pl.BlockSpec((tm, LRU_W), row),
                  pl.BlockSpec((tm, DIFF_W), row),
                  pl.BlockSpec((FOX_W, D_MODEL), fixed), pl.BlockSpec((LRU_W, D_MODEL), fixed),
                  pl.BlockSpec((DIFF_W, D_MODEL), fixed),
                  pl.BlockSpec((tm, D_MODEL), row),
                  pl.BlockSpec((1, D_MODEL), fixed), pl.BlockSpec((1, D_MODEL), fixed)],
        out_specs=[pl.BlockSpec((tm, D_MODEL), row), pl.BlockSpec((tm, D_MODEL), row)],
        out_shape=[jax.ShapeDtypeStruct((s, D_MODEL), F32),
                   jax.ShapeDtypeStruct((s, D_MODEL), BF16)],
        compiler_params=_cparams(("parallel",)),
        name="oproj_ln1",
    )(xa, xb, xc, wa, wb, wc, h, g, b)


def _router_kernel(h_ref, whi_ref, wlo_ref, b_ref, o_ref):
    h = h_ref[...]
    hi = h.astype(BF16)
    lo = (h - hi.astype(F32)).astype(BF16)
    whi = whi_ref[...]
    logits = (jnp.dot(hi, whi, preferred_element_type=F32)
              + jnp.dot(hi, wlo_ref[...], preferred_element_type=F32)
              + jnp.dot(lo, whi, preferred_element_type=F32)) + b_ref[...]
    lane = lax.broadcasted_iota(jnp.int32, logits.shape, 1)
    big = jnp.int32(1 << 20)

    def first_lane(cond):
        return jnp.min(jnp.where(cond, lane, big), axis=1, keepdims=True)

    gm = lane < N_GROUPS
    gl = jnp.where(gm, logits, -jnp.inf)
    gmax = jnp.max(gl, axis=1, keepdims=True)
    gexp = jnp.where(gm, jnp.exp(logits - gmax), 0.0)
    gprob = gexp / jnp.sum(gexp, axis=1, keepdims=True)
    gidx = first_lane(gl == gmax)
    g_weight = jnp.sum(jnp.where(lane == gidx, gprob, 0.0), axis=1, keepdims=True)

    e0 = N_GROUPS + EXPERTS_PER_GROUP * gidx
    em = jnp.logical_and(lane >= e0, lane < e0 + EXPERTS_PER_GROUP)
    el = jnp.where(em, logits, -jnp.inf)
    emax = jnp.max(el, axis=1, keepdims=True)
    eexp = jnp.where(em, jnp.exp(logits - emax), 0.0)
    eprob = jnp.where(em, eexp / jnp.sum(eexp, axis=1, keepdims=True), -1.0)
    p1 = jnp.max(eprob, axis=1, keepdims=True)
    i1 = first_lane(eprob == p1)
    eprob2 = jnp.where(lane == i1, -1.0, eprob)
    p2 = jnp.max(eprob2, axis=1, keepdims=True)
    i2 = first_lane(eprob2 == p2)
    den = p1 + p2
    gate1 = g_weight * (p1 / den)
    gate2 = g_weight * (p2 / den)
    id1 = (i1 - N_GROUPS).astype(F32)
    id2 = (i2 - N_GROUPS).astype(F32)
    o_ref[...] = jnp.where(lane == 0, id1,
                           jnp.where(lane == 1, id2,
                                     jnp.where(lane == 2, gate1,
                                               jnp.where(lane == 3, gate2, 0.0))))


def _router(h, whi, wlo, bias, tm):
    s = h.shape[0]
    fixed = lambda i: (0, 0)
    return pl.pallas_call(
        _router_kernel,
        grid=(s // tm,),
        in_specs=[pl.BlockSpec((tm, D_MODEL), lambda i: (i, 0)),
                  pl.BlockSpec((D_MODEL, LANES), fixed), pl.BlockSpec((D_MODEL, LANES), fixed),
                  pl.BlockSpec((1, LANES), fixed)],
        out_specs=pl.BlockSpec((tm, LANES), lambda i: (i, 0)),
        out_shape=jax.ShapeDtypeStruct((s, LANES), F32),
        compiler_params=_cparams(("parallel",)),
        name="moe_router",
    )(h, whi, wlo, bias)


def _row_copy(src_ref, src_row, dst_ref, dst_row, sem):
    return pltpu.make_async_copy(src_ref.at[pl.ds(src_row, 1), :],
                                 dst_ref.at[pl.ds(dst_row, 1), :], sem)


def _wait_rows(buf_ref, sem):
    pltpu.make_async_copy(buf_ref, buf_ref, sem).wait()


def _moe_kernel(te_ref, nt_ref, tok_ref, dst_ref, h_hbm, gate_ref, wg_ref, wu_ref, wd_ref,
                y_hbm, xbuf, ybuf, gsem, ssem, *, tm):
    del te_ref
    t = pl.program_id(0)
    nt = nt_ref[0]

    @pl.when(t == 0)
    def _():
        ybuf[...] = jnp.zeros_like(ybuf)
        n_spare = y_hbm.shape[0] // tm - 3
        for k in range(3):
            fill = pltpu.make_async_copy(ybuf.at[0], y_hbm.at[pl.ds((n_spare + k) * tm, tm), :],
                                         ssem.at[0])
            fill.start()
            fill.wait()

        def body(r, carry):
            _row_copy(h_hbm, tok_ref[r], xbuf.at[0], r, gsem.at[0]).start()
            return carry
        lax.fori_loop(0, tm, body, 0, unroll=8)

    def tile(slot):
        other = 1 - slot
        _wait_rows(xbuf.at[slot], gsem.at[slot])
        x = xbuf[slot].astype(BF16)
        nxt = jnp.minimum(t + 1, nt - 1) * tm
        for r in range(tm):
            _row_copy(h_hbm, tok_ref[nxt + r], xbuf.at[other], r, gsem.at[other]).start()
            _row_copy(ybuf.at[other], r, y_hbm, dst_ref[t * tm + r], ssem.at[other]).start()
        hg = jnp.dot(x, wg_ref[0].astype(BF16), preferred_element_type=F32)
        hu = jnp.dot(x, wu_ref[0].astype(BF16), preferred_element_type=F32)
        act = (hg * jax.nn.sigmoid(hg)) * hu * gate_ref[...]
        y = jnp.dot(act.astype(BF16), wd_ref[0].astype(BF16), preferred_element_type=F32)

        @pl.when(t >= 1)
        def _():
            _wait_rows(ybuf.at[slot], ssem.at[slot])
        ybuf[slot] = y

        @pl.when(t == nt - 1)
        def _():
            def body(r, carry):
                _row_copy(ybuf.at[slot], r, y_hbm, dst_ref[(t + 1) * tm + r], ssem.at[slot]).start()
                return carry
            lax.fori_loop(0, tm, body, 0, unroll=8)
            _wait_rows(ybuf.at[other], ssem.at[other])
            _wait_rows(ybuf.at[slot], ssem.at[slot])
            _wait_rows(xbuf.at[other], gsem.at[other])

    for slot in range(2):
        @pl.when(jnp.logical_and(t < nt, t % 2 == slot))
        def _(slot=slot):
            tile(slot)


def _moe_grouped(tile_expert, ntiles, tok_of_row, dst_of_row, h, gate_rows, wg, wu, wd, tm):
    n_tok = h.shape[0]
    nt_max = tok_of_row.shape[0] // tm
    imap = lambda t, te, nt, tok, dst: (te[t], 0, 0)
    grid_spec = pltpu.PrefetchScalarGridSpec(
        num_scalar_prefetch=4,
        grid=(nt_max,),
        in_specs=[pl.BlockSpec(memory_space=pl.ANY),
                  pl.BlockSpec((tm, 1), lambda t, te, nt, tok, dst: (t, 0)),
                  pl.BlockSpec((1, D_MODEL, D_FF_EXPERT), imap),
                  pl.BlockSpec((1, D_MODEL, D_FF_EXPERT), imap),
                  pl.BlockSpec((1, D_FF_EXPERT, D_MODEL), imap)],
        out_specs=pl.BlockSpec(memory_space=pl.ANY),
        scratch_shapes=[pltpu.VMEM((2, tm, D_MODEL), F32), pltpu.VMEM((2, tm, D_MODEL), F32),
                        pltpu.SemaphoreType.DMA((2,)), pltpu.SemaphoreType.DMA((2,))],
    )
    return pl.pallas_call(
        functools.partial(_moe_kernel, tm=tm),
        grid_spec=grid_spec,
        out_shape=jax.ShapeDtypeStruct((2 * n_tok + 3 * tm, D_MODEL), F32),
        compiler_params=_cparams(("arbitrary",)),
        name="moe_experts",
    )(tile_expert, ntiles, tok_of_row, dst_of_row, h, gate_rows, wg, wu, wd)


def _combine_ln_kernel(y0_ref, y1_ref, h_ref, g_ref, b_ref, o32_ref, o16_ref):
    y = ALPHA * h_ref[...] + (y0_ref[...] + y1_ref[...])
    o = _layer_norm_rows(y, g_ref[...], b_ref[...])
    o32_ref[...] = o
    o16_ref[...] = o.astype(BF16)


def _combine_ln(y2, h, g, b, tm):
    s = h.shape[0]
    row = lambda t: (t, 0)
    fixed = lambda t: (0, 0)
    return pl.pallas_call(
        _combine_ln_kernel,
        grid=(s // tm,),
        in_specs=[pl.BlockSpec((tm, D_MODEL), row),
                  pl.BlockSpec((tm, D_MODEL), lambda t: (t + s // tm, 0)),
                  pl.BlockSpec((tm, D_MODEL), row),
                  pl.BlockSpec((1, D_MODEL), fixed), pl.BlockSpec((1, D_MODEL), fixed)],
        out_specs=[pl.BlockSpec((tm, D_MODEL), row), pl.BlockSpec((tm, D_MODEL), row)],
        out_shape=[jax.ShapeDtypeStruct((s, D_MODEL), F32),
                   jax.ShapeDtypeStruct((s, D_MODEL), BF16)],
        compiler_params=_cparams(("parallel",)),
        name="combine_ln2",
    )(y2, y2, h, g, b)


def _dispatch_plan(meta, tm):
    t = meta.shape[0]
    eid = meta[:, 0:2].astype(jnp.int32).reshape(-1)
    gate = meta[:, 2:4].reshape(-1)
    onehot = (eid[:, None] == jnp.arange(N_EXPERTS, dtype=jnp.int32)[None, :]).astype(jnp.int32)
    counts = jnp.sum(onehot, axis=0)
    padded = ((counts + tm - 1) // tm) * tm
    pend = jnp.cumsum(padded)
    poff = pend - padded
    off = jnp.cumsum(counts) - counts
    p_max = 2 * t + N_EXPERTS * tm
    nt_max = p_max // tm
    ntiles = (pend[-1] // tm).astype(jnp.int32)
    tile_ids = jnp.minimum(jnp.arange(nt_max, dtype=jnp.int32), ntiles - 1)
    tile_expert = jnp.sum((pend[None, :] // tm <= tile_ids[:, None]).astype(jnp.int32), axis=1)
    tile_expert = jnp.minimum(tile_expert, N_EXPERTS - 1).astype(jnp.int32)
    order = jnp.argsort(eid, stable=True).astype(jnp.int32)
    k = (jnp.arange(p_max, dtype=jnp.int32).reshape(nt_max, tm)
         - poff[tile_expert][:, None])
    valid = jnp.logical_and(k >= 0, k < counts[tile_expert][:, None]).reshape(-1)
    src = jnp.clip(off[tile_expert][:, None] + k, 0, 2 * t - 1).reshape(-1)
    pair = order[src]
    tok_of_row = jnp.where(valid, pair // 2, 0).astype(jnp.int32)
    gate_rows = jnp.where(valid, gate[pair], 0.0).reshape(p_max, 1)
    r_in_tile = jnp.arange(p_max, dtype=jnp.int32) % tm
    spare = 2 * t + ((jnp.arange(p_max, dtype=jnp.int32) // tm) % 2) * tm + r_in_tile
    dst = jnp.where(valid, (pair % 2) * t + pair // 2, spare)
    fake = 2 * t + 2 * tm + jnp.arange(tm, dtype=jnp.int32)
    dst_of_row = jnp.concatenate([fake, dst]).astype(jnp.int32)
    return tile_expert, ntiles.reshape(1), tok_of_row, dst_of_row, gate_rows


def _pad_lanes(v, width=LANES):
    v = v.reshape(1, -1).astype(F32)
    return jnp.pad(v, ((0, 0), (0, width - v.shape[1])))


def _layer(l, h32, h16, pos_f, invf, p, stacked, tb, tm_moe):
    s = h32.shape[0]
    w16, w32 = _prep_w_in(stacked["w_in"], l)
    s16 = jnp.concatenate([jnp.full((1, FOX_W), FOX_HD ** -0.5 * LOG2E, F32),
                           jnp.ones((1, G16_W - FOX_W), F32)], axis=1)
    s32 = jnp.ones((1, G32_W), F32)
    tm_proj = min(s, 2048)
    g16 = _matmul(h16, w16, s16, BF16, tm_proj, 256, "in_proj_bf16")
    g32 = _matmul(h16, w32, s32, F32, tm_proj, 256, "in_proj_f32")

    tb_prep = min(s, 256)
    qaux, kaux, stat = _fox_prep(g32, g16, _pad_lanes(p["b_f"]), tb_prep)
    jmin = _fox_skip_table(stat, tb // tb_prep)
    out_a = _fox_attention(jmin, g16, qaux, kaux, tb)

    out_b = _lru_branch(g32, p["conv_w"], p["conv_b"].reshape(1, -1),
                        p["w_a"].astype(BF16), p["b_a"].reshape(1, -1),
                        p["w_i"].astype(BF16), p["b_i"].reshape(1, -1),
                        p["lru_lambda"].reshape(1, -1), min(s, 1024), min(s, 256))

    qk16 = _rope(g32, pos_f, invf, min(s, 512))
    lamv = jnp.concatenate([_pad_lanes(p["lam_q1"]), _pad_lanes(p["lam_k1"]),
                            _pad_lanes(p["lam_q2"]), _pad_lanes(p["lam_k2"])], axis=0)
    lam_init = 0.8 - 0.6 * math.exp(-0.3 * l)
    out_c = _diff_attention(qk16, g16, lamv, p["subln_g"].reshape(1, -1), lam_init, tb)

    w_o = p["w_o"].astype(BF16)
    h1_32, h1_16 = _oproj_ln(out_a, out_b, out_c, w_o[:FOX_W], w_o[FOX_W:FOX_W + LRU_W],
                             w_o[FOX_W + LRU_W:], h32, p["ln1_g"].reshape(1, -1),
                             p["ln1_b"].reshape(1, -1), min(s, 256))
    del h1_16

    w_rt = jnp.concatenate([p["w_group"], p["w_router"],
                            jnp.zeros((D_MODEL, LANES - N_GROUPS - N_EXPERTS), F32)], axis=1)
    w_rt_hi = w_rt.astype(BF16)
    w_rt_lo = (w_rt - w_rt_hi.astype(F32)).astype(BF16)
    b_rt = _pad_lanes(jnp.concatenate([p["b_group"], p["b_router"]]))
    meta = _router(h1_32, w_rt_hi, w_rt_lo, b_rt, min(s, 512))

    tile_expert, ntiles, tok_of_row, dst_of_row, gate_rows = _dispatch_plan(meta, tm_moe)
    y2 = _moe_grouped(tile_expert + l * N_EXPERTS, ntiles, tok_of_row, dst_of_row, h1_32,
                      gate_rows, stacked["w_gate"], stacked["w_up"], stacked["w_down"], tm_moe)
    return _combine_ln(y2, h1_32, p["ln2_g"].reshape(1, -1), p["ln2_b"].reshape(1, -1),
                       min(s, 256))


def _forward(x, positions, params, tb, tm_moe):
    bsz, s, d = x.shape
    assert bsz == 1 and d == D_MODEL
    h32 = x.reshape(s, d)
    h16 = h32.astype(BF16)
    half = ROT_DIM // 2
    inv_freq = ROPE_THETA ** (-jnp.arange(half, dtype=F32) * 2.0 / ROT_DIM)
    dlane = jnp.arange(LANES) % DIFF_HD
    invf = jnp.where(dlane < ROT_DIM, inv_freq[dlane % half], 0.0).reshape(1, LANES).astype(F32)
    pos_f = jnp.broadcast_to(positions.astype(F32).reshape(s, 1), (s, LANES))
    big = ("w_in", "w_gate", "w_up", "w_down")
    stacked = {
        "w_in": params["w_in"],
        "w_gate": params["w_gate"].reshape(DEPTH * N_EXPERTS, D_MODEL, D_FF_EXPERT),
        "w_up": params["w_up"].reshape(DEPTH * N_EXPERTS, D_MODEL, D_FF_EXPERT),
        "w_down": params["w_down"].reshape(DEPTH * N_EXPERTS, D_FF_EXPERT, D_MODEL),
    }
    for l in range(DEPTH):
        p = {k: v[l] for k, v in params.items() if k not in big}
        h32, h16 = _layer(l, h32, h16, pos_f, invf, p, stacked, tb, tm_moe)
    return h32.reshape(bsz, s, d)


def kernel(x, positions, w_in, b_f, conv_w, conv_b, w_a, b_a, w_i, b_i, lru_lambda, lam_q1, lam_k1, lam_q2, lam_k2, subln_g, w_o, ln1_g, ln1_b, w_group, b_group, w_router, b_router, w_gate, w_up, w_down, ln2_g, ln2_b):
    params = dict(w_in=w_in, b_f=b_f, conv_w=conv_w, conv_b=conv_b, w_a=w_a, b_a=b_a, w_i=w_i,
                  b_i=b_i, lru_lambda=lru_lambda, lam_q1=lam_q1, lam_k1=lam_k1, lam_q2=lam_q2,
                  lam_k2=lam_k2, subln_g=subln_g, w_o=w_o, ln1_g=ln1_g, ln1_b=ln1_b,
                  w_group=w_group, b_group=b_group, w_router=w_router, b_router=b_router,
                  w_gate=w_gate, w_up=w_up, w_down=w_down, ln2_g=ln2_g, ln2_b=ln2_b)
    s = x.shape[1]
    return _forward(x, positions, params, tb=min(s, 512), tm_moe=min(s, 256))
```

```python
import functools
import math

import jax
import jax.numpy as jnp
from jax import lax
from jax.experimental import pallas as pl
from jax.experimental.pallas import tpu as pltpu

F32 = jnp.float32
BF16 = jnp.bfloat16

D_MODEL = 2048
DEPTH = 2
CHUNK = 64
FOX_HEADS = 6
FOX_HD = 128
FOX_W = FOX_HEADS * FOX_HD
LRU_W = 768
LRU_BLOCKS = 6
LRU_BW = LRU_W // LRU_BLOCKS
CONV_W = 4
LRU_C = 8.0
DIFF_HEADS = 4
DIFF_HD = 64
DIFF_VD = 2 * DIFF_HD
DIFF_W = DIFF_HEADS * DIFF_VD
ROT_DIM = DIFF_HD // 4
ROPE_THETA = 500000.0
N_GROUPS = 4
EXPERTS_PER_GROUP = 8
N_EXPERTS = N_GROUPS * EXPERTS_PER_GROUP
D_FF_EXPERT = 512
ALPHA = (2.0 * DEPTH) ** 0.25
LN_EPS = 1e-5

LANES = 128
NEG_BIG = -1e30
LOG2E = math.log2(math.e)
SKIP_BITS = 72.0
VMEM_LIMIT = 56 * 1024 * 1024

G16_W = 3 * FOX_W + DIFF_W
G32_W = 2 * DIFF_W + 2 * LRU_W + 2 * LANES
G16_QA, G16_KA, G16_VA, G16_VC = 0, 6, 12, 18
G32_XR, G32_GR, G32_FA = 8, 14, 20


def _cparams(sem, vmem=VMEM_LIMIT):
    return pltpu.CompilerParams(dimension_semantics=sem, vmem_limit_bytes=vmem)


def _mm_kernel(x_ref, w_ref, s_ref, o_ref):
    acc = jnp.dot(x_ref[...], w_ref[...], preferred_element_type=F32)
    o_ref[...] = (acc * s_ref[...]).astype(o_ref.dtype)


def _matmul(x, w, s, out_dtype, tm, tn, name):
    m, k = x.shape
    n = w.shape[1]
    return pl.pallas_call(
        _mm_kernel,
        grid=(m // tm, n // tn),
        in_specs=[pl.BlockSpec((tm, k), lambda i, j: (i, 0)),
                  pl.BlockSpec((k, tn), lambda i, j: (0, j)),
                  pl.BlockSpec((1, tn), lambda i, j: (0, j))],
        out_specs=pl.BlockSpec((tm, tn), lambda i, j: (i, j)),
        out_shape=jax.ShapeDtypeStruct((m, n), out_dtype),
        compiler_params=_cparams(("parallel", "parallel")),
        name=name,
    )(x, w, s)


_IN_QA = 0
_IN_KA = _IN_QA + FOX_W
_IN_VA = _IN_KA + FOX_W
_IN_FA = _IN_VA + FOX_W
_IN_XR = _IN_FA + FOX_HEADS
_IN_GR = _IN_XR + LRU_W
_IN_QC = _IN_GR + LRU_W
_IN_KC = _IN_QC + DIFF_W
_IN_VC = _IN_KC + DIFF_W
N_IN = _IN_VC + DIFF_W


def _wprep_kernel(w_ref, o16_ref, o32_ref):
    def put(o_ref, dst, src, width):
        o_ref[:, dst:dst + width] = w_ref[0, :, src:src + width].astype(BF16)

    put(o16_ref, 0, _IN_QA, FOX_W)
    put(o16_ref, FOX_W, _IN_KA, FOX_W)
    put(o16_ref, 2 * FOX_W, _IN_VA, FOX_W)
    put(o16_ref, 3 * FOX_W, _IN_VC, DIFF_W)
    put(o32_ref, 0, _IN_QC, DIFF_W)
    put(o32_ref, DIFF_W, _IN_KC, DIFF_W)
    put(o32_ref, G32_XR * LANES, _IN_XR, LRU_W)
    put(o32_ref, G32_GR * LANES, _IN_GR, LRU_W)
    fa0 = G32_FA * LANES
    o32_ref[:, fa0:fa0 + 2 * LANES] = jnp.zeros((o32_ref.shape[0], 2 * LANES), BF16)
    put(o32_ref, fa0, _IN_FA, FOX_HEADS)


def _prep_w_in(w_in, layer, tk=256):
    return pl.pallas_call(
        _wprep_kernel,
        grid=(D_MODEL // tk,),
        in_specs=[pl.BlockSpec((1, tk, N_IN), lambda i: (layer, i, 0))],
        out_specs=[pl.BlockSpec((tk, G16_W), lambda i: (i, 0)),
                   pl.BlockSpec((tk, G32_W), lambda i: (i, 0))],
        out_shape=[jax.ShapeDtypeStruct((D_MODEL, G16_W), BF16),
                   jax.ShapeDtypeStruct((D_MODEL, G32_W), BF16)],
        compiler_params=_cparams(("parallel",)),
        name="w_in_regroup",
    )(w_in)


def _split3(x):
    hi = x.astype(BF16)
    r1 = x - hi.astype(F32)
    mid = r1.astype(BF16)
    lo = (r1 - mid.astype(F32)).astype(BF16)
    return hi, mid, lo


def _fox_prep_kernel(fa_ref, bf_ref, q_ref, k_ref, qaux_ref, kaux_ref, stat_ref, carry_ref):
    @pl.when(pl.program_id(0) == 0)
    def _():
        carry_ref[...] = jnp.zeros_like(carry_ref)

    z = fa_ref[...] + bf_ref[...]
    lf = (jnp.minimum(z, 0.0) - jnp.log1p(jnp.exp(-jnp.abs(z)))) * LOG2E
    t = z.shape[0]
    row = lax.broadcasted_iota(jnp.int32, (t, t), 0)
    col = lax.broadcasted_iota(jnp.int32, (t, t), 1)
    tri = jnp.where(row >= col, 1.0, 0.0).astype(BF16)
    hi, mid, lo = _split3(lf)
    cs = (jnp.dot(tri, hi, preferred_element_type=F32)
          + jnp.dot(tri, mid, preferred_element_type=F32)
          + jnp.dot(tri, lo, preferred_element_type=F32)) + carry_ref[...]
    carry_ref[...] = cs[t - 1:t, :]

    lane = lax.broadcasted_iota(jnp.int32, (t, LANES), 1)
    lane1 = lax.broadcasted_iota(jnp.int32, (1, LANES), 1)
    qn = jnp.zeros((1, LANES), F32)
    kn = jnp.zeros((1, LANES), F32)
    for h in range(FOX_HEADS):
        c = jnp.broadcast_to(jnp.sum(jnp.where(lane == h, cs, 0.0), axis=1, keepdims=True),
                             (t, LANES))
        c_hi = c.astype(BF16).astype(F32)
        r1 = c - c_hi
        c_mid = r1.astype(BF16).astype(F32)
        c_lo = (r1 - c_mid).astype(BF16).astype(F32)
        kaux = jnp.where(lane == 0, c_hi, jnp.where(lane == 1, c_mid, jnp.where(
            lane == 2, c_lo, jnp.where(lane < 6, 1.0, 0.0))))
        qaux = jnp.where(lane < 3, -1.0, jnp.where(lane == 3, c_hi, jnp.where(
            lane == 4, c_mid, jnp.where(lane == 5, c_lo, 0.0))))
        kaux_ref[h] = kaux.astype(BF16)
        qaux_ref[h] = qaux.astype(BF16)
        qh = q_ref[:, h * FOX_HD:(h + 1) * FOX_HD].astype(F32)
        kh = k_ref[:, h * FOX_HD:(h + 1) * FOX_HD].astype(F32)
        q2 = jnp.max(jnp.sum(qh * qh, axis=1, keepdims=True), axis=0, keepdims=True)
        k2 = jnp.max(jnp.sum(kh * kh, axis=1, keepdims=True), axis=0, keepdims=True)
        qn = jnp.where(lane1 == h, q2, qn)
        kn = jnp.where(lane1 == h, k2, kn)
    stat_ref[0] = jnp.concatenate([qn, kn, cs[0:1, :], cs[t - 1:t, :],
                                   jnp.zeros((4, LANES), F32)], axis=0)


def _fox_prep(g32, g16, bf_pad, tb):
    s = g32.shape[0]
    qk_blocks = FOX_W // LANES
    return pl.pallas_call(
        _fox_prep_kernel,
        grid=(s // tb,),
        in_specs=[pl.BlockSpec((tb, LANES), lambda i: (i, G32_FA)),
                  pl.BlockSpec((1, LANES), lambda i: (0, 0)),
                  pl.BlockSpec((tb, FOX_W), lambda i: (i, G16_QA // qk_blocks)),
                  pl.BlockSpec((tb, FOX_W), lambda i: (i, G16_KA // qk_blocks))],
        out_specs=[pl.BlockSpec((FOX_HEADS, tb, LANES), lambda i: (0, i, 0)),
                   pl.BlockSpec((FOX_HEADS, tb, LANES), lambda i: (0, i, 0)),
                   pl.BlockSpec((1, 8, LANES), lambda i: (i, 0, 0))],
        out_shape=[jax.ShapeDtypeStruct((FOX_HEADS, s, LANES), BF16),
                   jax.ShapeDtypeStruct((FOX_HEADS, s, LANES), BF16),
                   jax.ShapeDtypeStruct((s // tb, 8, LANES), F32)],
        scratch_shapes=[pltpu.VMEM((1, LANES), F32)],
        compiler_params=_cparams(("arbitrary",)),
        name="fox_prep",
    )(g32, bf_pad, g16, g16)


def _fox_skip_table(stat, per_block):
    nb = stat.shape[0] // per_block
    st = stat.reshape(nb, per_block, 8, LANES)
    qn = jnp.sqrt(jnp.max(st[:, :, 0, :FOX_HEADS], axis=1))
    kn = jnp.sqrt(jnp.max(st[:, :, 1, :FOX_HEADS], axis=1))
    c_first = st[:, 0, 2, :FOX_HEADS]
    c_last = st[:, per_block - 1, 3, :FOX_HEADS]
    kmax = jnp.max(kn, axis=0, keepdims=True)
    bound = 1.01 * qn * (kmax + kn) + c_first
    skip = (bound[:, None, :] - c_last[None, :, :]) < -SKIP_BITS
    jj = jnp.arange(nb)
    skip = jnp.logical_and(skip, (jj[None, :] < jj[:, None])[:, :, None])
    lead = jnp.cumsum(1 - skip.astype(jnp.int32), axis=1) == 0
    return jnp.sum(lead.astype(jnp.int32), axis=1).T.reshape(-1)


def _rope_kernel(x_ref, pos_ref, invf_ref, o_ref):
    ang = pos_ref[...] * invf_ref[...]
    c = jnp.cos(ang)
    s = jnp.sin(ang)
    half = ROT_DIM // 2
    d = lax.broadcasted_iota(jnp.int32, (1, LANES), 1) % DIFF_HD
    sa = jnp.where(d >= half, s, 0.0)
    sb = jnp.where(d < half, -s, 0.0)
    nblk = x_ref.shape[1] // LANES
    for j in range(nblk):
        t = x_ref[:, j * LANES:(j + 1) * LANES]
        r = t * c + pltpu.roll(t, half, 1) * sa + pltpu.roll(t, LANES - half, 1) * sb
        if j < nblk // 2:
            r = r * (DIFF_HD ** -0.5 * LOG2E)
        o_ref[:, j * LANES:(j + 1) * LANES] = r.astype(o_ref.dtype)


def _rope(g32, pos_f, invf, tb):
    s = g32.shape[0]
    w = 2 * DIFF_W
    return pl.pallas_call(
        _rope_kernel,
        grid=(s // tb,),
        in_specs=[pl.BlockSpec((tb, w), lambda i: (i, 0)),
                  pl.BlockSpec((tb, LANES), lambda i: (i, 0)),
                  pl.BlockSpec((1, LANES), lambda i: (0, 0))],
        out_specs=pl.BlockSpec((tb, w), lambda i: (i, 0)),
        out_shape=jax.ShapeDtypeStruct((s, w), BF16),
        compiler_params=_cparams(("parallel",)),
        name="diff_rope",
    )(g32, pos_f, invf)


def _flash_step(v, s, m_ref, l_ref, acc_ref):
    m_prev = m_ref[...]
    m_new = jnp.maximum(m_prev, jnp.max(s, axis=0, keepdims=True))
    a = jnp.exp2(m_prev - m_new)
    p = jnp.exp2(s - m_new)
    l_ref[...] = a * l_ref[...] + jnp.sum(p, axis=0, keepdims=True)
    pv = lax.dot_general(v, p.astype(v.dtype), (((0,), (0,)), ((), ())),
                         preferred_element_type=F32)
    acc_ref[...] = a * acc_ref[...] + pv
    m_ref[...] = m_new


def _flash_pipeline(j0, i, qk, sm):
    npairs = (i - j0) // 2
    qk(j0, 0)

    def body(p, carry):
        j = j0 + 2 * p
        qk(j + 1, 1)
        sm(j, 0, False)
        qk(j + 2, 0)
        sm(j + 1, 1, False)
        return carry

    lax.fori_loop(0, npairs, body, 0)
    jn = j0 + 2 * npairs

    @pl.when(jn == i)
    def _():
        sm(i, 0, True)

    @pl.when(jn != i)
    def _():
        qk(i, 1)
        sm(jn, 0, False)
        sm(i, 1, True)


def _init_state(m_ref, l_ref, acc_ref):
    m_ref[...] = jnp.full(m_ref.shape, NEG_BIG, F32)
    l_ref[...] = jnp.zeros(l_ref.shape, F32)
    acc_ref[...] = jnp.zeros(acc_ref.shape, F32)


def _fox_kernel(jmin_ref, q_ref, qaux_ref, k_ref, kaux_ref, v_ref, o_ref,
                m_ref, l_ref, acc_ref, sa_ref, sb_ref, *, tb):
    h = pl.program_id(0)
    i = pl.program_id(1)
    qf = jnp.concatenate([q_ref[...], qaux_ref[0]], axis=1)
    _init_state(m_ref, l_ref, acc_ref)
    s_refs = (sa_ref, sb_ref)

    def qk(j, slot):
        k0 = pl.multiple_of(j * tb, tb)
        kf = jnp.concatenate([k_ref[pl.ds(k0, tb), :], kaux_ref[0, pl.ds(k0, tb), :]], axis=1)
        s_refs[slot][...] = lax.dot_general(kf, qf, (((1,), (1,)), ((), ())),
                                            preferred_element_type=F32)

    def sm(j, slot, masked):
        s = s_refs[slot][...]
        if masked:
            kk = lax.broadcasted_iota(jnp.int32, (tb, tb), 0)
            qq = lax.broadcasted_iota(jnp.int32, (tb, tb), 1)
            s = jnp.where(kk <= qq, s, NEG_BIG)
        v = v_ref[pl.ds(pl.multiple_of(j * tb, tb), tb), :]
        _flash_step(v, s, m_ref, l_ref, acc_ref)

    _flash_pipeline(jmin_ref[h * pl.num_programs(1) + i], i, qk, sm)
    o = acc_ref[...] / l_ref[...]
    o_ref[...] = o.T.astype(o_ref.dtype)


def _fox_attention(jmin, g16, qaux, kaux, tb):
    s = g16.shape[0]
    grid_spec = pltpu.PrefetchScalarGridSpec(
        num_scalar_prefetch=1,
        grid=(FOX_HEADS, s // tb),
        in_specs=[pl.BlockSpec((tb, LANES), lambda h, i, jm: (i, G16_QA + h)),
                  pl.BlockSpec((1, tb, LANES), lambda h, i, jm: (h, i, 0)),
                  pl.BlockSpec((s, LANES), lambda h, i, jm: (0, G16_KA + h)),
                  pl.BlockSpec((1, s, LANES), lambda h, i, jm: (h, 0, 0)),
                  pl.BlockSpec((s, LANES), lambda h, i, jm: (0, G16_VA + h))],
        out_specs=pl.BlockSpec((tb, LANES), lambda h, i, jm: (i, h)),
        scratch_shapes=[pltpu.VMEM((1, tb), F32), pltpu.VMEM((1, tb), F32),
                        pltpu.VMEM((FOX_HD, tb), F32),
                        pltpu.VMEM((tb, tb), F32), pltpu.VMEM((tb, tb), F32)],
    )
    return pl.pallas_call(
        functools.partial(_fox_kernel, tb=tb),
        grid_spec=grid_spec,
        out_shape=jax.ShapeDtypeStruct((s, FOX_W), BF16),
        compiler_params=_cparams(("parallel", "parallel")),
        name="fox_attention",
    )(jmin, g16, qaux, g16, kaux, g16)


def _diff_kernel(q_ref, k_ref, v_ref, lamv_ref, g_ref, o_ref, m_ref, l_ref, acc_ref,
                 sa_ref, sb_ref, *, tb, lam_init):
    i = pl.program_id(1)
    q = q_ref[...].astype(F32)
    lane = lax.broadcasted_iota(jnp.int32, (tb, LANES), 1)
    qq2 = jnp.concatenate([jnp.where(lane < DIFF_HD, q, 0.0),
                           jnp.where(lane >= DIFF_HD, q, 0.0)], axis=0).astype(BF16)
    _init_state(m_ref, l_ref, acc_ref)
    s_refs = (sa_ref, sb_ref)

    def qk(j, slot):
        k = k_ref[pl.ds(pl.multiple_of(j * tb, tb), tb), :]
        s_refs[slot][...] = lax.dot_general(k, qq2, (((1,), (1,)), ((), ())),
                                            preferred_element_type=F32)

    def sm(j, slot, masked):
        s = s_refs[slot][...]
        if masked:
            kc = lax.broadcasted_iota(jnp.int32, (tb, 2 * tb), 0) // CHUNK
            qc = (lax.broadcasted_iota(jnp.int32, (tb, 2 * tb), 1) % tb) // CHUNK
            s = jnp.where(kc <= qc, s, NEG_BIG)
        v = v_ref[pl.ds(pl.multiple_of(j * tb, tb), tb), :]
        _flash_step(v, s, m_ref, l_ref, acc_ref)

    _flash_pipeline(0, i, qk, sm)

    lv = lamv_ref[...]
    lam = (jnp.exp(jnp.sum(lv[0:1] * lv[1:2], axis=1, keepdims=True))
           - jnp.exp(jnp.sum(lv[2:3] * lv[3:4], axis=1, keepdims=True)) + lam_init)
    on = acc_ref[...] / l_ref[...]
    o = (on[:, :tb] - lam * on[:, tb:]).T
    ms = jnp.mean(o * o, axis=-1, keepdims=True)
    o = o * lax.rsqrt(ms + LN_EPS) * g_ref[...] * (1.0 - lam_init)
    o_ref[...] = o.astype(o_ref.dtype)


def _diff_attention(qk16, g16, lamv, subg, lam_init, tb):
    s = g16.shape[0]
    return pl.pallas_call(
        functools.partial(_diff_kernel, tb=tb, lam_init=lam_init),
        grid=(DIFF_HEADS, s // tb),
        in_specs=[pl.BlockSpec((tb, LANES), lambda h, i: (i, h)),
                  pl.BlockSpec((s, LANES), lambda h, i: (0, DIFF_HEADS + h)),
                  pl.BlockSpec((s, LANES), lambda h, i: (0, G16_VC + h)),
                  pl.BlockSpec((4, LANES), lambda h, i: (0, 0)),
                  pl.BlockSpec((1, LANES), lambda h, i: (0, 0))],
        out_specs=pl.BlockSpec((tb, LANES), lambda h, i: (i, h)),
        out_shape=jax.ShapeDtypeStruct((s, DIFF_W), BF16),
        scratch_shapes=[pltpu.VMEM((1, 2 * tb), F32), pltpu.VMEM((1, 2 * tb), F32),
                        pltpu.VMEM((DIFF_VD, 2 * tb), F32),
                        pltpu.VMEM((tb, 2 * tb), F32), pltpu.VMEM((tb, 2 * tb), F32)],
        compiler_params=_cparams(("parallel", "parallel")),
        name="diff_attention",
    )(qk16, qk16, g16, lamv, subg)


def _shift_rows(x, d, fill):
    if d % 8 == 0:
        return jnp.concatenate([jnp.full((d, x.shape[1]), fill, x.dtype), x[:x.shape[0] - d]],
                               axis=0)
    rows = lax.broadcasted_iota(jnp.int32, x.shape, 0)
    return jnp.where(rows >= d, pltpu.roll(x, d, 0), fill)


def _lru_kernel(xr_ref, gr_ref, cw_ref, cb_ref, wa_ref, ba_ref, wi_ref, bi_ref, lam_ref,
                o_ref, halo_ref, h_ref, *, tb, sub):
    @pl.when(pl.program_id(1) == 0)
    def _():
        halo_ref[...] = jnp.zeros_like(halo_ref)
        h_ref[...] = jnp.zeros_like(h_ref)

    x = xr_ref[...]
    xe = jnp.concatenate([halo_ref[...], x], axis=0)
    halo_ref[...] = x[tb - 8:tb, :]
    cw = cw_ref[...]
    xc = cb_ref[...] + cw[CONV_W - 1:CONV_W, :] * x
    for j in range(CONV_W - 1):
        sh = CONV_W - 1 - j
        xc = xc + cw[j:j + 1, :] * pltpu.roll(xe, sh, 0)[8:8 + tb, :]

    xcb = xc.astype(BF16)
    r = jax.nn.sigmoid(jnp.dot(xcb, wa_ref[0], preferred_element_type=F32) + ba_ref[...])
    ig = jax.nn.sigmoid(jnp.dot(xcb, wi_ref[0], preferred_element_type=F32) + bi_ref[...])
    lam = lam_ref[...]
    ls = jnp.minimum(lam, 0.0) - jnp.log1p(jnp.exp(-jnp.abs(lam)))
    log_a = LRU_C * r * ls
    a = jnp.exp(log_a)
    z2 = 2.0 * log_a
    e2 = jnp.exp(z2)
    small = jnp.where(e2 == 1.0, -z2, (1.0 - e2) * z2 / jnp.log(e2))
    neg_expm1 = jnp.where(z2 < -1.0, 1.0 - e2, small)
    u = jnp.sqrt(neg_expm1) * (ig * xc)

    h = h_ref[...]
    for c in range(tb // sub):
        ac = a[c * sub:(c + 1) * sub, :]
        uc = u[c * sub:(c + 1) * sub, :]
        d = 1
        while d < sub:
            uc = ac * _shift_rows(uc, d, 0.0) + uc
            ac = ac * _shift_rows(ac, d, 1.0)
            d *= 2
        hc = uc + ac * h
        h = hc[sub - 1:sub, :]
        g = gr_ref[c * sub:(c + 1) * sub, :]
        gelu = 0.5 * g * (1.0 + jnp.tanh(math.sqrt(2.0 / math.pi) * (g + 0.044715 * (g * g * g))))
        o_ref[c * sub:(c + 1) * sub, :] = (gelu * hc).astype(o_ref.dtype)
    h_ref[...] = h


def _lru_branch(g32, cw, cb, wa, ba, wi, bi, lam, tb, sub):
    s = g32.shape[0]
    vec = lambda c, i: (0, c)
    return pl.pallas_call(
        functools.partial(_lru_kernel, tb=tb, sub=sub),
        grid=(LRU_BLOCKS, s // tb),
        in_specs=[pl.BlockSpec((tb, LANES), lambda c, i: (i, G32_XR + c)),
                  pl.BlockSpec((tb, LANES), lambda c, i: (i, G32_GR + c)),
                  pl.BlockSpec((CONV_W, LANES), vec),
                  pl.BlockSpec((1, LANES), vec),
                  pl.BlockSpec((1, LRU_BW, LRU_BW), lambda c, i: (c, 0, 0)),
                  pl.BlockSpec((1, LANES), vec),
                  pl.BlockSpec((1, LRU_BW, LRU_BW), lambda c, i: (c, 0, 0)),
                  pl.BlockSpec((1, LANES), vec),
                  pl.BlockSpec((1, LANES), vec)],
        out_specs=pl.BlockSpec((tb, LANES), lambda c, i: (i, c)),
        out_shape=jax.ShapeDtypeStruct((s, LRU_W), BF16),
        scratch_shapes=[pltpu.VMEM((8, LANES), F32), pltpu.VMEM((1, LANES), F32)],
        compiler_params=_cparams(("parallel", "arbitrary")),
        name="rg_lru",
    )(g32, g32, cw, cb, wa, ba, wi, bi, lam)


def _layer_norm_rows(y, g, b):
    mu = jnp.mean(y, axis=-1, keepdims=True)
    yc = y - mu
    var = jnp.mean(yc * yc, axis=-1, keepdims=True)
    return yc * lax.rsqrt(var + LN_EPS) * g + b


def _oproj_ln_kernel(xa_ref, xb_ref, xc_ref, wa_ref, wb_ref, wc_ref, h_ref, g_ref, b_ref,
                     o32_ref, oslab_ref):
    mix = (jnp.dot(xa_ref[...], wa_ref[...], preferred_element_type=F32)
           + jnp.dot(xb_ref[...], wb_ref[...], preferred_element_type=F32)
           + jnp.dot(xc_ref[...], wc_ref[...], preferred_element_type=F32))
    o = _layer_norm_rows(ALPHA * h_ref[...] + mix, g_ref[...], b_ref[...])
    o32_ref[...] = o
    _store_slabs(oslab_ref, o)


def _oproj_ln(xa, xb, xc, wa, wb, wc, h, g, b, tm):
    s = h.shape[0]
    row = lambda i: (i, 0)
    fixed = lambda i: (0, 0)
    return pl.pallas_call(
        _oproj_ln_kernel,
        grid=(s // tm,),
        in_specs=[pl.BlockSpec((tm, FOX_W), row), pl.BlockSpec((tm, LRU_W), row),
                  pl.BlockSpec((tm, DIFF_W), row),
                  pl.BlockSpec((FOX_W, D_MODEL), fixed), pl.BlockSpec((LRU_W, D_MODEL), fixed),
                  pl.BlockSpec((DIFF_W, D_MODEL), fixed),
                  pl.BlockSpec((tm, D_MODEL), row),
                  pl.BlockSpec((1, D_MODEL), fixed), pl.BlockSpec((1, D_MODEL), fixed)],
        out_specs=[pl.BlockSpec((tm, D_MODEL), row), pl.BlockSpec((tm * SLAB_ROWS, LANES), row)],
        out_shape=[jax.ShapeDtypeStruct((s, D_MODEL), F32),
                   jax.ShapeDtypeStruct((s * SLAB_ROWS, LANES), F32)],
        compiler_params=_cparams(("parallel",)),
        name="oproj_ln1",
    )(xa, xb, xc, wa, wb, wc, h, g, b)


def _router_kernel(h_ref, whi_ref, wlo_ref, b_ref, o_ref):
    h = h_ref[...]
    hi = h.astype(BF16)
    lo = (h - hi.astype(F32)).astype(BF16)
    whi = whi_ref[...]
    logits = (jnp.dot(hi, whi, preferred_element_type=F32)
              + jnp.dot(hi, wlo_ref[...], preferred_element_type=F32)
              + jnp.dot(lo, whi, preferred_element_type=F32)) + b_ref[...]
    lane = lax.broadcasted_iota(jnp.int32, logits.shape, 1)
    big = jnp.int32(1 << 20)

    def first_lane(cond):
        return jnp.min(jnp.where(cond, lane, big), axis=1, keepdims=True)

    gm = lane < N_GROUPS
    gl = jnp.where(gm, logits, -jnp.inf)
    gmax = jnp.max(gl, axis=1, keepdims=True)
    gexp = jnp.where(gm, jnp.exp(logits - gmax), 0.0)
    gprob = gexp / jnp.sum(gexp, axis=1, keepdims=True)
    gidx = first_lane(gl == gmax)
    g_weight = jnp.sum(jnp.where(lane == gidx, gprob, 0.0), axis=1, keepdims=True)

    e0 = N_GROUPS + EXPERTS_PER_GROUP * gidx
    em = jnp.logical_and(lane >= e0, lane < e0 + EXPERTS_PER_GROUP)
    el = jnp.where(em, logits, -jnp.inf)
    emax = jnp.max(el, axis=1, keepdims=True)
    eexp = jnp.where(em, jnp.exp(logits - emax), 0.0)
    eprob = jnp.where(em, eexp / jnp.sum(eexp, axis=1, keepdims=True), -1.0)
    p1 = jnp.max(eprob, axis=1, keepdims=True)
    i1 = first_lane(eprob == p1)
    eprob2 = jnp.where(lane == i1, -1.0, eprob)
    p2 = jnp.max(eprob2, axis=1, keepdims=True)
    i2 = first_lane(eprob2 == p2)
    den = p1 + p2
    gate1 = g_weight * (p1 / den)
    gate2 = g_weight * (p2 / den)
    id1 = (i1 - N_GROUPS).astype(F32)
    id2 = (i2 - N_GROUPS).astype(F32)
    o_ref[...] = jnp.where(lane == 0, id1,
                           jnp.where(lane == 1, id2,
                                     jnp.where(lane == 2, gate1,
                                               jnp.where(lane == 3, gate2, 0.0))))


def _router(h, whi, wlo, bias, tm):
    s = h.shape[0]
    fixed = lambda i: (0, 0)
    return pl.pallas_call(
        _router_kernel,
        grid=(s // tm,),
        in_specs=[pl.BlockSpec((tm, D_MODEL), lambda i: (i, 0)),
                  pl.BlockSpec((D_MODEL, LANES), fixed), pl.BlockSpec((D_MODEL, LANES), fixed),
                  pl.BlockSpec((1, LANES), fixed)],
        out_specs=pl.BlockSpec((tm, LANES), lambda i: (i, 0)),
        out_shape=jax.ShapeDtypeStruct((s, LANES), F32),
        compiler_params=_cparams(("parallel",)),
        name="moe_router",
    )(h, whi, wlo, bias)


SLAB_ROWS = D_MODEL // LANES
SLAB_PITCH = SLAB_ROWS + 4


def _store_slabs(o_ref, x):
    rows = x.shape[0]
    for c in range(SLAB_ROWS):
        o_ref[pl.ds(c, rows, stride=SLAB_ROWS), :] = x[:, c * LANES:(c + 1) * LANES]


def _load_slabs(buf_ref, rows):
    return jnp.concatenate([buf_ref[pl.ds(c, rows, stride=SLAB_PITCH), :]
                            for c in range(SLAB_ROWS)], axis=1)


def _issue_slab_gather(row_of, n, src_hbm, dst_ref, sem):
    def body(r, carry):
        src0 = pl.multiple_of(row_of(r) * SLAB_ROWS, SLAB_ROWS)
        dst0 = pl.multiple_of(r * SLAB_PITCH, 4)
        pltpu.make_async_copy(src_hbm.at[pl.ds(src0, SLAB_ROWS), :],
                              dst_ref.at[pl.ds(dst0, SLAB_ROWS), :], sem).start()
        return carry
    lax.fori_loop(0, n, body, 0, unroll=8)


def _wait_slab_gather(n, buf_ref, sem):
    part = buf_ref.at[pl.ds(0, n * SLAB_ROWS), :]
    pltpu.make_async_copy(part, part, sem).wait()


def _moe_kernel(te_ref, nt_ref, tok_ref, h_hbm, gate_ref, wg_ref, wu_ref, wd_ref, y_ref,
                xbuf, sem, *, tm):
    del te_ref
    t = pl.program_id(0)
    nt = nt_ref[0]
    slot = t % 2

    @pl.when(t == 0)
    def _():
        _issue_slab_gather(lambda r: tok_ref[r], tm, h_hbm, xbuf.at[0], sem.at[0])

    @pl.when(t + 1 < nt)
    def _():
        base = (t + 1) * tm
        _issue_slab_gather(lambda r: tok_ref[base + r], tm, h_hbm, xbuf.at[1 - slot],
                           sem.at[1 - slot])

    @pl.when(t < nt)
    def _():
        _wait_slab_gather(tm, xbuf.at[slot], sem.at[slot])
        x = _load_slabs(xbuf.at[slot], tm).astype(BF16)
        hg = jnp.dot(x, wg_ref[0].astype(BF16), preferred_element_type=F32)
        hu = jnp.dot(x, wu_ref[0].astype(BF16), preferred_element_type=F32)
        act = (hg * jax.nn.sigmoid(hg)) * hu * gate_ref[...]
        y = jnp.dot(act.astype(BF16), wd_ref[0].astype(BF16), preferred_element_type=F32)
        _store_slabs(y_ref, y)

    @pl.when(t >= nt)
    def _():
        y_ref[...] = jnp.zeros_like(y_ref)


def _moe_grouped(tile_expert, ntiles, tok_of_row, h_slab, gate_rows, wg, wu, wd, tm):
    p_max = tok_of_row.shape[0]
    nt_max = p_max // tm
    imap = lambda t, te, nt, tok: (te[t], 0, 0)
    grid_spec = pltpu.PrefetchScalarGridSpec(
        num_scalar_prefetch=3,
        grid=(nt_max,),
        in_specs=[pl.BlockSpec(memory_space=pl.ANY),
                  pl.BlockSpec((tm, 1), lambda t, te, nt, tok: (t, 0)),
                  pl.BlockSpec((1, D_MODEL, D_FF_EXPERT), imap),
                  pl.BlockSpec((1, D_MODEL, D_FF_EXPERT), imap),
                  pl.BlockSpec((1, D_FF_EXPERT, D_MODEL), imap)],
        out_specs=pl.BlockSpec((tm * SLAB_ROWS, LANES), lambda t, te, nt, tok: (t, 0)),
        scratch_shapes=[pltpu.VMEM((2, tm * SLAB_PITCH, LANES), F32),
                        pltpu.SemaphoreType.DMA((2,))],
    )
    return pl.pallas_call(
        functools.partial(_moe_kernel, tm=tm),
        grid_spec=grid_spec,
        out_shape=jax.ShapeDtypeStruct((p_max * SLAB_ROWS, LANES), F32),
        compiler_params=_cparams(("arbitrary",)),
        name="moe_experts",
    )(tile_expert, ntiles, tok_of_row, h_slab, gate_rows, wg, wu, wd)


def _combine_ln_kernel(pos_ref, y_hbm, h_ref, g_ref, b_ref, o32_ref, o16_ref, ybuf, sem, *, tm):
    t = pl.program_id(0)
    nt = pl.num_programs(0)
    slot = t % 2

    def issue(tile, sl):
        for j in range(2):
            _issue_slab_gather(lambda r, j=j: pos_ref[(tile * tm + r) * 2 + j], tm, y_hbm,
                               ybuf.at[sl, j], sem.at[sl])

    @pl.when(t == 0)
    def _():
        issue(0, 0)

    @pl.when(t + 1 < nt)
    def _():
        issue(t + 1, 1 - slot)

    _wait_slab_gather(tm, ybuf.at[slot, 0], sem.at[slot])
    _wait_slab_gather(tm, ybuf.at[slot, 1], sem.at[slot])
    y = ALPHA * h_ref[...] + (_load_slabs(ybuf.at[slot, 0], tm) + _load_slabs(ybuf.at[slot, 1], tm))
    o = _layer_norm_rows(y, g_ref[...], b_ref[...])
    o32_ref[...] = o
    o16_ref[...] = o.astype(BF16)


def _combine_ln(pos, y_slab, h, g, b, tm):
    s = h.shape[0]
    row = lambda t, pos: (t, 0)
    fixed = lambda t, pos: (0, 0)
    grid_spec = pltpu.PrefetchScalarGridSpec(
        num_scalar_prefetch=1,
        grid=(s // tm,),
        in_specs=[pl.BlockSpec(memory_space=pl.ANY),
                  pl.BlockSpec((tm, D_MODEL), row),
                  pl.BlockSpec((1, D_MODEL), fixed), pl.BlockSpec((1, D_MODEL), fixed)],
        out_specs=[pl.BlockSpec((tm, D_MODEL), row), pl.BlockSpec((tm, D_MODEL), row)],
        scratch_shapes=[pltpu.VMEM((2, 2, tm * SLAB_PITCH, LANES), F32),
                        pltpu.SemaphoreType.DMA((2,))],
    )
    return pl.pallas_call(
        functools.partial(_combine_ln_kernel, tm=tm),
        grid_spec=grid_spec,
        out_shape=[jax.ShapeDtypeStruct((s, D_MODEL), F32),
                   jax.ShapeDtypeStruct((s, D_MODEL), BF16)],
        compiler_params=_cparams(("arbitrary",)),
        name="combine_ln2",
    )(pos, y_slab, h, g, b)


def _dispatch_plan(meta, tm):
    t = meta.shape[0]
    eid = meta[:, 0:2].astype(jnp.int32).reshape(-1)
    gate = meta[:, 2:4].reshape(-1)
    onehot = (eid[:, None] == jnp.arange(N_EXPERTS, dtype=jnp.int32)[None, :]).astype(jnp.int32)
    csum = jnp.cumsum(onehot, axis=0)
    counts = csum[-1]
    padded = ((counts + tm - 1) // tm) * tm
    pend = jnp.cumsum(padded)
    poff = pend - padded
    off = jnp.cumsum(counts) - counts
    pos = jnp.sum(onehot * (csum + poff[None, :]), axis=1) - 1
    p_max = 2 * t + N_EXPERTS * tm
    nt_max = p_max // tm
    ntiles = (pend[-1] // tm).astype(jnp.int32)
    tile_ids = jnp.minimum(jnp.arange(nt_max, dtype=jnp.int32), ntiles - 1)
    tile_expert = jnp.sum((pend[None, :] // tm <= tile_ids[:, None]).astype(jnp.int32), axis=1)
    tile_expert = jnp.minimum(tile_expert, N_EXPERTS - 1).astype(jnp.int32)
    order = jnp.argsort(eid, stable=True).astype(jnp.int32)
    k = (jnp.arange(p_max, dtype=jnp.int32).reshape(nt_max, tm)
         - poff[tile_expert][:, None])
    valid = jnp.logical_and(k >= 0, k < counts[tile_expert][:, None]).reshape(-1)
    src = jnp.clip(off[tile_expert][:, None] + k, 0, 2 * t - 1).reshape(-1)
    pair = order[src]
    tok_of_row = jnp.where(valid, pair // 2, 0).astype(jnp.int32)
    gate_rows = jnp.where(valid, gate[pair], 0.0).reshape(p_max, 1)
    return tile_expert, ntiles.reshape(1), tok_of_row, gate_rows, pos.astype(jnp.int32)


def _pad_lanes(v, width=LANES):
    v = v.reshape(1, -1).astype(F32)
    return jnp.pad(v, ((0, 0), (0, width - v.shape[1])))


def _layer(l, h32, h16, pos_f, invf, p, stacked, tb, tm_moe):
    s = h32.shape[0]
    w16, w32 = _prep_w_in(stacked["w_in"], l)
    s16 = jnp.concatenate([jnp.full((1, FOX_W), FOX_HD ** -0.5 * LOG2E, F32),
                           jnp.ones((1, G16_W - FOX_W), F32)], axis=1)
    s32 = jnp.ones((1, G32_W), F32)
    tm_proj = min(s, 2048)
    g16 = _matmul(h16, w16, s16, BF16, tm_proj, 256, "in_proj_bf16")
    g32 = _matmul(h16, w32, s32, F32, tm_proj, 256, "in_proj_f32")

    tb_prep = min(s, 256)
    qaux, kaux, stat = _fox_prep(g32, g16, _pad_lanes(p["b_f"]), tb_prep)
    jmin = _fox_skip_table(stat, tb // tb_prep)
    out_a = _fox_attention(jmin, g16, qaux, kaux, tb)

    out_b = _lru_branch(g32, p["conv_w"], p["conv_b"].reshape(1, -1),
                        p["w_a"].astype(BF16), p["b_a"].reshape(1, -1),
                        p["w_i"].astype(BF16), p["b_i"].reshape(1, -1),
                        p["lru_lambda"].reshape(1, -1), min(s, 1024), min(s, 256))

    qk16 = _rope(g32, pos_f, invf, min(s, 512))
    lamv = jnp.concatenate([_pad_lanes(p["lam_q1"]), _pad_lanes(p["lam_k1"]),
                            _pad_lanes(p["lam_q2"]), _pad_lanes(p["lam_k2"])], axis=0)
    lam_init = 0.8 - 0.6 * math.exp(-0.3 * l)
    out_c = _diff_attention(qk16, g16, lamv, p["subln_g"].reshape(1, -1), lam_init, tb)

    w_o = p["w_o"].astype(BF16)
    h1_32, h1_slab = _oproj_ln(out_a, out_b, out_c, w_o[:FOX_W], w_o[FOX_W:FOX_W + LRU_W],
                               w_o[FOX_W + LRU_W:], h32, p["ln1_g"].reshape(1, -1),
                               p["ln1_b"].reshape(1, -1), min(s, 256))

    w_rt = jnp.concatenate([p["w_group"], p["w_router"],
                            jnp.zeros((D_MODEL, LANES - N_GROUPS - N_EXPERTS), F32)], axis=1)
    w_rt_hi = w_rt.astype(BF16)
    w_rt_lo = (w_rt - w_rt_hi.astype(F32)).astype(BF16)
    b_rt = _pad_lanes(jnp.concatenate([p["b_group"], p["b_router"]]))
    meta = _router(h1_32, w_rt_hi, w_rt_lo, b_rt, min(s, 512))

    tile_expert, ntiles, tok_of_row, gate_rows, pos = _dispatch_plan(meta, tm_moe)
    y_slab = _moe_grouped(tile_expert + l * N_EXPERTS, ntiles, tok_of_row, h1_slab, gate_rows,
                          stacked["w_gate"], stacked["w_up"], stacked["w_down"], tm_moe)
    return _combine_ln(pos, y_slab, h1_32, p["ln2_g"].reshape(1, -1),
                       p["ln2_b"].reshape(1, -1), min(s, 256))


def _forward(x, positions, params, tb, tm_moe):
    bsz, s, d = x.shape
    assert bsz == 1 and d == D_MODEL
    h32 = x.reshape(s, d)
    h16 = h32.astype(BF16)
    half = ROT_DIM // 2
    inv_freq = ROPE_THETA ** (-jnp.arange(half, dtype=F32) * 2.0 / ROT_DIM)
    dlane = jnp.arange(LANES) % DIFF_HD
    invf = jnp.where(dlane < ROT_DIM, inv_freq[dlane % half], 0.0).reshape(1, LANES).astype(F32)
    pos_f = jnp.broadcast_to(positions.astype(F32).reshape(s, 1), (s, LANES))
    big = ("w_in", "w_gate", "w_up", "w_down")
    stacked = {
        "w_in": params["w_in"],
        "w_gate": params["w_gate"].reshape(DEPTH * N_EXPERTS, D_MODEL, D_FF_EXPERT),
        "w_up": params["w_up"].reshape(DEPTH * N_EXPERTS, D_MODEL, D_FF_EXPERT),
        "w_down": params["w_down"].reshape(DEPTH * N_EXPERTS, D_FF_EXPERT, D_MODEL),
    }
    for l in range(DEPTH):
        p = {k: v[l] for k, v in params.items() if k not in big}
        h32, h16 = _layer(l, h32, h16, pos_f, invf, p, stacked, tb, tm_moe)
    return h32.reshape(bsz, s, d)


def kernel(x, positions, w_in, b_f, conv_w, conv_b, w_a, b_a, w_i, b_i, lru_lambda, lam_q1, lam_k1, lam_q2, lam_k2, subln_g, w_o, ln1_g, ln1_b, w_group, b_group, w_router, b_router, w_gate, w_up, w_down, ln2_g, ln2_b):
    params = dict(w_in=w_in, b_f=b_f, conv_w=conv_w, conv_b=conv_b, w_a=w_a, b_a=b_a, w_i=w_i,
                  b_i=b_i, lru_lambda=lru_lambda, lam_q1=lam_q1, lam_k1=lam_k1, lam_q2=lam_q2,
                  lam_k2=lam_k2, subln_g=subln_g, w_o=w_o, ln1_g=ln1_g, ln1_b=ln1_b,
                  w_group=w_group, b_group=b_group, w_router=w_router, b_router=b_router,
                  w_gate=w_gate, w_up=w_up, w_down=w_down, ln2_g=ln2_g, ln2_b=ln2_b)
    s = x.shape[1]
    return _forward(x, positions, params, tb=min(s, 512), tm_moe=min(s, 256))
```

```python
import functools
import math

import jax
import jax.numpy as jnp
from jax import lax
from jax.experimental import pallas as pl
from jax.experimental.pallas import tpu as pltpu

F32 = jnp.float32
BF16 = jnp.bfloat16

D_MODEL = 2048
DEPTH = 2
CHUNK = 64
FOX_HEADS = 6
FOX_HD = 128
FOX_W = FOX_HEADS * FOX_HD
LRU_W = 768
LRU_BLOCKS = 6
LRU_BW = LRU_W // LRU_BLOCKS
CONV_W = 4
LRU_C = 8.0
DIFF_HEADS = 4
DIFF_HD = 64
DIFF_VD = 2 * DIFF_HD
DIFF_W = DIFF_HEADS * DIFF_VD
ROT_DIM = DIFF_HD // 4
ROPE_THETA = 500000.0
N_GROUPS = 4
EXPERTS_PER_GROUP = 8
N_EXPERTS = N_GROUPS * EXPERTS_PER_GROUP
D_FF_EXPERT = 512
ALPHA = (2.0 * DEPTH) ** 0.25
LN_EPS = 1e-5

LANES = 128
NEG_BIG = -1e30
LOG2E = math.log2(math.e)
SKIP_BITS = 72.0
VMEM_LIMIT = 56 * 1024 * 1024

G16_W = 3 * FOX_W + DIFF_W
G32_W = 2 * DIFF_W + 2 * LRU_W + 2 * LANES
G16_QA, G16_KA, G16_VA, G16_VC = 0, 6, 12, 18
G32_XR, G32_GR, G32_FA = 8, 14, 20


def _cparams(sem, vmem=VMEM_LIMIT):
    return pltpu.CompilerParams(dimension_semantics=sem, vmem_limit_bytes=vmem)


def _mm_kernel(x_ref, w_ref, s_ref, o_ref):
    acc = jnp.dot(x_ref[...], w_ref[...], preferred_element_type=F32)
    o_ref[...] = (acc * s_ref[...]).astype(o_ref.dtype)


def _matmul(x, w, s, out_dtype, tm, tn, name):
    m, k = x.shape
    n = w.shape[1]
    return pl.pallas_call(
        _mm_kernel,
        grid=(m // tm, n // tn),
        in_specs=[pl.BlockSpec((tm, k), lambda i, j: (i, 0)),
                  pl.BlockSpec((k, tn), lambda i, j: (0, j)),
                  pl.BlockSpec((1, tn), lambda i, j: (0, j))],
        out_specs=pl.BlockSpec((tm, tn), lambda i, j: (i, j)),
        out_shape=jax.ShapeDtypeStruct((m, n), out_dtype),
        compiler_params=_cparams(("parallel", "parallel")),
        name=name,
    )(x, w, s)


_IN_QA = 0
_IN_KA = _IN_QA + FOX_W
_IN_VA = _IN_KA + FOX_W
_IN_FA = _IN_VA + FOX_W
_IN_XR = _IN_FA + FOX_HEADS
_IN_GR = _IN_XR + LRU_W
_IN_QC = _IN_GR + LRU_W
_IN_KC = _IN_QC + DIFF_W
_IN_VC = _IN_KC + DIFF_W
N_IN = _IN_VC + DIFF_W


def _wprep_kernel(w_ref, o16_ref, o32_ref):
    def put(o_ref, dst, src, width):
        o_ref[:, dst:dst + width] = w_ref[0, :, src:src + width].astype(BF16)

    put(o16_ref, 0, _IN_QA, FOX_W)
    put(o16_ref, FOX_W, _IN_KA, FOX_W)
    put(o16_ref, 2 * FOX_W, _IN_VA, FOX_W)
    put(o16_ref, 3 * FOX_W, _IN_VC, DIFF_W)
    put(o32_ref, 0, _IN_QC, DIFF_W)
    put(o32_ref, DIFF_W, _IN_KC, DIFF_W)
    put(o32_ref, G32_XR * LANES, _IN_XR, LRU_W)
    put(o32_ref, G32_GR * LANES, _IN_GR, LRU_W)
    fa0 = G32_FA * LANES
    o32_ref[:, fa0:fa0 + 2 * LANES] = jnp.zeros((o32_ref.shape[0], 2 * LANES), BF16)
    put(o32_ref, fa0, _IN_FA, FOX_HEADS)


def _prep_w_in(w_in, layer, tk=256):
    return pl.pallas_call(
        _wprep_kernel,
        grid=(D_MODEL // tk,),
        in_specs=[pl.BlockSpec((1, tk, N_IN), lambda i: (layer, i, 0))],
        out_specs=[pl.BlockSpec((tk, G16_W), lambda i: (i, 0)),
                   pl.BlockSpec((tk, G32_W), lambda i: (i, 0))],
        out_shape=[jax.ShapeDtypeStruct((D_MODEL, G16_W), BF16),
                   jax.ShapeDtypeStruct((D_MODEL, G32_W), BF16)],
        compiler_params=_cparams(("parallel",)),
        name="w_in_regroup",
    )(w_in)


def _split3(x):
    hi = x.astype(BF16)
    r1 = x - hi.astype(F32)
    mid = r1.astype(BF16)
    lo = (r1 - mid.astype(F32)).astype(BF16)
    return hi, mid, lo


def _fox_prep_kernel(fa_ref, bf_ref, q_ref, k_ref, qaux_ref, kaux_ref, stat_ref, carry_ref):
    @pl.when(pl.program_id(0) == 0)
    def _():
        carry_ref[...] = jnp.zeros_like(carry_ref)

    z = fa_ref[...] + bf_ref[...]
    lf = (jnp.minimum(z, 0.0) - jnp.log1p(jnp.exp(-jnp.abs(z)))) * LOG2E
    t = z.shape[0]
    row = lax.broadcasted_iota(jnp.int32, (t, t), 0)
    col = lax.broadcasted_iota(jnp.int32, (t, t), 1)
    tri = jnp.where(row >= col, 1.0, 0.0).astype(BF16)
    hi, mid, lo = _split3(lf)
    cs = (jnp.dot(tri, hi, preferred_element_type=F32)
          + jnp.dot(tri, mid, preferred_element_type=F32)
          + jnp.dot(tri, lo, preferred_element_type=F32)) + carry_ref[...]
    carry_ref[...] = cs[t - 1:t, :]

    lane = lax.broadcasted_iota(jnp.int32, (t, LANES), 1)
    lane1 = lax.broadcasted_iota(jnp.int32, (1, LANES), 1)
    qn = jnp.zeros((1, LANES), F32)
    kn = jnp.zeros((1, LANES), F32)
    for h in range(FOX_HEADS):
        c = jnp.broadcast_to(jnp.sum(jnp.where(lane == h, cs, 0.0), axis=1, keepdims=True),
                             (t, LANES))
        c_hi = c.astype(BF16).astype(F32)
        r1 = c - c_hi
        c_mid = r1.astype(BF16).astype(F32)
        c_lo = (r1 - c_mid).astype(BF16).astype(F32)
        kaux = jnp.where(lane == 0, c_hi, jnp.where(lane == 1, c_mid, jnp.where(
            lane == 2, c_lo, jnp.where(lane < 6, 1.0, 0.0))))
        qaux = jnp.where(lane < 3, -1.0, jnp.where(lane == 3, c_hi, jnp.where(
            lane == 4, c_mid, jnp.where(lane == 5, c_lo, 0.0))))
        kaux_ref[h] = kaux.astype(BF16)
        qaux_ref[h] = qaux.astype(BF16)
        qh = q_ref[:, h * FOX_HD:(h + 1) * FOX_HD].astype(F32)
        kh = k_ref[:, h * FOX_HD:(h + 1) * FOX_HD].astype(F32)
        q2 = jnp.max(jnp.sum(qh * qh, axis=1, keepdims=True), axis=0, keepdims=True)
        k2 = jnp.max(jnp.sum(kh * kh, axis=1, keepdims=True), axis=0, keepdims=True)
        qn = jnp.where(lane1 == h, q2, qn)
        kn = jnp.where(lane1 == h, k2, kn)
    stat_ref[0] = jnp.concatenate([qn, kn, cs[0:1, :], cs[t - 1:t, :],
                                   jnp.zeros((4, LANES), F32)], axis=0)


def _fox_prep(g32, g16, bf_pad, tb):
    s = g32.shape[0]
    qk_blocks = FOX_W // LANES
    return pl.pallas_call(
        _fox_prep_kernel,
        grid=(s // tb,),
        in_specs=[pl.BlockSpec((tb, LANES), lambda i: (i, G32_FA)),
                  pl.BlockSpec((1, LANES), lambda i: (0, 0)),
                  pl.BlockSpec((tb, FOX_W), lambda i: (i, G16_QA // qk_blocks)),
                  pl.BlockSpec((tb, FOX_W), lambda i: (i, G16_KA // qk_blocks))],
        out_specs=[pl.BlockSpec((FOX_HEADS, tb, LANES), lambda i: (0, i, 0)),
                   pl.BlockSpec((FOX_HEADS, tb, LANES), lambda i: (0, i, 0)),
                   pl.BlockSpec((1, 8, LANES), lambda i: (i, 0, 0))],
        out_shape=[jax.ShapeDtypeStruct((FOX_HEADS, s, LANES), BF16),
                   jax.ShapeDtypeStruct((FOX_HEADS, s, LANES), BF16),
                   jax.ShapeDtypeStruct((s // tb, 8, LANES), F32)],
        scratch_shapes=[pltpu.VMEM((1, LANES), F32)],
        compiler_params=_cparams(("arbitrary",)),
        name="fox_prep",
    )(g32, bf_pad, g16, g16)


def _fox_skip_table(stat, per_block):
    nb = stat.shape[0] // per_block
    st = stat.reshape(nb, per_block, 8, LANES)
    qn = jnp.sqrt(jnp.max(st[:, :, 0, :FOX_HEADS], axis=1))
    kn = jnp.sqrt(jnp.max(st[:, :, 1, :FOX_HEADS], axis=1))
    c_first = st[:, 0, 2, :FOX_HEADS]
    c_last = st[:, per_block - 1, 3, :FOX_HEADS]
    kmax = jnp.max(kn, axis=0, keepdims=True)
    bound = 1.01 * qn * (kmax + kn) + c_first
    skip = (bound[:, None, :] - c_last[None, :, :]) < -SKIP_BITS
    jj = jnp.arange(nb)
    skip = jnp.logical_and(skip, (jj[None, :] < jj[:, None])[:, :, None])
    lead = jnp.cumsum(1 - skip.astype(jnp.int32), axis=1) == 0
    return jnp.sum(lead.astype(jnp.int32), axis=1).T.reshape(-1)


def _rope_kernel(x_ref, pos_ref, invf_ref, o_ref):
    ang = pos_ref[...] * invf_ref[...]
    c = jnp.cos(ang)
    s = jnp.sin(ang)
    half = ROT_DIM // 2
    d = lax.broadcasted_iota(jnp.int32, (1, LANES), 1) % DIFF_HD
    sa = jnp.where(d >= half, s, 0.0)
    sb = jnp.where(d < half, -s, 0.0)
    nblk = x_ref.shape[1] // LANES
    for j in range(nblk):
        t = x_ref[:, j * LANES:(j + 1) * LANES]
        r = t * c + pltpu.roll(t, half, 1) * sa + pltpu.roll(t, LANES - half, 1) * sb
        if j < nblk // 2:
            r = r * (DIFF_HD ** -0.5 * LOG2E)
        o_ref[:, j * LANES:(j + 1) * LANES] = r.astype(o_ref.dtype)


def _rope(g32, pos_f, invf, tb):
    s = g32.shape[0]
    w = 2 * DIFF_W
    return pl.pallas_call(
        _rope_kernel,
        grid=(s // tb,),
        in_specs=[pl.BlockSpec((tb, w), lambda i: (i, 0)),
                  pl.BlockSpec((tb, LANES), lambda i: (i, 0)),
                  pl.BlockSpec((1, LANES), lambda i: (0, 0))],
        out_specs=pl.BlockSpec((tb, w), lambda i: (i, 0)),
        out_shape=jax.ShapeDtypeStruct((s, w), BF16),
        compiler_params=_cparams(("parallel",)),
        name="diff_rope",
    )(g32, pos_f, invf)


def _flash_step(v, s, m_ref, l_ref, acc_ref):
    m_prev = m_ref[...]
    m_new = jnp.maximum(m_prev, jnp.max(s, axis=0, keepdims=True))
    a = jnp.exp2(m_prev - m_new)
    p = jnp.exp2(s - m_new)
    l_ref[...] = a * l_ref[...] + jnp.sum(p, axis=0, keepdims=True)
    pv = lax.dot_general(v, p.astype(v.dtype), (((0,), (0,)), ((), ())),
                         preferred_element_type=F32)
    acc_ref[...] = a * acc_ref[...] + pv
    m_ref[...] = m_new


def _flash_pipeline(j0, i, qk, sm):
    npairs = (i - j0) // 2
    qk(j0, 0)

    def body(p, carry):
        j = j0 + 2 * p
        qk(j + 1, 1)
        sm(j, 0, False)
        qk(j + 2, 0)
        sm(j + 1, 1, False)
        return carry

    lax.fori_loop(0, npairs, body, 0)
    jn = j0 + 2 * npairs

    @pl.when(jn == i)
    def _():
        sm(i, 0, True)

    @pl.when(jn != i)
    def _():
        qk(i, 1)
        sm(jn, 0, False)
        sm(i, 1, True)


def _init_state(m_ref, l_ref, acc_ref):
    m_ref[...] = jnp.full(m_ref.shape, NEG_BIG, F32)
    l_ref[...] = jnp.zeros(l_ref.shape, F32)
    acc_ref[...] = jnp.zeros(acc_ref.shape, F32)


def _fox_kernel(jmin_ref, q_ref, qaux_ref, k_ref, kaux_ref, v_ref, o_ref,
                m_ref, l_ref, acc_ref, sa_ref, sb_ref, *, tb):
    h = pl.program_id(0)
    i = pl.program_id(1)
    qf = jnp.concatenate([q_ref[...], qaux_ref[0]], axis=1)
    _init_state(m_ref, l_ref, acc_ref)
    s_refs = (sa_ref, sb_ref)

    def qk(j, slot):
        k0 = pl.multiple_of(j * tb, tb)
        kf = jnp.concatenate([k_ref[pl.ds(k0, tb), :], kaux_ref[0, pl.ds(k0, tb), :]], axis=1)
        s_refs[slot][...] = lax.dot_general(kf, qf, (((1,), (1,)), ((), ())),
                                            preferred_element_type=F32)

    def sm(j, slot, masked):
        s = s_refs[slot][...]
        if masked:
            kk = lax.broadcasted_iota(jnp.int32, (tb, tb), 0)
            qq = lax.broadcasted_iota(jnp.int32, (tb, tb), 1)
            s = jnp.where(kk <= qq, s, NEG_BIG)
        v = v_ref[pl.ds(pl.multiple_of(j * tb, tb), tb), :]
        _flash_step(v, s, m_ref, l_ref, acc_ref)

    _flash_pipeline(jmin_ref[h * pl.num_programs(1) + i], i, qk, sm)
    o = acc_ref[...] / l_ref[...]
    o_ref[...] = o.T.astype(o_ref.dtype)


def _fox_attention(jmin, g16, qaux, kaux, tb):
    s = g16.shape[0]
    grid_spec = pltpu.PrefetchScalarGridSpec(
        num_scalar_prefetch=1,
        grid=(FOX_HEADS, s // tb),
        in_specs=[pl.BlockSpec((tb, LANES), lambda h, i, jm: (i, G16_QA + h)),
                  pl.BlockSpec((1, tb, LANES), lambda h, i, jm: (h, i, 0)),
                  pl.BlockSpec((s, LANES), lambda h, i, jm: (0, G16_KA + h)),
                  pl.BlockSpec((1, s, LANES), lambda h, i, jm: (h, 0, 0)),
                  pl.BlockSpec((s, LANES), lambda h, i, jm: (0, G16_VA + h))],
        out_specs=pl.BlockSpec((tb, LANES), lambda h, i, jm: (i, h)),
        scratch_shapes=[pltpu.VMEM((1, tb), F32), pltpu.VMEM((1, tb), F32),
                        pltpu.VMEM((FOX_HD, tb), F32),
                        pltpu.VMEM((tb, tb), F32), pltpu.VMEM((tb, tb), F32)],
    )
    return pl.pallas_call(
        functools.partial(_fox_kernel, tb=tb),
        grid_spec=grid_spec,
        out_shape=jax.ShapeDtypeStruct((s, FOX_W), BF16),
        compiler_params=_cparams(("parallel", "parallel")),
        name="fox_attention",
    )(jmin, g16, qaux, g16, kaux, g16)


def _diff_kernel(q_ref, k_ref, v_ref, lamv_ref, g_ref, o_ref, m_ref, l_ref, acc_ref,
                 sa_ref, sb_ref, *, tb, lam_init):
    i = pl.program_id(1)
    q = q_ref[...].astype(F32)
    lane = lax.broadcasted_iota(jnp.int32, (tb, LANES), 1)
    qq2 = jnp.concatenate([jnp.where(lane < DIFF_HD, q, 0.0),
                           jnp.where(lane >= DIFF_HD, q, 0.0)], axis=0).astype(BF16)
    _init_state(m_ref, l_ref, acc_ref)
    s_refs = (sa_ref, sb_ref)

    def qk(j, slot):
        k = k_ref[pl.ds(pl.multiple_of(j * tb, tb), tb), :]
        s_refs[slot][...] = lax.dot_general(k, qq2, (((1,), (1,)), ((), ())),
                                            preferred_element_type=F32)

    def sm(j, slot, masked):
        s = s_refs[slot][...]
        if masked:
            kc = lax.broadcasted_iota(jnp.int32, (tb, 2 * tb), 0) // CHUNK
            qc = (lax.broadcasted_iota(jnp.int32, (tb, 2 * tb), 1) % tb) // CHUNK
            s = jnp.where(kc <= qc, s, NEG_BIG)
        v = v_ref[pl.ds(pl.multiple_of(j * tb, tb), tb), :]
        _flash_step(v, s, m_ref, l_ref, acc_ref)

    _flash_pipeline(0, i, qk, sm)

    lv = lamv_ref[...]
    lam = (jnp.exp(jnp.sum(lv[0:1] * lv[1:2], axis=1, keepdims=True))
           - jnp.exp(jnp.sum(lv[2:3] * lv[3:4], axis=1, keepdims=True)) + lam_init)
    on = acc_ref[...] / l_ref[...]
    o = (on[:, :tb] - lam * on[:, tb:]).T
    ms = jnp.mean(o * o, axis=-1, keepdims=True)
    o = o * lax.rsqrt(ms + LN_EPS) * g_ref[...] * (1.0 - lam_init)
    o_ref[...] = o.astype(o_ref.dtype)


def _diff_attention(qk16, g16, lamv, subg, lam_init, tb):
    s = g16.shape[0]
    return pl.pallas_call(
        functools.partial(_diff_kernel, tb=tb, lam_init=lam_init),
        grid=(DIFF_HEADS, s // tb),
        in_specs=[pl.BlockSpec((tb, LANES), lambda h, i: (i, h)),
                  pl.BlockSpec((s, LANES), lambda h, i: (0, DIFF_HEADS + h)),
                  pl.BlockSpec((s, LANES), lambda h, i: (0, G16_VC + h)),
                  pl.BlockSpec((4, LANES), lambda h, i: (0, 0)),
                  pl.BlockSpec((1, LANES), lambda h, i: (0, 0))],
        out_specs=pl.BlockSpec((tb, LANES), lambda h, i: (i, h)),
        out_shape=jax.ShapeDtypeStruct((s, DIFF_W), BF16),
        scratch_shapes=[pltpu.VMEM((1, 2 * tb), F32), pltpu.VMEM((1, 2 * tb), F32),
                        pltpu.VMEM((DIFF_VD, 2 * tb), F32),
                        pltpu.VMEM((tb, 2 * tb), F32), pltpu.VMEM((tb, 2 * tb), F32)],
        compiler_params=_cparams(("parallel", "parallel")),
        name="diff_attention",
    )(qk16, qk16, g16, lamv, subg)


def _shift_rows(x, d, fill):
    if d % 8 == 0:
        return jnp.concatenate([jnp.full((d, x.shape[1]), fill, x.dtype), x[:x.shape[0] - d]],
                               axis=0)
    rows = lax.broadcasted_iota(jnp.int32, x.shape, 0)
    return jnp.where(rows >= d, pltpu.roll(x, d, 0), fill)


def _lru_kernel(xr_ref, gr_ref, cw_ref, cb_ref, wa_ref, ba_ref, wi_ref, bi_ref, lam_ref,
                o_ref, halo_ref, h_ref, *, tb, sub):
    @pl.when(pl.program_id(1) == 0)
    def _():
        halo_ref[...] = jnp.zeros_like(halo_ref)
        h_ref[...] = jnp.zeros_like(h_ref)

    x = xr_ref[...]
    xe = jnp.concatenate([halo_ref[...], x], axis=0)
    halo_ref[...] = x[tb - 8:tb, :]
    cw = cw_ref[...]
    xc = cb_ref[...] + cw[CONV_W - 1:CONV_W, :] * x
    for j in range(CONV_W - 1):
        sh = CONV_W - 1 - j
        xc = xc + cw[j:j + 1, :] * pltpu.roll(xe, sh, 0)[8:8 + tb, :]

    xcb = xc.astype(BF16)
    r = jax.nn.sigmoid(jnp.dot(xcb, wa_ref[0], preferred_element_type=F32) + ba_ref[...])
    ig = jax.nn.sigmoid(jnp.dot(xcb, wi_ref[0], preferred_element_type=F32) + bi_ref[...])
    lam = lam_ref[...]
    ls = jnp.minimum(lam, 0.0) - jnp.log1p(jnp.exp(-jnp.abs(lam)))
    log_a = LRU_C * r * ls
    a = jnp.exp(log_a)
    z2 = 2.0 * log_a
    e2 = jnp.exp(z2)
    small = jnp.where(e2 == 1.0, -z2, (1.0 - e2) * z2 / jnp.log(e2))
    neg_expm1 = jnp.where(z2 < -1.0, 1.0 - e2, small)
    u = jnp.sqrt(neg_expm1) * (ig * xc)

    h = h_ref[...]
    for c in range(tb // sub):
        ac = a[c * sub:(c + 1) * sub, :]
        uc = u[c * sub:(c + 1) * sub, :]
        d = 1
        while d < sub:
            uc = ac * _shift_rows(uc, d, 0.0) + uc
            ac = ac * _shift_rows(ac, d, 1.0)
            d *= 2
        hc = uc + ac * h
        h = hc[sub - 1:sub, :]
        g = gr_ref[c * sub:(c + 1) * sub, :]
        gelu = 0.5 * g * (1.0 + jnp.tanh(math.sqrt(2.0 / math.pi) * (g + 0.044715 * (g * g * g))))
        o_ref[c * sub:(c + 1) * sub, :] = (gelu * hc).astype(o_ref.dtype)
    h_ref[...] = h


def _lru_branch(g32, cw, cb, wa, ba, wi, bi, lam, tb, sub):
    s = g32.shape[0]
    vec = lambda c, i: (0, c)
    return pl.pallas_call(
        functools.partial(_lru_kernel, tb=tb, sub=sub),
        grid=(LRU_BLOCKS, s // tb),
        in_specs=[pl.BlockSpec((tb, LANES), lambda c, i: (i, G32_XR + c)),
                  pl.BlockSpec((tb, LANES), lambda c, i: (i, G32_GR + c)),
                  pl.BlockSpec((CONV_W, LANES), vec),
                  pl.BlockSpec((1, LANES), vec),
                  pl.BlockSpec((1, LRU_BW, LRU_BW), lambda c, i: (c, 0, 0)),
                  pl.BlockSpec((1, LANES), vec),
                  pl.BlockSpec((1, LRU_BW, LRU_BW), lambda c, i: (c, 0, 0)),
                  pl.BlockSpec((1, LANES), vec),
                  pl.BlockSpec((1, LANES), vec)],
        out_specs=pl.BlockSpec((tb, LANES), lambda c, i: (i, c)),
        out_shape=jax.ShapeDtypeStruct((s, LRU_W), BF16),
        scratch_shapes=[pltpu.VMEM((8, LANES), F32), pltpu.VMEM((1, LANES), F32)],
        compiler_params=_cparams(("parallel", "arbitrary")),
        name="rg_lru",
    )(g32, g32, cw, cb, wa, ba, wi, bi, lam)


def _layer_norm_rows(y, g, b):
    mu = jnp.mean(y, axis=-1, keepdims=True)
    yc = y - mu
    var = jnp.mean(yc * yc, axis=-1, keepdims=True)
    return yc * lax.rsqrt(var + LN_EPS) * g + b


def _oproj_ln_kernel(xa_ref, xb_ref, xc_ref, wa_ref, wb_ref, wc_ref, h_ref, g_ref, b_ref,
                     o32_ref, oslab_ref):
    mix = (jnp.dot(xa_ref[...], wa_ref[...], preferred_element_type=F32)
           + jnp.dot(xb_ref[...], wb_ref[...], preferred_element_type=F32)
           + jnp.dot(xc_ref[...], wc_ref[...], preferred_element_type=F32))
    o = _layer_norm_rows(ALPHA * h_ref[...] + mix, g_ref[...], b_ref[...])
    o32_ref[...] = o
    _store_slabs(oslab_ref, o)


def _oproj_ln(xa, xb, xc, wa, wb, wc, h, g, b, tm):
    s = h.shape[0]
    row = lambda i: (i, 0)
    fixed = lambda i: (0, 0)
    return pl.pallas_call(
        _oproj_ln_kernel,
        grid=(s // tm,),
        in_specs=[pl.BlockSpec((tm, FOX_W), row), pl.BlockSpec((tm, LRU_W), row),
                  pl.BlockSpec((tm, DIFF_W), row),
                  pl.BlockSpec((FOX_W, D_MODEL), fixed), pl.BlockSpec((LRU_W, D_MODEL), fixed),
                  pl.BlockSpec((DIFF_W, D_MODEL), fixed),
                  pl.BlockSpec((tm, D_MODEL), row),
                  pl.BlockSpec((1, D_MODEL), fixed), pl.BlockSpec((1, D_MODEL), fixed)],
        out_specs=[pl.BlockSpec((tm, D_MODEL), row), pl.BlockSpec((tm * SLAB_ROWS, LANES), row)],
        out_shape=[jax.ShapeDtypeStruct((s, D_MODEL), F32),
                   jax.ShapeDtypeStruct((s * SLAB_ROWS, LANES), F32)],
        compiler_params=_cparams(("parallel",)),
        name="oproj_ln1",
    )(xa, xb, xc, wa, wb, wc, h, g, b)


def _router_kernel(h_ref, whi_ref, wlo_ref, b_ref, o_ref):
    h = h_ref[...]
    hi = h.astype(BF16)
    lo = (h - hi.astype(F32)).astype(BF16)
    whi = whi_ref[...]
    logits = (jnp.dot(hi, whi, preferred_element_type=F32)
              + jnp.dot(hi, wlo_ref[...], preferred_element_type=F32)
              + jnp.dot(lo, whi, preferred_element_type=F32)) + b_ref[...]
    lane = lax.broadcasted_iota(jnp.int32, logits.shape, 1)
    big = jnp.int32(1 << 20)

    def first_lane(cond):
        return jnp.min(jnp.where(cond, lane, big), axis=1, keepdims=True)

    gm = lane < N_GROUPS
    gl = jnp.where(gm, logits, -jnp.inf)
    gmax = jnp.max(gl, axis=1, keepdims=True)
    gexp = jnp.where(gm, jnp.exp(logits - gmax), 0.0)
    gprob = gexp / jnp.sum(gexp, axis=1, keepdims=True)
    gidx = first_lane(gl == gmax)
    g_weight = jnp.sum(jnp.where(lane == gidx, gprob, 0.0), axis=1, keepdims=True)

    e0 = N_GROUPS + EXPERTS_PER_GROUP * gidx
    em = jnp.logical_and(lane >= e0, lane < e0 + EXPERTS_PER_GROUP)
    el = jnp.where(em, logits, -jnp.inf)
    emax = jnp.max(el, axis=1, keepdims=True)
    eexp = jnp.where(em, jnp.exp(logits - emax), 0.0)
    eprob = jnp.where(em, eexp / jnp.sum(eexp, axis=1, keepdims=True), -1.0)
    p1 = jnp.max(eprob, axis=1, keepdims=True)
    i1 = first_lane(eprob == p1)
    eprob2 = jnp.where(lane == i1, -1.0, eprob)
    p2 = jnp.max(eprob2, axis=1, keepdims=True)
    i2 = first_lane(eprob2 == p2)
    den = p1 + p2
    gate1 = g_weight * (p1 / den)
    gate2 = g_weight * (p2 / den)
    id1 = (i1 - N_GROUPS).astype(F32)
    id2 = (i2 - N_GROUPS).astype(F32)
    o_ref[...] = jnp.where(lane == 0, id1,
                           jnp.where(lane == 1, id2,
                                     jnp.where(lane == 2, gate1,
                                               jnp.where(lane == 3, gate2, 0.0))))


def _router(h, whi, wlo, bias, tm):
    s = h.shape[0]
    fixed = lambda i: (0, 0)
    return pl.pallas_call(
        _router_kernel,
        grid=(s // tm,),
        in_specs=[pl.BlockSpec((tm, D_MODEL), lambda i: (i, 0)),
                  pl.BlockSpec((D_MODEL, LANES), fixed), pl.BlockSpec((D_MODEL, LANES), fixed),
                  pl.BlockSpec((1, LANES), fixed)],
        out_specs=pl.BlockSpec((tm, LANES), lambda i: (i, 0)),
        out_shape=jax.ShapeDtypeStruct((s, LANES), F32),
        compiler_params=_cparams(("parallel",)),
        name="moe_router",
    )(h, whi, wlo, bias)


SLAB_ROWS = D_MODEL // LANES
SLAB_PITCH = SLAB_ROWS + 4


def _store_slabs(o_ref, x):
    rows = x.shape[0]
    for c in range(SLAB_ROWS):
        o_ref[pl.ds(c, rows, stride=SLAB_ROWS), :] = x[:, c * LANES:(c + 1) * LANES]


def _load_slabs(buf_ref, rows):
    return jnp.concatenate([buf_ref[pl.ds(c, rows, stride=SLAB_PITCH), :]
                            for c in range(SLAB_ROWS)], axis=1)


GATHER_GROUP = 8


def _issue_slab_gather(row_of, ngroups, src_hbm, dst_ref, sem):
    def body(gi, carry):
        for u in range(GATHER_GROUP):
            r = gi * GATHER_GROUP + u
            src0 = pl.multiple_of(row_of(r) * SLAB_ROWS, SLAB_ROWS)
            dst0 = pl.multiple_of(r * SLAB_PITCH, 4)
            pltpu.make_async_copy(src_hbm.at[pl.ds(src0, SLAB_ROWS), :],
                                  dst_ref.at[pl.ds(dst0, SLAB_ROWS), :], sem).start()
        return carry
    lax.fori_loop(0, ngroups, body, 0)


def _wait_slab_gather(ngroups, buf_ref, sem):
    part = buf_ref.at[pl.ds(0, ngroups * (GATHER_GROUP * SLAB_ROWS)), :]
    pltpu.make_async_copy(part, part, sem).wait()


def _moe_kernel(te_ref, nt_ref, start_ref, cnt_ref, tok_ref, h_hbm, wg_ref, wu_ref, wd_ref,
                y_ref, xbuf, sem, *, tm):
    del te_ref
    t = pl.program_id(0)
    nt = nt_ref[0]
    slot = t % 2
    last = tok_ref.shape[0] - 1

    def groups(tile):
        return (cnt_ref[tile] + (GATHER_GROUP - 1)) // GATHER_GROUP

    def issue(tile, sl):
        s0 = start_ref[tile]
        _issue_slab_gather(lambda r: tok_ref[jnp.minimum(s0 + r, last)], groups(tile), h_hbm,
                           xbuf.at[sl], sem.at[sl])

    @pl.when(t == 0)
    def _():
        xbuf[...] = jnp.zeros_like(xbuf)
        issue(0, 0)

    @pl.when(t + 1 < nt)
    def _():
        issue(t + 1, 1 - slot)

    @pl.when(t < nt)
    def _():
        _wait_slab_gather(groups(t), xbuf.at[slot], sem.at[slot])
        x = _load_slabs(xbuf.at[slot], tm).astype(BF16)
        hg = jnp.dot(x, wg_ref[0].astype(BF16), preferred_element_type=F32)
        hu = jnp.dot(x, wu_ref[0].astype(BF16), preferred_element_type=F32)
        act = (hg * jax.nn.sigmoid(hg)) * hu
        y = jnp.dot(act.astype(BF16), wd_ref[0].astype(BF16), preferred_element_type=F32)
        _store_slabs(y_ref, y)

    @pl.when(t >= nt)
    def _():
        y_ref[...] = jnp.zeros_like(y_ref)


def _moe_grouped(tile_expert, ntiles, tile_start, tile_count, tok_sorted, h_slab, wg, wu, wd, tm):
    nt_max = tile_expert.shape[0]
    imap = lambda t, te, nt, st, cn, tok: (te[t], 0, 0)
    grid_spec = pltpu.PrefetchScalarGridSpec(
        num_scalar_prefetch=5,
        grid=(nt_max,),
        in_specs=[pl.BlockSpec(memory_space=pl.ANY),
                  pl.BlockSpec((1, D_MODEL, D_FF_EXPERT), imap),
                  pl.BlockSpec((1, D_MODEL, D_FF_EXPERT), imap),
                  pl.BlockSpec((1, D_FF_EXPERT, D_MODEL), imap)],
        out_specs=pl.BlockSpec((tm * SLAB_ROWS, LANES), lambda t, te, nt, st, cn, tok: (t, 0)),
        scratch_shapes=[pltpu.VMEM((2, tm * SLAB_PITCH, LANES), F32),
                        pltpu.SemaphoreType.DMA((2,))],
    )
    return pl.pallas_call(
        functools.partial(_moe_kernel, tm=tm),
        grid_spec=grid_spec,
        out_shape=jax.ShapeDtypeStruct((nt_max * tm * SLAB_ROWS, LANES), F32),
        compiler_params=_cparams(("arbitrary",)),
        name="moe_experts",
    )(tile_expert, ntiles, tile_start, tile_count, tok_sorted, h_slab, wg, wu, wd)


def _combine_ln_kernel(pos_ref, y_hbm, h_ref, meta_ref, g_ref, b_ref, o32_ref, o16_ref,
                       ybuf, sem, *, tm):
    t = pl.program_id(0)
    nt = pl.num_programs(0)
    slot = t % 2
    ngroups = tm // GATHER_GROUP

    def issue(tile, sl):
        for j in range(2):
            _issue_slab_gather(lambda r, j=j: pos_ref[(tile * tm + r) * 2 + j], ngroups, y_hbm,
                               ybuf.at[sl, j], sem.at[sl])

    @pl.when(t == 0)
    def _():
        issue(0, 0)

    @pl.when(t + 1 < nt)
    def _():
        issue(t + 1, 1 - slot)

    _wait_slab_gather(ngroups, ybuf.at[slot, 0], sem.at[slot])
    _wait_slab_gather(ngroups, ybuf.at[slot, 1], sem.at[slot])
    meta = meta_ref[...]
    ffn = (meta[:, 2:3] * _load_slabs(ybuf.at[slot, 0], tm)
           + meta[:, 3:4] * _load_slabs(ybuf.at[slot, 1], tm))
    o = _layer_norm_rows(ALPHA * h_ref[...] + ffn, g_ref[...], b_ref[...])
    o32_ref[...] = o
    o16_ref[...] = o.astype(BF16)


def _combine_ln(pos, y_slab, h, meta, g, b, tm):
    s = h.shape[0]
    row = lambda t, pos: (t, 0)
    fixed = lambda t, pos: (0, 0)
    grid_spec = pltpu.PrefetchScalarGridSpec(
        num_scalar_prefetch=1,
        grid=(s // tm,),
        in_specs=[pl.BlockSpec(memory_space=pl.ANY),
                  pl.BlockSpec((tm, D_MODEL), row),
                  pl.BlockSpec((tm, LANES), row),
                  pl.BlockSpec((1, D_MODEL), fixed), pl.BlockSpec((1, D_MODEL), fixed)],
        out_specs=[pl.BlockSpec((tm, D_MODEL), row), pl.BlockSpec((tm, D_MODEL), row)],
        scratch_shapes=[pltpu.VMEM((2, 2, tm * SLAB_PITCH, LANES), F32),
                        pltpu.SemaphoreType.DMA((2,))],
    )
    return pl.pallas_call(
        functools.partial(_combine_ln_kernel, tm=tm),
        grid_spec=grid_spec,
        out_shape=[jax.ShapeDtypeStruct((s, D_MODEL), F32),
                   jax.ShapeDtypeStruct((s, D_MODEL), BF16)],
        compiler_params=_cparams(("arbitrary",)),
        name="combine_ln2",
    )(pos, y_slab, h, meta, g, b)


def _dispatch_plan(meta, tm):
    t = meta.shape[0]
    eid = meta[:, 0:2].astype(jnp.int32).reshape(-1)
    onehot = (eid[:, None] == jnp.arange(N_EXPERTS, dtype=jnp.int32)[None, :]).astype(jnp.int32)
    csum = jnp.cumsum(onehot, axis=0)
    counts = csum[-1]
    padded = ((counts + tm - 1) // tm) * tm
    pend = jnp.cumsum(padded)
    poff = pend - padded
    off = jnp.cumsum(counts) - counts
    pos = jnp.sum(onehot * (csum + poff[None, :]), axis=1) - 1
    p_max = 2 * t + N_EXPERTS * tm
    nt_max = p_max // tm
    ntiles = (pend[-1] // tm).astype(jnp.int32)
    tile_ids = jnp.minimum(jnp.arange(nt_max, dtype=jnp.int32), ntiles - 1)
    tile_expert = jnp.sum((pend[None, :] // tm <= tile_ids[:, None]).astype(jnp.int32), axis=1)
    tile_expert = jnp.minimum(tile_expert, N_EXPERTS - 1).astype(jnp.int32)
    tok_sorted = (jnp.argsort(eid, stable=True) // 2).astype(jnp.int32)
    k0 = jnp.arange(nt_max, dtype=jnp.int32) * tm - poff[tile_expert]
    tile_start = jnp.clip(off[tile_expert] + k0, 0, 2 * t - 1).astype(jnp.int32)
    tile_count = jnp.clip(counts[tile_expert] - k0, 0, tm).astype(jnp.int32)
    return (tile_expert, ntiles.reshape(1), tile_start, tile_count, tok_sorted,
            pos.astype(jnp.int32))


def _pad_lanes(v, width=LANES):
    v = v.reshape(1, -1).astype(F32)
    return jnp.pad(v, ((0, 0), (0, width - v.shape[1])))


def _layer(l, h32, h16, pos_f, invf, p, stacked, tb, tm_moe):
    s = h32.shape[0]
    w16, w32 = _prep_w_in(stacked["w_in"], l)
    s16 = jnp.concatenate([jnp.full((1, FOX_W), FOX_HD ** -0.5 * LOG2E, F32),
                           jnp.ones((1, G16_W - FOX_W), F32)], axis=1)
    s32 = jnp.ones((1, G32_W), F32)
    tm_proj = min(s, 4096)
    g16 = _matmul(h16, w16, s16, BF16, tm_proj, 256, "in_proj_bf16")
    g32 = _matmul(h16, w32, s32, F32, tm_proj, 256, "in_proj_f32")

    tb_prep = min(s, 256)
    qaux, kaux, stat = _fox_prep(g32, g16, _pad_lanes(p["b_f"]), tb_prep)
    jmin = _fox_skip_table(stat, tb // tb_prep)
    out_a = _fox_attention(jmin, g16, qaux, kaux, tb)

    out_b = _lru_branch(g32, p["conv_w"], p["conv_b"].reshape(1, -1),
                        p["w_a"].astype(BF16), p["b_a"].reshape(1, -1),
                        p["w_i"].astype(BF16), p["b_i"].reshape(1, -1),
                        p["lru_lambda"].reshape(1, -1), min(s, 1024), min(s, 256))

    qk16 = _rope(g32, pos_f, invf, min(s, 512))
    lamv = jnp.concatenate([_pad_lanes(p["lam_q1"]), _pad_lanes(p["lam_k1"]),
                            _pad_lanes(p["lam_q2"]), _pad_lanes(p["lam_k2"])], axis=0)
    lam_init = 0.8 - 0.6 * math.exp(-0.3 * l)
    out_c = _diff_attention(qk16, g16, lamv, p["subln_g"].reshape(1, -1), lam_init, tb)

    w_o = p["w_o"].astype(BF16)
    h1_32, h1_slab = _oproj_ln(out_a, out_b, out_c, w_o[:FOX_W], w_o[FOX_W:FOX_W + LRU_W],
                               w_o[FOX_W + LRU_W:], h32, p["ln1_g"].reshape(1, -1),
                               p["ln1_b"].reshape(1, -1), min(s, 512))

    w_rt = jnp.concatenate([p["w_group"], p["w_router"],
                            jnp.zeros((D_MODEL, LANES - N_GROUPS - N_EXPERTS), F32)], axis=1)
    w_rt_hi = w_rt.astype(BF16)
    w_rt_lo = (w_rt - w_rt_hi.astype(F32)).astype(BF16)
    b_rt = _pad_lanes(jnp.concatenate([p["b_group"], p["b_router"]]))
    meta = _router(h1_32, w_rt_hi, w_rt_lo, b_rt, min(s, 512))

    tile_expert, ntiles, tile_start, tile_count, tok_sorted, pos = _dispatch_plan(meta, tm_moe)
    y_slab = _moe_grouped(tile_expert + l * N_EXPERTS, ntiles, tile_start, tile_count, tok_sorted,
                          h1_slab, stacked["w_gate"], stacked["w_up"], stacked["w_down"], tm_moe)
    return _combine_ln(pos, y_slab, h1_32, meta, p["ln2_g"].reshape(1, -1),
                       p["ln2_b"].reshape(1, -1), min(s, 256))


def _forward(x, positions, params, tb, tm_moe):
    bsz, s, d = x.shape
    assert bsz == 1 and d == D_MODEL
    h32 = x.reshape(s, d)
    h16 = h32.astype(BF16)
    half = ROT_DIM // 2
    inv_freq = ROPE_THETA ** (-jnp.arange(half, dtype=F32) * 2.0 / ROT_DIM)
    dlane = jnp.arange(LANES) % DIFF_HD
    invf = jnp.where(dlane < ROT_DIM, inv_freq[dlane % half], 0.0).reshape(1, LANES).astype(F32)
    pos_f = jnp.broadcast_to(positions.astype(F32).reshape(s, 1), (s, LANES))
    big = ("w_in", "w_gate", "w_up", "w_down")
    stacked = {
        "w_in": params["w_in"],
        "w_gate": params["w_gate"].reshape(DEPTH * N_EXPERTS, D_MODEL, D_FF_EXPERT),
        "w_up": params["w_up"].reshape(DEPTH * N_EXPERTS, D_MODEL, D_FF_EXPERT),
        "w_down": params["w_down"].reshape(DEPTH * N_EXPERTS, D_FF_EXPERT, D_MODEL),
    }
    for l in range(DEPTH):
        p = {k: v[l] for k, v in params.items() if k not in big}
        h32, h16 = _layer(l, h32, h16, pos_f, invf, p, stacked, tb, tm_moe)
    return h32.reshape(bsz, s, d)


def kernel(x, positions, w_in, b_f, conv_w, conv_b, w_a, b_a, w_i, b_i, lru_lambda, lam_q1, lam_k1, lam_q2, lam_k2, subln_g, w_o, ln1_g, ln1_b, w_group, b_group, w_router, b_router, w_gate, w_up, w_down, ln2_g, ln2_b):
    params = dict(w_in=w_in, b_f=b_f, conv_w=conv_w, conv_b=conv_b, w_a=w_a, b_a=b_a, w_i=w_i,
                  b_i=b_i, lru_lambda=lru_lambda, lam_q1=lam_q1, lam_k1=lam_k1, lam_q2=lam_q2,
                  lam_k2=lam_k2, subln_g=subln_g, w_o=w_o, ln1_g=ln1_g, ln1_b=ln1_b,
                  w_group=w_group, b_group=b_group, w_router=w_router, b_router=b_router,
                  w_gate=w_gate, w_up=w_up, w_down=w_down, ln2_g=ln2_g, ln2_b=ln2_b)
    s = x.shape[1]
    return _forward(x, positions, params, tb=min(s, 512), tm_moe=min(s, 256))
```

```python
import functools
import math

import jax
import jax.numpy as jnp
from jax import lax
from jax.experimental import pallas as pl
from jax.experimental.pallas import tpu as pltpu

F32 = jnp.float32
BF16 = jnp.bfloat16

D_MODEL = 2048
DEPTH = 2
CHUNK = 64
FOX_HEADS = 6
FOX_HD = 128
FOX_W = FOX_HEADS * FOX_HD
LRU_W = 768
LRU_BLOCKS = 6
LRU_BW = LRU_W // LRU_BLOCKS
CONV_W = 4
LRU_C = 8.0
DIFF_HEADS = 4
DIFF_HD = 64
DIFF_VD = 2 * DIFF_HD
DIFF_W = DIFF_HEADS * DIFF_VD
ROT_DIM = DIFF_HD // 4
ROPE_THETA = 500000.0
N_GROUPS = 4
EXPERTS_PER_GROUP = 8
N_EXPERTS = N_GROUPS * EXPERTS_PER_GROUP
D_FF_EXPERT = 512
ALPHA = (2.0 * DEPTH) ** 0.25
LN_EPS = 1e-5

LANES = 128
NEG_BIG = -1e30
LOG2E = math.log2(math.e)
SKIP_BITS = 72.0
VMEM_LIMIT = 56 * 1024 * 1024

G16_W = 3 * FOX_W + DIFF_W
G32_W = 2 * DIFF_W + 2 * LRU_W + 2 * LANES
G16_QA, G16_KA, G16_VA, G16_VC = 0, 6, 12, 18
G32_XR, G32_GR, G32_FA = 8, 14, 20


def _cparams(sem, vmem=VMEM_LIMIT):
    return pltpu.CompilerParams(dimension_semantics=sem, vmem_limit_bytes=vmem)


def _mm_kernel(x_ref, w_ref, s_ref, o_ref):
    acc = jnp.dot(x_ref[...], w_ref[...], preferred_element_type=F32)
    o_ref[...] = (acc * s_ref[...]).astype(o_ref.dtype)


def _matmul(x, w, s, out_dtype, tm, tn, name):
    m, k = x.shape
    n = w.shape[1]
    return pl.pallas_call(
        _mm_kernel,
        grid=(m // tm, n // tn),
        in_specs=[pl.BlockSpec((tm, k), lambda i, j: (i, 0)),
                  pl.BlockSpec((k, tn), lambda i, j: (0, j)),
                  pl.BlockSpec((1, tn), lambda i, j: (0, j))],
        out_specs=pl.BlockSpec((tm, tn), lambda i, j: (i, j)),
        out_shape=jax.ShapeDtypeStruct((m, n), out_dtype),
        compiler_params=_cparams(("parallel", "parallel")),
        name=name,
    )(x, w, s)


_IN_QA = 0
_IN_KA = _IN_QA + FOX_W
_IN_VA = _IN_KA + FOX_W
_IN_FA = _IN_VA + FOX_W
_IN_XR = _IN_FA + FOX_HEADS
_IN_GR = _IN_XR + LRU_W
_IN_QC = _IN_GR + LRU_W
_IN_KC = _IN_QC + DIFF_W
_IN_VC = _IN_KC + DIFF_W
N_IN = _IN_VC + DIFF_W


def _wprep_kernel(w_ref, o16_ref, o32_ref):
    def put(o_ref, dst, src, width):
        o_ref[:, dst:dst + width] = w_ref[0, :, src:src + width].astype(BF16)

    put(o16_ref, 0, _IN_QA, FOX_W)
    put(o16_ref, FOX_W, _IN_KA, FOX_W)
    put(o16_ref, 2 * FOX_W, _IN_VA, FOX_W)
    put(o16_ref, 3 * FOX_W, _IN_VC, DIFF_W)
    put(o32_ref, 0, _IN_QC, DIFF_W)
    put(o32_ref, DIFF_W, _IN_KC, DIFF_W)
    put(o32_ref, G32_XR * LANES, _IN_XR, LRU_W)
    put(o32_ref, G32_GR * LANES, _IN_GR, LRU_W)
    fa0 = G32_FA * LANES
    o32_ref[:, fa0:fa0 + 2 * LANES] = jnp.zeros((o32_ref.shape[0], 2 * LANES), BF16)
    put(o32_ref, fa0, _IN_FA, FOX_HEADS)


def _prep_w_in(w_in, layer, tk=256):
    return pl.pallas_call(
        _wprep_kernel,
        grid=(D_MODEL // tk,),
        in_specs=[pl.BlockSpec((1, tk, N_IN), lambda i: (layer, i, 0))],
        out_specs=[pl.BlockSpec((tk, G16_W), lambda i: (i, 0)),
                   pl.BlockSpec((tk, G32_W), lambda i: (i, 0))],
        out_shape=[jax.ShapeDtypeStruct((D_MODEL, G16_W), BF16),
                   jax.ShapeDtypeStruct((D_MODEL, G32_W), BF16)],
        compiler_params=_cparams(("parallel",)),
        name="w_in_regroup",
    )(w_in)


def _split3(x):
    hi = x.astype(BF16)
    r1 = x - hi.astype(F32)
    mid = r1.astype(BF16)
    lo = (r1 - mid.astype(F32)).astype(BF16)
    return hi, mid, lo


def _fox_prep_kernel(fa_ref, bf_ref, q_ref, k_ref, qaux_ref, kaux_ref, stat_ref, carry_ref):
    @pl.when(pl.program_id(0) == 0)
    def _():
        carry_ref[...] = jnp.zeros_like(carry_ref)

    z = fa_ref[...] + bf_ref[...]
    lf = (jnp.minimum(z, 0.0) - jnp.log1p(jnp.exp(-jnp.abs(z)))) * LOG2E
    t = z.shape[0]
    row = lax.broadcasted_iota(jnp.int32, (t, t), 0)
    col = lax.broadcasted_iota(jnp.int32, (t, t), 1)
    tri = jnp.where(row >= col, 1.0, 0.0).astype(BF16)
    hi, mid, lo = _split3(lf)
    cs = (jnp.dot(tri, hi, preferred_element_type=F32)
          + jnp.dot(tri, mid, preferred_element_type=F32)
          + jnp.dot(tri, lo, preferred_element_type=F32)) + carry_ref[...]
    carry_ref[...] = cs[t - 1:t, :]

    lane = lax.broadcasted_iota(jnp.int32, (t, LANES), 1)
    lane1 = lax.broadcasted_iota(jnp.int32, (1, LANES), 1)
    qn = jnp.zeros((1, LANES), F32)
    kn = jnp.zeros((1, LANES), F32)
    for h in range(FOX_HEADS):
        c = jnp.broadcast_to(jnp.sum(jnp.where(lane == h, cs, 0.0), axis=1, keepdims=True),
                             (t, LANES))
        c_hi = c.astype(BF16).astype(F32)
        r1 = c - c_hi
        c_mid = r1.astype(BF16).astype(F32)
        c_lo = (r1 - c_mid).astype(BF16).astype(F32)
        kaux = jnp.where(lane == 0, c_hi, jnp.where(lane == 1, c_mid, jnp.where(
            lane == 2, c_lo, jnp.where(lane < 6, 1.0, 0.0))))
        qaux = jnp.where(lane < 3, -1.0, jnp.where(lane == 3, c_hi, jnp.where(
            lane == 4, c_mid, jnp.where(lane == 5, c_lo, 0.0))))
        kaux_ref[h] = kaux.astype(BF16)
        qaux_ref[h] = qaux.astype(BF16)
        qh = q_ref[:, h * FOX_HD:(h + 1) * FOX_HD].astype(F32)
        kh = k_ref[:, h * FOX_HD:(h + 1) * FOX_HD].astype(F32)
        q2 = jnp.max(jnp.sum(qh * qh, axis=1, keepdims=True), axis=0, keepdims=True)
        k2 = jnp.max(jnp.sum(kh * kh, axis=1, keepdims=True), axis=0, keepdims=True)
        qn = jnp.where(lane1 == h, q2, qn)
        kn = jnp.where(lane1 == h, k2, kn)
    stat_ref[0] = jnp.concatenate([qn, kn, cs[0:1, :], cs[t - 1:t, :],
                                   jnp.zeros((4, LANES), F32)], axis=0)


def _fox_prep(g32, g16, bf_pad, tb):
    s = g32.shape[0]
    qk_blocks = FOX_W // LANES
    return pl.pallas_call(
        _fox_prep_kernel,
        grid=(s // tb,),
        in_specs=[pl.BlockSpec((tb, LANES), lambda i: (i, G32_FA)),
                  pl.BlockSpec((1, LANES), lambda i: (0, 0)),
                  pl.BlockSpec((tb, FOX_W), lambda i: (i, G16_QA // qk_blocks)),
                  pl.BlockSpec((tb, FOX_W), lambda i: (i, G16_KA // qk_blocks))],
        out_specs=[pl.BlockSpec((FOX_HEADS, tb, LANES), lambda i: (0, i, 0)),
                   pl.BlockSpec((FOX_HEADS, tb, LANES), lambda i: (0, i, 0)),
                   pl.BlockSpec((1, 8, LANES), lambda i: (i, 0, 0))],
        out_shape=[jax.ShapeDtypeStruct((FOX_HEADS, s, LANES), BF16),
                   jax.ShapeDtypeStruct((FOX_HEADS, s, LANES), BF16),
                   jax.ShapeDtypeStruct((s // tb, 8, LANES), F32)],
        scratch_shapes=[pltpu.VMEM((1, LANES), F32)],
        compiler_params=_cparams(("arbitrary",)),
        name="fox_prep",
    )(g32, bf_pad, g16, g16)


def _fox_skip_table(stat, per_block):
    nb = stat.shape[0] // per_block
    st = stat.reshape(nb, per_block, 8, LANES)
    qn = jnp.sqrt(jnp.max(st[:, :, 0, :FOX_HEADS], axis=1))
    kn = jnp.sqrt(jnp.max(st[:, :, 1, :FOX_HEADS], axis=1))
    c_first = st[:, 0, 2, :FOX_HEADS]
    c_last = st[:, per_block - 1, 3, :FOX_HEADS]
    kmax = jnp.max(kn, axis=0, keepdims=True)
    bound = 1.01 * qn * (kmax + kn) + c_first
    skip = (bound[:, None, :] - c_last[None, :, :]) < -SKIP_BITS
    jj = jnp.arange(nb)
    skip = jnp.logical_and(skip, (jj[None, :] < jj[:, None])[:, :, None])
    lead = jnp.cumsum(1 - skip.astype(jnp.int32), axis=1) == 0
    return jnp.sum(lead.astype(jnp.int32), axis=1).T.reshape(-1)


def _rope_kernel(x_ref, pos_ref, invf_ref, o_ref):
    ang = pos_ref[...] * invf_ref[...]
    c = jnp.cos(ang)
    s = jnp.sin(ang)
    half = ROT_DIM // 2
    d = lax.broadcasted_iota(jnp.int32, (1, LANES), 1) % DIFF_HD
    sa = jnp.where(d >= half, s, 0.0)
    sb = jnp.where(d < half, -s, 0.0)
    nblk = x_ref.shape[1] // LANES
    for j in range(nblk):
        t = x_ref[:, j * LANES:(j + 1) * LANES]
        r = t * c + pltpu.roll(t, half, 1) * sa + pltpu.roll(t, LANES - half, 1) * sb
        if j < nblk // 2:
            r = r * (DIFF_HD ** -0.5 * LOG2E)
        o_ref[:, j * LANES:(j + 1) * LANES] = r.astype(o_ref.dtype)


def _rope(g32, pos_f, invf, tb):
    s = g32.shape[0]
    w = 2 * DIFF_W
    return pl.pallas_call(
        _rope_kernel,
        grid=(s // tb,),
        in_specs=[pl.BlockSpec((tb, w), lambda i: (i, 0)),
                  pl.BlockSpec((tb, LANES), lambda i: (i, 0)),
                  pl.BlockSpec((1, LANES), lambda i: (0, 0))],
        out_specs=pl.BlockSpec((tb, w), lambda i: (i, 0)),
        out_shape=jax.ShapeDtypeStruct((s, w), BF16),
        compiler_params=_cparams(("parallel",)),
        name="diff_rope",
    )(g32, pos_f, invf)


def _flash_step(v, s, m_ref, l_ref, acc_ref):
    m_prev = m_ref[...]
    m_new = jnp.maximum(m_prev, jnp.max(s, axis=0, keepdims=True))
    a = jnp.exp2(m_prev - m_new)
    p = jnp.exp2(s - m_new)
    l_ref[...] = a * l_ref[...] + jnp.sum(p, axis=0, keepdims=True)
    pv = lax.dot_general(v, p.astype(v.dtype), (((0,), (0,)), ((), ())),
                         preferred_element_type=F32)
    acc_ref[...] = a * acc_ref[...] + pv
    m_ref[...] = m_new


def _flash_pipeline(j0, i, qk, sm):
    npairs = (i - j0) // 2
    qk(j0, 0)

    def body(p, carry):
        j = j0 + 2 * p
        qk(j + 1, 1)
        sm(j, 0, False)
        qk(j + 2, 0)
        sm(j + 1, 1, False)
        return carry

    lax.fori_loop(0, npairs, body, 0)
    jn = j0 + 2 * npairs

    @pl.when(jn == i)
    def _():
        sm(i, 0, True)

    @pl.when(jn != i)
    def _():
        qk(i, 1)
        sm(jn, 0, False)
        sm(i, 1, True)


def _init_state(m_ref, l_ref, acc_ref):
    m_ref[...] = jnp.full(m_ref.shape, NEG_BIG, F32)
    l_ref[...] = jnp.zeros(l_ref.shape, F32)
    acc_ref[...] = jnp.zeros(acc_ref.shape, F32)


def _fox_kernel(jmin_ref, q_ref, qaux_ref, k_ref, kaux_ref, v_ref, o_ref,
                m_ref, l_ref, acc_ref, sa_ref, sb_ref, *, tb):
    h = pl.program_id(0)
    i = pl.program_id(1)
    qf = jnp.concatenate([q_ref[...], qaux_ref[0]], axis=1)
    _init_state(m_ref, l_ref, acc_ref)
    s_refs = (sa_ref, sb_ref)

    def qk(j, slot):
        k0 = pl.multiple_of(j * tb, tb)
        kf = jnp.concatenate([k_ref[pl.ds(k0, tb), :], kaux_ref[0, pl.ds(k0, tb), :]], axis=1)
        s_refs[slot][...] = lax.dot_general(kf, qf, (((1,), (1,)), ((), ())),
                                            preferred_element_type=F32)

    def sm(j, slot, masked):
        s = s_refs[slot][...]
        if masked:
            kk = lax.broadcasted_iota(jnp.int32, (tb, tb), 0)
            qq = lax.broadcasted_iota(jnp.int32, (tb, tb), 1)
            s = jnp.where(kk <= qq, s, NEG_BIG)
        v = v_ref[pl.ds(pl.multiple_of(j * tb, tb), tb), :]
        _flash_step(v, s, m_ref, l_ref, acc_ref)

    _flash_pipeline(jmin_ref[h * pl.num_programs(1) + i], i, qk, sm)
    o = acc_ref[...] / l_ref[...]
    o_ref[...] = o.T.astype(o_ref.dtype)


def _fox_attention(jmin, g16, qaux, kaux, tb):
    s = g16.shape[0]
    grid_spec = pltpu.PrefetchScalarGridSpec(
        num_scalar_prefetch=1,
        grid=(FOX_HEADS, s // tb),
        in_specs=[pl.BlockSpec((tb, LANES), lambda h, i, jm: (i, G16_QA + h)),
                  pl.BlockSpec((1, tb, LANES), lambda h, i, jm: (h, i, 0)),
                  pl.BlockSpec((s, LANES), lambda h, i, jm: (0, G16_KA + h)),
                  pl.BlockSpec((1, s, LANES), lambda h, i, jm: (h, 0, 0)),
                  pl.BlockSpec((s, LANES), lambda h, i, jm: (0, G16_VA + h))],
        out_specs=pl.BlockSpec((tb, LANES), lambda h, i, jm: (i, h)),
        scratch_shapes=[pltpu.VMEM((1, tb), F32), pltpu.VMEM((1, tb), F32),
                        pltpu.VMEM((FOX_HD, tb), F32),
                        pltpu.VMEM((tb, tb), F32), pltpu.VMEM((tb, tb), F32)],
    )
    return pl.pallas_call(
        functools.partial(_fox_kernel, tb=tb),
        grid_spec=grid_spec,
        out_shape=jax.ShapeDtypeStruct((s, FOX_W), BF16),
        compiler_params=_cparams(("parallel", "parallel")),
        name="fox_attention",
    )(jmin, g16, qaux, g16, kaux, g16)


def _diff_kernel(q_ref, k_ref, v_ref, lamv_ref, g_ref, o_ref, m_ref, l_ref, acc_ref,
                 sa_ref, sb_ref, *, tb, lam_init):
    i = pl.program_id(1)
    q = q_ref[...].astype(F32)
    lane = lax.broadcasted_iota(jnp.int32, (tb, LANES), 1)
    qq2 = jnp.concatenate([jnp.where(lane < DIFF_HD, q, 0.0),
                           jnp.where(lane >= DIFF_HD, q, 0.0)], axis=0).astype(BF16)
    _init_state(m_ref, l_ref, acc_ref)
    s_refs = (sa_ref, sb_ref)

    def qk(j, slot):
        k = k_ref[pl.ds(pl.multiple_of(j * tb, tb), tb), :]
        s_refs[slot][...] = lax.dot_general(k, qq2, (((1,), (1,)), ((), ())),
                                            preferred_element_type=F32)

    def sm(j, slot, masked):
        s = s_refs[slot][...]
        if masked:
            kc = lax.broadcasted_iota(jnp.int32, (tb, 2 * tb), 0) // CHUNK
            qc = (lax.broadcasted_iota(jnp.int32, (tb, 2 * tb), 1) % tb) // CHUNK
            s = jnp.where(kc <= qc, s, NEG_BIG)
        v = v_ref[pl.ds(pl.multiple_of(j * tb, tb), tb), :]
        _flash_step(v, s, m_ref, l_ref, acc_ref)

    _flash_pipeline(0, i, qk, sm)

    lv = lamv_ref[...]
    lam = (jnp.exp(jnp.sum(lv[0:1] * lv[1:2], axis=1, keepdims=True))
           - jnp.exp(jnp.sum(lv[2:3] * lv[3:4], axis=1, keepdims=True)) + lam_init)
    on = acc_ref[...] / l_ref[...]
    o = (on[:, :tb] - lam * on[:, tb:]).T
    ms = jnp.mean(o * o, axis=-1, keepdims=True)
    o = o * lax.rsqrt(ms + LN_EPS) * g_ref[...] * (1.0 - lam_init)
    o_ref[...] = o.astype(o_ref.dtype)


def _diff_attention(qk16, g16, lamv, subg, lam_init, tb):
    s = g16.shape[0]
    return pl.pallas_call(
        functools.partial(_diff_kernel, tb=tb, lam_init=lam_init),
        grid=(DIFF_HEADS, s // tb),
        in_specs=[pl.BlockSpec((tb, LANES), lambda h, i: (i, h)),
                  pl.BlockSpec((s, LANES), lambda h, i: (0, DIFF_HEADS + h)),
                  pl.BlockSpec((s, LANES), lambda h, i: (0, G16_VC + h)),
                  pl.BlockSpec((4, LANES), lambda h, i: (0, 0)),
                  pl.BlockSpec((1, LANES), lambda h, i: (0, 0))],
        out_specs=pl.BlockSpec((tb, LANES), lambda h, i: (i, h)),
        out_shape=jax.ShapeDtypeStruct((s, DIFF_W), BF16),
        scratch_shapes=[pltpu.VMEM((1, 2 * tb), F32), pltpu.VMEM((1, 2 * tb), F32),
                        pltpu.VMEM((DIFF_VD, 2 * tb), F32),
                        pltpu.VMEM((tb, 2 * tb), F32), pltpu.VMEM((tb, 2 * tb), F32)],
        compiler_params=_cparams(("parallel", "parallel")),
        name="diff_attention",
    )(qk16, qk16, g16, lamv, subg)


def _shift_rows(x, d, fill):
    if d % 8 == 0:
        return jnp.concatenate([jnp.full((d, x.shape[1]), fill, x.dtype), x[:x.shape[0] - d]],
                               axis=0)
    rows = lax.broadcasted_iota(jnp.int32, x.shape, 0)
    return jnp.where(rows >= d, pltpu.roll(x, d, 0), fill)


def _lru_kernel(xr_ref, gr_ref, cw_ref, cb_ref, wa_ref, ba_ref, wi_ref, bi_ref, lam_ref,
                o_ref, halo_ref, h_ref, *, tb, sub):
    @pl.when(pl.program_id(1) == 0)
    def _():
        halo_ref[...] = jnp.zeros_like(halo_ref)
        h_ref[...] = jnp.zeros_like(h_ref)

    x = xr_ref[...]
    xe = jnp.concatenate([halo_ref[...], x], axis=0)
    halo_ref[...] = x[tb - 8:tb, :]
    cw = cw_ref[...]
    xc = cb_ref[...] + cw[CONV_W - 1:CONV_W, :] * x
    for j in range(CONV_W - 1):
        sh = CONV_W - 1 - j
        xc = xc + cw[j:j + 1, :] * pltpu.roll(xe, sh, 0)[8:8 + tb, :]

    xcb = xc.astype(BF16)
    r = jax.nn.sigmoid(jnp.dot(xcb, wa_ref[0], preferred_element_type=F32) + ba_ref[...])
    ig = jax.nn.sigmoid(jnp.dot(xcb, wi_ref[0], preferred_element_type=F32) + bi_ref[...])
    lam = lam_ref[...]
    ls = jnp.minimum(lam, 0.0) - jnp.log1p(jnp.exp(-jnp.abs(lam)))
    log_a = LRU_C * r * ls
    a = jnp.exp(log_a)
    z2 = 2.0 * log_a
    e2 = jnp.exp(z2)
    small = jnp.where(e2 == 1.0, -z2, (1.0 - e2) * z2 / jnp.log(e2))
    neg_expm1 = jnp.where(z2 < -1.0, 1.0 - e2, small)
    u = jnp.sqrt(neg_expm1) * (ig * xc)

    h = h_ref[...]
    for c in range(tb // sub):
        ac = a[c * sub:(c + 1) * sub, :]
        uc = u[c * sub:(c + 1) * sub, :]
        d = 1
        while d < sub:
            uc = ac * _shift_rows(uc, d, 0.0) + uc
            ac = ac * _shift_rows(ac, d, 1.0)
            d *= 2
        hc = uc + ac * h
        h = hc[sub - 1:sub, :]
        g = gr_ref[c * sub:(c + 1) * sub, :]
        gelu = 0.5 * g * (1.0 + jnp.tanh(math.sqrt(2.0 / math.pi) * (g + 0.044715 * (g * g * g))))
        o_ref[c * sub:(c + 1) * sub, :] = (gelu * hc).astype(o_ref.dtype)
    h_ref[...] = h


def _lru_branch(g32, cw, cb, wa, ba, wi, bi, lam, tb, sub):
    s = g32.shape[0]
    vec = lambda c, i: (0, c)
    return pl.pallas_call(
        functools.partial(_lru_kernel, tb=tb, sub=sub),
        grid=(LRU_BLOCKS, s // tb),
        in_specs=[pl.BlockSpec((tb, LANES), lambda c, i: (i, G32_XR + c)),
                  pl.BlockSpec((tb, LANES), lambda c, i: (i, G32_GR + c)),
                  pl.BlockSpec((CONV_W, LANES), vec),
                  pl.BlockSpec((1, LANES), vec),
                  pl.BlockSpec((1, LRU_BW, LRU_BW), lambda c, i: (c, 0, 0)),
                  pl.BlockSpec((1, LANES), vec),
                  pl.BlockSpec((1, LRU_BW, LRU_BW), lambda c, i: (c, 0, 0)),
                  pl.BlockSpec((1, LANES), vec),
                  pl.BlockSpec((1, LANES), vec)],
        out_specs=pl.BlockSpec((tb, LANES), lambda c, i: (i, c)),
        out_shape=jax.ShapeDtypeStruct((s, LRU_W), BF16),
        scratch_shapes=[pltpu.VMEM((8, LANES), F32), pltpu.VMEM((1, LANES), F32)],
        compiler_params=_cparams(("parallel", "arbitrary")),
        name="rg_lru",
    )(g32, g32, cw, cb, wa, ba, wi, bi, lam)


def _layer_norm_rows(y, g, b):
    mu = jnp.mean(y, axis=-1, keepdims=True)
    yc = y - mu
    var = jnp.mean(yc * yc, axis=-1, keepdims=True)
    return yc * lax.rsqrt(var + LN_EPS) * g + b


def _oproj_ln_kernel(xa_ref, xb_ref, xc_ref, wa_ref, wb_ref, wc_ref, h_ref, g_ref, b_ref,
                     o32_ref, oslab_ref):
    half = h_ref.shape[0] // 2
    for part in range(2):
        rows = slice(part * half, (part + 1) * half)
        mix = (jnp.dot(xa_ref[rows, :], wa_ref[...], preferred_element_type=F32)
               + jnp.dot(xb_ref[rows, :], wb_ref[...], preferred_element_type=F32)
               + jnp.dot(xc_ref[rows, :], wc_ref[...], preferred_element_type=F32))
        o = _layer_norm_rows(ALPHA * h_ref[rows, :] + mix, g_ref[...], b_ref[...])
        o32_ref[rows, :] = o
        _store_slabs(oslab_ref.at[pl.ds(part * half * SLAB_ROWS, half * SLAB_ROWS), :], o)


def _oproj_ln(xa, xb, xc, wa, wb, wc, h, g, b, tm):
    s = h.shape[0]
    row = lambda i: (i, 0)
    fixed = lambda i: (0, 0)
    return pl.pallas_call(
        _oproj_ln_kernel,
        grid=(s // tm,),
        in_specs=[pl.BlockSpec((tm, FOX_W), row), pl.BlockSpec((tm, LRU_W), row),
                  pl.BlockSpec((tm, DIFF_W), row),
                  pl.BlockSpec((FOX_W, D_MODEL), fixed), pl.BlockSpec((LRU_W, D_MODEL), fixed),
                  pl.BlockSpec((DIFF_W, D_MODEL), fixed),
                  pl.BlockSpec((tm, D_MODEL), row),
                  pl.BlockSpec((1, D_MODEL), fixed), pl.BlockSpec((1, D_MODEL), fixed)],
        out_specs=[pl.BlockSpec((tm, D_MODEL), row), pl.BlockSpec((tm * SLAB_ROWS, LANES), row)],
        out_shape=[jax.ShapeDtypeStruct((s, D_MODEL), F32),
                   jax.ShapeDtypeStruct((s * SLAB_ROWS, LANES), F32)],
        compiler_params=_cparams(("parallel",)),
        name="oproj_ln1",
    )(xa, xb, xc, wa, wb, wc, h, g, b)


def _router_kernel(h_ref, whi_ref, wlo_ref, b_ref, o_ref):
    h = h_ref[...]
    hi = h.astype(BF16)
    lo = (h - hi.astype(F32)).astype(BF16)
    whi = whi_ref[...]
    logits = (jnp.dot(hi, whi, preferred_element_type=F32)
              + jnp.dot(hi, wlo_ref[...], preferred_element_type=F32)
              + jnp.dot(lo, whi, preferred_element_type=F32)) + b_ref[...]
    lane = lax.broadcasted_iota(jnp.int32, logits.shape, 1)
    big = jnp.int32(1 << 20)

    def first_lane(cond):
        return jnp.min(jnp.where(cond, lane, big), axis=1, keepdims=True)

    gm = lane < N_GROUPS
    gl = jnp.where(gm, logits, -jnp.inf)
    gmax = jnp.max(gl, axis=1, keepdims=True)
    gexp = jnp.where(gm, jnp.exp(logits - gmax), 0.0)
    gprob = gexp / jnp.sum(gexp, axis=1, keepdims=True)
    gidx = first_lane(gl == gmax)
    g_weight = jnp.sum(jnp.where(lane == gidx, gprob, 0.0), axis=1, keepdims=True)

    e0 = N_GROUPS + EXPERTS_PER_GROUP * gidx
    em = jnp.logical_and(lane >= e0, lane < e0 + EXPERTS_PER_GROUP)
    el = jnp.where(em, logits, -jnp.inf)
    emax = jnp.max(el, axis=1, keepdims=True)
    eexp = jnp.where(em, jnp.exp(logits - emax), 0.0)
    eprob = jnp.where(em, eexp / jnp.sum(eexp, axis=1, keepdims=True), -1.0)
    p1 = jnp.max(eprob, axis=1, keepdims=True)
    i1 = first_lane(eprob == p1)
    eprob2 = jnp.where(lane == i1, -1.0, eprob)
    p2 = jnp.max(eprob2, axis=1, keepdims=True)
    i2 = first_lane(eprob2 == p2)
    den = p1 + p2
    gate1 = g_weight * (p1 / den)
    gate2 = g_weight * (p2 / den)
    id1 = (i1 - N_GROUPS).astype(F32)
    id2 = (i2 - N_GROUPS).astype(F32)
    o_ref[...] = jnp.where(lane == 0, id1,
                           jnp.where(lane == 1, id2,
                                     jnp.where(lane == 2, gate1,
                                               jnp.where(lane == 3, gate2, 0.0))))


def _router(h, whi, wlo, bias, tm):
    s = h.shape[0]
    fixed = lambda i: (0, 0)
    return pl.pallas_call(
        _router_kernel,
        grid=(s // tm,),
        in_specs=[pl.BlockSpec((tm, D_MODEL), lambda i: (i, 0)),
                  pl.BlockSpec((D_MODEL, LANES), fixed), pl.BlockSpec((D_MODEL, LANES), fixed),
                  pl.BlockSpec((1, LANES), fixed)],
        out_specs=pl.BlockSpec((tm, LANES), lambda i: (i, 0)),
        out_shape=jax.ShapeDtypeStruct((s, LANES), F32),
        compiler_params=_cparams(("parallel",)),
        name="moe_router",
    )(h, whi, wlo, bias)


SLAB_ROWS = D_MODEL // LANES
SLAB_PITCH = SLAB_ROWS + 4


def _store_slabs(o_ref, x):
    rows = x.shape[0]
    for c in range(SLAB_ROWS):
        o_ref[pl.ds(c, rows, stride=SLAB_ROWS), :] = x[:, c * LANES:(c + 1) * LANES]


def _load_slabs(buf_ref, rows):
    return jnp.concatenate([buf_ref[pl.ds(c, rows, stride=SLAB_PITCH), :]
                            for c in range(SLAB_ROWS)], axis=1)


GATHER_GROUP = 8


def _issue_slab_gather(row_of, ngroups, src_hbm, dst_ref, sem):
    def body(gi, carry):
        for u in range(GATHER_GROUP):
            r = gi * GATHER_GROUP + u
            src0 = pl.multiple_of(row_of(r) * SLAB_ROWS, SLAB_ROWS)
            dst0 = pl.multiple_of(r * SLAB_PITCH, 4)
            pltpu.make_async_copy(src_hbm.at[pl.ds(src0, SLAB_ROWS), :],
                                  dst_ref.at[pl.ds(dst0, SLAB_ROWS), :], sem).start()
        return carry
    lax.fori_loop(0, ngroups, body, 0)


def _wait_slab_gather(ngroups, buf_ref, sem):
    part = buf_ref.at[pl.ds(0, ngroups * (GATHER_GROUP * SLAB_ROWS)), :]
    pltpu.make_async_copy(part, part, sem).wait()


def _moe_kernel(te_ref, nt_ref, start_ref, cnt_ref, tok_ref, wfirst_ref, wslot_ref, wnext_ref,
                h_hbm, wg_hbm, wu_hbm, wd_hbm, y_ref, xbuf, wgb, wub, wdb, sem, wsem, *, tm):
    t = pl.program_id(0)
    nt = nt_ref[0]
    slot = t % 2
    last = tok_ref.shape[0] - 1

    def groups(tile):
        return (cnt_ref[tile] + (GATHER_GROUP - 1)) // GATHER_GROUP

    def issue(tile, sl):
        s0 = start_ref[tile]
        _issue_slab_gather(lambda r: tok_ref[jnp.minimum(s0 + r, last)], groups(tile), h_hbm,
                           xbuf.at[sl], sem.at[sl])

    def weight_copies(e, ws):
        return (pltpu.make_async_copy(wg_hbm.at[e], wgb.at[ws], wsem.at[ws]),
                pltpu.make_async_copy(wu_hbm.at[e], wub.at[ws], wsem.at[ws]),
                pltpu.make_async_copy(wd_hbm.at[e], wdb.at[ws], wsem.at[ws]))

    @pl.when(t == 0)
    def _():
        xbuf[...] = jnp.zeros_like(xbuf)
        issue(0, 0)
        for c in weight_copies(te_ref[0], wslot_ref[0]):
            c.start(priority=1)

    @pl.when(t + 1 < nt)
    def _():
        issue(t + 1, 1 - slot)

    @pl.when(t < nt)
    def _():
        ws = wslot_ref[t]

        @pl.when(wfirst_ref[t] == 1)
        def _():
            for c in weight_copies(te_ref[t], ws):
                c.wait()

            @pl.when(wnext_ref[t] >= 0)
            def _():
                for c in weight_copies(wnext_ref[t], 1 - ws):
                    c.start(priority=1)

        _wait_slab_gather(groups(t), xbuf.at[slot], sem.at[slot])
        x = _load_slabs(xbuf.at[slot], tm).astype(BF16)
        hg = jnp.dot(x, wgb[ws].astype(BF16), preferred_element_type=F32)
        hu = jnp.dot(x, wub[ws].astype(BF16), preferred_element_type=F32)
        act = (hg * jax.nn.sigmoid(hg)) * hu
        y = jnp.dot(act.astype(BF16), wdb[ws].astype(BF16), preferred_element_type=F32)
        _store_slabs(y_ref, y)

    @pl.when(t >= nt)
    def _():
        y_ref[...] = jnp.zeros_like(y_ref)


def _moe_grouped(plan, tok_sorted, h_slab, wg, wu, wd, tm):
    tile_expert, ntiles, tile_start, tile_count, wfirst, wslot, wnext = plan
    nt_max = tile_expert.shape[0]
    grid_spec = pltpu.PrefetchScalarGridSpec(
        num_scalar_prefetch=8,
        grid=(nt_max,),
        in_specs=[pl.BlockSpec(memory_space=pl.ANY)] * 4,
        out_specs=pl.BlockSpec((tm * SLAB_ROWS, LANES), lambda t, *_: (t, 0)),
        scratch_shapes=[pltpu.VMEM((2, tm * SLAB_PITCH, LANES), F32),
                        pltpu.VMEM((2, D_MODEL, D_FF_EXPERT), F32),
                        pltpu.VMEM((2, D_MODEL, D_FF_EXPERT), F32),
                        pltpu.VMEM((2, D_FF_EXPERT, D_MODEL), F32),
                        pltpu.SemaphoreType.DMA((2,)), pltpu.SemaphoreType.DMA((2,))],
    )
    return pl.pallas_call(
        functools.partial(_moe_kernel, tm=tm),
        grid_spec=grid_spec,
        out_shape=jax.ShapeDtypeStruct((nt_max * tm * SLAB_ROWS, LANES), F32),
        compiler_params=_cparams(("arbitrary",)),
        name="moe_experts",
    )(tile_expert, ntiles, tile_start, tile_count, tok_sorted, wfirst, wslot, wnext,
      h_slab, wg, wu, wd)


def _combine_ln_kernel(pos_ref, y_hbm, h_ref, meta_ref, g_ref, b_ref, o32_ref, o16_ref,
                       ybuf, sem, *, tm):
    t = pl.program_id(0)
    nt = pl.num_programs(0)
    slot = t % 2
    ngroups = tm // GATHER_GROUP

    def issue(tile, sl):
        for j in range(2):
            _issue_slab_gather(lambda r, j=j: pos_ref[(tile * tm + r) * 2 + j], ngroups, y_hbm,
                               ybuf.at[sl, j], sem.at[sl])

    @pl.when(t == 0)
    def _():
        issue(0, 0)

    @pl.when(t + 1 < nt)
    def _():
        issue(t + 1, 1 - slot)

    _wait_slab_gather(ngroups, ybuf.at[slot, 0], sem.at[slot])
    _wait_slab_gather(ngroups, ybuf.at[slot, 1], sem.at[slot])
    meta = meta_ref[...]
    ffn = (meta[:, 2:3] * _load_slabs(ybuf.at[slot, 0], tm)
           + meta[:, 3:4] * _load_slabs(ybuf.at[slot, 1], tm))
    o = _layer_norm_rows(ALPHA * h_ref[...] + ffn, g_ref[...], b_ref[...])
    o32_ref[...] = o
    o16_ref[...] = o.astype(BF16)


def _combine_ln(pos, y_slab, h, meta, g, b, tm):
    s = h.shape[0]
    row = lambda t, pos: (t, 0)
    fixed = lambda t, pos: (0, 0)
    grid_spec = pltpu.PrefetchScalarGridSpec(
        num_scalar_prefetch=1,
        grid=(s // tm,),
        in_specs=[pl.BlockSpec(memory_space=pl.ANY),
                  pl.BlockSpec((tm, D_MODEL), row),
                  pl.BlockSpec((tm, LANES), row),
                  pl.BlockSpec((1, D_MODEL), fixed), pl.BlockSpec((1, D_MODEL), fixed)],
        out_specs=[pl.BlockSpec((tm, D_MODEL), row), pl.BlockSpec((tm, D_MODEL), row)],
        scratch_shapes=[pltpu.VMEM((2, 2, tm * SLAB_PITCH, LANES), F32),
                        pltpu.SemaphoreType.DMA((2,))],
    )
    return pl.pallas_call(
        functools.partial(_combine_ln_kernel, tm=tm),
        grid_spec=grid_spec,
        out_shape=[jax.ShapeDtypeStruct((s, D_MODEL), F32),
                   jax.ShapeDtypeStruct((s, D_MODEL), BF16)],
        compiler_params=_cparams(("arbitrary",)),
        name="combine_ln2",
    )(pos, y_slab, h, meta, g, b)


def _dispatch_plan(meta, tm, layer):
    t = meta.shape[0]
    eid = meta[:, 0:2].astype(jnp.int32).reshape(-1)
    onehot = (eid[:, None] == jnp.arange(N_EXPERTS, dtype=jnp.int32)[None, :]).astype(jnp.int32)
    csum = jnp.cumsum(onehot, axis=0)
    counts = csum[-1]
    padded = ((counts + tm - 1) // tm) * tm
    pend = jnp.cumsum(padded)
    poff = pend - padded
    off = jnp.cumsum(counts) - counts
    pos = jnp.sum(onehot * (csum + poff[None, :]), axis=1) - 1
    p_max = 2 * t + N_EXPERTS * tm
    nt_max = p_max // tm
    ntiles = (pend[-1] // tm).astype(jnp.int32)
    tile_ids = jnp.minimum(jnp.arange(nt_max, dtype=jnp.int32), ntiles - 1)
    tile_expert = jnp.sum((pend[None, :] // tm <= tile_ids[:, None]).astype(jnp.int32), axis=1)
    tile_expert = jnp.minimum(tile_expert, N_EXPERTS - 1).astype(jnp.int32)
    tok_sorted = (jnp.argsort(eid, stable=True) // 2).astype(jnp.int32)
    k0 = jnp.arange(nt_max, dtype=jnp.int32) * tm - poff[tile_expert]
    tile_start = jnp.clip(off[tile_expert] + k0, 0, 2 * t - 1).astype(jnp.int32)
    tile_count = jnp.clip(counts[tile_expert] - k0, 0, tm).astype(jnp.int32)
    ids = jnp.arange(N_EXPERTS, dtype=jnp.int32)
    has = counts > 0
    rank = jnp.cumsum(has.astype(jnp.int32)) - 1
    later = jnp.logical_and(has[None, :], ids[None, :] > ids[:, None])
    next_e = jnp.min(jnp.where(later, ids[None, :], N_EXPERTS), axis=1)
    next_e = jnp.where(next_e < N_EXPERTS, next_e + layer * N_EXPERTS, -1)
    wfirst = (k0 == 0).astype(jnp.int32)
    wslot = (rank[tile_expert] % 2).astype(jnp.int32)
    wnext = next_e[tile_expert].astype(jnp.int32)
    plan = (tile_expert + layer * N_EXPERTS, ntiles.reshape(1), tile_start, tile_count,
            wfirst, wslot, wnext)
    return plan, tok_sorted, pos.astype(jnp.int32)


def _pad_lanes(v, width=LANES):
    v = v.reshape(1, -1).astype(F32)
    return jnp.pad(v, ((0, 0), (0, width - v.shape[1])))


TM_PROJ, TN_PROJ = 4096, 256
TB_PREP = 256
TB_FOX = 256
TB_DIFF = 512
TB_LRU, SUB_LRU = 1024, 256
TB_ROPE = 512
TM_OPROJ = 512
TM_ROUTER = 512
TM_MOE = 256
TM_COMBINE = 256


def _layer(l, h32, h16, pos_f, invf, p, stacked):
    s = h32.shape[0]
    w16, w32 = _prep_w_in(stacked["w_in"], l)
    s16 = jnp.concatenate([jnp.full((1, FOX_W), FOX_HD ** -0.5 * LOG2E, F32),
                           jnp.ones((1, G16_W - FOX_W), F32)], axis=1)
    s32 = jnp.ones((1, G32_W), F32)
    tm_proj = min(s, TM_PROJ)
    g16 = _matmul(h16, w16, s16, BF16, tm_proj, TN_PROJ, "in_proj_bf16")
    g32 = _matmul(h16, w32, s32, F32, tm_proj, TN_PROJ, "in_proj_f32")

    tb_prep = min(s, TB_PREP)
    qaux, kaux, stat = _fox_prep(g32, g16, _pad_lanes(p["b_f"]), tb_prep)
    tb_fox = min(s, TB_FOX)
    jmin = _fox_skip_table(stat, tb_fox // tb_prep)
    out_a = _fox_attention(jmin, g16, qaux, kaux, tb_fox)

    out_b = _lru_branch(g32, p["conv_w"], p["conv_b"].reshape(1, -1),
                        p["w_a"].astype(BF16), p["b_a"].reshape(1, -1),
                        p["w_i"].astype(BF16), p["b_i"].reshape(1, -1),
                        p["lru_lambda"].reshape(1, -1), min(s, TB_LRU), min(s, SUB_LRU))

    qk16 = _rope(g32, pos_f, invf, min(s, TB_ROPE))
    lamv = jnp.concatenate([_pad_lanes(p["lam_q1"]), _pad_lanes(p["lam_k1"]),
                            _pad_lanes(p["lam_q2"]), _pad_lanes(p["lam_k2"])], axis=0)
    lam_init = 0.8 - 0.6 * math.exp(-0.3 * l)
    out_c = _diff_attention(qk16, g16, lamv, p["subln_g"].reshape(1, -1), lam_init,
                            min(s, TB_DIFF))

    w_o = p["w_o"].astype(BF16)
    h1_32, h1_slab = _oproj_ln(out_a, out_b, out_c, w_o[:FOX_W], w_o[FOX_W:FOX_W + LRU_W],
                               w_o[FOX_W + LRU_W:], h32, p["ln1_g"].reshape(1, -1),
                               p["ln1_b"].reshape(1, -1), min(s, TM_OPROJ))

    w_rt = jnp.concatenate([p["w_group"], p["w_router"],
                            jnp.zeros((D_MODEL, LANES - N_GROUPS - N_EXPERTS), F32)], axis=1)
    w_rt_hi = w_rt.astype(BF16)
    w_rt_lo = (w_rt - w_rt_hi.astype(F32)).astype(BF16)
    b_rt = _pad_lanes(jnp.concatenate([p["b_group"], p["b_router"]]))
    meta = _router(h1_32, w_rt_hi, w_rt_lo, b_rt, min(s, TM_ROUTER))

    tm_moe = min(s, TM_MOE)
    plan, tok_sorted, pos = _dispatch_plan(meta, tm_moe, l)
    y_slab = _moe_grouped(plan, tok_sorted, h1_slab, stacked["w_gate"], stacked["w_up"],
                          stacked["w_down"], tm_moe)
    return _combine_ln(pos, y_slab, h1_32, meta, p["ln2_g"].reshape(1, -1),
                       p["ln2_b"].reshape(1, -1), min(s, TM_COMBINE))


def _forward(x, positions, params):
    bsz, s, d = x.shape
    assert bsz == 1 and d == D_MODEL
    h32 = x.reshape(s, d)
    h16 = h32.astype(BF16)
    half = ROT_DIM // 2
    inv_freq = ROPE_THETA ** (-jnp.arange(half, dtype=F32) * 2.0 / ROT_DIM)
    dlane = jnp.arange(LANES) % DIFF_HD
    invf = jnp.where(dlane < ROT_DIM, inv_freq[dlane % half], 0.0).reshape(1, LANES).astype(F32)
    pos_f = jnp.broadcast_to(positions.astype(F32).reshape(s, 1), (s, LANES))
    big = ("w_in", "w_gate", "w_up", "w_down")
    stacked = {
        "w_in": params["w_in"],
        "w_gate": params["w_gate"].reshape(DEPTH * N_EXPERTS, D_MODEL, D_FF_EXPERT),
        "w_up": params["w_up"].reshape(DEPTH * N_EXPERTS, D_MODEL, D_FF_EXPERT),
        "w_down": params["w_down"].reshape(DEPTH * N_EXPERTS, D_FF_EXPERT, D_MODEL),
    }
    for l in range(DEPTH):
        p = {k: v[l] for k, v in params.items() if k not in big}
        h32, h16 = _layer(l, h32, h16, pos_f, invf, p, stacked)
    return h32.reshape(bsz, s, d)


def kernel(x, positions, w_in, b_f, conv_w, conv_b, w_a, b_a, w_i, b_i, lru_lambda, lam_q1, lam_k1, lam_q2, lam_k2, subln_g, w_o, ln1_g, ln1_b, w_group, b_group, w_router, b_router, w_gate, w_up, w_down, ln2_g, ln2_b):
    params = dict(w_in=w_in, b_f=b_f, conv_w=conv_w, conv_b=conv_b, w_a=w_a, b_a=b_a, w_i=w_i,
                  b_i=b_i, lru_lambda=lru_lambda, lam_q1=lam_q1, lam_k1=lam_k1, lam_q2=lam_q2,
                  lam_k2=lam_k2, subln_g=subln_g, w_o=w_o, ln1_g=ln1_g, ln1_b=ln1_b,
                  w_group=w_group, b_group=b_group, w_router=w_router, b_router=b_router,
                  w_gate=w_gate, w_up=w_up, w_down=w_down, ln2_g=ln2_g, ln2_b=ln2_b)
    return _forward(x, positions, params)
```

```python
import functools
import math

import jax
import jax.numpy as jnp
from jax import lax
from jax.experimental import pallas as pl
from jax.experimental.pallas import tpu as pltpu

F32 = jnp.float32
BF16 = jnp.bfloat16

D_MODEL = 2048
DEPTH = 2
CHUNK = 64
FOX_HEADS = 6
FOX_HD = 128
FOX_W = FOX_HEADS * FOX_HD
LRU_W = 768
LRU_BLOCKS = 6
LRU_BW = LRU_W // LRU_BLOCKS
CONV_W = 4
LRU_C = 8.0
DIFF_HEADS = 4
DIFF_HD = 64
DIFF_VD = 2 * DIFF_HD
DIFF_W = DIFF_HEADS * DIFF_VD
ROT_DIM = DIFF_HD // 4
ROPE_THETA = 500000.0
N_GROUPS = 4
EXPERTS_PER_GROUP = 8
N_EXPERTS = N_GROUPS * EXPERTS_PER_GROUP
D_FF_EXPERT = 512
ALPHA = (2.0 * DEPTH) ** 0.25
LN_EPS = 1e-5

LANES = 128
NEG_BIG = -1e30
LOG2E = math.log2(math.e)
SKIP_BITS = 72.0
VMEM_LIMIT = 56 * 1024 * 1024

G16_W = 3 * FOX_W + DIFF_W
G32_W = 2 * DIFF_W + 2 * LRU_W + 2 * LANES
G16_QA, G16_KA, G16_VA, G16_VC = 0, 6, 12, 18
G32_XR, G32_GR, G32_FA = 8, 14, 20


def _cparams(sem, vmem=VMEM_LIMIT):
    return pltpu.CompilerParams(dimension_semantics=sem, vmem_limit_bytes=vmem)


def _mm_kernel(x_ref, w_ref, s_ref, o_ref):
    acc = jnp.dot(x_ref[...], w_ref[...], preferred_element_type=F32)
    o_ref[...] = (acc * s_ref[...]).astype(o_ref.dtype)


def _matmul(x, w, s, out_dtype, tm, tn, name):
    m, k = x.shape
    n = w.shape[1]
    return pl.pallas_call(
        _mm_kernel,
        grid=(m // tm, n // tn),
        in_specs=[pl.BlockSpec((tm, k), lambda i, j: (i, 0)),
                  pl.BlockSpec((k, tn), lambda i, j: (0, j)),
                  pl.BlockSpec((1, tn), lambda i, j: (0, j))],
        out_specs=pl.BlockSpec((tm, tn), lambda i, j: (i, j)),
        out_shape=jax.ShapeDtypeStruct((m, n), out_dtype),
        compiler_params=_cparams(("parallel", "parallel")),
        name=name,
    )(x, w, s)


_IN_QA = 0
_IN_KA = _IN_QA + FOX_W
_IN_VA = _IN_KA + FOX_W
_IN_FA = _IN_VA + FOX_W
_IN_XR = _IN_FA + FOX_HEADS
_IN_GR = _IN_XR + LRU_W
_IN_QC = _IN_GR + LRU_W
_IN_KC = _IN_QC + DIFF_W
_IN_VC = _IN_KC + DIFF_W
N_IN = _IN_VC + DIFF_W


def _wprep_kernel(w_ref, o16_ref, o32_ref):
    def put(o_ref, dst, src, width):
        o_ref[:, dst:dst + width] = w_ref[0, :, src:src + width].astype(BF16)

    put(o16_ref, 0, _IN_QA, FOX_W)
    put(o16_ref, FOX_W, _IN_KA, FOX_W)
    put(o16_ref, 2 * FOX_W, _IN_VA, FOX_W)
    put(o16_ref, 3 * FOX_W, _IN_VC, DIFF_W)
    put(o32_ref, 0, _IN_QC, DIFF_W)
    put(o32_ref, DIFF_W, _IN_KC, DIFF_W)
    put(o32_ref, G32_XR * LANES, _IN_XR, LRU_W)
    put(o32_ref, G32_GR * LANES, _IN_GR, LRU_W)
    fa0 = G32_FA * LANES
    o32_ref[:, fa0:fa0 + 2 * LANES] = jnp.zeros((o32_ref.shape[0], 2 * LANES), BF16)
    put(o32_ref, fa0, _IN_FA, FOX_HEADS)


def _prep_w_in(w_in, layer, tk=256):
    return pl.pallas_call(
        _wprep_kernel,
        grid=(D_MODEL // tk,),
        in_specs=[pl.BlockSpec((1, tk, N_IN), lambda i: (layer, i, 0))],
        out_specs=[pl.BlockSpec((tk, G16_W), lambda i: (i, 0)),
                   pl.BlockSpec((tk, G32_W), lambda i: (i, 0))],
        out_shape=[jax.ShapeDtypeStruct((D_MODEL, G16_W), BF16),
                   jax.ShapeDtypeStruct((D_MODEL, G32_W), BF16)],
        compiler_params=_cparams(("parallel",)),
        name="w_in_regroup",
    )(w_in)


def _split3(x):
    hi = x.astype(BF16)
    r1 = x - hi.astype(F32)
    mid = r1.astype(BF16)
    lo = (r1 - mid.astype(F32)).astype(BF16)
    return hi, mid, lo


def _fox_prep_kernel(fa_ref, bf_ref, q_ref, k_ref, qaux_ref, kaux_ref, stat_ref, carry_ref):
    @pl.when(pl.program_id(0) == 0)
    def _():
        carry_ref[...] = jnp.zeros_like(carry_ref)

    z = fa_ref[...] + bf_ref[...]
    lf = (jnp.minimum(z, 0.0) - jnp.log1p(jnp.exp(-jnp.abs(z)))) * LOG2E
    t = z.shape[0]
    row = lax.broadcasted_iota(jnp.int32, (t, t), 0)
    col = lax.broadcasted_iota(jnp.int32, (t, t), 1)
    tri = jnp.where(row >= col, 1.0, 0.0).astype(BF16)
    hi, mid, lo = _split3(lf)
    cs = (jnp.dot(tri, hi, preferred_element_type=F32)
          + jnp.dot(tri, mid, preferred_element_type=F32)
          + jnp.dot(tri, lo, preferred_element_type=F32)) + carry_ref[...]
    carry_ref[...] = cs[t - 1:t, :]

    lane = lax.broadcasted_iota(jnp.int32, (t, LANES), 1)
    lane1 = lax.broadcasted_iota(jnp.int32, (1, LANES), 1)
    qn = jnp.zeros((1, LANES), F32)
    kn = jnp.zeros((1, LANES), F32)
    for h in range(FOX_HEADS):
        c = jnp.broadcast_to(jnp.sum(jnp.where(lane == h, cs, 0.0), axis=1, keepdims=True),
                             (t, LANES))
        c_hi = c.astype(BF16).astype(F32)
        r1 = c - c_hi
        c_mid = r1.astype(BF16).astype(F32)
        c_lo = (r1 - c_mid).astype(BF16).astype(F32)
        kaux = jnp.where(lane == 0, c_hi, jnp.where(lane == 1, c_mid, jnp.where(
            lane == 2, c_lo, jnp.where(lane < 6, 1.0, 0.0))))
        qaux = jnp.where(lane < 3, -1.0, jnp.where(lane == 3, c_hi, jnp.where(
            lane == 4, c_mid, jnp.where(lane == 5, c_lo, 0.0))))
        kaux_ref[h] = kaux.astype(BF16)
        qaux_ref[h] = qaux.astype(BF16)
        qh = q_ref[:, h * FOX_HD:(h + 1) * FOX_HD].astype(F32)
        kh = k_ref[:, h * FOX_HD:(h + 1) * FOX_HD].astype(F32)
        q2 = jnp.max(jnp.sum(qh * qh, axis=1, keepdims=True), axis=0, keepdims=True)
        k2 = jnp.max(jnp.sum(kh * kh, axis=1, keepdims=True), axis=0, keepdims=True)
        qn = jnp.where(lane1 == h, q2, qn)
        kn = jnp.where(lane1 == h, k2, kn)
    stat_ref[0] = jnp.concatenate([qn, kn, cs[0:1, :], cs[t - 1:t, :],
                                   jnp.zeros((4, LANES), F32)], axis=0)


def _fox_prep(g32, g16, bf_pad, tb):
    s = g32.shape[0]
    qk_blocks = FOX_W // LANES
    return pl.pallas_call(
        _fox_prep_kernel,
        grid=(s // tb,),
        in_specs=[pl.BlockSpec((tb, LANES), lambda i: (i, G32_FA)),
                  pl.BlockSpec((1, LANES), lambda i: (0, 0)),
                  pl.BlockSpec((tb, FOX_W), lambda i: (i, G16_QA // qk_blocks)),
                  pl.BlockSpec((tb, FOX_W), lambda i: (i, G16_KA // qk_blocks))],
        out_specs=[pl.BlockSpec((FOX_HEADS, tb, LANES), lambda i: (0, i, 0)),
                   pl.BlockSpec((FOX_HEADS, tb, LANES), lambda i: (0, i, 0)),
                   pl.BlockSpec((1, 8, LANES), lambda i: (i, 0, 0))],
        out_shape=[jax.ShapeDtypeStruct((FOX_HEADS, s, LANES), BF16),
                   jax.ShapeDtypeStruct((FOX_HEADS, s, LANES), BF16),
                   jax.ShapeDtypeStruct((s // tb, 8, LANES), F32)],
        scratch_shapes=[pltpu.VMEM((1, LANES), F32)],
        compiler_params=_cparams(("arbitrary",)),
        name="fox_prep",
    )(g32, bf_pad, g16, g16)


def _fox_skip_table(stat, per_block):
    nb = stat.shape[0] // per_block
    st = stat.reshape(nb, per_block, 8, LANES)
    qn = jnp.sqrt(jnp.max(st[:, :, 0, :FOX_HEADS], axis=1))
    kn = jnp.sqrt(jnp.max(st[:, :, 1, :FOX_HEADS], axis=1))
    c_first = st[:, 0, 2, :FOX_HEADS]
    c_last = st[:, per_block - 1, 3, :FOX_HEADS]
    kmax = jnp.max(kn, axis=0, keepdims=True)
    bound = 1.01 * qn * (kmax + kn) + c_first
    skip = (bound[:, None, :] - c_last[None, :, :]) < -SKIP_BITS
    jj = jnp.arange(nb)
    skip = jnp.logical_and(skip, (jj[None, :] < jj[:, None])[:, :, None])
    lead = jnp.cumsum(1 - skip.astype(jnp.int32), axis=1) == 0
    return jnp.sum(lead.astype(jnp.int32), axis=1).T.reshape(-1)


def _rope_kernel(x_ref, pos_ref, invf_ref, o_ref):
    ang = pos_ref[...] * invf_ref[...]
    c = jnp.cos(ang)
    s = jnp.sin(ang)
    half = ROT_DIM // 2
    d = lax.broadcasted_iota(jnp.int32, (1, LANES), 1) % DIFF_HD
    sa = jnp.where(d >= half, s, 0.0)
    sb = jnp.where(d < half, -s, 0.0)
    nblk = x_ref.shape[1] // LANES
    for j in range(nblk):
        t = x_ref[:, j * LANES:(j + 1) * LANES]
        r = t * c + pltpu.roll(t, half, 1) * sa + pltpu.roll(t, LANES - half, 1) * sb
        if j < nblk // 2:
            r = r * (DIFF_HD ** -0.5 * LOG2E)
        o_ref[:, j * LANES:(j + 1) * LANES] = r.astype(o_ref.dtype)


def _rope(g32, pos_f, invf, tb):
    s = g32.shape[0]
    w = 2 * DIFF_W
    return pl.pallas_call(
        _rope_kernel,
        grid=(s // tb,),
        in_specs=[pl.BlockSpec((tb, w), lambda i: (i, 0)),
                  pl.BlockSpec((tb, LANES), lambda i: (i, 0)),
                  pl.BlockSpec((1, LANES), lambda i: (0, 0))],
        out_specs=pl.BlockSpec((tb, w), lambda i: (i, 0)),
        out_shape=jax.ShapeDtypeStruct((s, w), BF16),
        compiler_params=_cparams(("parallel",)),
        name="diff_rope",
    )(g32, pos_f, invf)


def _flash_step(v, s, m_ref, l_ref, acc_ref):
    m_prev = m_ref[...]
    m_new = jnp.maximum(m_prev, jnp.max(s, axis=0, keepdims=True))
    a = jnp.exp2(m_prev - m_new)
    p = jnp.exp2(s - m_new)
    l_ref[...] = a * l_ref[...] + jnp.sum(p, axis=0, keepdims=True)
    pv = lax.dot_general(v, p.astype(v.dtype), (((0,), (0,)), ((), ())),
                         preferred_element_type=F32)
    acc_ref[...] = a * acc_ref[...] + pv
    m_ref[...] = m_new


def _flash_pipeline(j0, i, qk, sm):
    npairs = (i - j0) // 2
    qk(j0, 0)

    def body(p, carry):
        j = j0 + 2 * p
        qk(j + 1, 1)
        sm(j, 0, False)
        qk(j + 2, 0)
        sm(j + 1, 1, False)
        return carry

    lax.fori_loop(0, npairs, body, 0)
    jn = j0 + 2 * npairs

    @pl.when(jn == i)
    def _():
        sm(i, 0, True)

    @pl.when(jn != i)
    def _():
        qk(i, 1)
        sm(jn, 0, False)
        sm(i, 1, True)


def _init_state(m_ref, l_ref, acc_ref):
    m_ref[...] = jnp.full(m_ref.shape, NEG_BIG, F32)
    l_ref[...] = jnp.zeros(l_ref.shape, F32)
    acc_ref[...] = jnp.zeros(acc_ref.shape, F32)


HEADS_PER_STEP = 2


def _head_lanes(g):
    return slice(g * LANES, (g + 1) * LANES)


def _fox_kernel(jmin_ref, q_ref, qaux_ref, k_ref, kaux_ref, v_ref, o_ref,
                m_ref, l_ref, acc_ref, sa_ref, sb_ref, *, tb):
    hp = pl.program_id(0)
    i = pl.program_id(1)
    heads = range(HEADS_PER_STEP)
    qf = [jnp.concatenate([q_ref[:, _head_lanes(g)], qaux_ref[g]], axis=1)
          for g in heads]
    _init_state(m_ref, l_ref, acc_ref)
    s_refs = (sa_ref, sb_ref)

    def qk(j, slot):
        k0 = pl.multiple_of(j * tb, tb)
        for g in heads:
            kf = jnp.concatenate([k_ref[pl.ds(k0, tb), _head_lanes(g)],
                                  kaux_ref[g, pl.ds(k0, tb), :]], axis=1)
            s_refs[slot][g] = lax.dot_general(kf, qf[g], (((1,), (1,)), ((), ())),
                                              preferred_element_type=F32)

    def sm(j, slot, masked):
        k0 = pl.multiple_of(j * tb, tb)
        for g in heads:
            s = s_refs[slot][g]
            if masked:
                kk = lax.broadcasted_iota(jnp.int32, (tb, tb), 0)
                qq = lax.broadcasted_iota(jnp.int32, (tb, tb), 1)
                s = jnp.where(kk <= qq, s, NEG_BIG)
            v = v_ref[pl.ds(k0, tb), _head_lanes(g)]
            _flash_step(v, s, m_ref.at[g], l_ref.at[g], acc_ref.at[g])

    nq = pl.num_programs(1)
    j0 = jmin_ref[hp * HEADS_PER_STEP * nq + i]
    for g in range(1, HEADS_PER_STEP):
        j0 = jnp.minimum(j0, jmin_ref[(hp * HEADS_PER_STEP + g) * nq + i])
    _flash_pipeline(j0, i, qk, sm)
    for g in heads:
        o = acc_ref[g] / l_ref[g]
        o_ref[:, _head_lanes(g)] = o.T.astype(o_ref.dtype)


def _fox_attention(jmin, g16, qaux, kaux, tb):
    s = g16.shape[0]
    hps = HEADS_PER_STEP
    wide = hps * LANES
    grid_spec = pltpu.PrefetchScalarGridSpec(
        num_scalar_prefetch=1,
        grid=(FOX_HEADS // hps, s // tb),
        in_specs=[pl.BlockSpec((tb, wide), lambda h, i, jm: (i, G16_QA // hps + h)),
                  pl.BlockSpec((hps, tb, LANES), lambda h, i, jm: (h, i, 0)),
                  pl.BlockSpec((s, wide), lambda h, i, jm: (0, G16_KA // hps + h)),
                  pl.BlockSpec((hps, s, LANES), lambda h, i, jm: (h, 0, 0)),
                  pl.BlockSpec((s, wide), lambda h, i, jm: (0, G16_VA // hps + h))],
        out_specs=pl.BlockSpec((tb, wide), lambda h, i, jm: (i, h)),
        scratch_shapes=[pltpu.VMEM((hps, 1, tb), F32), pltpu.VMEM((hps, 1, tb), F32),
                        pltpu.VMEM((hps, FOX_HD, tb), F32),
                        pltpu.VMEM((hps, tb, tb), F32), pltpu.VMEM((hps, tb, tb), F32)],
    )
    return pl.pallas_call(
        functools.partial(_fox_kernel, tb=tb),
        grid_spec=grid_spec,
        out_shape=jax.ShapeDtypeStruct((s, FOX_W), BF16),
        compiler_params=_cparams(("parallel", "parallel")),
        name="fox_attention",
    )(jmin, g16, qaux, g16, kaux, g16)


def _diff_kernel(q_ref, k_ref, v_ref, lamv_ref, g_ref, o_ref, m_ref, l_ref, acc_ref,
                 sa_ref, sb_ref, *, tb, lam_init):
    i = pl.program_id(1)
    heads = range(HEADS_PER_STEP)
    lane = lax.broadcasted_iota(jnp.int32, (tb, LANES), 1)
    qq2 = []
    for g in heads:
        q = q_ref[:, _head_lanes(g)].astype(F32)
        qq2.append(jnp.concatenate([jnp.where(lane < DIFF_HD, q, 0.0),
                                    jnp.where(lane >= DIFF_HD, q, 0.0)], axis=0).astype(BF16))
    _init_state(m_ref, l_ref, acc_ref)
    s_refs = (sa_ref, sb_ref)

    def qk(j, slot):
        k0 = pl.multiple_of(j * tb, tb)
        for g in heads:
            k = k_ref[pl.ds(k0, tb), _head_lanes(g)]
            s_refs[slot][g] = lax.dot_general(k, qq2[g], (((1,), (1,)), ((), ())),
                                              preferred_element_type=F32)

    def sm(j, slot, masked):
        k0 = pl.multiple_of(j * tb, tb)
        for g in heads:
            s = s_refs[slot][g]
            if masked:
                kc = lax.broadcasted_iota(jnp.int32, (tb, 2 * tb), 0) // CHUNK
                qc = (lax.broadcasted_iota(jnp.int32, (tb, 2 * tb), 1) % tb) // CHUNK
                s = jnp.where(kc <= qc, s, NEG_BIG)
            v = v_ref[pl.ds(k0, tb), _head_lanes(g)]
            _flash_step(v, s, m_ref.at[g], l_ref.at[g], acc_ref.at[g])

    _flash_pipeline(0, i, qk, sm)

    lv = lamv_ref[...]
    lam = (jnp.exp(jnp.sum(lv[0:1] * lv[1:2], axis=1, keepdims=True))
           - jnp.exp(jnp.sum(lv[2:3] * lv[3:4], axis=1, keepdims=True)) + lam_init)
    for g in heads:
        on = acc_ref[g] / l_ref[g]
        o = (on[:, :tb] - lam * on[:, tb:]).T
        ms = jnp.mean(o * o, axis=-1, keepdims=True)
        o = o * lax.rsqrt(ms + LN_EPS) * g_ref[...] * (1.0 - lam_init)
        o_ref[:, _head_lanes(g)] = o.astype(o_ref.dtype)


def _diff_attention(qk16, g16, lamv, subg, lam_init, tb):
    s = g16.shape[0]
    hps = HEADS_PER_STEP
    wide = hps * LANES
    return pl.pallas_call(
        functools.partial(_diff_kernel, tb=tb, lam_init=lam_init),
        grid=(DIFF_HEADS // hps, s // tb),
        in_specs=[pl.BlockSpec((tb, wide), lambda h, i: (i, h)),
                  pl.BlockSpec((s, wide), lambda h, i: (0, DIFF_HEADS // hps + h)),
                  pl.BlockSpec((s, wide), lambda h, i: (0, G16_VC // hps + h)),
                  pl.BlockSpec((4, LANES), lambda h, i: (0, 0)),
                  pl.BlockSpec((1, LANES), lambda h, i: (0, 0))],
        out_specs=pl.BlockSpec((tb, wide), lambda h, i: (i, h)),
        out_shape=jax.ShapeDtypeStruct((s, DIFF_W), BF16),
        scratch_shapes=[pltpu.VMEM((hps, 1, 2 * tb), F32), pltpu.VMEM((hps, 1, 2 * tb), F32),
                        pltpu.VMEM((hps, DIFF_VD, 2 * tb), F32),
                        pltpu.VMEM((hps, tb, 2 * tb), F32), pltpu.VMEM((hps, tb, 2 * tb), F32)],
        compiler_params=_cparams(("parallel", "parallel")),
        name="diff_attention",
    )(qk16, qk16, g16, lamv, subg)


def _shift_rows(x, d, fill):
    if d % 8 == 0:
        return jnp.concatenate([jnp.full((d, x.shape[1]), fill, x.dtype), x[:x.shape[0] - d]],
                               axis=0)
    rows = lax.broadcasted_iota(jnp.int32, x.shape, 0)
    return jnp.where(rows >= d, pltpu.roll(x, d, 0), fill)


def _lru_kernel(xr_ref, gr_ref, cw_ref, cb_ref, wa_ref, ba_ref, wi_ref, bi_ref, lam_ref,
                o_ref, halo_ref, h_ref, *, tb, sub):
    @pl.when(pl.program_id(1) == 0)
    def _():
        halo_ref[...] = jnp.zeros_like(halo_ref)
        h_ref[...] = jnp.zeros_like(h_ref)

    x = xr_ref[...]
    xe = jnp.concatenate([halo_ref[...], x], axis=0)
    halo_ref[...] = x[tb - 8:tb, :]
    cw = cw_ref[...]
    xc = cb_ref[...] + cw[CONV_W - 1:CONV_W, :] * x
    for j in range(CONV_W - 1):
        sh = CONV_W - 1 - j
        xc = xc + cw[j:j + 1, :] * pltpu.roll(xe, sh, 0)[8:8 + tb, :]

    xcb = xc.astype(BF16)
    r = jax.nn.sigmoid(jnp.dot(xcb, wa_ref[0], preferred_element_type=F32) + ba_ref[...])
    ig = jax.nn.sigmoid(jnp.dot(xcb, wi_ref[0], preferred_element_type=F32) + bi_ref[...])
    lam = lam_ref[...]
    ls = jnp.minimum(lam, 0.0) - jnp.log1p(jnp.exp(-jnp.abs(lam)))
    log_a = LRU_C * r * ls
    a = jnp.exp(log_a)
    z2 = 2.0 * log_a
    e2 = jnp.exp(z2)
    small = jnp.where(e2 == 1.0, -z2, (1.0 - e2) * z2 / jnp.log(e2))
    neg_expm1 = jnp.where(z2 < -1.0, 1.0 - e2, small)
    u = jnp.sqrt(neg_expm1) * (ig * xc)

    h = h_ref[...]
    for c in range(tb // sub):
        ac = a[c * sub:(c + 1) * sub, :]
        uc = u[c * sub:(c + 1) * sub, :]
        d = 1
        while d < sub:
            uc = ac * _shift_rows(uc, d, 0.0) + uc
            ac = ac * _shift_rows(ac, d, 1.0)
            d *= 2
        hc = uc + ac * h
        h = hc[sub - 1:sub, :]
        g = gr_ref[c * sub:(c + 1) * sub, :]
        gelu = 0.5 * g * (1.0 + jnp.tanh(math.sqrt(2.0 / math.pi) * (g + 0.044715 * (g * g * g))))
        o_ref[c * sub:(c + 1) * sub, :] = (gelu * hc).astype(o_ref.dtype)
    h_ref[...] = h


def _lru_branch(g32, cw, cb, wa, ba, wi, bi, lam, tb, sub):
    s = g32.shape[0]
    vec = lambda c, i: (0, c)
    return pl.pallas_call(
        functools.partial(_lru_kernel, tb=tb, sub=sub),
        grid=(LRU_BLOCKS, s // tb),
        in_specs=[pl.BlockSpec((tb, LANES), lambda c, i: (i, G32_XR + c)),
                  pl.BlockSpec((tb, LANES), lambda c, i: (i, G32_GR + c)),
                  pl.BlockSpec((CONV_W, LANES), vec),
                  pl.BlockSpec((1, LANES), vec),
                  pl.BlockSpec((1, LRU_BW, LRU_BW), lambda c, i: (c, 0, 0)),
                  pl.BlockSpec((1, LANES), vec),
                  pl.BlockSpec((1, LRU_BW, LRU_BW), lambda c, i: (c, 0, 0)),
                  pl.BlockSpec((1, LANES), vec),
                  pl.BlockSpec((1, LANES), vec)],
        out_specs=pl.BlockSpec((tb, LANES), lambda c, i: (i, c)),
        out_shape=jax.ShapeDtypeStruct((s, LRU_W), BF16),
        scratch_shapes=[pltpu.VMEM((8, LANES), F32), pltpu.VMEM((1, LANES), F32)],
        compiler_params=_cparams(("parallel", "arbitrary")),
        name="rg_lru",
    )(g32, g32, cw, cb, wa, ba, wi, bi, lam)


def _layer_norm_rows(y, g, b):
    mu = jnp.mean(y, axis=-1, keepdims=True)
    yc = y - mu
    var = jnp.mean(yc * yc, axis=-1, keepdims=True)
    return yc * lax.rsqrt(var + LN_EPS) * g + b


def _oproj_ln_kernel(xa_ref, xb_ref, xc_ref, wa_ref, wb_ref, wc_ref, h_ref, g_ref, b_ref,
                     o32_ref, oslab_ref):
    half = h_ref.shape[0] // 2
    for part in range(2):
        rows = slice(part * half, (part + 1) * half)
        mix = (jnp.dot(xa_ref[rows, :], wa_ref[...], preferred_element_type=F32)
               + jnp.dot(xb_ref[rows, :], wb_ref[...], preferred_element_type=F32)
               + jnp.dot(xc_ref[rows, :], wc_ref[...], preferred_element_type=F32))
        o = _layer_norm_rows(ALPHA * h_ref[rows, :] + mix, g_ref[...], b_ref[...])
        o32_ref[rows, :] = o
        _store_slabs(oslab_ref.at[pl.ds(part * half * SLAB_ROWS, half * SLAB_ROWS), :], o)


def _oproj_ln(xa, xb, xc, wa, wb, wc, h, g, b, tm):
    s = h.shape[0]
    row = lambda i: (i, 0)
    fixed = lambda i: (0, 0)
    return pl.pallas_call(
        _oproj_ln_kernel,
        grid=(s // tm,),
        in_specs=[pl.BlockSpec((tm, FOX_W), row), pl.BlockSpec((tm, LRU_W), row),
                  pl.BlockSpec((tm, DIFF_W), row),
                  pl.BlockSpec((FOX_W, D_MODEL), fixed), pl.BlockSpec((LRU_W, D_MODEL), fixed),
                  pl.BlockSpec((DIFF_W, D_MODEL), fixed),
                  pl.BlockSpec((tm, D_MODEL), row),
                  pl.BlockSpec((1, D_MODEL), fixed), pl.BlockSpec((1, D_MODEL), fixed)],
        out_specs=[pl.BlockSpec((tm, D_MODEL), row), pl.BlockSpec((tm * SLAB_ROWS, LANES), row)],
        out_shape=[jax.ShapeDtypeStruct((s, D_MODEL), F32),
                   jax.ShapeDtypeStruct((s * SLAB_ROWS, LANES), F32)],
        compiler_params=_cparams(("parallel",)),
        name="oproj_ln1",
    )(xa, xb, xc, wa, wb, wc, h, g, b)


def _router_kernel(h_ref, whi_ref, wlo_ref, b_ref, o_ref):
    h = h_ref[...]
    hi = h.astype(BF16)
    lo = (h - hi.astype(F32)).astype(BF16)
    whi = whi_ref[...]
    logits = (jnp.dot(hi, whi, preferred_element_type=F32)
              + jnp.dot(hi, wlo_ref[...], preferred_element_type=F32)
              + jnp.dot(lo, whi, preferred_element_type=F32)) + b_ref[...]
    lane = lax.broadcasted_iota(jnp.int32, logits.shape, 1)
    big = jnp.int32(1 << 20)

    def first_lane(cond):
        return jnp.min(jnp.where(cond, lane, big), axis=1, keepdims=True)

    gm = lane < N_GROUPS
    gl = jnp.where(gm, logits, -jnp.inf)
    gmax = jnp.max(gl, axis=1, keepdims=True)
    gexp = jnp.where(gm, jnp.exp(logits - gmax), 0.0)
    gprob = gexp / jnp.sum(gexp, axis=1, keepdims=True)
    gidx = first_lane(gl == gmax)
    g_weight = jnp.sum(jnp.where(lane == gidx, gprob, 0.0), axis=1, keepdims=True)

    e0 = N_GROUPS + EXPERTS_PER_GROUP * gidx
    em = jnp.logical_and(lane >= e0, lane < e0 + EXPERTS_PER_GROUP)
    el = jnp.where(em, logits, -jnp.inf)
    emax = jnp.max(el, axis=1, keepdims=True)
    eexp = jnp.where(em, jnp.exp(logits - emax), 0.0)
    eprob = jnp.where(em, eexp / jnp.sum(eexp, axis=1, keepdims=True), -1.0)
    p1 = jnp.max(eprob, axis=1, keepdims=True)
    i1 = first_lane(eprob == p1)
    eprob2 = jnp.where(lane == i1, -1.0, eprob)
    p2 = jnp.max(eprob2, axis=1, keepdims=True)
    i2 = first_lane(eprob2 == p2)
    den = p1 + p2
    gate1 = g_weight * (p1 / den)
    gate2 = g_weight * (p2 / den)
    id1 = (i1 - N_GROUPS).astype(F32)
    id2 = (i2 - N_GROUPS).astype(F32)
    o_ref[...] = jnp.where(lane == 0, id1,
                           jnp.where(lane == 1, id2,
                                     jnp.where(lane == 2, gate1,
                                               jnp.where(lane == 3, gate2, 0.0))))


def _router(h, whi, wlo, bias, tm):
    s = h.shape[0]
    fixed = lambda i: (0, 0)
    return pl.pallas_call(
        _router_kernel,
        grid=(s // tm,),
        in_specs=[pl.BlockSpec((tm, D_MODEL), lambda i: (i, 0)),
                  pl.BlockSpec((D_MODEL, LANES), fixed), pl.BlockSpec((D_MODEL, LANES), fixed),
                  pl.BlockSpec((1, LANES), fixed)],
        out_specs=pl.BlockSpec((tm, LANES), lambda i: (i, 0)),
        out_shape=jax.ShapeDtypeStruct((s, LANES), F32),
        compiler_params=_cparams(("parallel",)),
        name="moe_router",
    )(h, whi, wlo, bias)


SLAB_ROWS = D_MODEL // LANES
SLAB_PITCH = SLAB_ROWS + 4


def _store_slabs(o_ref, x):
    rows = x.shape[0]
    for c in range(SLAB_ROWS):
        o_ref[pl.ds(c, rows, stride=SLAB_ROWS), :] = x[:, c * LANES:(c + 1) * LANES]


def _load_slabs(buf_ref, rows):
    return jnp.concatenate([buf_ref[pl.ds(c, rows, stride=SLAB_PITCH), :]
                            for c in range(SLAB_ROWS)], axis=1)


GATHER_GROUP = 8


def _issue_slab_gather(row_of, ngroups, src_hbm, dst_ref, sem, both_queues=False):
    def body(gi, carry):
        for u in range(GATHER_GROUP):
            r = gi * GATHER_GROUP + u
            src0 = pl.multiple_of(row_of(r) * SLAB_ROWS, SLAB_ROWS)
            dst0 = pl.multiple_of(r * SLAB_PITCH, 4)
            pltpu.make_async_copy(src_hbm.at[pl.ds(src0, SLAB_ROWS), :],
                                  dst_ref.at[pl.ds(dst0, SLAB_ROWS), :],
                                  sem).start(priority=u % 2 if both_queues else 0)
        return carry
    lax.fori_loop(0, ngroups, body, 0)


def _wait_slab_gather(ngroups, buf_ref, sem):
    part = buf_ref.at[pl.ds(0, ngroups * (GATHER_GROUP * SLAB_ROWS)), :]
    pltpu.make_async_copy(part, part, sem).wait()


def _moe_kernel(te_ref, nt_ref, start_ref, cnt_ref, tok_ref, wfirst_ref, wslot_ref, wnext_ref,
                h_hbm, wg_hbm, wu_hbm, wd_hbm, y_ref, xbuf, wgb, wub, wdb, sem, wsem, *, tm):
    t = pl.program_id(0)
    nt = nt_ref[0]
    slot = t % 2
    last = tok_ref.shape[0] - 1

    def groups(tile):
        return (cnt_ref[tile] + (GATHER_GROUP - 1)) // GATHER_GROUP

    def issue(tile, sl):
        s0 = start_ref[tile]
        _issue_slab_gather(lambda r: tok_ref[jnp.minimum(s0 + r, last)], groups(tile), h_hbm,
                           xbuf.at[sl], sem.at[sl])

    def weight_copies(e, ws):
        return (pltpu.make_async_copy(wg_hbm.at[e], wgb.at[ws], wsem.at[ws]),
                pltpu.make_async_copy(wu_hbm.at[e], wub.at[ws], wsem.at[ws]),
                pltpu.make_async_copy(wd_hbm.at[e], wdb.at[ws], wsem.at[ws]))

    @pl.when(t == 0)
    def _():
        xbuf[...] = jnp.zeros_like(xbuf)
        issue(0, 0)
        for c in weight_copies(te_ref[0], wslot_ref[0]):
            c.start(priority=1)

    @pl.when(t + 1 < nt)
    def _():
        issue(t + 1, 1 - slot)

    @pl.when(t < nt)
    def _():
        ws = wslot_ref[t]

        @pl.when(wfirst_ref[t] == 1)
        def _():
            for c in weight_copies(te_ref[t], ws):
                c.wait()

            @pl.when(wnext_ref[t] >= 0)
            def _():
                for c in weight_copies(wnext_ref[t], 1 - ws):
                    c.start(priority=1)

        _wait_slab_gather(groups(t), xbuf.at[slot], sem.at[slot])
        x = _load_slabs(xbuf.at[slot], tm).astype(BF16)
        hg = jnp.dot(x, wgb[ws].astype(BF16), preferred_element_type=F32)
        hu = jnp.dot(x, wub[ws].astype(BF16), preferred_element_type=F32)
        act = (hg * jax.nn.sigmoid(hg)) * hu
        y = jnp.dot(act.astype(BF16), wdb[ws].astype(BF16), preferred_element_type=F32)
        _store_slabs(y_ref, y)

    @pl.when(t >= nt)
    def _():
        y_ref[...] = jnp.zeros_like(y_ref)


def _moe_grouped(plan, tok_sorted, h_slab, wg, wu, wd, tm):
    tile_expert, ntiles, tile_start, tile_count, wfirst, wslot, wnext = plan
    nt_max = tile_expert.shape[0]
    grid_spec = pltpu.PrefetchScalarGridSpec(
        num_scalar_prefetch=8,
        grid=(nt_max,),
        in_specs=[pl.BlockSpec(memory_space=pl.ANY)] * 4,
        out_specs=pl.BlockSpec((tm * SLAB_ROWS, LANES), lambda t, *_: (t, 0)),
        scratch_shapes=[pltpu.VMEM((2, tm * SLAB_PITCH, LANES), F32),
                        pltpu.VMEM((2, D_MODEL, D_FF_EXPERT), F32),
                        pltpu.VMEM((2, D_MODEL, D_FF_EXPERT), F32),
                        pltpu.VMEM((2, D_FF_EXPERT, D_MODEL), F32),
                        pltpu.SemaphoreType.DMA((2,)), pltpu.SemaphoreType.DMA((2,))],
    )
    return pl.pallas_call(
        functools.partial(_moe_kernel, tm=tm),
        grid_spec=grid_spec,
        out_shape=jax.ShapeDtypeStruct((nt_max * tm * SLAB_ROWS, LANES), F32),
        compiler_params=_cparams(("arbitrary",)),
        name="moe_experts",
    )(tile_expert, ntiles, tile_start, tile_count, tok_sorted, wfirst, wslot, wnext,
      h_slab, wg, wu, wd)


def _combine_ln_kernel(pos_ref, y_hbm, h_ref, meta_ref, g_ref, b_ref, o32_ref, o16_ref,
                       ybuf, sem, *, tm):
    t = pl.program_id(0)
    nt = pl.num_programs(0)
    slot = t % 2
    ngroups = tm // GATHER_GROUP

    def issue(tile, sl):
        for j in range(2):
            _issue_slab_gather(lambda r, j=j: pos_ref[(tile * tm + r) * 2 + j], ngroups, y_hbm,
                               ybuf.at[sl, j], sem.at[sl], both_queues=True)

    @pl.when(t == 0)
    def _():
        issue(0, 0)

    @pl.when(t + 1 < nt)
    def _():
        issue(t + 1, 1 - slot)

    _wait_slab_gather(ngroups, ybuf.at[slot, 0], sem.at[slot])
    _wait_slab_gather(ngroups, ybuf.at[slot, 1], sem.at[slot])
    meta = meta_ref[...]
    ffn = (meta[:, 2:3] * _load_slabs(ybuf.at[slot, 0], tm)
           + meta[:, 3:4] * _load_slabs(ybuf.at[slot, 1], tm))
    o = _layer_norm_rows(ALPHA * h_ref[...] + ffn, g_ref[...], b_ref[...])
    o32_ref[...] = o
    o16_ref[...] = o.astype(BF16)


def _combine_ln(pos, y_slab, h, meta, g, b, tm):
    s = h.shape[0]
    row = lambda t, pos: (t, 0)
    fixed = lambda t, pos: (0, 0)
    grid_spec = pltpu.PrefetchScalarGridSpec(
        num_scalar_prefetch=1,
        grid=(s // tm,),
        in_specs=[pl.BlockSpec(memory_space=pl.ANY),
                  pl.BlockSpec((tm, D_MODEL), row),
                  pl.BlockSpec((tm, LANES), row),
                  pl.BlockSpec((1, D_MODEL), fixed), pl.BlockSpec((1, D_MODEL), fixed)],
        out_specs=[pl.BlockSpec((tm, D_MODEL), row), pl.BlockSpec((tm, D_MODEL), row)],
        scratch_shapes=[pltpu.VMEM((2, 2, tm * SLAB_PITCH, LANES), F32),
                        pltpu.SemaphoreType.DMA((2,))],
    )
    return pl.pallas_call(
        functools.partial(_combine_ln_kernel, tm=tm),
        grid_spec=grid_spec,
        out_shape=[jax.ShapeDtypeStruct((s, D_MODEL), F32),
                   jax.ShapeDtypeStruct((s, D_MODEL), BF16)],
        compiler_params=_cparams(("arbitrary",)),
        name="combine_ln2",
    )(pos, y_slab, h, meta, g, b)


def _dispatch_plan(meta, tm, layer):
    t = meta.shape[0]
    eid = meta[:, 0:2].astype(jnp.int32).reshape(-1)
    onehot = (eid[:, None] == jnp.arange(N_EXPERTS, dtype=jnp.int32)[None, :]).astype(jnp.int32)
    csum = jnp.cumsum(onehot, axis=0)
    counts = csum[-1]
    padded = ((counts + tm - 1) // tm) * tm
    pend = jnp.cumsum(padded)
    poff = pend - padded
    off = jnp.cumsum(counts) - counts
    pos = jnp.sum(onehot * (csum + poff[None, :]), axis=1) - 1
    p_max = 2 * t + N_EXPERTS * tm
    nt_max = p_max // tm
    ntiles = (pend[-1] // tm).astype(jnp.int32)
    tile_ids = jnp.minimum(jnp.arange(nt_max, dtype=jnp.int32), ntiles - 1)
    tile_expert = jnp.sum((pend[None, :] // tm <= tile_ids[:, None]).astype(jnp.int32), axis=1)
    tile_expert = jnp.minimum(tile_expert, N_EXPERTS - 1).astype(jnp.int32)
    tok_sorted = (jnp.argsort(eid, stable=True) // 2).astype(jnp.int32)
    k0 = jnp.arange(nt_max, dtype=jnp.int32) * tm - poff[tile_expert]
    tile_start = jnp.clip(off[tile_expert] + k0, 0, 2 * t - 1).astype(jnp.int32)
    tile_count = jnp.clip(counts[tile_expert] - k0, 0, tm).astype(jnp.int32)
    ids = jnp.arange(N_EXPERTS, dtype=jnp.int32)
    has = counts > 0
    rank = jnp.cumsum(has.astype(jnp.int32)) - 1
    later = jnp.logical_and(has[None, :], ids[None, :] > ids[:, None])
    next_e = jnp.min(jnp.where(later, ids[None, :], N_EXPERTS), axis=1)
    next_e = jnp.where(next_e < N_EXPERTS, next_e + layer * N_EXPERTS, -1)
    wfirst = (k0 == 0).astype(jnp.int32)
    wslot = (rank[tile_expert] % 2).astype(jnp.int32)
    wnext = next_e[tile_expert].astype(jnp.int32)
    plan = (tile_expert + layer * N_EXPERTS, ntiles.reshape(1), tile_start, tile_count,
            wfirst, wslot, wnext)
    return plan, tok_sorted, pos.astype(jnp.int32)


def _pad_lanes(v, width=LANES):
    v = v.reshape(1, -1).astype(F32)
    return jnp.pad(v, ((0, 0), (0, width - v.shape[1])))


TM_PROJ, TN_PROJ = 4096, 256
TB_PREP = 256
TB_FOX = 512
TB_DIFF = 512
TB_LRU, SUB_LRU = 1024, 256
TB_ROPE = 512
TM_OPROJ = 512
TM_ROUTER = 512
TM_MOE = 256
TM_COMBINE = 256


def _layer(l, h32, h16, pos_f, invf, p, stacked):
    s = h32.shape[0]
    w16, w32 = _prep_w_in(stacked["w_in"], l)
    s16 = jnp.concatenate([jnp.full((1, FOX_W), FOX_HD ** -0.5 * LOG2E, F32),
                           jnp.ones((1, G16_W - FOX_W), F32)], axis=1)
    s32 = jnp.ones((1, G32_W), F32)
    tm_proj = min(s, TM_PROJ)
    g16 = _matmul(h16, w16, s16, BF16, tm_proj, TN_PROJ, "in_proj_bf16")
    g32 = _matmul(h16, w32, s32, F32, tm_proj, TN_PROJ, "in_proj_f32")

    tb_prep = min(s, TB_PREP)
    qaux, kaux, stat = _fox_prep(g32, g16, _pad_lanes(p["b_f"]), tb_prep)
    tb_fox = min(s, TB_FOX)
    jmin = _fox_skip_table(stat, tb_fox // tb_prep)
    out_a = _fox_attention(jmin, g16, qaux, kaux, tb_fox)

    out_b = _lru_branch(g32, p["conv_w"], p["conv_b"].reshape(1, -1),
                        p["w_a"].astype(BF16), p["b_a"].reshape(1, -1),
                        p["w_i"].astype(BF16), p["b_i"].reshape(1, -1),
                        p["lru_lambda"].reshape(1, -1), min(s, TB_LRU), min(s, SUB_LRU))

    qk16 = _rope(g32, pos_f, invf, min(s, TB_ROPE))
    lamv = jnp.concatenate([_pad_lanes(p["lam_q1"]), _pad_lanes(p["lam_k1"]),
                            _pad_lanes(p["lam_q2"]), _pad_lanes(p["lam_k2"])], axis=0)
    lam_init = 0.8 - 0.6 * math.exp(-0.3 * l)
    out_c = _diff_attention(qk16, g16, lamv, p["subln_g"].reshape(1, -1), lam_init,
                            min(s, TB_DIFF))

    w_o = p["w_o"].astype(BF16)
    h1_32, h1_slab = _oproj_ln(out_a, out_b, out_c, w_o[:FOX_W], w_o[FOX_W:FOX_W + LRU_W],
                               w_o[FOX_W + LRU_W:], h32, p["ln1_g"].reshape(1, -1),
                               p["ln1_b"].reshape(1, -1), min(s, TM_OPROJ))

    w_rt = jnp.concatenate([p["w_group"], p["w_router"],
                            jnp.zeros((D_MODEL, LANES - N_GROUPS - N_EXPERTS), F32)], axis=1)
    w_rt_hi = w_rt.astype(BF16)
    w_rt_lo = (w_rt - w_rt_hi.astype(F32)).astype(BF16)
    b_rt = _pad_lanes(jnp.concatenate([p["b_group"], p["b_router"]]))
    meta = _router(h1_32, w_rt_hi, w_rt_lo, b_rt, min(s, TM_ROUTER))

    tm_moe = min(s, TM_MOE)
    plan, tok_sorted, pos = _dispatch_plan(meta, tm_moe, l)
    y_slab = _moe_grouped(plan, tok_sorted, h1_slab, stacked["w_gate"], stacked["w_up"],
                          stacked["w_down"], tm_moe)
    return _combine_ln(pos, y_slab, h1_32, meta, p["ln2_g"].reshape(1, -1),
                       p["ln2_b"].reshape(1, -1), min(s, TM_COMBINE))


def _forward(x, positions, params):
    bsz, s, d = x.shape
    assert bsz == 1 and d == D_MODEL
    h32 = x.reshape(s, d)
    h16 = h32.astype(BF16)
    half = ROT_DIM // 2
    inv_freq = ROPE_THETA ** (-jnp.arange(half, dtype=F32) * 2.0 / ROT_DIM)
    dlane = jnp.arange(LANES) % DIFF_HD
    invf = jnp.where(dlane < ROT_DIM, inv_freq[dlane % half], 0.0).reshape(1, LANES).astype(F32)
    pos_f = jnp.broadcast_to(positions.astype(F32).reshape(s, 1), (s, LANES))
    big = ("w_in", "w_gate", "w_up", "w_down")
    stacked = {
        "w_in": params["w_in"],
        "w_gate": params["w_gate"].reshape(DEPTH * N_EXPERTS, D_MODEL, D_FF_EXPERT),
        "w_up": params["w_up"].reshape(DEPTH * N_EXPERTS, D_MODEL, D_FF_EXPERT),
        "w_down": params["w_down"].reshape(DEPTH * N_EXPERTS, D_FF_EXPERT, D_MODEL),
    }
    for l in range(DEPTH):
        p = {k: v[l] for k, v in params.items() if k not in big}
        h32, h16 = _layer(l, h32, h16, pos_f, invf, p, stacked)
    return h32.reshape(bsz, s, d)


def kernel(x, positions, w_in, b_f, conv_w, conv_b, w_a, b_a, w_i, b_i, lru_lambda, lam_q1, lam_k1, lam_q2, lam_k2, subln_g, w_o, ln1_g, ln1_b, w_group, b_group, w_router, b_router, w_gate, w_up, w_down, ln2_g, ln2_b):
    params = dict(w_in=w_in, b_f=b_f, conv_w=conv_w, conv_b=conv_b, w_a=w_a, b_a=b_a, w_i=w_i,
                  b_i=b_i, lru_lambda=lru_lambda, lam_q1=lam_q1, lam_k1=lam_k1, lam_q2=lam_q2,
                  lam_k2=lam_k2, subln_g=subln_g, w_o=w_o, ln1_g=ln1_g, ln1_b=ln1_b,
                  w_group=w_group, b_group=b_group, w_router=w_router, b_router=b_router,
                  w_gate=w_gate, w_up=w_up, w_down=w_down, ln2_g=ln2_g, ln2_b=ln2_b)
    return _forward(x, positions, params)
```

```python
import functools
import math

import jax
import jax.numpy as jnp
from jax import lax
from jax.experimental import pallas as pl
from jax.experimental.pallas import tpu as pltpu

F32 = jnp.float32
BF16 = jnp.bfloat16

D_MODEL = 2048
DEPTH = 2
CHUNK = 64
FOX_HEADS = 6
FOX_HD = 128
FOX_W = FOX_HEADS * FOX_HD
LRU_W = 768
LRU_BLOCKS = 6
LRU_BW = LRU_W // LRU_BLOCKS
CONV_W = 4
LRU_C = 8.0
DIFF_HEADS = 4
DIFF_HD = 64
DIFF_VD = 2 * DIFF_HD
DIFF_W = DIFF_HEADS * DIFF_VD
ROT_DIM = DIFF_HD // 4
ROPE_THETA = 500000.0
N_GROUPS = 4
EXPERTS_PER_GROUP = 8
N_EXPERTS = N_GROUPS * EXPERTS_PER_GROUP
D_FF_EXPERT = 512
ALPHA = (2.0 * DEPTH) ** 0.25
LN_EPS = 1e-5

LANES = 128
NEG_BIG = -1e30
LOG2E = math.log2(math.e)
SKIP_BITS = 72.0
VMEM_LIMIT = 56 * 1024 * 1024

G16_W = 3 * FOX_W + DIFF_W
G32_W = 2 * DIFF_W + 2 * LRU_W + 2 * LANES
G16_QA, G16_KA, G16_VA, G16_VC = 0, 6, 12, 18
G32_XR, G32_GR, G32_FA = 8, 14, 20


def _cparams(sem, vmem=VMEM_LIMIT):
    return pltpu.CompilerParams(dimension_semantics=sem, vmem_limit_bytes=vmem)


def _mm_kernel(x_ref, wt_ref, s_ref, o_ref):
    acc = lax.dot_general(x_ref[...], wt_ref[...], (((1,), (1,)), ((), ())),
                          preferred_element_type=F32)
    o_ref[...] = (acc * s_ref[...]).astype(o_ref.dtype)


def _matmul_nt(x, wt, row0, n, s, out_dtype, tm, tn, name):
    m, k = x.shape
    off = row0 // tn
    return pl.pallas_call(
        _mm_kernel,
        grid=(m // tm, n // tn),
        in_specs=[pl.BlockSpec((tm, k), lambda i, j: (i, 0)),
                  pl.BlockSpec((tn, k), lambda i, j: (j + off, 0)),
                  pl.BlockSpec((1, tn), lambda i, j: (0, j))],
        out_specs=pl.BlockSpec((tm, tn), lambda i, j: (i, j)),
        out_shape=jax.ShapeDtypeStruct((m, n), out_dtype),
        compiler_params=_cparams(("parallel", "parallel")),
        name=name,
    )(x, wt, s)


_IN_QA = 0
_IN_KA = _IN_QA + FOX_W
_IN_VA = _IN_KA + FOX_W
_IN_FA = _IN_VA + FOX_W
_IN_XR = _IN_FA + FOX_HEADS
_IN_GR = _IN_XR + LRU_W
_IN_QC = _IN_GR + LRU_W
_IN_KC = _IN_QC + DIFF_W
_IN_VC = _IN_KC + DIFF_W
N_IN = _IN_VC + DIFF_W


K_TILES = D_MODEL // LANES
ROWS_PER_COL = K_TILES * DEPTH
WCHUNK = 256
_SEGMENTS = ((_IN_QA, FOX_W), (_IN_KA, FOX_W), (_IN_VA, FOX_W), (_IN_VC, DIFF_W),
             (_IN_QC, DIFF_W), (_IN_KC, DIFF_W), (_IN_XR, LRU_W), (_IN_GR, LRU_W),
             (_IN_FA, FOX_HEADS))
_CHUNK_COL0 = [c0 + d for c0, width in _SEGMENTS for d in range(0, width, WCHUNK)]
_CHUNK_COLS = [min(WCHUNK, width - d) for c0, width in _SEGMENTS for d in range(0, width, WCHUNK)]
assert len(_CHUNK_COL0) * WCHUNK == G16_W + G32_W and max(_CHUNK_COL0) + WCHUNK <= N_IN


def _wprep_kernel(col0_ref, cols_ref, w_hbm, o_ref, buf, sem, *, layer):
    c = pl.program_id(0)
    slot = c % 2

    def chunk_copy(ci, sl):
        r0 = pl.multiple_of(col0_ref[ci] * ROWS_PER_COL, ROWS_PER_COL)
        return pltpu.make_async_copy(w_hbm.at[pl.ds(r0, WCHUNK * ROWS_PER_COL), :], buf.at[sl],
                                     sem.at[sl])

    @pl.when(c == 0)
    def _():
        chunk_copy(0, 0).start()

    @pl.when(c + 1 < pl.num_programs(0))
    def _():
        chunk_copy(c + 1, 1 - slot).start()

    chunk_copy(c, slot).wait()
    keep = lax.broadcasted_iota(jnp.int32, (WCHUNK, LANES), 0) < cols_ref[c]
    for kt in range(K_TILES):
        piece = buf[slot, pl.ds(kt * DEPTH + layer, WCHUNK, stride=ROWS_PER_COL), :]
        o_ref[:, kt * LANES:(kt + 1) * LANES] = jnp.where(keep, piece, 0.0).astype(BF16)


def _prep_w_in(w_view, layer):
    nchunks = len(_CHUNK_COL0)
    grid_spec = pltpu.PrefetchScalarGridSpec(
        num_scalar_prefetch=2,
        grid=(nchunks,),
        in_specs=[pl.BlockSpec(memory_space=pl.ANY)],
        out_specs=pl.BlockSpec((WCHUNK, D_MODEL), lambda c, col0, cols: (c, 0)),
        scratch_shapes=[pltpu.VMEM((2, WCHUNK * ROWS_PER_COL, LANES), F32),
                        pltpu.SemaphoreType.DMA((2,))],
    )
    return pl.pallas_call(
        functools.partial(_wprep_kernel, layer=layer),
        grid_spec=grid_spec,
        out_shape=jax.ShapeDtypeStruct((nchunks * WCHUNK, D_MODEL), BF16),
        compiler_params=_cparams(("arbitrary",)),
        name="w_in_regroup",
    )(jnp.asarray(_CHUNK_COL0, jnp.int32), jnp.asarray(_CHUNK_COLS, jnp.int32), w_view)


def _split3(x):
    hi = x.astype(BF16)
    r1 = x - hi.astype(F32)
    mid = r1.astype(BF16)
    lo = (r1 - mid.astype(F32)).astype(BF16)
    return hi, mid, lo


def _fox_prep_kernel(fa_ref, bf_ref, q_ref, k_ref, qaux_ref, kaux_ref, stat_ref, carry_ref):
    @pl.when(pl.program_id(0) == 0)
    def _():
        carry_ref[...] = jnp.zeros_like(carry_ref)

    z = fa_ref[...] + bf_ref[...]
    lf = (jnp.minimum(z, 0.0) - jnp.log1p(jnp.exp(-jnp.abs(z)))) * LOG2E
    t = z.shape[0]
    row = lax.broadcasted_iota(jnp.int32, (t, t), 0)
    col = lax.broadcasted_iota(jnp.int32, (t, t), 1)
    tri = jnp.where(row >= col, 1.0, 0.0).astype(BF16)
    hi, mid, lo = _split3(lf)
    cs = (jnp.dot(tri, hi, preferred_element_type=F32)
          + jnp.dot(tri, mid, preferred_element_type=F32)
          + jnp.dot(tri, lo, preferred_element_type=F32)) + carry_ref[...]
    carry_ref[...] = cs[t - 1:t, :]

    lane = lax.broadcasted_iota(jnp.int32, (t, LANES), 1)
    lane1 = lax.broadcasted_iota(jnp.int32, (1, LANES), 1)
    qn = jnp.zeros((1, LANES), F32)
    kn = jnp.zeros((1, LANES), F32)
    for h in range(FOX_HEADS):
        c = jnp.broadcast_to(jnp.sum(jnp.where(lane == h, cs, 0.0), axis=1, keepdims=True),
                             (t, LANES))
        c_hi = c.astype(BF16).astype(F32)
        r1 = c - c_hi
        c_mid = r1.astype(BF16).astype(F32)
        c_lo = (r1 - c_mid).astype(BF16).astype(F32)
        kaux = jnp.where(lane == 0, c_hi, jnp.where(lane == 1, c_mid, jnp.where(
            lane == 2, c_lo, jnp.where(lane < 6, 1.0, 0.0))))
        qaux = jnp.where(lane < 3, -1.0, jnp.where(lane == 3, c_hi, jnp.where(
            lane == 4, c_mid, jnp.where(lane == 5, c_lo, 0.0))))
        kaux_ref[h] = kaux.astype(BF16)
        qaux_ref[h] = qaux.astype(BF16)
        qh = q_ref[:, h * FOX_HD:(h + 1) * FOX_HD].astype(F32)
        kh = k_ref[:, h * FOX_HD:(h + 1) * FOX_HD].astype(F32)
        q2 = jnp.max(jnp.sum(qh * qh, axis=1, keepdims=True), axis=0, keepdims=True)
        k2 = jnp.max(jnp.sum(kh * kh, axis=1, keepdims=True), axis=0, keepdims=True)
        qn = jnp.where(lane1 == h, q2, qn)
        kn = jnp.where(lane1 == h, k2, kn)
    stat_ref[0] = jnp.concatenate([qn, kn, cs[0:1, :], cs[t - 1:t, :],
                                   jnp.zeros((4, LANES), F32)], axis=0)


def _fox_prep(g32, g16, bf_pad, tb):
    s = g32.shape[0]
    qk_blocks = FOX_W // LANES
    return pl.pallas_call(
        _fox_prep_kernel,
        grid=(s // tb,),
        in_specs=[pl.BlockSpec((tb, LANES), lambda i: (i, G32_FA)),
                  pl.BlockSpec((1, LANES), lambda i: (0, 0)),
                  pl.BlockSpec((tb, FOX_W), lambda i: (i, G16_QA // qk_blocks)),
                  pl.BlockSpec((tb, FOX_W), lambda i: (i, G16_KA // qk_blocks))],
        out_specs=[pl.BlockSpec((FOX_HEADS, tb, LANES), lambda i: (0, i, 0)),
                   pl.BlockSpec((FOX_HEADS, tb, LANES), lambda i: (0, i, 0)),
                   pl.BlockSpec((1, 8, LANES), lambda i: (i, 0, 0))],
        out_shape=[jax.ShapeDtypeStruct((FOX_HEADS, s, LANES), BF16),
                   jax.ShapeDtypeStruct((FOX_HEADS, s, LANES), BF16),
                   jax.ShapeDtypeStruct((s // tb, 8, LANES), F32)],
        scratch_shapes=[pltpu.VMEM((1, LANES), F32)],
        compiler_params=_cparams(("arbitrary",)),
        name="fox_prep",
    )(g32, bf_pad, g16, g16)


def _fox_skip_table(stat, per_block):
    nb = stat.shape[0] // per_block
    st = stat.reshape(nb, per_block, 8, LANES)
    qn = jnp.sqrt(jnp.max(st[:, :, 0, :FOX_HEADS], axis=1))
    kn = jnp.sqrt(jnp.max(st[:, :, 1, :FOX_HEADS], axis=1))
    c_first = st[:, 0, 2, :FOX_HEADS]
    c_last = st[:, per_block - 1, 3, :FOX_HEADS]
    kmax = jnp.max(kn, axis=0, keepdims=True)
    bound = 1.01 * qn * (kmax + kn) + c_first
    skip = (bound[:, None, :] - c_last[None, :, :]) < -SKIP_BITS
    jj = jnp.arange(nb)
    skip = jnp.logical_and(skip, (jj[None, :] < jj[:, None])[:, :, None])
    lead = jnp.cumsum(1 - skip.astype(jnp.int32), axis=1) == 0
    return jnp.sum(lead.astype(jnp.int32), axis=1).T.reshape(-1)


def _rope_kernel(x_ref, pos_ref, invf_ref, o_ref):
    ang = pos_ref[...] * invf_ref[...]
    c = jnp.cos(ang)
    s = jnp.sin(ang)
    half = ROT_DIM // 2
    d = lax.broadcasted_iota(jnp.int32, (1, LANES), 1) % DIFF_HD
    sa = jnp.where(d >= half, s, 0.0)
    sb = jnp.where(d < half, -s, 0.0)
    nblk = x_ref.shape[1] // LANES
    for j in range(nblk):
        t = x_ref[:, j * LANES:(j + 1) * LANES]
        r = t * c + pltpu.roll(t, half, 1) * sa + pltpu.roll(t, LANES - half, 1) * sb
        if j < nblk // 2:
            r = r * (DIFF_HD ** -0.5 * LOG2E)
        o_ref[:, j * LANES:(j + 1) * LANES] = r.astype(o_ref.dtype)


def _rope(g32, pos_f, invf, tb):
    s = g32.shape[0]
    w = 2 * DIFF_W
    return pl.pallas_call(
        _rope_kernel,
        grid=(s // tb,),
        in_specs=[pl.BlockSpec((tb, w), lambda i: (i, 0)),
                  pl.BlockSpec((tb, LANES), lambda i: (i, 0)),
                  pl.BlockSpec((1, LANES), lambda i: (0, 0))],
        out_specs=pl.BlockSpec((tb, w), lambda i: (i, 0)),
        out_shape=jax.ShapeDtypeStruct((s, w), BF16),
        compiler_params=_cparams(("parallel",)),
        name="diff_rope",
    )(g32, pos_f, invf)


def _flash_step(v, s, m_ref, l_ref, acc_ref):
    nq = s.shape[1]
    cw = min(nq, FLASH_COLS)
    for c0 in range(0, nq, cw):
        cols = slice(c0, c0 + cw)
        sc = s[:, cols]
        m_prev = m_ref[:, cols]
        m_new = jnp.maximum(m_prev, jnp.max(sc, axis=0, keepdims=True))
        a = jnp.exp2(m_prev - m_new)
        p = jnp.exp2(sc - m_new)
        l_ref[:, cols] = a * l_ref[:, cols] + jnp.sum(p, axis=0, keepdims=True)
        pv = lax.dot_general(v, p.astype(v.dtype), (((0,), (0,)), ((), ())),
                             preferred_element_type=F32)
        acc_ref[:, cols] = a * acc_ref[:, cols] + pv
        m_ref[:, cols] = m_new


def _flash_pipeline(j0, i, qk, sm):
    npairs = (i - j0) // 2
    qk(j0, 0)

    def body(p, carry):
        j = j0 + 2 * p
        qk(j + 1, 1)
        sm(j, 0, False)
        qk(j + 2, 0)
        sm(j + 1, 1, False)
        return carry

    lax.fori_loop(0, npairs, body, 0)
    jn = j0 + 2 * npairs

    @pl.when(jn == i)
    def _():
        sm(i, 0, True)

    @pl.when(jn != i)
    def _():
        qk(i, 1)
        sm(jn, 0, False)
        sm(i, 1, True)


def _init_state(m_ref, l_ref, acc_ref):
    m_ref[...] = jnp.full(m_ref.shape, NEG_BIG, F32)
    l_ref[...] = jnp.zeros(l_ref.shape, F32)
    acc_ref[...] = jnp.zeros(acc_ref.shape, F32)


HEADS_PER_STEP = 2
FLASH_COLS = 256


def _head_lanes(g):
    return slice(g * LANES, (g + 1) * LANES)


def _fox_kernel(jmin_ref, q_ref, qaux_ref, k_ref, kaux_ref, v_ref, o_ref,
                m_ref, l_ref, acc_ref, sa_ref, sb_ref, *, tb):
    hp = pl.program_id(0)
    i = pl.program_id(1)
    heads = range(HEADS_PER_STEP)
    qf = [jnp.concatenate([q_ref[:, _head_lanes(g)], qaux_ref[g]], axis=1)
          for g in heads]
    _init_state(m_ref, l_ref, acc_ref)
    s_refs = (sa_ref, sb_ref)

    def qk(j, slot):
        k0 = pl.multiple_of(j * tb, tb)
        for g in heads:
            kf = jnp.concatenate([k_ref[pl.ds(k0, tb), _head_lanes(g)],
                                  kaux_ref[g, pl.ds(k0, tb), :]], axis=1)
            s_refs[slot][g] = lax.dot_general(kf, qf[g], (((1,), (1,)), ((), ())),
                                              preferred_element_type=F32)

    def sm(j, slot, masked):
        k0 = pl.multiple_of(j * tb, tb)
        for g in heads:
            s = s_refs[slot][g]
            if masked:
                kk = lax.broadcasted_iota(jnp.int32, (tb, tb), 0)
                qq = lax.broadcasted_iota(jnp.int32, (tb, tb), 1)
                s = jnp.where(kk <= qq, s, NEG_BIG)
            v = v_ref[pl.ds(k0, tb), _head_lanes(g)]
            _flash_step(v, s, m_ref.at[g], l_ref.at[g], acc_ref.at[g])

    nq = pl.num_programs(1)
    j0 = jmin_ref[hp * HEADS_PER_STEP * nq + i]
    for g in range(1, HEADS_PER_STEP):
        j0 = jnp.minimum(j0, jmin_ref[(hp * HEADS_PER_STEP + g) * nq + i])
    _flash_pipeline(j0, i, qk, sm)
    for g in heads:
        o = acc_ref[g] / l_ref[g]
        o_ref[:, _head_lanes(g)] = o.T.astype(o_ref.dtype)


def _fox_attention(jmin, g16, qaux, kaux, tb):
    s = g16.shape[0]
    hps = HEADS_PER_STEP
    wide = hps * LANES
    grid_spec = pltpu.PrefetchScalarGridSpec(
        num_scalar_prefetch=1,
        grid=(FOX_HEADS // hps, s // tb),
        in_specs=[pl.BlockSpec((tb, wide), lambda h, i, jm: (i, G16_QA // hps + h)),
                  pl.BlockSpec((hps, tb, LANES), lambda h, i, jm: (h, i, 0)),
                  pl.BlockSpec((s, wide), lambda h, i, jm: (0, G16_KA // hps + h)),
                  pl.BlockSpec((hps, s, LANES), lambda h, i, jm: (h, 0, 0)),
                  pl.BlockSpec((s, wide), lambda h, i, jm: (0, G16_VA // hps + h))],
        out_specs=pl.BlockSpec((tb, wide), lambda h, i, jm: (i, h)),
        scratch_shapes=[pltpu.VMEM((hps, 1, tb), F32), pltpu.VMEM((hps, 1, tb), F32),
                        pltpu.VMEM((hps, FOX_HD, tb), F32),
                        pltpu.VMEM((hps, tb, tb), F32), pltpu.VMEM((hps, tb, tb), F32)],
    )
    return pl.pallas_call(
        functools.partial(_fox_kernel, tb=tb),
        grid_spec=grid_spec,
        out_shape=jax.ShapeDtypeStruct((s, FOX_W), BF16),
        compiler_params=_cparams(("parallel", "parallel")),
        name="fox_attention",
    )(jmin, g16, qaux, g16, kaux, g16)


def _diff_kernel(q_ref, k_ref, v_ref, lamv_ref, g_ref, o_ref, m_ref, l_ref, acc_ref,
                 sa_ref, sb_ref, *, tb, lam_init):
    i = pl.program_id(1)
    heads = range(HEADS_PER_STEP)
    lane = lax.broadcasted_iota(jnp.int32, (tb, LANES), 1)
    qq2 = []
    for g in heads:
        q = q_ref[:, _head_lanes(g)].astype(F32)
        qq2.append(jnp.concatenate([jnp.where(lane < DIFF_HD, q, 0.0),
                                    jnp.where(lane >= DIFF_HD, q, 0.0)], axis=0).astype(BF16))
    _init_state(m_ref, l_ref, acc_ref)
    s_refs = (sa_ref, sb_ref)

    def qk(j, slot):
        k0 = pl.multiple_of(j * tb, tb)
        for g in heads:
            k = k_ref[pl.ds(k0, tb), _head_lanes(g)]
            s_refs[slot][g] = lax.dot_general(k, qq2[g], (((1,), (1,)), ((), ())),
                                              preferred_element_type=F32)

    def sm(j, slot, masked):
        k0 = pl.multiple_of(j * tb, tb)
        for g in heads:
            s = s_refs[slot][g]
            if masked:
                kc = lax.broadcasted_iota(jnp.int32, (tb, 2 * tb), 0) // CHUNK
                qc = (lax.broadcasted_iota(jnp.int32, (tb, 2 * tb), 1) % tb) // CHUNK
                s = jnp.where(kc <= qc, s, NEG_BIG)
            v = v_ref[pl.ds(k0, tb), _head_lanes(g)]
            _flash_step(v, s, m_ref.at[g], l_ref.at[g], acc_ref.at[g])

    _flash_pipeline(0, i, qk, sm)

    lv = lamv_ref[...]
    lam = (jnp.exp(jnp.sum(lv[0:1] * lv[1:2], axis=1, keepdims=True))
           - jnp.exp(jnp.sum(lv[2:3] * lv[3:4], axis=1, keepdims=True)) + lam_init)
    for g in heads:
        on = acc_ref[g] / l_ref[g]
        o = (on[:, :tb] - lam * on[:, tb:]).T
        ms = jnp.mean(o * o, axis=-1, keepdims=True)
        o = o * lax.rsqrt(ms + LN_EPS) * g_ref[...] * (1.0 - lam_init)
        o_ref[:, _head_lanes(g)] = o.astype(o_ref.dtype)


def _diff_attention(qk16, g16, lamv, subg, lam_init, tb):
    s = g16.shape[0]
    hps = HEADS_PER_STEP
    wide = hps * LANES
    return pl.pallas_call(
        functools.partial(_diff_kernel, tb=tb, lam_init=lam_init),
        grid=(DIFF_HEADS // hps, s // tb),
        in_specs=[pl.BlockSpec((tb, wide), lambda h, i: (i, h)),
                  pl.BlockSpec((s, wide), lambda h, i: (0, DIFF_HEADS // hps + h)),
                  pl.BlockSpec((s, wide), lambda h, i: (0, G16_VC // hps + h)),
                  pl.BlockSpec((4, LANES), lambda h, i: (0, 0)),
                  pl.BlockSpec((1, LANES), lambda h, i: (0, 0))],
        out_specs=pl.BlockSpec((tb, wide), lambda h, i: (i, h)),
        out_shape=jax.ShapeDtypeStruct((s, DIFF_W), BF16),
        scratch_shapes=[pltpu.VMEM((hps, 1, 2 * tb), F32), pltpu.VMEM((hps, 1, 2 * tb), F32),
                        pltpu.VMEM((hps, DIFF_VD, 2 * tb), F32),
                        pltpu.VMEM((hps, tb, 2 * tb), F32), pltpu.VMEM((hps, tb, 2 * tb), F32)],
        compiler_params=_cparams(("parallel", "parallel")),
        name="diff_attention",
    )(qk16, qk16, g16, lamv, subg)


def _shift_rows(x, d, fill):
    if d % 8 == 0:
        return jnp.concatenate([jnp.full((d, x.shape[1]), fill, x.dtype), x[:x.shape[0] - d]],
                               axis=0)
    rows = lax.broadcasted_iota(jnp.int32, x.shape, 0)
    return jnp.where(rows >= d, pltpu.roll(x, d, 0), fill)


def _lru_kernel(xr_ref, gr_ref, cw_ref, cb_ref, wa_ref, ba_ref, wi_ref, bi_ref, lam_ref,
                o_ref, halo_ref, h_ref, *, tb, sub):
    @pl.when(pl.program_id(1) == 0)
    def _():
        halo_ref[...] = jnp.zeros_like(halo_ref)
        h_ref[...] = jnp.zeros_like(h_ref)

    x = xr_ref[...]
    xe = jnp.concatenate([halo_ref[...], x], axis=0)
    halo_ref[...] = x[tb - 8:tb, :]
    cw = cw_ref[...]
    xc = cb_ref[...] + cw[CONV_W - 1:CONV_W, :] * x
    for j in range(CONV_W - 1):
        sh = CONV_W - 1 - j
        xc = xc + cw[j:j + 1, :] * pltpu.roll(xe, sh, 0)[8:8 + tb, :]

    xcb = xc.astype(BF16)
    r = jax.nn.sigmoid(jnp.dot(xcb, wa_ref[0], preferred_element_type=F32) + ba_ref[...])
    ig = jax.nn.sigmoid(jnp.dot(xcb, wi_ref[0], preferred_element_type=F32) + bi_ref[...])
    lam = lam_ref[...]
    ls = jnp.minimum(lam, 0.0) - jnp.log1p(jnp.exp(-jnp.abs(lam)))
    log_a = LRU_C * r * ls
    a = jnp.exp(log_a)
    z2 = 2.0 * log_a
    e2 = jnp.exp(z2)
    small = jnp.where(e2 == 1.0, -z2, (1.0 - e2) * z2 / jnp.log(e2))
    neg_expm1 = jnp.where(z2 < -1.0, 1.0 - e2, small)
    u = jnp.sqrt(neg_expm1) * (ig * xc)

    h = h_ref[...]
    for c in range(tb // sub):
        ac = a[c * sub:(c + 1) * sub, :]
        uc = u[c * sub:(c + 1) * sub, :]
        d = 1
        while d < sub:
            uc = ac * _shift_rows(uc, d, 0.0) + uc
            ac = ac * _shift_rows(ac, d, 1.0)
            d *= 2
        hc = uc + ac * h
        h = hc[sub - 1:sub, :]
        g = gr_ref[c * sub:(c + 1) * sub, :]
        gelu = 0.5 * g * (1.0 + jnp.tanh(math.sqrt(2.0 / math.pi) * (g + 0.044715 * (g * g * g))))
        o_ref[c * sub:(c + 1) * sub, :] = (gelu * hc).astype(o_ref.dtype)
    h_ref[...] = h


def _lru_branch(g32, cw, cb, wa, ba, wi, bi, lam, tb, sub):
    s = g32.shape[0]
    vec = lambda c, i: (0, c)
    return pl.pallas_call(
        functools.partial(_lru_kernel, tb=tb, sub=sub),
        grid=(LRU_BLOCKS, s // tb),
        in_specs=[pl.BlockSpec((tb, LANES), lambda c, i: (i, G32_XR + c)),
                  pl.BlockSpec((tb, LANES), lambda c, i: (i, G32_GR + c)),
                  pl.BlockSpec((CONV_W, LANES), vec),
                  pl.BlockSpec((1, LANES), vec),
                  pl.BlockSpec((1, LRU_BW, LRU_BW), lambda c, i: (c, 0, 0)),
                  pl.BlockSpec((1, LANES), vec),
                  pl.BlockSpec((1, LRU_BW, LRU_BW), lambda c, i: (c, 0, 0)),
                  pl.BlockSpec((1, LANES), vec),
                  pl.BlockSpec((1, LANES), vec)],
        out_specs=pl.BlockSpec((tb, LANES), lambda c, i: (i, c)),
        out_shape=jax.ShapeDtypeStruct((s, LRU_W), BF16),
        scratch_shapes=[pltpu.VMEM((8, LANES), F32), pltpu.VMEM((1, LANES), F32)],
        compiler_params=_cparams(("parallel", "arbitrary")),
        name="rg_lru",
    )(g32, g32, cw, cb, wa, ba, wi, bi, lam)


def _layer_norm_rows(y, g, b):
    mu = jnp.mean(y, axis=-1, keepdims=True)
    yc = y - mu
    var = jnp.mean(yc * yc, axis=-1, keepdims=True)
    return yc * lax.rsqrt(var + LN_EPS) * g + b


def _oproj_ln_kernel(xa_ref, xb_ref, xc_ref, wa_ref, wb_ref, wc_ref, h_ref, g_ref, b_ref,
                     o32_ref, oslab_ref):
    half = h_ref.shape[0] // 2
    for part in range(2):
        rows = slice(part * half, (part + 1) * half)
        mix = (jnp.dot(xa_ref[rows, :], wa_ref[...], preferred_element_type=F32)
               + jnp.dot(xb_ref[rows, :], wb_ref[...], preferred_element_type=F32)
               + jnp.dot(xc_ref[rows, :], wc_ref[...], preferred_element_type=F32))
        o = _layer_norm_rows(ALPHA * h_ref[rows, :] + mix, g_ref[...], b_ref[...])
        o32_ref[rows, :] = o
        _store_slabs(oslab_ref.at[pl.ds(part * half * SLAB_ROWS, half * SLAB_ROWS), :], o)


def _oproj_ln(xa, xb, xc, wa, wb, wc, h, g, b, tm):
    s = h.shape[0]
    row = lambda i: (i, 0)
    fixed = lambda i: (0, 0)
    return pl.pallas_call(
        _oproj_ln_kernel,
        grid=(s // tm,),
        in_specs=[pl.BlockSpec((tm, FOX_W), row), pl.BlockSpec((tm, LRU_W), row),
                  pl.BlockSpec((tm, DIFF_W), row),
                  pl.BlockSpec((FOX_W, D_MODEL), fixed), pl.BlockSpec((LRU_W, D_MODEL), fixed),
                  pl.BlockSpec((DIFF_W, D_MODEL), fixed),
                  pl.BlockSpec((tm, D_MODEL), row),
                  pl.BlockSpec((1, D_MODEL), fixed), pl.BlockSpec((1, D_MODEL), fixed)],
        out_specs=[pl.BlockSpec((tm, D_MODEL), row), pl.BlockSpec((tm * SLAB_ROWS, LANES), row)],
        out_shape=[jax.ShapeDtypeStruct((s, D_MODEL), F32),
                   jax.ShapeDtypeStruct((s * SLAB_ROWS, LANES), F32)],
        compiler_params=_cparams(("parallel",)),
        name="oproj_ln1",
    )(xa, xb, xc, wa, wb, wc, h, g, b)


def _router_kernel(h_ref, whi_ref, wlo_ref, b_ref, o_ref):
    h = h_ref[...]
    hi = h.astype(BF16)
    lo = (h - hi.astype(F32)).astype(BF16)
    whi = whi_ref[...]
    logits = (jnp.dot(hi, whi, preferred_element_type=F32)
              + jnp.dot(hi, wlo_ref[...], preferred_element_type=F32)
              + jnp.dot(lo, whi, preferred_element_type=F32)) + b_ref[...]
    lane = lax.broadcasted_iota(jnp.int32, logits.shape, 1)
    big = jnp.int32(1 << 20)

    def first_lane(cond):
        return jnp.min(jnp.where(cond, lane, big), axis=1, keepdims=True)

    gm = lane < N_GROUPS
    gl = jnp.where(gm, logits, -jnp.inf)
    gmax = jnp.max(gl, axis=1, keepdims=True)
    gexp = jnp.where(gm, jnp.exp(logits - gmax), 0.0)
    gprob = gexp / jnp.sum(gexp, axis=1, keepdims=True)
    gidx = first_lane(gl == gmax)
    g_weight = jnp.sum(jnp.where(lane == gidx, gprob, 0.0), axis=1, keepdims=True)

    e0 = N_GROUPS + EXPERTS_PER_GROUP * gidx
    em = jnp.logical_and(lane >= e0, lane < e0 + EXPERTS_PER_GROUP)
    el = jnp.where(em, logits, -jnp.inf)
    emax = jnp.max(el, axis=1, keepdims=True)
    eexp = jnp.where(em, jnp.exp(logits - emax), 0.0)
    eprob = jnp.where(em, eexp / jnp.sum(eexp, axis=1, keepdims=True), -1.0)
    p1 = jnp.max(eprob, axis=1, keepdims=True)
    i1 = first_lane(eprob == p1)
    eprob2 = jnp.where(lane == i1, -1.0, eprob)
    p2 = jnp.max(eprob2, axis=1, keepdims=True)
    i2 = first_lane(eprob2 == p2)
    den = p1 + p2
    gate1 = g_weight * (p1 / den)
    gate2 = g_weight * (p2 / den)
    id1 = (i1 - N_GROUPS).astype(F32)
    id2 = (i2 - N_GROUPS).astype(F32)
    o_ref[...] = jnp.where(lane == 0, id1,
                           jnp.where(lane == 1, id2,
                                     jnp.where(lane == 2, gate1,
                                               jnp.where(lane == 3, gate2, 0.0))))


def _router(h, whi, wlo, bias, tm):
    s = h.shape[0]
    fixed = lambda i: (0, 0)
    return pl.pallas_call(
        _router_kernel,
        grid=(s // tm,),
        in_specs=[pl.BlockSpec((tm, D_MODEL), lambda i: (i, 0)),
                  pl.BlockSpec((D_MODEL, LANES), fixed), pl.BlockSpec((D_MODEL, LANES), fixed),
                  pl.BlockSpec((1, LANES), fixed)],
        out_specs=pl.BlockSpec((tm, LANES), lambda i: (i, 0)),
        out_shape=jax.ShapeDtypeStruct((s, LANES), F32),
        compiler_params=_cparams(("parallel",)),
        name="moe_router",
    )(h, whi, wlo, bias)


SLAB_ROWS = D_MODEL // LANES
SLAB_PITCH = SLAB_ROWS + 4


def _store_slabs(o_ref, x):
    rows = x.shape[0]
    for c in range(SLAB_ROWS):
        o_ref[pl.ds(c, rows, stride=SLAB_ROWS), :] = x[:, c * LANES:(c + 1) * LANES]


def _load_slabs(buf_ref, rows):
    return jnp.concatenate([buf_ref[pl.ds(c, rows, stride=SLAB_PITCH), :]
                            for c in range(SLAB_ROWS)], axis=1)


GATHER_GROUP = 8


def _issue_slab_gather(row_of, ngroups, src_hbm, dst_ref, sem, both_queues=False):
    def body(gi, carry):
        for u in range(GATHER_GROUP):
            r = gi * GATHER_GROUP + u
            src0 = pl.multiple_of(row_of(r) * SLAB_ROWS, SLAB_ROWS)
            dst0 = pl.multiple_of(r * SLAB_PITCH, 4)
            pltpu.make_async_copy(src_hbm.at[pl.ds(src0, SLAB_ROWS), :],
                                  dst_ref.at[pl.ds(dst0, SLAB_ROWS), :],
                                  sem).start(priority=u % 2 if both_queues else 0)
        return carry
    lax.fori_loop(0, ngroups, body, 0)


def _wait_slab_gather(ngroups, buf_ref, sem):
    part = buf_ref.at[pl.ds(0, ngroups * (GATHER_GROUP * SLAB_ROWS)), :]
    pltpu.make_async_copy(part, part, sem).wait()


def _moe_kernel(te_ref, nt_ref, start_ref, cnt_ref, tok_ref, wfirst_ref, wslot_ref, wnext_ref,
                h_hbm, wg_hbm, wu_hbm, wd_hbm, y_ref, xbuf, wgb, wub, wdb, sem, wsem, *, tm):
    t = pl.program_id(0)
    nt = nt_ref[0]
    slot = t % 2
    last = tok_ref.shape[0] - 1

    def groups(tile):
        return (cnt_ref[tile] + (GATHER_GROUP - 1)) // GATHER_GROUP

    def issue(tile, sl):
        s0 = start_ref[tile]
        _issue_slab_gather(lambda r: tok_ref[jnp.minimum(s0 + r, last)], groups(tile), h_hbm,
                           xbuf.at[sl], sem.at[sl])

    def weight_copies(e, ws):
        return (pltpu.make_async_copy(wg_hbm.at[e], wgb.at[ws], wsem.at[ws]),
                pltpu.make_async_copy(wu_hbm.at[e], wub.at[ws], wsem.at[ws]),
                pltpu.make_async_copy(wd_hbm.at[e], wdb.at[ws], wsem.at[ws]))

    @pl.when(t == 0)
    def _():
        xbuf[...] = jnp.zeros_like(xbuf)
        issue(0, 0)
        for c in weight_copies(te_ref[0], wslot_ref[0]):
            c.start(priority=1)

    @pl.when(t + 1 < nt)
    def _():
        issue(t + 1, 1 - slot)

    @pl.when(t < nt)
    def _():
        ws = wslot_ref[t]

        @pl.when(wfirst_ref[t] == 1)
        def _():
            for c in weight_copies(te_ref[t], ws):
                c.wait()

            @pl.when(wnext_ref[t] >= 0)
            def _():
                for c in weight_copies(wnext_ref[t], 1 - ws):
                    c.start(priority=1)

        _wait_slab_gather(groups(t), xbuf.at[slot], sem.at[slot])
        x = _load_slabs(xbuf.at[slot], tm).astype(BF16)
        hg = jnp.dot(x, wgb[ws].astype(BF16), preferred_element_type=F32)
        hu = jnp.dot(x, wub[ws].astype(BF16), preferred_element_type=F32)
        act = (hg * jax.nn.sigmoid(hg)) * hu
        y = jnp.dot(act.astype(BF16), wdb[ws].astype(BF16), preferred_element_type=F32)
        _store_slabs(y_ref, y)

    @pl.when(t >= nt)
    def _():
        y_ref[...] = jnp.zeros_like(y_ref)


def _moe_grouped(plan, tok_sorted, h_slab, wg, wu, wd, tm):
    tile_expert, ntiles, tile_start, tile_count, wfirst, wslot, wnext = plan
    nt_max = tile_expert.shape[0]
    grid_spec = pltpu.PrefetchScalarGridSpec(
        num_scalar_prefetch=8,
        grid=(nt_max,),
        in_specs=[pl.BlockSpec(memory_space=pl.ANY)] * 4,
        out_specs=pl.BlockSpec((tm * SLAB_ROWS, LANES), lambda t, *_: (t, 0)),
        scratch_shapes=[pltpu.VMEM((2, tm * SLAB_PITCH, LANES), F32),
                        pltpu.VMEM((2, D_MODEL, D_FF_EXPERT), F32),
                        pltpu.VMEM((2, D_MODEL, D_FF_EXPERT), F32),
                        pltpu.VMEM((2, D_FF_EXPERT, D_MODEL), F32),
                        pltpu.SemaphoreType.DMA((2,)), pltpu.SemaphoreType.DMA((2,))],
    )
    return pl.pallas_call(
        functools.partial(_moe_kernel, tm=tm),
        grid_spec=grid_spec,
        out_shape=jax.ShapeDtypeStruct((nt_max * tm * SLAB_ROWS, LANES), F32),
        compiler_params=_cparams(("arbitrary",)),
        name="moe_experts",
    )(tile_expert, ntiles, tile_start, tile_count, tok_sorted, wfirst, wslot, wnext,
      h_slab, wg, wu, wd)


def _combine_ln_kernel(pos_ref, y_hbm, h_ref, meta_ref, g_ref, b_ref, o32_ref, o16_ref,
                       ybuf, sem, *, tm):
    t = pl.program_id(0)
    nt = pl.num_programs(0)
    slot = t % 2
    ngroups = tm // GATHER_GROUP

    def issue(tile, sl):
        for j in range(2):
            _issue_slab_gather(lambda r, j=j: pos_ref[(tile * tm + r) * 2 + j], ngroups, y_hbm,
                               ybuf.at[sl, j], sem.at[sl], both_queues=True)

    @pl.when(t == 0)
    def _():
        issue(0, 0)

    @pl.when(t + 1 < nt)
    def _():
        issue(t + 1, 1 - slot)

    _wait_slab_gather(ngroups, ybuf.at[slot, 0], sem.at[slot])
    _wait_slab_gather(ngroups, ybuf.at[slot, 1], sem.at[slot])
    meta = meta_ref[...]
    ffn = (meta[:, 2:3] * _load_slabs(ybuf.at[slot, 0], tm)
           + meta[:, 3:4] * _load_slabs(ybuf.at[slot, 1], tm))
    o = _layer_norm_rows(ALPHA * h_ref[...] + ffn, g_ref[...], b_ref[...])
    o32_ref[...] = o
    o16_ref[...] = o.astype(BF16)


def _combine_ln(pos, y_slab, h, meta, g, b, tm):
    s = h.shape[0]
    row = lambda t, pos: (t, 0)
    fixed = lambda t, pos: (0, 0)
    grid_spec = pltpu.PrefetchScalarGridSpec(
        num_scalar_prefetch=1,
        grid=(s // tm,),
        in_specs=[pl.BlockSpec(memory_space=pl.ANY),
                  pl.BlockSpec((tm, D_MODEL), row),
                  pl.BlockSpec((tm, LANES), row),
                  pl.BlockSpec((1, D_MODEL), fixed), pl.BlockSpec((1, D_MODEL), fixed)],
        out_specs=[pl.BlockSpec((tm, D_MODEL), row), pl.BlockSpec((tm, D_MODEL), row)],
        scratch_shapes=[pltpu.VMEM((2, 2, tm * SLAB_PITCH, LANES), F32),
                        pltpu.SemaphoreType.DMA((2,))],
    )
    return pl.pallas_call(
        functools.partial(_combine_ln_kernel, tm=tm),
        grid_spec=grid_spec,
        out_shape=[jax.ShapeDtypeStruct((s, D_MODEL), F32),
                   jax.ShapeDtypeStruct((s, D_MODEL), BF16)],
        compiler_params=_cparams(("arbitrary",)),
        name="combine_ln2",
    )(pos, y_slab, h, meta, g, b)


def _dispatch_plan(meta, tm, layer):
    t = meta.shape[0]
    eid = meta[:, 0:2].astype(jnp.int32).reshape(-1)
    onehot = (eid[:, None] == jnp.arange(N_EXPERTS, dtype=jnp.int32)[None, :]).astype(jnp.int32)
    csum = jnp.cumsum(onehot, axis=0)
    counts = csum[-1]
    padded = ((counts + tm - 1) // tm) * tm
    pend = jnp.cumsum(padded)
    poff = pend - padded
    off = jnp.cumsum(counts) - counts
    pos = jnp.sum(onehot * (csum + poff[None, :]), axis=1) - 1
    p_max = 2 * t + N_EXPERTS * tm
    nt_max = p_max // tm
    ntiles = (pend[-1] // tm).astype(jnp.int32)
    tile_ids = jnp.minimum(jnp.arange(nt_max, dtype=jnp.int32), ntiles - 1)
    tile_expert = jnp.sum((pend[None, :] // tm <= tile_ids[:, None]).astype(jnp.int32), axis=1)
    tile_expert = jnp.minimum(tile_expert, N_EXPERTS - 1).astype(jnp.int32)
    tok_sorted = (jnp.argsort(eid, stable=True) // 2).astype(jnp.int32)
    k0 = jnp.arange(nt_max, dtype=jnp.int32) * tm - poff[tile_expert]
    tile_start = jnp.clip(off[tile_expert] + k0, 0, 2 * t - 1).astype(jnp.int32)
    tile_count = jnp.clip(counts[tile_expert] - k0, 0, tm).astype(jnp.int32)
    ids = jnp.arange(N_EXPERTS, dtype=jnp.int32)
    has = counts > 0
    rank = jnp.cumsum(has.astype(jnp.int32)) - 1
    later = jnp.logical_and(has[None, :], ids[None, :] > ids[:, None])
    next_e = jnp.min(jnp.where(later, ids[None, :], N_EXPERTS), axis=1)
    next_e = jnp.where(next_e < N_EXPERTS, next_e + layer * N_EXPERTS, -1)
    wfirst = (k0 == 0).astype(jnp.int32)
    wslot = (rank[tile_expert] % 2).astype(jnp.int32)
    wnext = next_e[tile_expert].astype(jnp.int32)
    plan = (tile_expert + layer * N_EXPERTS, ntiles.reshape(1), tile_start, tile_count,
            wfirst, wslot, wnext)
    return plan, tok_sorted, pos.astype(jnp.int32)


def _pad_lanes(v, width=LANES):
    v = v.reshape(1, -1).astype(F32)
    return jnp.pad(v, ((0, 0), (0, width - v.shape[1])))


TM_PROJ, TN_PROJ = 4096, 256
TB_PREP = 256
TB_FOX = 512
TB_DIFF = 512
TB_LRU, SUB_LRU = 1024, 256
TB_ROPE = 512
TM_OPROJ = 512
TM_ROUTER = 512
TM_MOE = 256
TM_COMBINE = 256


def _layer(l, h32, h16, pos_f, invf, p, stacked):
    s = h32.shape[0]
    wt = _prep_w_in(stacked["w_in"], l)
    s16 = jnp.concatenate([jnp.full((1, FOX_W), FOX_HD ** -0.5 * LOG2E, F32),
                           jnp.ones((1, G16_W - FOX_W), F32)], axis=1)
    s32 = jnp.ones((1, G32_W), F32)
    tm_proj = min(s, TM_PROJ)
    g16 = _matmul_nt(h16, wt, 0, G16_W, s16, BF16, tm_proj, TN_PROJ, "in_proj_bf16")
    g32 = _matmul_nt(h16, wt, G16_W, G32_W, s32, F32, tm_proj, TN_PROJ, "in_proj_f32")

    tb_prep = min(s, TB_PREP)
    qaux, kaux, stat = _fox_prep(g32, g16, _pad_lanes(p["b_f"]), tb_prep)
    tb_fox = min(s, TB_FOX)
    jmin = _fox_skip_table(stat, tb_fox // tb_prep)
    out_a = _fox_attention(jmin, g16, qaux, kaux, tb_fox)

    out_b = _lru_branch(g32, p["conv_w"], p["conv_b"].reshape(1, -1),
                        p["w_a"].astype(BF16), p["b_a"].reshape(1, -1),
                        p["w_i"].astype(BF16), p["b_i"].reshape(1, -1),
                        p["lru_lambda"].reshape(1, -1), min(s, TB_LRU), min(s, SUB_LRU))

    qk16 = _rope(g32, pos_f, invf, min(s, TB_ROPE))
    lamv = jnp.concatenate([_pad_lanes(p["lam_q1"]), _pad_lanes(p["lam_k1"]),
                            _pad_lanes(p["lam_q2"]), _pad_lanes(p["lam_k2"])], axis=0)
    lam_init = 0.8 - 0.6 * math.exp(-0.3 * l)
    out_c = _diff_attention(qk16, g16, lamv, p["subln_g"].reshape(1, -1), lam_init,
                            min(s, TB_DIFF))

    w_o = p["w_o"].astype(BF16)
    h1_32, h1_slab = _oproj_ln(out_a, out_b, out_c, w_o[:FOX_W], w_o[FOX_W:FOX_W + LRU_W],
                               w_o[FOX_W + LRU_W:], h32, p["ln1_g"].reshape(1, -1),
                               p["ln1_b"].reshape(1, -1), min(s, TM_OPROJ))

    w_rt = jnp.concatenate([p["w_group"], p["w_router"],
                            jnp.zeros((D_MODEL, LANES - N_GROUPS - N_EXPERTS), F32)], axis=1)
    w_rt_hi = w_rt.astype(BF16)
    w_rt_lo = (w_rt - w_rt_hi.astype(F32)).astype(BF16)
    b_rt = _pad_lanes(jnp.concatenate([p["b_group"], p["b_router"]]))
    meta = _router(h1_32, w_rt_hi, w_rt_lo, b_rt, min(s, TM_ROUTER))

    tm_moe = min(s, TM_MOE)
    plan, tok_sorted, pos = _dispatch_plan(meta, tm_moe, l)
    y_slab = _moe_grouped(plan, tok_sorted, h1_slab, stacked["w_gate"], stacked["w_up"],
                          stacked["w_down"], tm_moe)
    return _combine_ln(pos, y_slab, h1_32, meta, p["ln2_g"].reshape(1, -1),
                       p["ln2_b"].reshape(1, -1), min(s, TM_COMBINE))


def _forward(x, positions, params):
    bsz, s, d = x.shape
    assert bsz == 1 and d == D_MODEL
    h32 = x.reshape(s, d)
    h16 = h32.astype(BF16)
    half = ROT_DIM // 2
    inv_freq = ROPE_THETA ** (-jnp.arange(half, dtype=F32) * 2.0 / ROT_DIM)
    dlane = jnp.arange(LANES) % DIFF_HD
    invf = jnp.where(dlane < ROT_DIM, inv_freq[dlane % half], 0.0).reshape(1, LANES).astype(F32)
    pos_f = jnp.broadcast_to(positions.astype(F32).reshape(s, 1), (s, LANES))
    big = ("w_in", "w_gate", "w_up", "w_down")
    w_view = jnp.transpose(params["w_in"].reshape(DEPTH, K_TILES, LANES, N_IN), (3, 1, 0, 2))
    stacked = {
        "w_in": w_view.reshape(N_IN * ROWS_PER_COL, LANES),
        "w_gate": params["w_gate"].reshape(DEPTH * N_EXPERTS, D_MODEL, D_FF_EXPERT),
        "w_up": params["w_up"].reshape(DEPTH * N_EXPERTS, D_MODEL, D_FF_EXPERT),
        "w_down": params["w_down"].reshape(DEPTH * N_EXPERTS, D_FF_EXPERT, D_MODEL),
    }
    for l in range(DEPTH):
        p = {k: v[l] for k, v in params.items() if k not in big}
        h32, h16 = _layer(l, h32, h16, pos_f, invf, p, stacked)
    return h32.reshape(bsz, s, d)


def kernel(x, positions, w_in, b_f, conv_w, conv_b, w_a, b_a, w_i, b_i, lru_lambda, lam_q1, lam_k1, lam_q2, lam_k2, subln_g, w_o, ln1_g, ln1_b, w_group, b_group, w_router, b_router, w_gate, w_up, w_down, ln2_g, ln2_b):
    params = dict(w_in=w_in, b_f=b_f, conv_w=conv_w, conv_b=conv_b, w_a=w_a, b_a=b_a, w_i=w_i,
                  b_i=b_i, lru_lambda=lru_lambda, lam_q1=lam_q1, lam_k1=lam_k1, lam_q2=lam_q2,
                  lam_k2=lam_k2, subln_g=subln_g, w_o=w_o, ln1_g=ln1_g, ln1_b=ln1_b,
                  w_group=w_group, b_group=b_group, w_router=w_router, b_router=b_router,
                  w_gate=w_gate, w_up=w_up, w_down=w_down, ln2_g=ln2_g, ln2_b=ln2_b)
    return _forward(x, positions, params)
```

```python
import functools
import math

import jax
import jax.numpy as jnp
from jax import lax
from jax.experimental import pallas as pl
from jax.experimental.pallas import tpu as pltpu

F32 = jnp.float32
BF16 = jnp.bfloat16

D_MODEL = 2048
DEPTH = 2
CHUNK = 64
FOX_HEADS = 6
FOX_HD = 128
FOX_W = FOX_HEADS * FOX_HD
LRU_W = 768
LRU_BLOCKS = 6
LRU_BW = LRU_W // LRU_BLOCKS
CONV_W = 4
LRU_C = 8.0
DIFF_HEADS = 4
DIFF_HD = 64
DIFF_VD = 2 * DIFF_HD
DIFF_W = DIFF_HEADS * DIFF_VD
ROT_DIM = DIFF_HD // 4
ROPE_THETA = 500000.0
N_GROUPS = 4
EXPERTS_PER_GROUP = 8
N_EXPERTS = N_GROUPS * EXPERTS_PER_GROUP
D_FF_EXPERT = 512
ALPHA = (2.0 * DEPTH) ** 0.25
LN_EPS = 1e-5

LANES = 128
NEG_BIG = -1e30
LOG2E = math.log2(math.e)
SKIP_BITS = 72.0
VMEM_LIMIT = 56 * 1024 * 1024

G16_W = 3 * FOX_W + DIFF_W
G32_W = 2 * DIFF_W + 2 * LRU_W + 2 * LANES
G16_QA, G16_KA, G16_VA, G16_VC = 0, 6, 12, 18
G32_XR, G32_GR, G32_FA = 8, 14, 20


def _cparams(sem, vmem=VMEM_LIMIT):
    return pltpu.CompilerParams(dimension_semantics=sem, vmem_limit_bytes=vmem)


def _mm_kernel(x_ref, wt_ref, s_ref, o_ref):
    acc = lax.dot_general(x_ref[...], wt_ref[...], (((1,), (1,)), ((), ())),
                          preferred_element_type=F32)
    o_ref[...] = (acc * s_ref[...]).astype(o_ref.dtype)


def _matmul_nt(x, wt, row0, n, s, out_dtype, tm, tn, name):
    m, k = x.shape
    off = row0 // tn
    return pl.pallas_call(
        _mm_kernel,
        grid=(m // tm, n // tn),
        in_specs=[pl.BlockSpec((tm, k), lambda i, j: (i, 0)),
                  pl.BlockSpec((tn, k), lambda i, j: (j + off, 0)),
                  pl.BlockSpec((1, tn), lambda i, j: (0, j))],
        out_specs=pl.BlockSpec((tm, tn), lambda i, j: (i, j)),
        out_shape=jax.ShapeDtypeStruct((m, n), out_dtype),
        compiler_params=_cparams(("parallel", "parallel")),
        name=name,
    )(x, wt, s)


_IN_QA = 0
_IN_KA = _IN_QA + FOX_W
_IN_VA = _IN_KA + FOX_W
_IN_FA = _IN_VA + FOX_W
_IN_XR = _IN_FA + FOX_HEADS
_IN_GR = _IN_XR + LRU_W
_IN_QC = _IN_GR + LRU_W
_IN_KC = _IN_QC + DIFF_W
_IN_VC = _IN_KC + DIFF_W
N_IN = _IN_VC + DIFF_W


K_TILES = D_MODEL // LANES
ROWS_PER_COL = K_TILES * DEPTH
WCHUNK = 256
_SEGMENTS = ((_IN_QA, FOX_W), (_IN_KA, FOX_W), (_IN_VA, FOX_W), (_IN_VC, DIFF_W),
             (_IN_QC, DIFF_W), (_IN_KC, DIFF_W), (_IN_XR, LRU_W), (_IN_GR, LRU_W),
             (_IN_FA, FOX_HEADS))
_CHUNK_COL0 = [c0 + d for c0, width in _SEGMENTS for d in range(0, width, WCHUNK)]
_CHUNK_COLS = [min(WCHUNK, width - d) for c0, width in _SEGMENTS for d in range(0, width, WCHUNK)]
assert len(_CHUNK_COL0) * WCHUNK == G16_W + G32_W and max(_CHUNK_COL0) + WCHUNK <= N_IN


def _wprep_kernel(col0_ref, cols_ref, w_hbm, o_ref, buf, sem, *, layer):
    c = pl.program_id(0)
    slot = c % 2

    def chunk_copy(ci, sl):
        r0 = pl.multiple_of(col0_ref[ci] * ROWS_PER_COL, ROWS_PER_COL)
        return pltpu.make_async_copy(w_hbm.at[pl.ds(r0, WCHUNK * ROWS_PER_COL), :], buf.at[sl],
                                     sem.at[sl])

    @pl.when(c == 0)
    def _():
        chunk_copy(0, 0).start()

    @pl.when(c + 1 < pl.num_programs(0))
    def _():
        chunk_copy(c + 1, 1 - slot).start()

    chunk_copy(c, slot).wait()
    keep = lax.broadcasted_iota(jnp.int32, (WCHUNK, LANES), 0) < cols_ref[c]
    for kt in range(K_TILES):
        piece = buf[slot, pl.ds(kt * DEPTH + layer, WCHUNK, stride=ROWS_PER_COL), :]
        o_ref[:, kt * LANES:(kt + 1) * LANES] = jnp.where(keep, piece, 0.0).astype(BF16)


def _prep_w_in(w_view, layer):
    nchunks = len(_CHUNK_COL0)
    grid_spec = pltpu.PrefetchScalarGridSpec(
        num_scalar_prefetch=2,
        grid=(nchunks,),
        in_specs=[pl.BlockSpec(memory_space=pl.ANY)],
        out_specs=pl.BlockSpec((WCHUNK, D_MODEL), lambda c, col0, cols: (c, 0)),
        scratch_shapes=[pltpu.VMEM((2, WCHUNK * ROWS_PER_COL, LANES), F32),
                        pltpu.SemaphoreType.DMA((2,))],
    )
    return pl.pallas_call(
        functools.partial(_wprep_kernel, layer=layer),
        grid_spec=grid_spec,
        out_shape=jax.ShapeDtypeStruct((nchunks * WCHUNK, D_MODEL), BF16),
        compiler_params=_cparams(("arbitrary",)),
        name="w_in_regroup",
    )(jnp.asarray(_CHUNK_COL0, jnp.int32), jnp.asarray(_CHUNK_COLS, jnp.int32), w_view)


def _split3(x):
    hi = x.astype(BF16)
    r1 = x - hi.astype(F32)
    mid = r1.astype(BF16)
    lo = (r1 - mid.astype(F32)).astype(BF16)
    return hi, mid, lo


def _fox_prep_kernel(fa_ref, bf_ref, q_ref, k_ref, qaux_ref, kaux_ref, stat_ref, carry_ref):
    @pl.when(pl.program_id(0) == 0)
    def _():
        carry_ref[...] = jnp.zeros_like(carry_ref)

    z = fa_ref[...] + bf_ref[...]
    lf = (jnp.minimum(z, 0.0) - jnp.log1p(jnp.exp(-jnp.abs(z)))) * LOG2E
    t = z.shape[0]
    row = lax.broadcasted_iota(jnp.int32, (t, t), 0)
    col = lax.broadcasted_iota(jnp.int32, (t, t), 1)
    tri = jnp.where(row >= col, 1.0, 0.0).astype(BF16)
    hi, mid, lo = _split3(lf)
    cs = (jnp.dot(tri, hi, preferred_element_type=F32)
          + jnp.dot(tri, mid, preferred_element_type=F32)
          + jnp.dot(tri, lo, preferred_element_type=F32)) + carry_ref[...]
    carry_ref[...] = cs[t - 1:t, :]

    lane = lax.broadcasted_iota(jnp.int32, (t, LANES), 1)
    lane1 = lax.broadcasted_iota(jnp.int32, (1, LANES), 1)
    qn = jnp.zeros((1, LANES), F32)
    kn = jnp.zeros((1, LANES), F32)
    for h in range(FOX_HEADS):
        c = jnp.broadcast_to(jnp.sum(jnp.where(lane == h, cs, 0.0), axis=1, keepdims=True),
                             (t, LANES))
        c_hi = c.astype(BF16).astype(F32)
        r1 = c - c_hi
        c_mid = r1.astype(BF16).astype(F32)
        c_lo = (r1 - c_mid).astype(BF16).astype(F32)
        kaux = jnp.where(lane == 0, c_hi, jnp.where(lane == 1, c_mid, jnp.where(
            lane == 2, c_lo, jnp.where(lane < 6, 1.0, 0.0))))
        qaux = jnp.where(lane < 3, -1.0, jnp.where(lane == 3, c_hi, jnp.where(
            lane == 4, c_mid, jnp.where(lane == 5, c_lo, 0.0))))
        kaux_ref[h] = kaux.astype(BF16)
        qaux_ref[h] = qaux.astype(BF16)
        qh = q_ref[:, h * FOX_HD:(h + 1) * FOX_HD].astype(F32)
        kh = k_ref[:, h * FOX_HD:(h + 1) * FOX_HD].astype(F32)
        q2 = jnp.max(jnp.sum(qh * qh, axis=1, keepdims=True), axis=0, keepdims=True)
        k2 = jnp.max(jnp.sum(kh * kh, axis=1, keepdims=True), axis=0, keepdims=True)
        qn = jnp.where(lane1 == h, q2, qn)
        kn = jnp.where(lane1 == h, k2, kn)
    stat_ref[0] = jnp.concatenate([qn, kn, cs[0:1, :], cs[t - 1:t, :],
                                   jnp.zeros((4, LANES), F32)], axis=0)


def _fox_prep(g32, g16, bf_pad, tb):
    s = g32.shape[0]
    qk_blocks = FOX_W // LANES
    return pl.pallas_call(
        _fox_prep_kernel,
        grid=(s // tb,),
        in_specs=[pl.BlockSpec((tb, LANES), lambda i: (i, G32_FA)),
                  pl.BlockSpec((1, LANES), lambda i: (0, 0)),
                  pl.BlockSpec((tb, FOX_W), lambda i: (i, G16_QA // qk_blocks)),
                  pl.BlockSpec((tb, FOX_W), lambda i: (i, G16_KA // qk_blocks))],
        out_specs=[pl.BlockSpec((FOX_HEADS, tb, LANES), lambda i: (0, i, 0)),
                   pl.BlockSpec((FOX_HEADS, tb, LANES), lambda i: (0, i, 0)),
                   pl.BlockSpec((1, 8, LANES), lambda i: (i, 0, 0))],
        out_shape=[jax.ShapeDtypeStruct((FOX_HEADS, s, LANES), BF16),
                   jax.ShapeDtypeStruct((FOX_HEADS, s, LANES), BF16),
                   jax.ShapeDtypeStruct((s // tb, 8, LANES), F32)],
        scratch_shapes=[pltpu.VMEM((1, LANES), F32)],
        compiler_params=_cparams(("arbitrary",)),
        name="fox_prep",
    )(g32, bf_pad, g16, g16)


def _fox_skip_table(stat, per_block):
    nb = stat.shape[0] // per_block
    st = stat.reshape(nb, per_block, 8, LANES)
    qn = jnp.sqrt(jnp.max(st[:, :, 0, :FOX_HEADS], axis=1))
    kn = jnp.sqrt(jnp.max(st[:, :, 1, :FOX_HEADS], axis=1))
    c_first = st[:, 0, 2, :FOX_HEADS]
    c_last = st[:, per_block - 1, 3, :FOX_HEADS]
    kmax = jnp.max(kn, axis=0, keepdims=True)
    bound = 1.01 * qn * (kmax + kn) + c_first
    skip = (bound[:, None, :] - c_last[None, :, :]) < -SKIP_BITS
    jj = jnp.arange(nb)
    skip = jnp.logical_and(skip, (jj[None, :] < jj[:, None])[:, :, None])
    lead = jnp.cumsum(1 - skip.astype(jnp.int32), axis=1) == 0
    return jnp.sum(lead.astype(jnp.int32), axis=1).T.reshape(-1)


def _rope_kernel(x_ref, pos_ref, invf_ref, o_ref):
    ang = pos_ref[...] * invf_ref[...]
    c = jnp.cos(ang)
    s = jnp.sin(ang)
    half = ROT_DIM // 2
    d = lax.broadcasted_iota(jnp.int32, (1, LANES), 1) % DIFF_HD
    sa = jnp.where(d >= half, s, 0.0)
    sb = jnp.where(d < half, -s, 0.0)
    nblk = x_ref.shape[1] // LANES
    for j in range(nblk):
        t = x_ref[:, j * LANES:(j + 1) * LANES]
        r = t * c + pltpu.roll(t, half, 1) * sa + pltpu.roll(t, LANES - half, 1) * sb
        if j < nblk // 2:
            r = r * (DIFF_HD ** -0.5 * LOG2E)
        o_ref[:, j * LANES:(j + 1) * LANES] = r.astype(o_ref.dtype)


def _rope(g32, pos_f, invf, tb):
    s = g32.shape[0]
    w = 2 * DIFF_W
    return pl.pallas_call(
        _rope_kernel,
        grid=(s // tb,),
        in_specs=[pl.BlockSpec((tb, w), lambda i: (i, 0)),
                  pl.BlockSpec((tb, LANES), lambda i: (i, 0)),
                  pl.BlockSpec((1, LANES), lambda i: (0, 0))],
        out_specs=pl.BlockSpec((tb, w), lambda i: (i, 0)),
        out_shape=jax.ShapeDtypeStruct((s, w), BF16),
        compiler_params=_cparams(("parallel",)),
        name="diff_rope",
    )(g32, pos_f, invf)


def _flash_step(v, s, m_ref, l_ref, acc_ref):
    nq = s.shape[1]
    cw = min(nq, FLASH_COLS)
    for c0 in range(0, nq, cw):
        cols = slice(c0, c0 + cw)
        sc = s[:, cols]
        m_prev = m_ref[:, cols]
        m_new = jnp.maximum(m_prev, jnp.max(sc, axis=0, keepdims=True))
        a = jnp.exp2(m_prev - m_new)
        p = jnp.exp2(sc - m_new)
        l_ref[:, cols] = a * l_ref[:, cols] + jnp.sum(p, axis=0, keepdims=True)
        pv = lax.dot_general(v, p.astype(v.dtype), (((0,), (0,)), ((), ())),
                             preferred_element_type=F32)
        acc_ref[:, cols] = a * acc_ref[:, cols] + pv
        m_ref[:, cols] = m_new


def _flash_pipeline(j0, i, qk, sm):
    npairs = (i - j0) // 2
    qk(j0, 0)

    def body(p, carry):
        j = j0 + 2 * p
        qk(j + 1, 1)
        sm(j, 0, False)
        qk(j + 2, 0)
        sm(j + 1, 1, False)
        return carry

    lax.fori_loop(0, npairs, body, 0)
    jn = j0 + 2 * npairs

    @pl.when(jn == i)
    def _():
        sm(i, 0, True)

    @pl.when(jn != i)
    def _():
        qk(i, 1)
        sm(jn, 0, False)
        sm(i, 1, True)


def _init_state(m_ref, l_ref, acc_ref):
    m_ref[...] = jnp.full(m_ref.shape, NEG_BIG, F32)
    l_ref[...] = jnp.zeros(l_ref.shape, F32)
    acc_ref[...] = jnp.zeros(acc_ref.shape, F32)


HEADS_PER_STEP = 2
FLASH_COLS = 256


def _head_lanes(g):
    return slice(g * LANES, (g + 1) * LANES)


def _fox_kernel(jmin_ref, q_ref, qaux_ref, k_ref, kaux_ref, v_ref, o_ref,
                m_ref, l_ref, acc_ref, sa_ref, sb_ref, *, tb):
    hp = pl.program_id(0)
    i = pl.program_id(1)
    heads = range(HEADS_PER_STEP)
    qf = [jnp.concatenate([q_ref[:, _head_lanes(g)], qaux_ref[g]], axis=1)
          for g in heads]
    _init_state(m_ref, l_ref, acc_ref)
    s_refs = (sa_ref, sb_ref)

    def qk(j, slot):
        k0 = pl.multiple_of(j * tb, tb)
        for g in heads:
            kf = jnp.concatenate([k_ref[pl.ds(k0, tb), _head_lanes(g)],
                                  kaux_ref[g, pl.ds(k0, tb), :]], axis=1)
            s_refs[slot][g] = lax.dot_general(kf, qf[g], (((1,), (1,)), ((), ())),
                                              preferred_element_type=F32)

    def sm(j, slot, masked):
        k0 = pl.multiple_of(j * tb, tb)
        for g in heads:
            s = s_refs[slot][g]
            if masked:
                kk = lax.broadcasted_iota(jnp.int32, (tb, tb), 0)
                qq = lax.broadcasted_iota(jnp.int32, (tb, tb), 1)
                s = jnp.where(kk <= qq, s, NEG_BIG)
            v = v_ref[pl.ds(k0, tb), _head_lanes(g)]
            _flash_step(v, s, m_ref.at[g], l_ref.at[g], acc_ref.at[g])

    nq = pl.num_programs(1)
    j0 = jmin_ref[hp * HEADS_PER_STEP * nq + i]
    for g in range(1, HEADS_PER_STEP):
        j0 = jnp.minimum(j0, jmin_ref[(hp * HEADS_PER_STEP + g) * nq + i])
    _flash_pipeline(j0, i, qk, sm)
    for g in heads:
        o = acc_ref[g] / l_ref[g]
        o_ref[:, _head_lanes(g)] = o.T.astype(o_ref.dtype)


def _fox_attention(jmin, g16, qaux, kaux, tb):
    s = g16.shape[0]
    hps = HEADS_PER_STEP
    wide = hps * LANES
    grid_spec = pltpu.PrefetchScalarGridSpec(
        num_scalar_prefetch=1,
        grid=(FOX_HEADS // hps, s // tb),
        in_specs=[pl.BlockSpec((tb, wide), lambda h, i, jm: (i, G16_QA // hps + h)),
                  pl.BlockSpec((hps, tb, LANES), lambda h, i, jm: (h, i, 0)),
                  pl.BlockSpec((s, wide), lambda h, i, jm: (0, G16_KA // hps + h)),
                  pl.BlockSpec((hps, s, LANES), lambda h, i, jm: (h, 0, 0)),
                  pl.BlockSpec((s, wide), lambda h, i, jm: (0, G16_VA // hps + h))],
        out_specs=pl.BlockSpec((tb, wide), lambda h, i, jm: (i, h)),
        scratch_shapes=[pltpu.VMEM((hps, 1, tb), F32), pltpu.VMEM((hps, 1, tb), F32),
                        pltpu.VMEM((hps, FOX_HD, tb), F32),
                        pltpu.VMEM((hps, tb, tb), F32), pltpu.VMEM((hps, tb, tb), F32)],
    )
    return pl.pallas_call(
        functools.partial(_fox_kernel, tb=tb),
        grid_spec=grid_spec,
        out_shape=jax.ShapeDtypeStruct((s, FOX_W), BF16),
        compiler_params=_cparams(("parallel", "parallel")),
        name="fox_attention",
    )(jmin, g16, qaux, g16, kaux, g16)


def _diff_kernel(q_ref, k_ref, v_ref, lamv_ref, g_ref, o_ref, m_ref, l_ref, acc_ref,
                 sa_ref, sb_ref, *, tq, tk, lam_init):
    i = pl.program_id(1)
    heads = range(HEADS_PER_STEP)
    lane = lax.broadcasted_iota(jnp.int32, (tq, LANES), 1)
    qq2 = []
    for g in heads:
        q = q_ref[:, _head_lanes(g)].astype(F32)
        qq2.append(jnp.concatenate([jnp.where(lane < DIFF_HD, q, 0.0),
                                    jnp.where(lane >= DIFF_HD, q, 0.0)], axis=0).astype(BF16))
    _init_state(m_ref, l_ref, acc_ref)
    s_refs = (sa_ref, sb_ref)

    def qk(j, slot):
        k0 = pl.multiple_of(j * tk, tk)
        for g in heads:
            k = k_ref[pl.ds(k0, tk), _head_lanes(g)]
            s_refs[slot][g] = lax.dot_general(k, qq2[g], (((1,), (1,)), ((), ())),
                                              preferred_element_type=F32)

    def sm(j, slot, diag):
        k0 = pl.multiple_of(j * tk, tk)
        for g in heads:
            s = s_refs[slot][g]
            if diag is not None:
                kc = (lax.broadcasted_iota(jnp.int32, (tk, 2 * tq), 0) + diag * tk) // CHUNK
                qc = (lax.broadcasted_iota(jnp.int32, (tk, 2 * tq), 1) % tq) // CHUNK
                s = jnp.where(kc <= qc, s, NEG_BIG)
            v = v_ref[pl.ds(k0, tk), _head_lanes(g)]
            _flash_step(v, s, m_ref.at[g], l_ref.at[g], acc_ref.at[g])

    if tq == tk:
        _flash_pipeline(0, i, qk, lambda j, slot, masked: sm(j, slot, 0 if masked else None))
    else:
        qk(0, 0)

        def body(p, carry):
            j = 2 * p
            qk(j + 1, 1)
            sm(j, 0, None)
            qk(j + 2, 0)
            sm(j + 1, 1, None)
            return carry

        lax.fori_loop(0, i, body, 0)
        qk(2 * i + 1, 1)
        sm(2 * i, 0, 0)
        sm(2 * i + 1, 1, 1)

    lv = lamv_ref[...]
    lam = (jnp.exp(jnp.sum(lv[0:1] * lv[1:2], axis=1, keepdims=True))
           - jnp.exp(jnp.sum(lv[2:3] * lv[3:4], axis=1, keepdims=True)) + lam_init)
    for g in heads:
        on = acc_ref[g] / l_ref[g]
        o = (on[:, :tq] - lam * on[:, tq:]).T
        ms = jnp.mean(o * o, axis=-1, keepdims=True)
        o = o * lax.rsqrt(ms + LN_EPS) * g_ref[...] * (1.0 - lam_init)
        o_ref[:, _head_lanes(g)] = o.astype(o_ref.dtype)


def _diff_attention(qk16, g16, lamv, subg, lam_init, tq, tk):
    s = g16.shape[0]
    assert tq in (tk, 2 * tk)
    hps = HEADS_PER_STEP
    wide = hps * LANES
    return pl.pallas_call(
        functools.partial(_diff_kernel, tq=tq, tk=tk, lam_init=lam_init),
        grid=(DIFF_HEADS // hps, s // tq),
        in_specs=[pl.BlockSpec((tq, wide), lambda h, i: (i, h)),
                  pl.BlockSpec((s, wide), lambda h, i: (0, DIFF_HEADS // hps + h)),
                  pl.BlockSpec((s, wide), lambda h, i: (0, G16_VC // hps + h)),
                  pl.BlockSpec((4, LANES), lambda h, i: (0, 0)),
                  pl.BlockSpec((1, LANES), lambda h, i: (0, 0))],
        out_specs=pl.BlockSpec((tq, wide), lambda h, i: (i, h)),
        out_shape=jax.ShapeDtypeStruct((s, DIFF_W), BF16),
        scratch_shapes=[pltpu.VMEM((hps, 1, 2 * tq), F32), pltpu.VMEM((hps, 1, 2 * tq), F32),
                        pltpu.VMEM((hps, DIFF_VD, 2 * tq), F32),
                        pltpu.VMEM((hps, tk, 2 * tq), F32), pltpu.VMEM((hps, tk, 2 * tq), F32)],
        compiler_params=_cparams(("parallel", "parallel")),
        name="diff_attention",
    )(qk16, qk16, g16, lamv, subg)


def _shift_rows(x, d, fill):
    if d % 8 == 0:
        return jnp.concatenate([jnp.full((d, x.shape[1]), fill, x.dtype), x[:x.shape[0] - d]],
                               axis=0)
    rows = lax.broadcasted_iota(jnp.int32, x.shape, 0)
    return jnp.where(rows >= d, pltpu.roll(x, d, 0), fill)


def _lru_kernel(xr_ref, gr_ref, cw_ref, cb_ref, wa_ref, ba_ref, wi_ref, bi_ref, lam_ref,
                o_ref, halo_ref, h_ref, *, tb, sub):
    @pl.when(pl.program_id(1) == 0)
    def _():
        halo_ref[...] = jnp.zeros_like(halo_ref)
        h_ref[...] = jnp.zeros_like(h_ref)

    x = xr_ref[...]
    xe = jnp.concatenate([halo_ref[...], x], axis=0)
    halo_ref[...] = x[tb - 8:tb, :]
    cw = cw_ref[...]
    xc = cb_ref[...] + cw[CONV_W - 1:CONV_W, :] * x
    for j in range(CONV_W - 1):
        sh = CONV_W - 1 - j
        xc = xc + cw[j:j + 1, :] * pltpu.roll(xe, sh, 0)[8:8 + tb, :]

    xcb = xc.astype(BF16)
    r = jax.nn.sigmoid(jnp.dot(xcb, wa_ref[0], preferred_element_type=F32) + ba_ref[...])
    ig = jax.nn.sigmoid(jnp.dot(xcb, wi_ref[0], preferred_element_type=F32) + bi_ref[...])
    lam = lam_ref[...]
    ls = jnp.minimum(lam, 0.0) - jnp.log1p(jnp.exp(-jnp.abs(lam)))
    log_a = LRU_C * r * ls
    a = jnp.exp(log_a)
    z2 = 2.0 * log_a
    e2 = jnp.exp(z2)
    small = jnp.where(e2 == 1.0, -z2, (1.0 - e2) * z2 / jnp.log(e2))
    neg_expm1 = jnp.where(z2 < -1.0, 1.0 - e2, small)
    u = jnp.sqrt(neg_expm1) * (ig * xc)

    h = h_ref[...]
    for c in range(tb // sub):
        ac = a[c * sub:(c + 1) * sub, :]
        uc = u[c * sub:(c + 1) * sub, :]
        d = 1
        while d < sub:
            uc = ac * _shift_rows(uc, d, 0.0) + uc
            ac = ac * _shift_rows(ac, d, 1.0)
            d *= 2
        hc = uc + ac * h
        h = hc[sub - 1:sub, :]
        g = gr_ref[c * sub:(c + 1) * sub, :]
        gelu = 0.5 * g * (1.0 + jnp.tanh(math.sqrt(2.0 / math.pi) * (g + 0.044715 * (g * g * g))))
        o_ref[c * sub:(c + 1) * sub, :] = (gelu * hc).astype(o_ref.dtype)
    h_ref[...] = h


def _lru_branch(g32, cw, cb, wa, ba, wi, bi, lam, tb, sub):
    s = g32.shape[0]
    vec = lambda c, i: (0, c)
    return pl.pallas_call(
        functools.partial(_lru_kernel, tb=tb, sub=sub),
        grid=(LRU_BLOCKS, s // tb),
        in_specs=[pl.BlockSpec((tb, LANES), lambda c, i: (i, G32_XR + c)),
                  pl.BlockSpec((tb, LANES), lambda c, i: (i, G32_GR + c)),
                  pl.BlockSpec((CONV_W, LANES), vec),
                  pl.BlockSpec((1, LANES), vec),
                  pl.BlockSpec((1, LRU_BW, LRU_BW), lambda c, i: (c, 0, 0)),
                  pl.BlockSpec((1, LANES), vec),
                  pl.BlockSpec((1, LRU_BW, LRU_BW), lambda c, i: (c, 0, 0)),
                  pl.BlockSpec((1, LANES), vec),
                  pl.BlockSpec((1, LANES), vec)],
        out_specs=pl.BlockSpec((tb, LANES), lambda c, i: (i, c)),
        out_shape=jax.ShapeDtypeStruct((s, LRU_W), BF16),
        scratch_shapes=[pltpu.VMEM((8, LANES), F32), pltpu.VMEM((1, LANES), F32)],
        compiler_params=_cparams(("parallel", "arbitrary")),
        name="rg_lru",
    )(g32, g32, cw, cb, wa, ba, wi, bi, lam)


def _layer_norm_rows(y, g, b):
    mu = jnp.mean(y, axis=-1, keepdims=True)
    yc = y - mu
    var = jnp.mean(yc * yc, axis=-1, keepdims=True)
    return yc * lax.rsqrt(var + LN_EPS) * g + b


def _oproj_ln_kernel(xa_ref, xb_ref, xc_ref, wa_ref, wb_ref, wc_ref, h_ref, g_ref, b_ref,
                     o32_ref, oslab_ref):
    half = h_ref.shape[0] // 2
    for part in range(2):
        rows = slice(part * half, (part + 1) * half)
        mix = (jnp.dot(xa_ref[rows, :], wa_ref[...], preferred_element_type=F32)
               + jnp.dot(xb_ref[rows, :], wb_ref[...], preferred_element_type=F32)
               + jnp.dot(xc_ref[rows, :], wc_ref[...], preferred_element_type=F32))
        o = _layer_norm_rows(ALPHA * h_ref[rows, :] + mix, g_ref[...], b_ref[...])
        o32_ref[rows, :] = o
        _store_slabs(oslab_ref.at[pl.ds(part * half * SLAB_ROWS, half * SLAB_ROWS), :], o)


def _oproj_ln(xa, xb, xc, wa, wb, wc, h, g, b, tm):
    s = h.shape[0]
    row = lambda i: (i, 0)
    fixed = lambda i: (0, 0)
    return pl.pallas_call(
        _oproj_ln_kernel,
        grid=(s // tm,),
        in_specs=[pl.BlockSpec((tm, FOX_W), row), pl.BlockSpec((tm, LRU_W), row),
                  pl.BlockSpec((tm, DIFF_W), row),
                  pl.BlockSpec((FOX_W, D_MODEL), fixed), pl.BlockSpec((LRU_W, D_MODEL), fixed),
                  pl.BlockSpec((DIFF_W, D_MODEL), fixed),
                  pl.BlockSpec((tm, D_MODEL), row),
                  pl.BlockSpec((1, D_MODEL), fixed), pl.BlockSpec((1, D_MODEL), fixed)],
        out_specs=[pl.BlockSpec((tm, D_MODEL), row), pl.BlockSpec((tm * SLAB_ROWS, LANES), row)],
        out_shape=[jax.ShapeDtypeStruct((s, D_MODEL), F32),
                   jax.ShapeDtypeStruct((s * SLAB_ROWS, LANES), F32)],
        compiler_params=_cparams(("parallel",)),
        name="oproj_ln1",
    )(xa, xb, xc, wa, wb, wc, h, g, b)


def _router_kernel(h_ref, whi_ref, wlo_ref, b_ref, o_ref):
    h = h_ref[...]
    hi = h.astype(BF16)
    lo = (h - hi.astype(F32)).astype(BF16)
    whi = whi_ref[...]
    logits = (jnp.dot(hi, whi, preferred_element_type=F32)
              + jnp.dot(hi, wlo_ref[...], preferred_element_type=F32)
              + jnp.dot(lo, whi, preferred_element_type=F32)) + b_ref[...]
    lane = lax.broadcasted_iota(jnp.int32, logits.shape, 1)
    big = jnp.int32(1 << 20)

    def first_lane(cond):
        return jnp.min(jnp.where(cond, lane, big), axis=1, keepdims=True)

    gm = lane < N_GROUPS
    gl = jnp.where(gm, logits, -jnp.inf)
    gmax = jnp.max(gl, axis=1, keepdims=True)
    gexp = jnp.where(gm, jnp.exp(logits - gmax), 0.0)
    gprob = gexp / jnp.sum(gexp, axis=1, keepdims=True)
    gidx = first_lane(gl == gmax)
    g_weight = jnp.sum(jnp.where(lane == gidx, gprob, 0.0), axis=1, keepdims=True)

    e0 = N_GROUPS + EXPERTS_PER_GROUP * gidx
    em = jnp.logical_and(lane >= e0, lane < e0 + EXPERTS_PER_GROUP)
    el = jnp.where(em, logits, -jnp.inf)
    emax = jnp.max(el, axis=1, keepdims=True)
    eexp = jnp.where(em, jnp.exp(logits - emax), 0.0)
    eprob = jnp.where(em, eexp / jnp.sum(eexp, axis=1, keepdims=True), -1.0)
    p1 = jnp.max(eprob, axis=1, keepdims=True)
    i1 = first_lane(eprob == p1)
    eprob2 = jnp.where(lane == i1, -1.0, eprob)
    p2 = jnp.max(eprob2, axis=1, keepdims=True)
    i2 = first_lane(eprob2 == p2)
    den = p1 + p2
    gate1 = g_weight * (p1 / den)
    gate2 = g_weight * (p2 / den)
    id1 = (i1 - N_GROUPS).astype(F32)
    id2 = (i2 - N_GROUPS).astype(F32)
    o_ref[...] = jnp.where(lane == 0, id1,
                           jnp.where(lane == 1, id2,
                                     jnp.where(lane == 2, gate1,
                                               jnp.where(lane == 3, gate2, 0.0))))


def _router(h, whi, wlo, bias, tm):
    s = h.shape[0]
    fixed = lambda i: (0, 0)
    return pl.pallas_call(
        _router_kernel,
        grid=(s // tm,),
        in_specs=[pl.BlockSpec((tm, D_MODEL), lambda i: (i, 0)),
                  pl.BlockSpec((D_MODEL, LANES), fixed), pl.BlockSpec((D_MODEL, LANES), fixed),
                  pl.BlockSpec((1, LANES), fixed)],
        out_specs=pl.BlockSpec((tm, LANES), lambda i: (i, 0)),
        out_shape=jax.ShapeDtypeStruct((s, LANES), F32),
        compiler_params=_cparams(("parallel",)),
        name="moe_router",
    )(h, whi, wlo, bias)


SLAB_ROWS = D_MODEL // LANES
SLAB_PITCH = SLAB_ROWS + 4


def _store_slabs(o_ref, x):
    rows = x.shape[0]
    for c in range(SLAB_ROWS):
        o_ref[pl.ds(c, rows, stride=SLAB_ROWS), :] = x[:, c * LANES:(c + 1) * LANES]


def _load_slabs(buf_ref, rows):
    return jnp.concatenate([buf_ref[pl.ds(c, rows, stride=SLAB_PITCH), :]
                            for c in range(SLAB_ROWS)], axis=1)


GATHER_GROUP = 8


def _issue_slab_gather(row_of, ngroups, src_hbm, dst_ref, sem, both_queues=False):
    def body(gi, carry):
        for u in range(GATHER_GROUP):
            r = gi * GATHER_GROUP + u
            src0 = pl.multiple_of(row_of(r) * SLAB_ROWS, SLAB_ROWS)
            dst0 = pl.multiple_of(r * SLAB_PITCH, 4)
            pltpu.make_async_copy(src_hbm.at[pl.ds(src0, SLAB_ROWS), :],
                                  dst_ref.at[pl.ds(dst0, SLAB_ROWS), :],
                                  sem).start(priority=u % 2 if both_queues else 0)
        return carry
    lax.fori_loop(0, ngroups, body, 0)


def _wait_slab_gather(ngroups, buf_ref, sem):
    part = buf_ref.at[pl.ds(0, ngroups * (GATHER_GROUP * SLAB_ROWS)), :]
    pltpu.make_async_copy(part, part, sem).wait()


def _moe_kernel(te_ref, nt_ref, start_ref, cnt_ref, tok_ref, wfirst_ref, wslot_ref, wnext_ref,
                h_hbm, wg_hbm, wu_hbm, wd_hbm, y_ref, xbuf, wgb, wub, wdb, sem, wsem, *, tm):
    t = pl.program_id(0)
    nt = nt_ref[0]
    slot = t % 2
    last = tok_ref.shape[0] - 1

    def groups(tile):
        return (cnt_ref[tile] + (GATHER_GROUP - 1)) // GATHER_GROUP

    def issue(tile, sl):
        s0 = start_ref[tile]
        _issue_slab_gather(lambda r: tok_ref[jnp.minimum(s0 + r, last)], groups(tile), h_hbm,
                           xbuf.at[sl], sem.at[sl])

    def weight_copies(e, ws):
        return (pltpu.make_async_copy(wg_hbm.at[e], wgb.at[ws], wsem.at[ws]),
                pltpu.make_async_copy(wu_hbm.at[e], wub.at[ws], wsem.at[ws]),
                pltpu.make_async_copy(wd_hbm.at[e], wdb.at[ws], wsem.at[ws]))

    @pl.when(t == 0)
    def _():
        xbuf[...] = jnp.zeros_like(xbuf)
        issue(0, 0)
        for c in weight_copies(te_ref[0], wslot_ref[0]):
            c.start(priority=1)

    @pl.when(t + 1 < nt)
    def _():
        issue(t + 1, 1 - slot)

    @pl.when(t < nt)
    def _():
        ws = wslot_ref[t]

        @pl.when(wfirst_ref[t] == 1)
        def _():
            for c in weight_copies(te_ref[t], ws):
                c.wait()

            @pl.when(wnext_ref[t] >= 0)
            def _():
                for c in weight_copies(wnext_ref[t], 1 - ws):
                    c.start(priority=1)

        _wait_slab_gather(groups(t), xbuf.at[slot], sem.at[slot])
        x = _load_slabs(xbuf.at[slot], tm).astype(BF16)
        hg = jnp.dot(x, wgb[ws].astype(BF16), preferred_element_type=F32)
        hu = jnp.dot(x, wub[ws].astype(BF16), preferred_element_type=F32)
        act = (hg * jax.nn.sigmoid(hg)) * hu
        y = jnp.dot(act.astype(BF16), wdb[ws].astype(BF16), preferred_element_type=F32)
        _store_slabs(y_ref, y)

    @pl.when(t >= nt)
    def _():
        y_ref[...] = jnp.zeros_like(y_ref)


def _moe_grouped(plan, tok_sorted, h_slab, wg, wu, wd, tm):
    tile_expert, ntiles, tile_start, tile_count, wfirst, wslot, wnext = plan
    nt_max = tile_expert.shape[0]
    grid_spec = pltpu.PrefetchScalarGridSpec(
        num_scalar_prefetch=8,
        grid=(nt_max,),
        in_specs=[pl.BlockSpec(memory_space=pl.ANY)] * 4,
        out_specs=pl.BlockSpec((tm * SLAB_ROWS, LANES), lambda t, *_: (t, 0)),
        scratch_shapes=[pltpu.VMEM((2, tm * SLAB_PITCH, LANES), F32),
                        pltpu.VMEM((2, D_MODEL, D_FF_EXPERT), F32),
                        pltpu.VMEM((2, D_MODEL, D_FF_EXPERT), F32),
                        pltpu.VMEM((2, D_FF_EXPERT, D_MODEL), F32),
                        pltpu.SemaphoreType.DMA((2,)), pltpu.SemaphoreType.DMA((2,))],
    )
    return pl.pallas_call(
        functools.partial(_moe_kernel, tm=tm),
        grid_spec=grid_spec,
        out_shape=jax.ShapeDtypeStruct((nt_max * tm * SLAB_ROWS, LANES), F32),
        compiler_params=_cparams(("arbitrary",)),
        name="moe_experts",
    )(tile_expert, ntiles, tile_start, tile_count, tok_sorted, wfirst, wslot, wnext,
      h_slab, wg, wu, wd)


def _combine_ln_kernel(pos_ref, y_hbm, h_ref, meta_ref, g_ref, b_ref, o32_ref, o16_ref,
                       ybuf, sem, *, tm):
    t = pl.program_id(0)
    nt = pl.num_programs(0)
    slot = t % 2
    ngroups = tm // GATHER_GROUP

    def issue(tile, sl):
        for j in range(2):
            _issue_slab_gather(lambda r, j=j: pos_ref[(tile * tm + r) * 2 + j], ngroups, y_hbm,
                               ybuf.at[sl, j], sem.at[sl], both_queues=True)

    @pl.when(t == 0)
    def _():
        issue(0, 0)

    @pl.when(t + 1 < nt)
    def _():
        issue(t + 1, 1 - slot)

    _wait_slab_gather(ngroups, ybuf.at[slot, 0], sem.at[slot])
    _wait_slab_gather(ngroups, ybuf.at[slot, 1], sem.at[slot])
    meta = meta_ref[...]
    ffn = (meta[:, 2:3] * _load_slabs(ybuf.at[slot, 0], tm)
           + meta[:, 3:4] * _load_slabs(ybuf.at[slot, 1], tm))
    o = _layer_norm_rows(ALPHA * h_ref[...] + ffn, g_ref[...], b_ref[...])
    o32_ref[...] = o
    o16_ref[...] = o.astype(BF16)


def _combine_ln(pos, y_slab, h, meta, g, b, tm):
    s = h.shape[0]
    row = lambda t, pos: (t, 0)
    fixed = lambda t, pos: (0, 0)
    grid_spec = pltpu.PrefetchScalarGridSpec(
        num_scalar_prefetch=1,
        grid=(s // tm,),
        in_specs=[pl.BlockSpec(memory_space=pl.ANY),
                  pl.BlockSpec((tm, D_MODEL), row),
                  pl.BlockSpec((tm, LANES), row),
                  pl.BlockSpec((1, D_MODEL), fixed), pl.BlockSpec((1, D_MODEL), fixed)],
        out_specs=[pl.BlockSpec((tm, D_MODEL), row), pl.BlockSpec((tm, D_MODEL), row)],
        scratch_shapes=[pltpu.VMEM((2, 2, tm * SLAB_PITCH, LANES), F32),
                        pltpu.SemaphoreType.DMA((2,))],
    )
    return pl.pallas_call(
        functools.partial(_combine_ln_kernel, tm=tm),
        grid_spec=grid_spec,
        out_shape=[jax.ShapeDtypeStruct((s, D_MODEL), F32),
                   jax.ShapeDtypeStruct((s, D_MODEL), BF16)],
        compiler_params=_cparams(("arbitrary",)),
        name="combine_ln2",
    )(pos, y_slab, h, meta, g, b)


def _dispatch_plan(meta, tm, layer):
    t = meta.shape[0]
    eid = meta[:, 0:2].astype(jnp.int32).reshape(-1)
    onehot = (eid[:, None] == jnp.arange(N_EXPERTS, dtype=jnp.int32)[None, :]).astype(jnp.int32)
    csum = jnp.cumsum(onehot, axis=0)
    counts = csum[-1]
    padded = ((counts + tm - 1) // tm) * tm
    pend = jnp.cumsum(padded)
    poff = pend - padded
    off = jnp.cumsum(counts) - counts
    pos = jnp.sum(onehot * (csum + poff[None, :]), axis=1) - 1
    p_max = 2 * t + N_EXPERTS * tm
    nt_max = p_max // tm
    ntiles = (pend[-1] // tm).astype(jnp.int32)
    tile_ids = jnp.minimum(jnp.arange(nt_max, dtype=jnp.int32), ntiles - 1)
    tile_expert = jnp.sum((pend[None, :] // tm <= tile_ids[:, None]).astype(jnp.int32), axis=1)
    tile_expert = jnp.minimum(tile_expert, N_EXPERTS - 1).astype(jnp.int32)
    tok_sorted = (jnp.argsort(eid, stable=True) // 2).astype(jnp.int32)
    k0 = jnp.arange(nt_max, dtype=jnp.int32) * tm - poff[tile_expert]
    tile_start = jnp.clip(off[tile_expert] + k0, 0, 2 * t - 1).astype(jnp.int32)
    tile_count = jnp.clip(counts[tile_expert] - k0, 0, tm).astype(jnp.int32)
    ids = jnp.arange(N_EXPERTS, dtype=jnp.int32)
    has = counts > 0
    rank = jnp.cumsum(has.astype(jnp.int32)) - 1
    later = jnp.logical_and(has[None, :], ids[None, :] > ids[:, None])
    next_e = jnp.min(jnp.where(later, ids[None, :], N_EXPERTS), axis=1)
    next_e = jnp.where(next_e < N_EXPERTS, next_e + layer * N_EXPERTS, -1)
    wfirst = (k0 == 0).astype(jnp.int32)
    wslot = (rank[tile_expert] % 2).astype(jnp.int32)
    wnext = next_e[tile_expert].astype(jnp.int32)
    plan = (tile_expert + layer * N_EXPERTS, ntiles.reshape(1), tile_start, tile_count,
            wfirst, wslot, wnext)
    return plan, tok_sorted, pos.astype(jnp.int32)


def _pad_lanes(v, width=LANES):
    v = v.reshape(1, -1).astype(F32)
    return jnp.pad(v, ((0, 0), (0, width - v.shape[1])))


TM_PROJ, TN_PROJ = 4096, 256
TB_PREP = 256
TB_FOX = 512
TQ_DIFF, TK_DIFF = 1024, 512
TB_LRU, SUB_LRU = 1024, 256
TB_ROPE = 512
TM_OPROJ = 512
TM_ROUTER = 512
TM_MOE = 256
TM_COMBINE = 256


def _layer(l, h32, h16, pos_f, invf, p, stacked):
    s = h32.shape[0]
    wt = _prep_w_in(stacked["w_in"], l)
    s16 = jnp.concatenate([jnp.full((1, FOX_W), FOX_HD ** -0.5 * LOG2E, F32),
                           jnp.ones((1, G16_W - FOX_W), F32)], axis=1)
    s32 = jnp.ones((1, G32_W), F32)
    tm_proj = min(s, TM_PROJ)
    g16 = _matmul_nt(h16, wt, 0, G16_W, s16, BF16, tm_proj, TN_PROJ, "in_proj_bf16")
    g32 = _matmul_nt(h16, wt, G16_W, G32_W, s32, F32, tm_proj, TN_PROJ, "in_proj_f32")

    tb_prep = min(s, TB_PREP)
    qaux, kaux, stat = _fox_prep(g32, g16, _pad_lanes(p["b_f"]), tb_prep)
    tb_fox = min(s, TB_FOX)
    jmin = _fox_skip_table(stat, tb_fox // tb_prep)
    out_a = _fox_attention(jmin, g16, qaux, kaux, tb_fox)

    out_b = _lru_branch(g32, p["conv_w"], p["conv_b"].reshape(1, -1),
                        p["w_a"].astype(BF16), p["b_a"].reshape(1, -1),
                        p["w_i"].astype(BF16), p["b_i"].reshape(1, -1),
                        p["lru_lambda"].reshape(1, -1), min(s, TB_LRU), min(s, SUB_LRU))

    qk16 = _rope(g32, pos_f, invf, min(s, TB_ROPE))
    lamv = jnp.concatenate([_pad_lanes(p["lam_q1"]), _pad_lanes(p["lam_k1"]),
                            _pad_lanes(p["lam_q2"]), _pad_lanes(p["lam_k2"])], axis=0)
    lam_init = 0.8 - 0.6 * math.exp(-0.3 * l)
    tk_diff = min(s, TK_DIFF)
    tq_diff = TQ_DIFF if s % TQ_DIFF == 0 else tk_diff
    out_c = _diff_attention(qk16, g16, lamv, p["subln_g"].reshape(1, -1), lam_init,
                            tq_diff, tk_diff)

    w_o = p["w_o"].astype(BF16)
    h1_32, h1_slab = _oproj_ln(out_a, out_b, out_c, w_o[:FOX_W], w_o[FOX_W:FOX_W + LRU_W],
                               w_o[FOX_W + LRU_W:], h32, p["ln1_g"].reshape(1, -1),
                               p["ln1_b"].reshape(1, -1), min(s, TM_OPROJ))

    w_rt = jnp.concatenate([p["w_group"], p["w_router"],
                            jnp.zeros((D_MODEL, LANES - N_GROUPS - N_EXPERTS), F32)], axis=1)
    w_rt_hi = w_rt.astype(BF16)
    w_rt_lo = (w_rt - w_rt_hi.astype(F32)).astype(BF16)
    b_rt = _pad_lanes(jnp.concatenate([p["b_group"], p["b_router"]]))
    meta = _router(h1_32, w_rt_hi, w_rt_lo, b_rt, min(s, TM_ROUTER))

    tm_moe = min(s, TM_MOE)
    plan, tok_sorted, pos = _dispatch_plan(meta, tm_moe, l)
    y_slab = _moe_grouped(plan, tok_sorted, h1_slab, stacked["w_gate"], stacked["w_up"],
                          stacked["w_down"], tm_moe)
    return _combine_ln(pos, y_slab, h1_32, meta, p["ln2_g"].reshape(1, -1),
                       p["ln2_b"].reshape(1, -1), min(s, TM_COMBINE))


def _forward(x, positions, params):
    bsz, s, d = x.shape
    assert bsz == 1 and d == D_MODEL
    h32 = x.reshape(s, d)
    h16 = h32.astype(BF16)
    half = ROT_DIM // 2
    inv_freq = ROPE_THETA ** (-jnp.arange(half, dtype=F32) * 2.0 / ROT_DIM)
    dlane = jnp.arange(LANES) % DIFF_HD
    invf = jnp.where(dlane < ROT_DIM, inv_freq[dlane % half], 0.0).reshape(1, LANES).astype(F32)
    pos_f = jnp.broadcast_to(positions.astype(F32).reshape(s, 1), (s, LANES))
    big = ("w_in", "w_gate", "w_up", "w_down")
    w_view = jnp.transpose(params["w_in"].reshape(DEPTH, K_TILES, LANES, N_IN), (3, 1, 0, 2))
    stacked = {
        "w_in": w_view.reshape(N_IN * ROWS_PER_COL, LANES),
        "w_gate": params["w_gate"].reshape(DEPTH * N_EXPERTS, D_MODEL, D_FF_EXPERT),
        "w_up": params["w_up"].reshape(DEPTH * N_EXPERTS, D_MODEL, D_FF_EXPERT),
        "w_down": params["w_down"].reshape(DEPTH * N_EXPERTS, D_FF_EXPERT, D_MODEL),
    }
    for l in range(DEPTH):
        p = {k: v[l] for k, v in params.items() if k not in big}
        h32, h16 = _layer(l, h32, h16, pos_f, invf, p, stacked)
    return h32.reshape(bsz, s, d)


def kernel(x, positions, w_in, b_f, conv_w, conv_b, w_a, b_a, w_i, b_i, lru_lambda, lam_q1, lam_k1, lam_q2, lam_k2, subln_g, w_o, ln1_g, ln1_b, w_group, b_group, w_router, b_router, w_gate, w_up, w_down, ln2_g, ln2_b):
    params = dict(w_in=w_in, b_f=b_f, conv_w=conv_w, conv_b=conv_b, w_a=w_a, b_a=b_a, w_i=w_i,
                  b_i=b_i, lru_lambda=lru_lambda, lam_q1=lam_q1, lam_k1=lam_k1, lam_q2=lam_q2,
                  lam_k2=lam_k2, subln_g=subln_g, w_o=w_o, ln1_g=ln1_g, ln1_b=ln1_b,
                  w_group=w_group, b_group=b_group, w_router=w_router, b_router=b_router,
                  w_gate=w_gate, w_up=w_up, w_down=w_down, ln2_g=ln2_g, ln2_b=ln2_b)
    return _forward(x, positions, params)
```

```python
import functools
import math

import jax
import jax.numpy as jnp
from jax import lax
from jax.experimental import pallas as pl
from jax.experimental.pallas import tpu as pltpu

F32 = jnp.float32
BF16 = jnp.bfloat16

D_MODEL = 2048
DEPTH = 2
CHUNK = 64
FOX_HEADS = 6
FOX_HD = 128
FOX_W = FOX_HEADS * FOX_HD
LRU_W = 768
LRU_BLOCKS = 6
LRU_BW = LRU_W // LRU_BLOCKS
CONV_W = 4
LRU_C = 8.0
DIFF_HEADS = 4
DIFF_HD = 64
DIFF_VD = 2 * DIFF_HD
DIFF_W = DIFF_HEADS * DIFF_VD
ROT_DIM = DIFF_HD // 4
ROPE_THETA = 500000.0
N_GROUPS = 4
EXPERTS_PER_GROUP = 8
N_EXPERTS = N_GROUPS * EXPERTS_PER_GROUP
D_FF_EXPERT = 512
ALPHA = (2.0 * DEPTH) ** 0.25
LN_EPS = 1e-5

LANES = 128
NEG_BIG = -1e30
LOG2E = math.log2(math.e)
SKIP_BITS = 72.0
VMEM_LIMIT = 56 * 1024 * 1024

G16_W = 3 * FOX_W + DIFF_W
G32_W = 2 * DIFF_W + 2 * LRU_W + 2 * LANES
G16_QA, G16_KA, G16_VA, G16_VC = 0, 6, 12, 18
G32_XR, G32_GR, G32_FA = 8, 14, 20


def _cparams(sem, vmem=VMEM_LIMIT):
    return pltpu.CompilerParams(dimension_semantics=sem, vmem_limit_bytes=vmem)


def _mm_kernel(x_ref, wt_ref, s_ref, o_ref):
    acc = lax.dot_general(x_ref[...], wt_ref[...], (((1,), (1,)), ((), ())),
                          preferred_element_type=F32)
    o_ref[...] = (acc * s_ref[...]).astype(o_ref.dtype)


def _matmul_nt(x, wt, row0, n, s, out_dtype, tm, tn, name):
    m, k = x.shape
    off = row0 // tn
    return pl.pallas_call(
        _mm_kernel,
        grid=(m // tm, n // tn),
        in_specs=[pl.BlockSpec((tm, k), lambda i, j: (i, 0)),
                  pl.BlockSpec((tn, k), lambda i, j: (j + off, 0)),
                  pl.BlockSpec((1, tn), lambda i, j: (0, j))],
        out_specs=pl.BlockSpec((tm, tn), lambda i, j: (i, j)),
        out_shape=jax.ShapeDtypeStruct((m, n), out_dtype),
        compiler_params=_cparams(("parallel", "parallel")),
        name=name,
    )(x, wt, s)


_IN_QA = 0
_IN_KA = _IN_QA + FOX_W
_IN_VA = _IN_KA + FOX_W
_IN_FA = _IN_VA + FOX_W
_IN_XR = _IN_FA + FOX_HEADS
_IN_GR = _IN_XR + LRU_W
_IN_QC = _IN_GR + LRU_W
_IN_KC = _IN_QC + DIFF_W
_IN_VC = _IN_KC + DIFF_W
N_IN = _IN_VC + DIFF_W


K_TILES = D_MODEL // LANES
ROWS_PER_COL = K_TILES * DEPTH
WCHUNK = 256
_SEGMENTS = ((_IN_QA, FOX_W), (_IN_KA, FOX_W), (_IN_VA, FOX_W), (_IN_VC, DIFF_W),
             (_IN_QC, DIFF_W), (_IN_KC, DIFF_W), (_IN_XR, LRU_W), (_IN_GR, LRU_W),
             (_IN_FA, FOX_HEADS))
_CHUNK_COL0 = [c0 + d for c0, width in _SEGMENTS for d in range(0, width, WCHUNK)]
_CHUNK_COLS = [min(WCHUNK, width - d) for c0, width in _SEGMENTS for d in range(0, width, WCHUNK)]
assert len(_CHUNK_COL0) * WCHUNK == G16_W + G32_W and max(_CHUNK_COL0) + WCHUNK <= N_IN


def _wprep_kernel(col0_ref, cols_ref, w_hbm, o_ref, buf, sem, *, layer):
    c = pl.program_id(0)
    slot = c % 2

    def chunk_copy(ci, sl):
        r0 = pl.multiple_of(col0_ref[ci] * ROWS_PER_COL, ROWS_PER_COL)
        return pltpu.make_async_copy(w_hbm.at[pl.ds(r0, WCHUNK * ROWS_PER_COL), :], buf.at[sl],
                                     sem.at[sl])

    @pl.when(c == 0)
    def _():
        chunk_copy(0, 0).start()

    @pl.when(c + 1 < pl.num_programs(0))
    def _():
        chunk_copy(c + 1, 1 - slot).start()

    chunk_copy(c, slot).wait()
    keep = lax.broadcasted_iota(jnp.int32, (WCHUNK, LANES), 0) < cols_ref[c]
    for kt in range(K_TILES):
        piece = buf[slot, pl.ds(kt * DEPTH + layer, WCHUNK, stride=ROWS_PER_COL), :]
        o_ref[:, kt * LANES:(kt + 1) * LANES] = jnp.where(keep, piece, 0.0).astype(BF16)


def _prep_w_in(w_view, layer):
    nchunks = len(_CHUNK_COL0)
    grid_spec = pltpu.PrefetchScalarGridSpec(
        num_scalar_prefetch=2,
        grid=(nchunks,),
        in_specs=[pl.BlockSpec(memory_space=pl.ANY)],
        out_specs=pl.BlockSpec((WCHUNK, D_MODEL), lambda c, col0, cols: (c, 0)),
        scratch_shapes=[pltpu.VMEM((2, WCHUNK * ROWS_PER_COL, LANES), F32),
                        pltpu.SemaphoreType.DMA((2,))],
    )
    return pl.pallas_call(
        functools.partial(_wprep_kernel, layer=layer),
        grid_spec=grid_spec,
        out_shape=jax.ShapeDtypeStruct((nchunks * WCHUNK, D_MODEL), BF16),
        compiler_params=_cparams(("arbitrary",)),
        name="w_in_regroup",
    )(jnp.asarray(_CHUNK_COL0, jnp.int32), jnp.asarray(_CHUNK_COLS, jnp.int32), w_view)


def _split3(x):
    hi = x.astype(BF16)
    r1 = x - hi.astype(F32)
    mid = r1.astype(BF16)
    lo = (r1 - mid.astype(F32)).astype(BF16)
    return hi, mid, lo


def _fox_prep_kernel(fa_ref, bf_ref, q_ref, k_ref, qaux_ref, kaux_ref, stat_ref, carry_ref):
    @pl.when(pl.program_id(0) == 0)
    def _():
        carry_ref[...] = jnp.zeros_like(carry_ref)

    z = fa_ref[...] + bf_ref[...]
    lf = (jnp.minimum(z, 0.0) - jnp.log1p(jnp.exp(-jnp.abs(z)))) * LOG2E
    t = z.shape[0]
    row = lax.broadcasted_iota(jnp.int32, (t, t), 0)
    col = lax.broadcasted_iota(jnp.int32, (t, t), 1)
    tri = jnp.where(row >= col, 1.0, 0.0).astype(BF16)
    hi, mid, lo = _split3(lf)
    cs = (jnp.dot(tri, hi, preferred_element_type=F32)
          + jnp.dot(tri, mid, preferred_element_type=F32)
          + jnp.dot(tri, lo, preferred_element_type=F32)) + carry_ref[...]
    carry_ref[...] = cs[t - 1:t, :]

    lane = lax.broadcasted_iota(jnp.int32, (t, LANES), 1)
    lane1 = lax.broadcasted_iota(jnp.int32, (1, LANES), 1)
    qn = jnp.zeros((1, LANES), F32)
    kn = jnp.zeros((1, LANES), F32)
    for h in range(FOX_HEADS):
        c = jnp.broadcast_to(jnp.sum(jnp.where(lane == h, cs, 0.0), axis=1, keepdims=True),
                             (t, LANES))
        c_hi = c.astype(BF16).astype(F32)
        r1 = c - c_hi
        c_mid = r1.astype(BF16).astype(F32)
        c_lo = (r1 - c_mid).astype(BF16).astype(F32)
        kaux = jnp.where(lane == 0, c_hi, jnp.where(lane == 1, c_mid, jnp.where(
            lane == 2, c_lo, jnp.where(lane < 6, 1.0, 0.0))))
        qaux = jnp.where(lane < 3, -1.0, jnp.where(lane == 3, c_hi, jnp.where(
            lane == 4, c_mid, jnp.where(lane == 5, c_lo, 0.0))))
        kaux_ref[h] = kaux.astype(BF16)
        qaux_ref[h] = qaux.astype(BF16)
        qh = q_ref[:, h * FOX_HD:(h + 1) * FOX_HD].astype(F32)
        kh = k_ref[:, h * FOX_HD:(h + 1) * FOX_HD].astype(F32)
        q2 = jnp.max(jnp.sum(qh * qh, axis=1, keepdims=True), axis=0, keepdims=True)
        k2 = jnp.max(jnp.sum(kh * kh, axis=1, keepdims=True), axis=0, keepdims=True)
        qn = jnp.where(lane1 == h, q2, qn)
        kn = jnp.where(lane1 == h, k2, kn)
    stat_ref[0] = jnp.concatenate([qn, kn, cs[0:1, :], cs[t - 1:t, :],
                                   jnp.zeros((4, LANES), F32)], axis=0)


def _fox_prep(g32, g16, bf_pad, tb):
    s = g32.shape[0]
    qk_blocks = FOX_W // LANES
    return pl.pallas_call(
        _fox_prep_kernel,
        grid=(s // tb,),
        in_specs=[pl.BlockSpec((tb, LANES), lambda i: (i, G32_FA)),
                  pl.BlockSpec((1, LANES), lambda i: (0, 0)),
                  pl.BlockSpec((tb, FOX_W), lambda i: (i, G16_QA // qk_blocks)),
                  pl.BlockSpec((tb, FOX_W), lambda i: (i, G16_KA // qk_blocks))],
        out_specs=[pl.BlockSpec((FOX_HEADS, tb, LANES), lambda i: (0, i, 0)),
                   pl.BlockSpec((FOX_HEADS, tb, LANES), lambda i: (0, i, 0)),
                   pl.BlockSpec((1, 8, LANES), lambda i: (i, 0, 0))],
        out_shape=[jax.ShapeDtypeStruct((FOX_HEADS, s, LANES), BF16),
                   jax.ShapeDtypeStruct((FOX_HEADS, s, LANES), BF16),
                   jax.ShapeDtypeStruct((s // tb, 8, LANES), F32)],
        scratch_shapes=[pltpu.VMEM((1, LANES), F32)],
        compiler_params=_cparams(("arbitrary",)),
        name="fox_prep",
    )(g32, bf_pad, g16, g16)


def _fox_skip_table(stat, per_q, per_k):
    nq = stat.shape[0] // per_q
    nk = stat.shape[0] // per_k
    sq = stat.reshape(nq, per_q, 8, LANES)
    sk = stat.reshape(nk, per_k, 8, LANES)
    qn = jnp.sqrt(jnp.max(sq[:, :, 0, :FOX_HEADS], axis=1))
    kn_own = jnp.sqrt(jnp.max(sq[:, :, 1, :FOX_HEADS], axis=1))
    kn = jnp.sqrt(jnp.max(sk[:, :, 1, :FOX_HEADS], axis=1))
    c_first = sq[:, 0, 2, :FOX_HEADS]
    c_last = sk[:, per_k - 1, 3, :FOX_HEADS]
    kmax = jnp.max(kn, axis=0, keepdims=True)
    bound = 1.01 * qn * (kmax + kn_own) + c_first
    skip = (bound[:, None, :] - c_last[None, :, :]) < -SKIP_BITS
    before = jnp.arange(nk)[None, :] < (jnp.arange(nq) * (per_q // per_k))[:, None]
    skip = jnp.logical_and(skip, before[:, :, None])
    lead = jnp.cumsum(1 - skip.astype(jnp.int32), axis=1) == 0
    return jnp.sum(lead.astype(jnp.int32), axis=1).T.reshape(-1)


def _rope_kernel(x_ref, pos_ref, invf_ref, o_ref):
    ang = pos_ref[...] * invf_ref[...]
    c = jnp.cos(ang)
    s = jnp.sin(ang)
    half = ROT_DIM // 2
    d = lax.broadcasted_iota(jnp.int32, (1, LANES), 1) % DIFF_HD
    sa = jnp.where(d >= half, s, 0.0)
    sb = jnp.where(d < half, -s, 0.0)
    nblk = x_ref.shape[1] // LANES
    for j in range(nblk):
        t = x_ref[:, j * LANES:(j + 1) * LANES]
        r = t * c + pltpu.roll(t, half, 1) * sa + pltpu.roll(t, LANES - half, 1) * sb
        if j < nblk // 2:
            r = r * (DIFF_HD ** -0.5 * LOG2E)
        o_ref[:, j * LANES:(j + 1) * LANES] = r.astype(o_ref.dtype)


def _rope(g32, pos_f, invf, tb):
    s = g32.shape[0]
    w = 2 * DIFF_W
    return pl.pallas_call(
        _rope_kernel,
        grid=(s // tb,),
        in_specs=[pl.BlockSpec((tb, w), lambda i: (i, 0)),
                  pl.BlockSpec((tb, LANES), lambda i: (i, 0)),
                  pl.BlockSpec((1, LANES), lambda i: (0, 0))],
        out_specs=pl.BlockSpec((tb, w), lambda i: (i, 0)),
        out_shape=jax.ShapeDtypeStruct((s, w), BF16),
        compiler_params=_cparams(("parallel",)),
        name="diff_rope",
    )(g32, pos_f, invf)


def _flash_step(v, s, m_ref, l_ref, acc_ref):
    nq = s.shape[1]
    cw = min(nq, FLASH_COLS)
    for c0 in range(0, nq, cw):
        cols = slice(c0, c0 + cw)
        sc = s[:, cols]
        m_prev = m_ref[:, cols]
        m_new = jnp.maximum(m_prev, jnp.max(sc, axis=0, keepdims=True))
        a = jnp.exp2(m_prev - m_new)
        p = jnp.exp2(sc - m_new)
        l_ref[:, cols] = a * l_ref[:, cols] + jnp.sum(p, axis=0, keepdims=True)
        pv = lax.dot_general(v, p.astype(v.dtype), (((0,), (0,)), ((), ())),
                             preferred_element_type=F32)
        acc_ref[:, cols] = a * acc_ref[:, cols] + pv
        m_ref[:, cols] = m_new


def _flash_pipeline(j0, i, qk, sm):
    npairs = (i - j0) // 2
    qk(j0, 0)

    def body(p, carry):
        j = j0 + 2 * p
        qk(j + 1, 1)
        sm(j, 0, False)
        qk(j + 2, 0)
        sm(j + 1, 1, False)
        return carry

    lax.fori_loop(0, npairs, body, 0)
    jn = j0 + 2 * npairs

    @pl.when(jn == i)
    def _():
        sm(i, 0, True)

    @pl.when(jn != i)
    def _():
        qk(i, 1)
        sm(jn, 0, False)
        sm(i, 1, True)


def _init_state(m_ref, l_ref, acc_ref):
    m_ref[...] = jnp.full(m_ref.shape, NEG_BIG, F32)
    l_ref[...] = jnp.zeros(l_ref.shape, F32)
    acc_ref[...] = jnp.zeros(acc_ref.shape, F32)


HEADS_PER_STEP = 2
FLASH_COLS = 256


def _head_lanes(g):
    return slice(g * LANES, (g + 1) * LANES)


def _fox_kernel(jmin_ref, q_ref, qaux_ref, k_ref, kaux_ref, v_ref, o_ref,
                m_ref, l_ref, acc_ref, sa_ref, sb_ref, *, tq, tk):
    hp = pl.program_id(0)
    i = pl.program_id(1)
    heads = range(HEADS_PER_STEP)
    qf = [jnp.concatenate([q_ref[:, _head_lanes(g)], qaux_ref[g]], axis=1)
          for g in heads]
    _init_state(m_ref, l_ref, acc_ref)
    s_refs = (sa_ref, sb_ref)

    def qk(j, slot):
        k0 = pl.multiple_of(j * tk, tk)
        for g in heads:
            kf = jnp.concatenate([k_ref[pl.ds(k0, tk), _head_lanes(g)],
                                  kaux_ref[g, pl.ds(k0, tk), :]], axis=1)
            s_refs[slot][g] = lax.dot_general(kf, qf[g], (((1,), (1,)), ((), ())),
                                              preferred_element_type=F32)

    def sm(j, slot, diag):
        k0 = pl.multiple_of(j * tk, tk)
        for g in heads:
            s = s_refs[slot][g]
            if diag is not None:
                kk = lax.broadcasted_iota(jnp.int32, (tk, tq), 0) + diag * tk
                qq = lax.broadcasted_iota(jnp.int32, (tk, tq), 1)
                s = jnp.where(kk <= qq, s, NEG_BIG)
            v = v_ref[pl.ds(k0, tk), _head_lanes(g)]
            _flash_step(v, s, m_ref.at[g], l_ref.at[g], acc_ref.at[g])

    nq = pl.num_programs(1)
    j0 = jmin_ref[hp * HEADS_PER_STEP * nq + i]
    for g in range(1, HEADS_PER_STEP):
        j0 = jnp.minimum(j0, jmin_ref[(hp * HEADS_PER_STEP + g) * nq + i])

    if tq == tk:
        _flash_pipeline(j0, i, qk, lambda j, slot, masked: sm(j, slot, 0 if masked else None))
    else:
        d0 = 2 * i
        npairs = (d0 - j0) // 2
        qk(j0, 0)

        def body(p, carry):
            j = j0 + 2 * p
            qk(j + 1, 1)
            sm(j, 0, None)
            qk(j + 2, 0)
            sm(j + 1, 1, None)
            return carry

        lax.fori_loop(0, npairs, body, 0)
        jn = j0 + 2 * npairs

        @pl.when(jn == d0)
        def _():
            qk(d0 + 1, 1)
            sm(d0, 0, 0)
            sm(d0 + 1, 1, 1)

        @pl.when(jn != d0)
        def _():
            qk(d0, 1)
            sm(jn, 0, None)
            qk(d0 + 1, 0)
            sm(d0, 1, 0)
            sm(d0 + 1, 0, 1)

    for g in heads:
        o = acc_ref[g] / l_ref[g]
        o_ref[:, _head_lanes(g)] = o.T.astype(o_ref.dtype)


def _fox_attention(jmin, g16, qaux, kaux, tq, tk):
    s = g16.shape[0]
    assert tq in (tk, 2 * tk)
    hps = HEADS_PER_STEP
    wide = hps * LANES
    grid_spec = pltpu.PrefetchScalarGridSpec(
        num_scalar_prefetch=1,
        grid=(FOX_HEADS // hps, s // tq),
        in_specs=[pl.BlockSpec((tq, wide), lambda h, i, jm: (i, G16_QA // hps + h)),
                  pl.BlockSpec((hps, tq, LANES), lambda h, i, jm: (h, i, 0)),
                  pl.BlockSpec((s, wide), lambda h, i, jm: (0, G16_KA // hps + h)),
                  pl.BlockSpec((hps, s, LANES), lambda h, i, jm: (h, 0, 0)),
                  pl.BlockSpec((s, wide), lambda h, i, jm: (0, G16_VA // hps + h))],
        out_specs=pl.BlockSpec((tq, wide), lambda h, i, jm: (i, h)),
        scratch_shapes=[pltpu.VMEM((hps, 1, tq), F32), pltpu.VMEM((hps, 1, tq), F32),
                        pltpu.VMEM((hps, FOX_HD, tq), F32),
                        pltpu.VMEM((hps, tk, tq), F32), pltpu.VMEM((hps, tk, tq), F32)],
    )
    return pl.pallas_call(
        functools.partial(_fox_kernel, tq=tq, tk=tk),
        grid_spec=grid_spec,
        out_shape=jax.ShapeDtypeStruct((s, FOX_W), BF16),
        compiler_params=_cparams(("parallel", "parallel")),
        name="fox_attention",
    )(jmin, g16, qaux, g16, kaux, g16)


def _diff_kernel(q_ref, k_ref, v_ref, lamv_ref, g_ref, o_ref, m_ref, l_ref, acc_ref,
                 sa_ref, sb_ref, *, tq, tk, lam_init):
    i = pl.program_id(1)
    heads = range(HEADS_PER_STEP)
    lane = lax.broadcasted_iota(jnp.int32, (tq, LANES), 1)
    qq2 = []
    for g in heads:
        q = q_ref[:, _head_lanes(g)].astype(F32)
        qq2.append(jnp.concatenate([jnp.where(lane < DIFF_HD, q, 0.0),
                                    jnp.where(lane >= DIFF_HD, q, 0.0)], axis=0).astype(BF16))
    _init_state(m_ref, l_ref, acc_ref)
    s_refs = (sa_ref, sb_ref)

    def qk(j, slot):
        k0 = pl.multiple_of(j * tk, tk)
        for g in heads:
            k = k_ref[pl.ds(k0, tk), _head_lanes(g)]
            s_refs[slot][g] = lax.dot_general(k, qq2[g], (((1,), (1,)), ((), ())),
                                              preferred_element_type=F32)

    def sm(j, slot, diag):
        k0 = pl.multiple_of(j * tk, tk)
        for g in heads:
            s = s_refs[slot][g]
            if diag is not None:
                kc = (lax.broadcasted_iota(jnp.int32, (tk, 2 * tq), 0) + diag * tk) // CHUNK
                qc = (lax.broadcasted_iota(jnp.int32, (tk, 2 * tq), 1) % tq) // CHUNK
                s = jnp.where(kc <= qc, s, NEG_BIG)
            v = v_ref[pl.ds(k0, tk), _head_lanes(g)]
            _flash_step(v, s, m_ref.at[g], l_ref.at[g], acc_ref.at[g])

    if tq == tk:
        _flash_pipeline(0, i, qk, lambda j, slot, masked: sm(j, slot, 0 if masked else None))
    else:
        qk(0, 0)

        def body(p, carry):
            j = 2 * p
            qk(j + 1, 1)
            sm(j, 0, None)
            qk(j + 2, 0)
            sm(j + 1, 1, None)
            return carry

        lax.fori_loop(0, i, body, 0)
        qk(2 * i + 1, 1)
        sm(2 * i, 0, 0)
        sm(2 * i + 1, 1, 1)

    lv = lamv_ref[...]
    lam = (jnp.exp(jnp.sum(lv[0:1] * lv[1:2], axis=1, keepdims=True))
           - jnp.exp(jnp.sum(lv[2:3] * lv[3:4], axis=1, keepdims=True)) + lam_init)
    for g in heads:
        on = acc_ref[g] / l_ref[g]
        o = (on[:, :tq] - lam * on[:, tq:]).T
        ms = jnp.mean(o * o, axis=-1, keepdims=True)
        o = o * lax.rsqrt(ms + LN_EPS) * g_ref[...] * (1.0 - lam_init)
        o_ref[:, _head_lanes(g)] = o.astype(o_ref.dtype)


def _diff_attention(qk16, g16, lamv, subg, lam_init, tq, tk):
    s = g16.shape[0]
    assert tq in (tk, 2 * tk)
    hps = HEADS_PER_STEP
    wide = hps * LANES
    return pl.pallas_call(
        functools.partial(_diff_kernel, tq=tq, tk=tk, lam_init=lam_init),
        grid=(DIFF_HEADS // hps, s // tq),
        in_specs=[pl.BlockSpec((tq, wide), lambda h, i: (i, h)),
                  pl.BlockSpec((s, wide), lambda h, i: (0, DIFF_HEADS // hps + h)),
                  pl.BlockSpec((s, wide), lambda h, i: (0, G16_VC // hps + h)),
                  pl.BlockSpec((4, LANES), lambda h, i: (0, 0)),
                  pl.BlockSpec((1, LANES), lambda h, i: (0, 0))],
        out_specs=pl.BlockSpec((tq, wide), lambda h, i: (i, h)),
        out_shape=jax.ShapeDtypeStruct((s, DIFF_W), BF16),
        scratch_shapes=[pltpu.VMEM((hps, 1, 2 * tq), F32), pltpu.VMEM((hps, 1, 2 * tq), F32),
                        pltpu.VMEM((hps, DIFF_VD, 2 * tq), F32),
                        pltpu.VMEM((hps, tk, 2 * tq), F32), pltpu.VMEM((hps, tk, 2 * tq), F32)],
        compiler_params=_cparams(("parallel", "parallel")),
        name="diff_attention",
    )(qk16, qk16, g16, lamv, subg)


def _shift_rows(x, d, fill):
    if d % 8 == 0:
        return jnp.concatenate([jnp.full((d, x.shape[1]), fill, x.dtype), x[:x.shape[0] - d]],
                               axis=0)
    rows = lax.broadcasted_iota(jnp.int32, x.shape, 0)
    return jnp.where(rows >= d, pltpu.roll(x, d, 0), fill)


def _lru_kernel(xr_ref, gr_ref, cw_ref, cb_ref, wa_ref, ba_ref, wi_ref, bi_ref, lam_ref,
                o_ref, halo_ref, h_ref, *, tb, sub):
    @pl.when(pl.program_id(1) == 0)
    def _():
        halo_ref[...] = jnp.zeros_like(halo_ref)
        h_ref[...] = jnp.zeros_like(h_ref)

    x = xr_ref[...]
    xe = jnp.concatenate([halo_ref[...], x], axis=0)
    halo_ref[...] = x[tb - 8:tb, :]
    cw = cw_ref[...]
    xc = cb_ref[...] + cw[CONV_W - 1:CONV_W, :] * x
    for j in range(CONV_W - 1):
        sh = CONV_W - 1 - j
        xc = xc + cw[j:j + 1, :] * pltpu.roll(xe, sh, 0)[8:8 + tb, :]

    xcb = xc.astype(BF16)
    r = jax.nn.sigmoid(jnp.dot(xcb, wa_ref[0], preferred_element_type=F32) + ba_ref[...])
    ig = jax.nn.sigmoid(jnp.dot(xcb, wi_ref[0], preferred_element_type=F32) + bi_ref[...])
    lam = lam_ref[...]
    ls = jnp.minimum(lam, 0.0) - jnp.log1p(jnp.exp(-jnp.abs(lam)))
    log_a = LRU_C * r * ls
    a = jnp.exp(log_a)
    z2 = 2.0 * log_a
    e2 = jnp.exp(z2)
    small = jnp.where(e2 == 1.0, -z2, (1.0 - e2) * z2 / jnp.log(e2))
    neg_expm1 = jnp.where(z2 < -1.0, 1.0 - e2, small)
    u = jnp.sqrt(neg_expm1) * (ig * xc)

    h = h_ref[...]
    for c in range(tb // sub):
        ac = a[c * sub:(c + 1) * sub, :]
        uc = u[c * sub:(c + 1) * sub, :]
        d = 1
        while d < sub:
            uc = ac * _shift_rows(uc, d, 0.0) + uc
            ac = ac * _shift_rows(ac, d, 1.0)
            d *= 2
        hc = uc + ac * h
        h = hc[sub - 1:sub, :]
        g = gr_ref[c * sub:(c + 1) * sub, :]
        gelu = 0.5 * g * (1.0 + jnp.tanh(math.sqrt(2.0 / math.pi) * (g + 0.044715 * (g * g * g))))
        o_ref[c * sub:(c + 1) * sub, :] = (gelu * hc).astype(o_ref.dtype)
    h_ref[...] = h


def _lru_branch(g32, cw, cb, wa, ba, wi, bi, lam, tb, sub):
    s = g32.shape[0]
    vec = lambda c, i: (0, c)
    return pl.pallas_call(
        functools.partial(_lru_kernel, tb=tb, sub=sub),
        grid=(LRU_BLOCKS, s // tb),
        in_specs=[pl.BlockSpec((tb, LANES), lambda c, i: (i, G32_XR + c)),
                  pl.BlockSpec((tb, LANES), lambda c, i: (i, G32_GR + c)),
                  pl.BlockSpec((CONV_W, LANES), vec),
                  pl.BlockSpec((1, LANES), vec),
                  pl.BlockSpec((1, LRU_BW, LRU_BW), lambda c, i: (c, 0, 0)),
                  pl.BlockSpec((1, LANES), vec),
                  pl.BlockSpec((1, LRU_BW, LRU_BW), lambda c, i: (c, 0, 0)),
                  pl.BlockSpec((1, LANES), vec),
                  pl.BlockSpec((1, LANES), vec)],
        out_specs=pl.BlockSpec((tb, LANES), lambda c, i: (i, c)),
        out_shape=jax.ShapeDtypeStruct((s, LRU_W), BF16),
        scratch_shapes=[pltpu.VMEM((8, LANES), F32), pltpu.VMEM((1, LANES), F32)],
        compiler_params=_cparams(("parallel", "arbitrary")),
        name="rg_lru",
    )(g32, g32, cw, cb, wa, ba, wi, bi, lam)


def _layer_norm_rows(y, g, b):
    mu = jnp.mean(y, axis=-1, keepdims=True)
    yc = y - mu
    var = jnp.mean(yc * yc, axis=-1, keepdims=True)
    return yc * lax.rsqrt(var + LN_EPS) * g + b


def _oproj_ln_kernel(xa_ref, xb_ref, xc_ref, wa_ref, wb_ref, wc_ref, h_ref, g_ref, b_ref,
                     o32_ref, oslab_ref):
    half = h_ref.shape[0] // 2
    for part in range(2):
        rows = slice(part * half, (part + 1) * half)
        mix = (jnp.dot(xa_ref[rows, :], wa_ref[...], preferred_element_type=F32)
               + jnp.dot(xb_ref[rows, :], wb_ref[...], preferred_element_type=F32)
               + jnp.dot(xc_ref[rows, :], wc_ref[...], preferred_element_type=F32))
        o = _layer_norm_rows(ALPHA * h_ref[rows, :] + mix, g_ref[...], b_ref[...])
        o32_ref[rows, :] = o
        _store_slabs(oslab_ref.at[pl.ds(part * half * SLAB_ROWS, half * SLAB_ROWS), :], o)


def _oproj_ln(xa, xb, xc, wa, wb, wc, h, g, b, tm):
    s = h.shape[0]
    row = lambda i: (i, 0)
    fixed = lambda i: (0, 0)
    return pl.pallas_call(
        _oproj_ln_kernel,
        grid=(s // tm,),
        in_specs=[pl.BlockSpec((tm, FOX_W), row), pl.BlockSpec((tm, LRU_W), row),
                  pl.BlockSpec((tm, DIFF_W), row),
                  pl.BlockSpec((FOX_W, D_MODEL), fixed), pl.BlockSpec((LRU_W, D_MODEL), fixed),
                  pl.BlockSpec((DIFF_W, D_MODEL), fixed),
                  pl.BlockSpec((tm, D_MODEL), row),
                  pl.BlockSpec((1, D_MODEL), fixed), pl.BlockSpec((1, D_MODEL), fixed)],
        out_specs=[pl.BlockSpec((tm, D_MODEL), row), pl.BlockSpec((tm * SLAB_ROWS, LANES), row)],
        out_shape=[jax.ShapeDtypeStruct((s, D_MODEL), F32),
                   jax.ShapeDtypeStruct((s * SLAB_ROWS, LANES), F32)],
        compiler_params=_cparams(("parallel",)),
        name="oproj_ln1",
    )(xa, xb, xc, wa, wb, wc, h, g, b)


def _router_kernel(h_ref, whi_ref, wlo_ref, b_ref, o_ref):
    h = h_ref[...]
    hi = h.astype(BF16)
    lo = (h - hi.astype(F32)).astype(BF16)
    whi = whi_ref[...]
    logits = (jnp.dot(hi, whi, preferred_element_type=F32)
              + jnp.dot(hi, wlo_ref[...], preferred_element_type=F32)
              + jnp.dot(lo, whi, preferred_element_type=F32)) + b_ref[...]
    lane = lax.broadcasted_iota(jnp.int32, logits.shape, 1)
    big = jnp.int32(1 << 20)

    def first_lane(cond):
        return jnp.min(jnp.where(cond, lane, big), axis=1, keepdims=True)

    gm = lane < N_GROUPS
    gl = jnp.where(gm, logits, -jnp.inf)
    gmax = jnp.max(gl, axis=1, keepdims=True)
    gexp = jnp.where(gm, jnp.exp(logits - gmax), 0.0)
    gprob = gexp / jnp.sum(gexp, axis=1, keepdims=True)
    gidx = first_lane(gl == gmax)
    g_weight = jnp.sum(jnp.where(lane == gidx, gprob, 0.0), axis=1, keepdims=True)

    e0 = N_GROUPS + EXPERTS_PER_GROUP * gidx
    em = jnp.logical_and(lane >= e0, lane < e0 + EXPERTS_PER_GROUP)
    el = jnp.where(em, logits, -jnp.inf)
    emax = jnp.max(el, axis=1, keepdims=True)
    eexp = jnp.where(em, jnp.exp(logits - emax), 0.0)
    eprob = jnp.where(em, eexp / jnp.sum(eexp, axis=1, keepdims=True), -1.0)
    p1 = jnp.max(eprob, axis=1, keepdims=True)
    i1 = first_lane(eprob == p1)
    eprob2 = jnp.where(lane == i1, -1.0, eprob)
    p2 = jnp.max(eprob2, axis=1, keepdims=True)
    i2 = first_lane(eprob2 == p2)
    den = p1 + p2
    gate1 = g_weight * (p1 / den)
    gate2 = g_weight * (p2 / den)
    id1 = (i1 - N_GROUPS).astype(F32)
    id2 = (i2 - N_GROUPS).astype(F32)
    o_ref[...] = jnp.where(lane == 0, id1,
                           jnp.where(lane == 1, id2,
                                     jnp.where(lane == 2, gate1,
                                               jnp.where(lane == 3, gate2, 0.0))))


def _router(h, whi, wlo, bias, tm):
    s = h.shape[0]
    fixed = lambda i: (0, 0)
    return pl.pallas_call(
        _router_kernel,
        grid=(s // tm,),
        in_specs=[pl.BlockSpec((tm, D_MODEL), lambda i: (i, 0)),
                  pl.BlockSpec((D_MODEL, LANES), fixed), pl.BlockSpec((D_MODEL, LANES), fixed),
                  pl.BlockSpec((1, LANES), fixed)],
        out_specs=pl.BlockSpec((tm, LANES), lambda i: (i, 0)),
        out_shape=jax.ShapeDtypeStruct((s, LANES), F32),
        compiler_params=_cparams(("parallel",)),
        name="moe_router",
    )(h, whi, wlo, bias)


SLAB_ROWS = D_MODEL // LANES
SLAB_PITCH = SLAB_ROWS + 4


def _store_slabs(o_ref, x):
    rows = x.shape[0]
    for c in range(SLAB_ROWS):
        o_ref[pl.ds(c, rows, stride=SLAB_ROWS), :] = x[:, c * LANES:(c + 1) * LANES]


def _load_slabs(buf_ref, rows):
    return jnp.concatenate([buf_ref[pl.ds(c, rows, stride=SLAB_PITCH), :]
                            for c in range(SLAB_ROWS)], axis=1)


GATHER_GROUP = 8


def _issue_slab_gather(slab_row0_of, ngroups, src_hbm, dst_ref, sem, both_queues=False):
    def body(gi, carry):
        for u in range(GATHER_GROUP):
            r = gi * GATHER_GROUP + u
            src0 = pl.multiple_of(slab_row0_of(r), SLAB_ROWS)
            dst0 = pl.multiple_of(r * SLAB_PITCH, 4)
            pltpu.make_async_copy(src_hbm.at[pl.ds(src0, SLAB_ROWS), :],
                                  dst_ref.at[pl.ds(dst0, SLAB_ROWS), :],
                                  sem).start(priority=u % 2 if both_queues else 0)
        return carry
    lax.fori_loop(0, ngroups, body, 0)


def _wait_slab_gather(ngroups, buf_ref, sem):
    part = buf_ref.at[pl.ds(0, ngroups * (GATHER_GROUP * SLAB_ROWS)), :]
    pltpu.make_async_copy(part, part, sem).wait()


def _moe_kernel(te_ref, nt_ref, start_ref, cnt_ref, tok_ref, wfirst_ref, wslot_ref, wnext_ref,
                h_hbm, wg_hbm, wu_hbm, wd_hbm, y_ref, xbuf, wgb, wub, wdb, sem, wsem, *, tm):
    t = pl.program_id(0)
    nt = nt_ref[0]
    slot = t % 2
    last = tok_ref.shape[0] - 1

    def groups(tile):
        return (cnt_ref[tile] + (GATHER_GROUP - 1)) // GATHER_GROUP

    def issue(tile, sl):
        s0 = start_ref[tile]
        _issue_slab_gather(lambda r: tok_ref[jnp.minimum(s0 + r, last)], groups(tile), h_hbm,
                           xbuf.at[sl], sem.at[sl])

    def weight_copies(e, ws):
        return (pltpu.make_async_copy(wg_hbm.at[e], wgb.at[ws], wsem.at[ws]),
                pltpu.make_async_copy(wu_hbm.at[e], wub.at[ws], wsem.at[ws]),
                pltpu.make_async_copy(wd_hbm.at[e], wdb.at[ws], wsem.at[ws]))

    @pl.when(t == 0)
    def _():
        xbuf[...] = jnp.zeros_like(xbuf)
        issue(0, 0)
        for c in weight_copies(te_ref[0], wslot_ref[0]):
            c.start(priority=1)

    @pl.when(t + 1 < nt)
    def _():
        issue(t + 1, 1 - slot)

    @pl.when(t < nt)
    def _():
        ws = wslot_ref[t]

        @pl.when(wfirst_ref[t] == 1)
        def _():
            for c in weight_copies(te_ref[t], ws):
                c.wait()

            @pl.when(wnext_ref[t] >= 0)
            def _():
                for c in weight_copies(wnext_ref[t], 1 - ws):
                    c.start(priority=1)

        _wait_slab_gather(groups(t), xbuf.at[slot], sem.at[slot])
        x = _load_slabs(xbuf.at[slot], tm).astype(BF16)
        hg = jnp.dot(x, wgb[ws].astype(BF16), preferred_element_type=F32)
        hu = jnp.dot(x, wub[ws].astype(BF16), preferred_element_type=F32)
        act = (hg * jax.nn.sigmoid(hg)) * hu
        y = jnp.dot(act.astype(BF16), wdb[ws].astype(BF16), preferred_element_type=F32)
        _store_slabs(y_ref, y)

    @pl.when(t >= nt)
    def _():
        y_ref[...] = jnp.zeros_like(y_ref)


def _moe_grouped(plan, tok_sorted, h_slab, wg, wu, wd, tm):
    tile_expert, ntiles, tile_start, tile_count, wfirst, wslot, wnext = plan
    nt_max = tile_expert.shape[0]
    grid_spec = pltpu.PrefetchScalarGridSpec(
        num_scalar_prefetch=8,
        grid=(nt_max,),
        in_specs=[pl.BlockSpec(memory_space=pl.ANY)] * 4,
        out_specs=pl.BlockSpec((tm * SLAB_ROWS, LANES), lambda t, *_: (t, 0)),
        scratch_shapes=[pltpu.VMEM((2, tm * SLAB_PITCH, LANES), F32),
                        pltpu.VMEM((2, D_MODEL, D_FF_EXPERT), F32),
                        pltpu.VMEM((2, D_MODEL, D_FF_EXPERT), F32),
                        pltpu.VMEM((2, D_FF_EXPERT, D_MODEL), F32),
                        pltpu.SemaphoreType.DMA((2,)), pltpu.SemaphoreType.DMA((2,))],
    )
    return pl.pallas_call(
        functools.partial(_moe_kernel, tm=tm),
        grid_spec=grid_spec,
        out_shape=jax.ShapeDtypeStruct((nt_max * tm * SLAB_ROWS, LANES), F32),
        compiler_params=_cparams(("arbitrary",)),
        name="moe_experts",
    )(tile_expert, ntiles, tile_start, tile_count, tok_sorted, wfirst, wslot, wnext,
      h_slab, wg, wu, wd)


def _combine_ln_kernel(pos_ref, y_hbm, h_ref, meta_ref, g_ref, b_ref, o32_ref, o16_ref,
                       ybuf, sem, *, tm):
    t = pl.program_id(0)
    nt = pl.num_programs(0)
    slot = t % 2
    ngroups = tm // GATHER_GROUP

    def issue(tile, sl):
        for j in range(2):
            base = j * (nt * tm) + tile * tm
            _issue_slab_gather(lambda r, base=base: pos_ref[base + r], ngroups, y_hbm,
                               ybuf.at[sl, j], sem.at[sl], both_queues=True)

    @pl.when(t == 0)
    def _():
        issue(0, 0)

    @pl.when(t + 1 < nt)
    def _():
        issue(t + 1, 1 - slot)

    _wait_slab_gather(ngroups, ybuf.at[slot, 0], sem.at[slot])
    _wait_slab_gather(ngroups, ybuf.at[slot, 1], sem.at[slot])
    meta = meta_ref[...]
    ffn = (meta[:, 2:3] * _load_slabs(ybuf.at[slot, 0], tm)
           + meta[:, 3:4] * _load_slabs(ybuf.at[slot, 1], tm))
    o = _layer_norm_rows(ALPHA * h_ref[...] + ffn, g_ref[...], b_ref[...])
    o32_ref[...] = o
    o16_ref[...] = o.astype(BF16)


def _combine_ln(pos, y_slab, h, meta, g, b, tm):
    s = h.shape[0]
    row = lambda t, pos: (t, 0)
    fixed = lambda t, pos: (0, 0)
    grid_spec = pltpu.PrefetchScalarGridSpec(
        num_scalar_prefetch=1,
        grid=(s // tm,),
        in_specs=[pl.BlockSpec(memory_space=pl.ANY),
                  pl.BlockSpec((tm, D_MODEL), row),
                  pl.BlockSpec((tm, LANES), row),
                  pl.BlockSpec((1, D_MODEL), fixed), pl.BlockSpec((1, D_MODEL), fixed)],
        out_specs=[pl.BlockSpec((tm, D_MODEL), row), pl.BlockSpec((tm, D_MODEL), row)],
        scratch_shapes=[pltpu.VMEM((2, 2, tm * SLAB_PITCH, LANES), F32),
                        pltpu.SemaphoreType.DMA((2,))],
    )
    return pl.pallas_call(
        functools.partial(_combine_ln_kernel, tm=tm),
        grid_spec=grid_spec,
        out_shape=[jax.ShapeDtypeStruct((s, D_MODEL), F32),
                   jax.ShapeDtypeStruct((s, D_MODEL), BF16)],
        compiler_params=_cparams(("arbitrary",)),
        name="combine_ln2",
    )(pos, y_slab, h, meta, g, b)


def _dispatch_plan(meta, tm, layer):
    t = meta.shape[0]
    eid = meta[:, 0:2].astype(jnp.int32).reshape(-1)
    onehot = (eid[:, None] == jnp.arange(N_EXPERTS, dtype=jnp.int32)[None, :]).astype(jnp.int32)
    csum = jnp.cumsum(onehot, axis=0)
    counts = csum[-1]
    padded = ((counts + tm - 1) // tm) * tm
    pend = jnp.cumsum(padded)
    poff = pend - padded
    off = jnp.cumsum(counts) - counts
    pos = jnp.sum(onehot * (csum + poff[None, :]), axis=1) - 1
    p_max = 2 * t + N_EXPERTS * tm
    nt_max = p_max // tm
    ntiles = (pend[-1] // tm).astype(jnp.int32)
    tile_ids = jnp.minimum(jnp.arange(nt_max, dtype=jnp.int32), ntiles - 1)
    tile_expert = jnp.sum((pend[None, :] // tm <= tile_ids[:, None]).astype(jnp.int32), axis=1)
    tile_expert = jnp.minimum(tile_expert, N_EXPERTS - 1).astype(jnp.int32)
    tok_sorted = ((jnp.argsort(eid, stable=True) // 2) * SLAB_ROWS).astype(jnp.int32)
    k0 = jnp.arange(nt_max, dtype=jnp.int32) * tm - poff[tile_expert]
    tile_start = jnp.clip(off[tile_expert] + k0, 0, 2 * t - 1).astype(jnp.int32)
    tile_count = jnp.clip(counts[tile_expert] - k0, 0, tm).astype(jnp.int32)
    ids = jnp.arange(N_EXPERTS, dtype=jnp.int32)
    has = counts > 0
    rank = jnp.cumsum(has.astype(jnp.int32)) - 1
    later = jnp.logical_and(has[None, :], ids[None, :] > ids[:, None])
    next_e = jnp.min(jnp.where(later, ids[None, :], N_EXPERTS), axis=1)
    next_e = jnp.where(next_e < N_EXPERTS, next_e + layer * N_EXPERTS, -1)
    wfirst = (k0 == 0).astype(jnp.int32)
    wslot = (rank[tile_expert] % 2).astype(jnp.int32)
    wnext = next_e[tile_expert].astype(jnp.int32)
    plan = (tile_expert + layer * N_EXPERTS, ntiles.reshape(1), tile_start, tile_count,
            wfirst, wslot, wnext)
    pos_rows = (pos * SLAB_ROWS).astype(jnp.int32).reshape(t, 2).T.reshape(-1)
    return plan, tok_sorted, pos_rows


def _pad_lanes(v, width=LANES):
    v = v.reshape(1, -1).astype(F32)
    return jnp.pad(v, ((0, 0), (0, width - v.shape[1])))


TM_PROJ, TN_PROJ = 4096, 256
TB_PREP = 256
TQ_FOX, TK_FOX = 1024, 512
TQ_DIFF, TK_DIFF = 1024, 512
TB_LRU, SUB_LRU = 1024, 256
TB_ROPE = 512
TM_OPROJ = 512
TM_ROUTER = 512
TM_MOE = 256
TM_COMBINE = 256


def _layer(l, h32, h16, pos_f, invf, p, stacked):
    s = h32.shape[0]
    wt = _prep_w_in(stacked["w_in"], l)
    s16 = jnp.concatenate([jnp.full((1, FOX_W), FOX_HD ** -0.5 * LOG2E, F32),
                           jnp.ones((1, G16_W - FOX_W), F32)], axis=1)
    s32 = jnp.ones((1, G32_W), F32)
    tm_proj = min(s, TM_PROJ)
    g16 = _matmul_nt(h16, wt, 0, G16_W, s16, BF16, tm_proj, TN_PROJ, "in_proj_bf16")
    g32 = _matmul_nt(h16, wt, G16_W, G32_W, s32, F32, tm_proj, TN_PROJ, "in_proj_f32")

    tb_prep = min(s, TB_PREP)
    qaux, kaux, stat = _fox_prep(g32, g16, _pad_lanes(p["b_f"]), tb_prep)
    tk_fox = min(s, TK_FOX)
    tq_fox = TQ_FOX if s % TQ_FOX == 0 else tk_fox
    jmin = _fox_skip_table(stat, tq_fox // tb_prep, tk_fox // tb_prep)
    out_a = _fox_attention(jmin, g16, qaux, kaux, tq_fox, tk_fox)

    out_b = _lru_branch(g32, p["conv_w"], p["conv_b"].reshape(1, -1),
                        p["w_a"].astype(BF16), p["b_a"].reshape(1, -1),
                        p["w_i"].astype(BF16), p["b_i"].reshape(1, -1),
                        p["lru_lambda"].reshape(1, -1), min(s, TB_LRU), min(s, SUB_LRU))

    qk16 = _rope(g32, pos_f, invf, min(s, TB_ROPE))
    lamv = jnp.concatenate([_pad_lanes(p["lam_q1"]), _pad_lanes(p["lam_k1"]),
                            _pad_lanes(p["lam_q2"]), _pad_lanes(p["lam_k2"])], axis=0)
    lam_init = 0.8 - 0.6 * math.exp(-0.3 * l)
    tk_diff = min(s, TK_DIFF)
    tq_diff = TQ_DIFF if s % TQ_DIFF == 0 else tk_diff
    out_c = _diff_attention(qk16, g16, lamv, p["subln_g"].reshape(1, -1), lam_init,
                            tq_diff, tk_diff)

    w_o = p["w_o"].astype(BF16)
    h1_32, h1_slab = _oproj_ln(out_a, out_b, out_c, w_o[:FOX_W], w_o[FOX_W:FOX_W + LRU_W],
                               w_o[FOX_W + LRU_W:], h32, p["ln1_g"].reshape(1, -1),
                               p["ln1_b"].reshape(1, -1), min(s, TM_OPROJ))

    w_rt = jnp.concatenate([p["w_group"], p["w_router"],
                            jnp.zeros((D_MODEL, LANES - N_GROUPS - N_EXPERTS), F32)], axis=1)
    w_rt_hi = w_rt.astype(BF16)
    w_rt_lo = (w_rt - w_rt_hi.astype(F32)).astype(BF16)
    b_rt = _pad_lanes(jnp.concatenate([p["b_group"], p["b_router"]]))
    meta = _router(h1_32, w_rt_hi, w_rt_lo, b_rt, min(s, TM_ROUTER))

    tm_moe = min(s, TM_MOE)
    plan, tok_sorted, pos = _dispatch_plan(meta, tm_moe, l)
    y_slab = _moe_grouped(plan, tok_sorted, h1_slab, stacked["w_gate"], stacked["w_up"],
                          stacked["w_down"], tm_moe)
    return _combine_ln(pos, y_slab, h1_32, meta, p["ln2_g"].reshape(1, -1),
                       p["ln2_b"].reshape(1, -1), min(s, TM_COMBINE))


def _forward(x, positions, params):
    bsz, s, d = x.shape
    assert bsz == 1 and d == D_MODEL
    h32 = x.reshape(s, d)
    h16 = h32.astype(BF16)
    half = ROT_DIM // 2
    inv_freq = ROPE_THETA ** (-jnp.arange(half, dtype=F32) * 2.0 / ROT_DIM)
    dlane = jnp.arange(LANES) % DIFF_HD
    invf = jnp.where(dlane < ROT_DIM, inv_freq[dlane % half], 0.0).reshape(1, LANES).astype(F32)
    pos_f = jnp.broadcast_to(positions.astype(F32).reshape(s, 1), (s, LANES))
    big = ("w_in", "w_gate", "w_up", "w_down")
    w_view = jnp.transpose(params["w_in"].reshape(DEPTH, K_TILES, LANES, N_IN), (3, 1, 0, 2))
    stacked = {
        "w_in": w_view.reshape(N_IN * ROWS_PER_COL, LANES),
        "w_gate": params["w_gate"].reshape(DEPTH * N_EXPERTS, D_MODEL, D_FF_EXPERT),
        "w_up": params["w_up"].reshape(DEPTH * N_EXPERTS, D_MODEL, D_FF_EXPERT),
        "w_down": params["w_down"].reshape(DEPTH * N_EXPERTS, D_FF_EXPERT, D_MODEL),
    }
    for l in range(DEPTH):
        p = {k: v[l] for k, v in params.items() if k not in big}
        h32, h16 = _layer(l, h32, h16, pos_f, invf, p, stacked)
    return h32.reshape(bsz, s, d)


def kernel(x, positions, w_in, b_f, conv_w, conv_b, w_a, b_a, w_i, b_i, lru_lambda, lam_q1, lam_k1, lam_q2, lam_k2, subln_g, w_o, ln1_g, ln1_b, w_group, b_group, w_router, b_router, w_gate, w_up, w_down, ln2_g, ln2_b):
    params = dict(w_in=w_in, b_f=b_f, conv_w=conv_w, conv_b=conv_b, w_a=w_a, b_a=b_a, w_i=w_i,
                  b_i=b_i, lru_lambda=lru_lambda, lam_q1=lam_q1, lam_k1=lam_k1, lam_q2=lam_q2,
                  lam_k2=lam_k2, subln_g=subln_g, w_o=w_o, ln1_g=ln1_g, ln1_b=ln1_b,
                  w_group=w_group, b_group=b_group, w_router=w_router, b_router=b_router,
                  w_gate=w_gate, w_up=w_up, w_down=w_down, ln2_g=ln2_g, ln2_b=ln2_b)
    return _forward(x, positions, params)
```

```python
import functools
import math

import jax
import jax.numpy as jnp
from jax import lax
from jax.experimental import pallas as pl
from jax.experimental.pallas import tpu as pltpu

F32 = jnp.float32
BF16 = jnp.bfloat16

D_MODEL = 2048
DEPTH = 2
CHUNK = 64
FOX_HEADS = 6
FOX_HD = 128
FOX_W = FOX_HEADS * FOX_HD
LRU_W = 768
LRU_BLOCKS = 6
LRU_BW = LRU_W // LRU_BLOCKS
CONV_W = 4
LRU_C = 8.0
DIFF_HEADS = 4
DIFF_HD = 64
DIFF_VD = 2 * DIFF_HD
DIFF_W = DIFF_HEADS * DIFF_VD
ROT_DIM = DIFF_HD // 4
ROPE_THETA = 500000.0
N_GROUPS = 4
EXPERTS_PER_GROUP = 8
N_EXPERTS = N_GROUPS * EXPERTS_PER_GROUP
D_FF_EXPERT = 512
ALPHA = (2.0 * DEPTH) ** 0.25
LN_EPS = 1e-5

LANES = 128
NEG_BIG = -1e30
LOG2E = math.log2(math.e)
SKIP_BITS = 72.0
VMEM_LIMIT = 56 * 1024 * 1024

G16_W = 3 * FOX_W + DIFF_W
G32_W = 2 * DIFF_W + 2 * LRU_W + 2 * LANES
G16_QA, G16_KA, G16_VA, G16_VC = 0, 6, 12, 18
G32_XR, G32_GR, G32_FA = 8, 14, 20


def _cparams(sem, vmem=VMEM_LIMIT):
    return pltpu.CompilerParams(dimension_semantics=sem, vmem_limit_bytes=vmem)


def _mm_kernel(x_ref, wt_ref, s_ref, o_ref):
    acc = lax.dot_general(x_ref[...], wt_ref[...], (((1,), (1,)), ((), ())),
                          preferred_element_type=F32)
    o_ref[...] = (acc * s_ref[...]).astype(o_ref.dtype)


def _matmul_nt(x, wt, row0, n, s, out_dtype, tm, tn, name):
    m, k = x.shape
    off = row0 // tn
    return pl.pallas_call(
        _mm_kernel,
        grid=(m // tm, n // tn),
        in_specs=[pl.BlockSpec((tm, k), lambda i, j: (i, 0)),
                  pl.BlockSpec((tn, k), lambda i, j: (j + off, 0)),
                  pl.BlockSpec((1, tn), lambda i, j: (0, j))],
        out_specs=pl.BlockSpec((tm, tn), lambda i, j: (i, j)),
        out_shape=jax.ShapeDtypeStruct((m, n), out_dtype),
        compiler_params=_cparams(("parallel", "parallel")),
        name=name,
    )(x, wt, s)


_IN_QA = 0
_IN_KA = _IN_QA + FOX_W
_IN_VA = _IN_KA + FOX_W
_IN_FA = _IN_VA + FOX_W
_IN_XR = _IN_FA + FOX_HEADS
_IN_GR = _IN_XR + LRU_W
_IN_QC = _IN_GR + LRU_W
_IN_KC = _IN_QC + DIFF_W
_IN_VC = _IN_KC + DIFF_W
N_IN = _IN_VC + DIFF_W


K_TILES = D_MODEL // LANES
ROWS_PER_COL = K_TILES * DEPTH
WCHUNK = 256
_SEGMENTS = ((_IN_QA, FOX_W), (_IN_KA, FOX_W), (_IN_VA, FOX_W), (_IN_VC, DIFF_W),
             (_IN_QC, DIFF_W), (_IN_KC, DIFF_W), (_IN_XR, LRU_W), (_IN_GR, LRU_W),
             (_IN_FA, FOX_HEADS))
_CHUNK_COL0 = [c0 + d for c0, width in _SEGMENTS for d in range(0, width, WCHUNK)]
_CHUNK_COLS = [min(WCHUNK, width - d) for c0, width in _SEGMENTS for d in range(0, width, WCHUNK)]
assert len(_CHUNK_COL0) * WCHUNK == G16_W + G32_W and max(_CHUNK_COL0) + WCHUNK <= N_IN


def _wprep_kernel(col0_ref, cols_ref, w_hbm, o_ref, buf, sem, *, layer):
    c = pl.program_id(0)
    slot = c % 2

    def chunk_copy(ci, sl):
        r0 = pl.multiple_of(col0_ref[ci] * ROWS_PER_COL, ROWS_PER_COL)
        return pltpu.make_async_copy(w_hbm.at[pl.ds(r0, WCHUNK * ROWS_PER_COL), :], buf.at[sl],
                                     sem.at[sl])

    @pl.when(c == 0)
    def _():
        chunk_copy(0, 0).start()

    @pl.when(c + 1 < pl.num_programs(0))
    def _():
        chunk_copy(c + 1, 1 - slot).start()

    chunk_copy(c, slot).wait()
    keep = lax.broadcasted_iota(jnp.int32, (WCHUNK, LANES), 0) < cols_ref[c]
    for kt in range(K_TILES):
        piece = buf[slot, pl.ds(kt * DEPTH + layer, WCHUNK, stride=ROWS_PER_COL), :]
        o_ref[:, kt * LANES:(kt + 1) * LANES] = jnp.where(keep, piece, 0.0).astype(BF16)


def _prep_w_in(w_view, layer):
    nchunks = len(_CHUNK_COL0)
    grid_spec = pltpu.PrefetchScalarGridSpec(
        num_scalar_prefetch=2,
        grid=(nchunks,),
        in_specs=[pl.BlockSpec(memory_space=pl.ANY)],
        out_specs=pl.BlockSpec((WCHUNK, D_MODEL), lambda c, col0, cols: (c, 0)),
        scratch_shapes=[pltpu.VMEM((2, WCHUNK * ROWS_PER_COL, LANES), F32),
                        pltpu.SemaphoreType.DMA((2,))],
    )
    return pl.pallas_call(
        functools.partial(_wprep_kernel, layer=layer),
        grid_spec=grid_spec,
        out_shape=jax.ShapeDtypeStruct((nchunks * WCHUNK, D_MODEL), BF16),
        compiler_params=_cparams(("arbitrary",)),
        name="w_in_regroup",
    )(jnp.asarray(_CHUNK_COL0, jnp.int32), jnp.asarray(_CHUNK_COLS, jnp.int32), w_view)


def _split3(x):
    hi = x.astype(BF16)
    r1 = x - hi.astype(F32)
    mid = r1.astype(BF16)
    lo = (r1 - mid.astype(F32)).astype(BF16)
    return hi, mid, lo


def _fox_prep_kernel(fa_ref, bf_ref, q_ref, k_ref, qaux_ref, kaux_ref, stat_ref, carry_ref):
    @pl.when(pl.program_id(0) == 0)
    def _():
        carry_ref[...] = jnp.zeros_like(carry_ref)

    z = fa_ref[...] + bf_ref[...]
    lf = (jnp.minimum(z, 0.0) - jnp.log1p(jnp.exp(-jnp.abs(z)))) * LOG2E
    t = z.shape[0]
    row = lax.broadcasted_iota(jnp.int32, (t, t), 0)
    col = lax.broadcasted_iota(jnp.int32, (t, t), 1)
    tri = jnp.where(row >= col, 1.0, 0.0).astype(BF16)
    hi, mid, lo = _split3(lf)
    cs = (jnp.dot(tri, hi, preferred_element_type=F32)
          + jnp.dot(tri, mid, preferred_element_type=F32)
          + jnp.dot(tri, lo, preferred_element_type=F32)) + carry_ref[...]
    carry_ref[...] = cs[t - 1:t, :]

    lane = lax.broadcasted_iota(jnp.int32, (t, LANES), 1)
    lane1 = lax.broadcasted_iota(jnp.int32, (1, LANES), 1)
    qn = jnp.zeros((1, LANES), F32)
    kn = jnp.zeros((1, LANES), F32)
    for h in range(FOX_HEADS):
        c = jnp.broadcast_to(jnp.sum(jnp.where(lane == h, cs, 0.0), axis=1, keepdims=True),
                             (t, LANES))
        c_hi = c.astype(BF16).astype(F32)
        r1 = c - c_hi
        c_mid = r1.astype(BF16).astype(F32)
        c_lo = (r1 - c_mid).astype(BF16).astype(F32)
        kaux = jnp.where(lane == 0, c_hi, jnp.where(lane == 1, c_mid, jnp.where(
            lane == 2, c_lo, jnp.where(lane < 6, 1.0, 0.0))))
        qaux = jnp.where(lane < 3, -1.0, jnp.where(lane == 3, c_hi, jnp.where(
            lane == 4, c_mid, jnp.where(lane == 5, c_lo, 0.0))))
        kaux_ref[h] = kaux.astype(BF16)
        qaux_ref[h] = qaux.astype(BF16)
        qh = q_ref[:, h * FOX_HD:(h + 1) * FOX_HD].astype(F32)
        kh = k_ref[:, h * FOX_HD:(h + 1) * FOX_HD].astype(F32)
        q2 = jnp.max(jnp.sum(qh * qh, axis=1, keepdims=True), axis=0, keepdims=True)
        k2 = jnp.max(jnp.sum(kh * kh, axis=1, keepdims=True), axis=0, keepdims=True)
        qn = jnp.where(lane1 == h, q2, qn)
        kn = jnp.where(lane1 == h, k2, kn)
    stat_ref[0] = jnp.concatenate([qn, kn, cs[0:1, :], cs[t - 1:t, :],
                                   jnp.zeros((4, LANES), F32)], axis=0)


def _fox_prep(g32, g16, bf_pad, tb):
    s = g32.shape[0]
    qk_blocks = FOX_W // LANES
    return pl.pallas_call(
        _fox_prep_kernel,
        grid=(s // tb,),
        in_specs=[pl.BlockSpec((tb, LANES), lambda i: (i, G32_FA)),
                  pl.BlockSpec((1, LANES), lambda i: (0, 0)),
                  pl.BlockSpec((tb, FOX_W), lambda i: (i, G16_QA // qk_blocks)),
                  pl.BlockSpec((tb, FOX_W), lambda i: (i, G16_KA // qk_blocks))],
        out_specs=[pl.BlockSpec((FOX_HEADS, tb, LANES), lambda i: (0, i, 0)),
                   pl.BlockSpec((FOX_HEADS, tb, LANES), lambda i: (0, i, 0)),
                   pl.BlockSpec((1, 8, LANES), lambda i: (i, 0, 0))],
        out_shape=[jax.ShapeDtypeStruct((FOX_HEADS, s, LANES), BF16),
                   jax.ShapeDtypeStruct((FOX_HEADS, s, LANES), BF16),
                   jax.ShapeDtypeStruct((s // tb, 8, LANES), F32)],
        scratch_shapes=[pltpu.VMEM((1, LANES), F32)],
        compiler_params=_cparams(("arbitrary",)),
        name="fox_prep",
    )(g32, bf_pad, g16, g16)


def _fox_skip_table(stat, per_q, per_k):
    nq = stat.shape[0] // per_q
    nk = stat.shape[0] // per_k
    sq = stat.reshape(nq, per_q, 8, LANES)
    sk = stat.reshape(nk, per_k, 8, LANES)
    qn = jnp.sqrt(jnp.max(sq[:, :, 0, :FOX_HEADS], axis=1))
    kn_own = jnp.sqrt(jnp.max(sq[:, :, 1, :FOX_HEADS], axis=1))
    kn = jnp.sqrt(jnp.max(sk[:, :, 1, :FOX_HEADS], axis=1))
    c_first = sq[:, 0, 2, :FOX_HEADS]
    c_last = sk[:, per_k - 1, 3, :FOX_HEADS]
    kmax = jnp.max(kn, axis=0, keepdims=True)
    bound = 1.01 * qn * (kmax + kn_own) + c_first
    skip = (bound[:, None, :] - c_last[None, :, :]) < -SKIP_BITS
    before = jnp.arange(nk)[None, :] < (jnp.arange(nq) * (per_q // per_k))[:, None]
    skip = jnp.logical_and(skip, before[:, :, None])
    lead = jnp.cumsum(1 - skip.astype(jnp.int32), axis=1) == 0
    return jnp.sum(lead.astype(jnp.int32), axis=1).T.reshape(-1)


def _rope_kernel(x_ref, pos_ref, invf_ref, o_ref):
    ang = pos_ref[...] * invf_ref[...]
    c = jnp.cos(ang)
    s = jnp.sin(ang)
    half = ROT_DIM // 2
    d = lax.broadcasted_iota(jnp.int32, (1, LANES), 1) % DIFF_HD
    sa = jnp.where(d >= half, s, 0.0)
    sb = jnp.where(d < half, -s, 0.0)
    nblk = x_ref.shape[1] // LANES
    for j in range(nblk):
        t = x_ref[:, j * LANES:(j + 1) * LANES]
        r = t * c + pltpu.roll(t, half, 1) * sa + pltpu.roll(t, LANES - half, 1) * sb
        if j < nblk // 2:
            r = r * (DIFF_HD ** -0.5 * LOG2E)
        o_ref[:, j * LANES:(j + 1) * LANES] = r.astype(o_ref.dtype)


def _rope(g32, pos_f, invf, tb):
    s = g32.shape[0]
    w = 2 * DIFF_W
    return pl.pallas_call(
        _rope_kernel,
        grid=(s // tb,),
        in_specs=[pl.BlockSpec((tb, w), lambda i: (i, 0)),
                  pl.BlockSpec((tb, LANES), lambda i: (i, 0)),
                  pl.BlockSpec((1, LANES), lambda i: (0, 0))],
        out_specs=pl.BlockSpec((tb, w), lambda i: (i, 0)),
        out_shape=jax.ShapeDtypeStruct((s, w), BF16),
        compiler_params=_cparams(("parallel",)),
        name="diff_rope",
    )(g32, pos_f, invf)


def _flash_step(v, s, m_ref, l_ref, acc_ref):
    nq = s.shape[1]
    cw = min(nq, FLASH_COLS)
    for c0 in range(0, nq, cw):
        cols = slice(c0, c0 + cw)
        sc = s[:, cols]
        m_prev = m_ref[:, cols]
        m_new = jnp.maximum(m_prev, jnp.max(sc, axis=0, keepdims=True))
        a = jnp.exp2(m_prev - m_new)
        p = jnp.exp2(sc - m_new)
        l_ref[:, cols] = a * l_ref[:, cols] + jnp.sum(p, axis=0, keepdims=True)
        pv = lax.dot_general(v, p.astype(v.dtype), (((0,), (0,)), ((), ())),
                             preferred_element_type=F32)
        acc_ref[:, cols] = a * acc_ref[:, cols] + pv
        m_ref[:, cols] = m_new


def _flash_pipeline(j0, i, qk, sm):
    npairs = (i - j0) // 2
    qk(j0, 0)

    def body(p, carry):
        j = j0 + 2 * p
        qk(j + 1, 1)
        sm(j, 0, False)
        qk(j + 2, 0)
        sm(j + 1, 1, False)
        return carry

    lax.fori_loop(0, npairs, body, 0)
    jn = j0 + 2 * npairs

    @pl.when(jn == i)
    def _():
        sm(i, 0, True)

    @pl.when(jn != i)
    def _():
        qk(i, 1)
        sm(jn, 0, False)
        sm(i, 1, True)


def _init_state(m_ref, l_ref, acc_ref):
    m_ref[...] = jnp.full(m_ref.shape, NEG_BIG, F32)
    l_ref[...] = jnp.zeros(l_ref.shape, F32)
    acc_ref[...] = jnp.zeros(acc_ref.shape, F32)


HEADS_PER_STEP = 2
FLASH_COLS = 256


def _head_lanes(g):
    return slice(g * LANES, (g + 1) * LANES)


def _fox_kernel(jmin_ref, q_ref, qaux_ref, k_ref, kaux_ref, v_ref, o_ref,
                m_ref, l_ref, acc_ref, sa_ref, sb_ref, *, tq, tk):
    hp = pl.program_id(0)
    i = pl.program_id(1)
    heads = range(HEADS_PER_STEP)
    qf = [jnp.concatenate([q_ref[:, _head_lanes(g)], qaux_ref[g]], axis=1)
          for g in heads]
    _init_state(m_ref, l_ref, acc_ref)
    s_refs = (sa_ref, sb_ref)

    def qk(j, slot):
        k0 = pl.multiple_of(j * tk, tk)
        for g in heads:
            kf = jnp.concatenate([k_ref[pl.ds(k0, tk), _head_lanes(g)],
                                  kaux_ref[g, pl.ds(k0, tk), :]], axis=1)
            s_refs[slot][g] = lax.dot_general(kf, qf[g], (((1,), (1,)), ((), ())),
                                              preferred_element_type=F32)

    def sm(j, slot, diag):
        k0 = pl.multiple_of(j * tk, tk)
        for g in heads:
            s = s_refs[slot][g]
            if diag is not None:
                kk = lax.broadcasted_iota(jnp.int32, (tk, tq), 0) + diag * tk
                qq = lax.broadcasted_iota(jnp.int32, (tk, tq), 1)
                s = jnp.where(kk <= qq, s, NEG_BIG)
            v = v_ref[pl.ds(k0, tk), _head_lanes(g)]
            _flash_step(v, s, m_ref.at[g], l_ref.at[g], acc_ref.at[g])

    nq = pl.num_programs(1)
    j0 = jmin_ref[hp * HEADS_PER_STEP * nq + i]
    for g in range(1, HEADS_PER_STEP):
        j0 = jnp.minimum(j0, jmin_ref[(hp * HEADS_PER_STEP + g) * nq + i])

    if tq == tk:
        _flash_pipeline(j0, i, qk, lambda j, slot, masked: sm(j, slot, 0 if masked else None))
    else:
        d0 = 2 * i
        npairs = (d0 - j0) // 2
        qk(j0, 0)

        def body(p, carry):
            j = j0 + 2 * p
            qk(j + 1, 1)
            sm(j, 0, None)
            qk(j + 2, 0)
            sm(j + 1, 1, None)
            return carry

        lax.fori_loop(0, npairs, body, 0)
        jn = j0 + 2 * npairs

        @pl.when(jn == d0)
        def _():
            qk(d0 + 1, 1)
            sm(d0, 0, 0)
            sm(d0 + 1, 1, 1)

        @pl.when(jn != d0)
        def _():
            qk(d0, 1)
            sm(jn, 0, None)
            qk(d0 + 1, 0)
            sm(d0, 1, 0)
            sm(d0 + 1, 0, 1)

    for g in heads:
        o = acc_ref[g] / l_ref[g]
        o_ref[:, _head_lanes(g)] = o.T.astype(o_ref.dtype)


def _fox_attention(jmin, g16, qaux, kaux, tq, tk):
    s = g16.shape[0]
    assert tq in (tk, 2 * tk)
    hps = HEADS_PER_STEP
    wide = hps * LANES
    grid_spec = pltpu.PrefetchScalarGridSpec(
        num_scalar_prefetch=1,
        grid=(FOX_HEADS // hps, s // tq),
        in_specs=[pl.BlockSpec((tq, wide), lambda h, i, jm: (i, G16_QA // hps + h)),
                  pl.BlockSpec((hps, tq, LANES), lambda h, i, jm: (h, i, 0)),
                  pl.BlockSpec((s, wide), lambda h, i, jm: (0, G16_KA // hps + h)),
                  pl.BlockSpec((hps, s, LANES), lambda h, i, jm: (h, 0, 0)),
                  pl.BlockSpec((s, wide), lambda h, i, jm: (0, G16_VA // hps + h))],
        out_specs=pl.BlockSpec((tq, wide), lambda h, i, jm: (i, h)),
        scratch_shapes=[pltpu.VMEM((hps, 1, tq), F32), pltpu.VMEM((hps, 1, tq), F32),
                        pltpu.VMEM((hps, FOX_HD, tq), F32),
                        pltpu.VMEM((hps, tk, tq), F32), pltpu.VMEM((hps, tk, tq), F32)],
    )
    return pl.pallas_call(
        functools.partial(_fox_kernel, tq=tq, tk=tk),
        grid_spec=grid_spec,
        out_shape=jax.ShapeDtypeStruct((s, FOX_W), BF16),
        compiler_params=_cparams(("parallel", "parallel")),
        name="fox_attention",
    )(jmin, g16, qaux, g16, kaux, g16)


def _diff_kernel(q_ref, k_ref, v_ref, lamv_ref, g_ref, o_ref, m_ref, l_ref, acc_ref,
                 sa_ref, sb_ref, *, tq, tk, lam_init):
    i = pl.program_id(1)
    heads = range(HEADS_PER_STEP)
    lane = lax.broadcasted_iota(jnp.int32, (tq, LANES), 1)
    qq2 = []
    for g in heads:
        q = q_ref[:, _head_lanes(g)].astype(F32)
        qq2.append(jnp.concatenate([jnp.where(lane < DIFF_HD, q, 0.0),
                                    jnp.where(lane >= DIFF_HD, q, 0.0)], axis=0).astype(BF16))
    _init_state(m_ref, l_ref, acc_ref)
    s_refs = (sa_ref, sb_ref)

    def qk(j, slot):
        k0 = pl.multiple_of(j * tk, tk)
        for g in heads:
            k = k_ref[pl.ds(k0, tk), _head_lanes(g)]
            s_refs[slot][g] = lax.dot_general(k, qq2[g], (((1,), (1,)), ((), ())),
                                              preferred_element_type=F32)

    def sm(j, slot, diag):
        k0 = pl.multiple_of(j * tk, tk)
        for g in heads:
            s = s_refs[slot][g]
            if diag is not None:
                kc = (lax.broadcasted_iota(jnp.int32, (tk, 2 * tq), 0) + diag * tk) // CHUNK
                qc = (lax.broadcasted_iota(jnp.int32, (tk, 2 * tq), 1) % tq) // CHUNK
                s = jnp.where(kc <= qc, s, NEG_BIG)
            v = v_ref[pl.ds(k0, tk), _head_lanes(g)]
            _flash_step(v, s, m_ref.at[g], l_ref.at[g], acc_ref.at[g])

    if tq == tk:
        _flash_pipeline(0, i, qk, lambda j, slot, masked: sm(j, slot, 0 if masked else None))
    else:
        qk(0, 0)

        def body(p, carry):
            j = 2 * p
            qk(j + 1, 1)
            sm(j, 0, None)
            qk(j + 2, 0)
            sm(j + 1, 1, None)
            return carry

        lax.fori_loop(0, i, body, 0)
        qk(2 * i + 1, 1)
        sm(2 * i, 0, 0)
        sm(2 * i + 1, 1, 1)

    lv = lamv_ref[...]
    lam = (jnp.exp(jnp.sum(lv[0:1] * lv[1:2], axis=1, keepdims=True))
           - jnp.exp(jnp.sum(lv[2:3] * lv[3:4], axis=1, keepdims=True)) + lam_init)
    for g in heads:
        on = acc_ref[g] / l_ref[g]
        o = (on[:, :tq] - lam * on[:, tq:]).T
        ms = jnp.mean(o * o, axis=-1, keepdims=True)
        o = o * lax.rsqrt(ms + LN_EPS) * g_ref[...] * (1.0 - lam_init)
        o_ref[:, _head_lanes(g)] = o.astype(o_ref.dtype)


def _diff_attention(qk16, g16, lamv, subg, lam_init, tq, tk):
    s = g16.shape[0]
    assert tq in (tk, 2 * tk)
    hps = HEADS_PER_STEP
    wide = hps * LANES
    return pl.pallas_call(
        functools.partial(_diff_kernel, tq=tq, tk=tk, lam_init=lam_init),
        grid=(DIFF_HEADS // hps, s // tq),
        in_specs=[pl.BlockSpec((tq, wide), lambda h, i: (i, h)),
                  pl.BlockSpec((s, wide), lambda h, i: (0, DIFF_HEADS // hps + h)),
                  pl.BlockSpec((s, wide), lambda h, i: (0, G16_VC // hps + h)),
                  pl.BlockSpec((4, LANES), lambda h, i: (0, 0)),
                  pl.BlockSpec((1, LANES), lambda h, i: (0, 0))],
        out_specs=pl.BlockSpec((tq, wide), lambda h, i: (i, h)),
        out_shape=jax.ShapeDtypeStruct((s, DIFF_W), BF16),
        scratch_shapes=[pltpu.VMEM((hps, 1, 2 * tq), F32), pltpu.VMEM((hps, 1, 2 * tq), F32),
                        pltpu.VMEM((hps, DIFF_VD, 2 * tq), F32),
                        pltpu.VMEM((hps, tk, 2 * tq), F32), pltpu.VMEM((hps, tk, 2 * tq), F32)],
        compiler_params=_cparams(("parallel", "parallel")),
        name="diff_attention",
    )(qk16, qk16, g16, lamv, subg)


def _shift_rows(x, d, fill):
    if d % 8 == 0:
        return jnp.concatenate([jnp.full((d, x.shape[1]), fill, x.dtype), x[:x.shape[0] - d]],
                               axis=0)
    rows = lax.broadcasted_iota(jnp.int32, x.shape, 0)
    return jnp.where(rows >= d, pltpu.roll(x, d, 0), fill)


def _lru_kernel(xr_ref, gr_ref, cw_ref, cb_ref, wa_ref, ba_ref, wi_ref, bi_ref, lam_ref,
                o_ref, halo_ref, h_ref, *, tb, sub):
    @pl.when(pl.program_id(1) == 0)
    def _():
        halo_ref[...] = jnp.zeros_like(halo_ref)
        h_ref[...] = jnp.zeros_like(h_ref)

    x = xr_ref[...]
    xe = jnp.concatenate([halo_ref[...], x], axis=0)
    halo_ref[...] = x[tb - 8:tb, :]
    cw = cw_ref[...]
    xc = cb_ref[...] + cw[CONV_W - 1:CONV_W, :] * x
    for j in range(CONV_W - 1):
        sh = CONV_W - 1 - j
        xc = xc + cw[j:j + 1, :] * pltpu.roll(xe, sh, 0)[8:8 + tb, :]

    xcb = xc.astype(BF16)
    r = jax.nn.sigmoid(jnp.dot(xcb, wa_ref[0], preferred_element_type=F32) + ba_ref[...])
    ig = jax.nn.sigmoid(jnp.dot(xcb, wi_ref[0], preferred_element_type=F32) + bi_ref[...])
    lam = lam_ref[...]
    ls = jnp.minimum(lam, 0.0) - jnp.log1p(jnp.exp(-jnp.abs(lam)))
    log_a = LRU_C * r * ls
    a = jnp.exp(log_a)
    z2 = 2.0 * log_a
    e2 = jnp.exp(z2)
    small = jnp.where(e2 == 1.0, -z2, (1.0 - e2) * z2 / jnp.log(e2))
    neg_expm1 = jnp.where(z2 < -1.0, 1.0 - e2, small)
    u = jnp.sqrt(neg_expm1) * (ig * xc)

    h = h_ref[...]
    for c in range(tb // sub):
        ac = a[c * sub:(c + 1) * sub, :]
        uc = u[c * sub:(c + 1) * sub, :]
        d = 1
        while d < sub:
            uc = ac * _shift_rows(uc, d, 0.0) + uc
            ac = ac * _shift_rows(ac, d, 1.0)
            d *= 2
        hc = uc + ac * h
        h = hc[sub - 1:sub, :]
        g = gr_ref[c * sub:(c + 1) * sub, :]
        gelu = 0.5 * g * (1.0 + jnp.tanh(math.sqrt(2.0 / math.pi) * (g + 0.044715 * (g * g * g))))
        o_ref[c * sub:(c + 1) * sub, :] = (gelu * hc).astype(o_ref.dtype)
    h_ref[...] = h


def _lru_branch(g32, cw, cb, wa, ba, wi, bi, lam, tb, sub):
    s = g32.shape[0]
    vec = lambda c, i: (0, c)
    return pl.pallas_call(
        functools.partial(_lru_kernel, tb=tb, sub=sub),
        grid=(LRU_BLOCKS, s // tb),
        in_specs=[pl.BlockSpec((tb, LANES), lambda c, i: (i, G32_XR + c)),
                  pl.BlockSpec((tb, LANES), lambda c, i: (i, G32_GR + c)),
                  pl.BlockSpec((CONV_W, LANES), vec),
                  pl.BlockSpec((1, LANES), vec),
                  pl.BlockSpec((1, LRU_BW, LRU_BW), lambda c, i: (c, 0, 0)),
                  pl.BlockSpec((1, LANES), vec),
                  pl.BlockSpec((1, LRU_BW, LRU_BW), lambda c, i: (c, 0, 0)),
                  pl.BlockSpec((1, LANES), vec),
                  pl.BlockSpec((1, LANES), vec)],
        out_specs=pl.BlockSpec((tb, LANES), lambda c, i: (i, c)),
        out_shape=jax.ShapeDtypeStruct((s, LRU_W), BF16),
        scratch_shapes=[pltpu.VMEM((8, LANES), F32), pltpu.VMEM((1, LANES), F32)],
        compiler_params=_cparams(("parallel", "arbitrary")),
        name="rg_lru",
    )(g32, g32, cw, cb, wa, ba, wi, bi, lam)


def _layer_norm_rows(y, g, b):
    mu = jnp.mean(y, axis=-1, keepdims=True)
    yc = y - mu
    var = jnp.mean(yc * yc, axis=-1, keepdims=True)
    return yc * lax.rsqrt(var + LN_EPS) * g + b


def _oproj_ln_kernel(xa_ref, xb_ref, xc_ref, wa_ref, wb_ref, wc_ref, h_ref, g_ref, b_ref,
                     o32_ref, oslab_ref):
    half = h_ref.shape[0] // 2
    for part in range(2):
        rows = slice(part * half, (part + 1) * half)
        mix = (jnp.dot(xa_ref[rows, :], wa_ref[...], preferred_element_type=F32)
               + jnp.dot(xb_ref[rows, :], wb_ref[...], preferred_element_type=F32)
               + jnp.dot(xc_ref[rows, :], wc_ref[...], preferred_element_type=F32))
        o = _layer_norm_rows(ALPHA * h_ref[rows, :] + mix, g_ref[...], b_ref[...])
        o32_ref[rows, :] = o
        _store_slabs(oslab_ref.at[pl.ds(part * half * SLAB_ROWS, half * SLAB_ROWS), :], o)


def _oproj_ln(xa, xb, xc, wa, wb, wc, h, g, b, tm):
    s = h.shape[0]
    row = lambda i: (i, 0)
    fixed = lambda i: (0, 0)
    return pl.pallas_call(
        _oproj_ln_kernel,
        grid=(s // tm,),
        in_specs=[pl.BlockSpec((tm, FOX_W), row), pl.BlockSpec((tm, LRU_W), row),
                  pl.BlockSpec((tm, DIFF_W), row),
                  pl.BlockSpec((FOX_W, D_MODEL), fixed), pl.BlockSpec((LRU_W, D_MODEL), fixed),
                  pl.BlockSpec((DIFF_W, D_MODEL), fixed),
                  pl.BlockSpec((tm, D_MODEL), row),
                  pl.BlockSpec((1, D_MODEL), fixed), pl.BlockSpec((1, D_MODEL), fixed)],
        out_specs=[pl.BlockSpec((tm, D_MODEL), row), pl.BlockSpec((tm * SLAB_ROWS, LANES), row)],
        out_shape=[jax.ShapeDtypeStruct((s, D_MODEL), F32),
                   jax.ShapeDtypeStruct((s * SLAB_ROWS, LANES), F32)],
        compiler_params=_cparams(("parallel",)),
        name="oproj_ln1",
    )(xa, xb, xc, wa, wb, wc, h, g, b)


def _router_kernel(h_ref, whi_ref, wlo_ref, b_ref, o_ref):
    h = h_ref[...]
    hi = h.astype(BF16)
    lo = (h - hi.astype(F32)).astype(BF16)
    whi = whi_ref[...]
    both = jnp.dot(hi, jnp.concatenate([whi, wlo_ref[...]], axis=1), preferred_element_type=F32)
    logits = (both[:, :LANES] + both[:, LANES:]
              + jnp.dot(lo, whi, preferred_element_type=F32)) + b_ref[...]
    lane = lax.broadcasted_iota(jnp.int32, logits.shape, 1)
    big = jnp.int32(1 << 20)

    def first_lane(cond):
        return jnp.min(jnp.where(cond, lane, big), axis=1, keepdims=True)

    gm = lane < N_GROUPS
    gl = jnp.where(gm, logits, -jnp.inf)
    gmax = jnp.max(gl, axis=1, keepdims=True)
    gexp = jnp.where(gm, jnp.exp(logits - gmax), 0.0)
    gprob = gexp / jnp.sum(gexp, axis=1, keepdims=True)
    gidx = first_lane(gl == gmax)
    g_weight = jnp.sum(jnp.where(lane == gidx, gprob, 0.0), axis=1, keepdims=True)

    e0 = N_GROUPS + EXPERTS_PER_GROUP * gidx
    em = jnp.logical_and(lane >= e0, lane < e0 + EXPERTS_PER_GROUP)
    el = jnp.where(em, logits, -jnp.inf)
    emax = jnp.max(el, axis=1, keepdims=True)
    eexp = jnp.where(em, jnp.exp(logits - emax), 0.0)
    eprob = jnp.where(em, eexp / jnp.sum(eexp, axis=1, keepdims=True), -1.0)
    p1 = jnp.max(eprob, axis=1, keepdims=True)
    i1 = first_lane(eprob == p1)
    eprob2 = jnp.where(lane == i1, -1.0, eprob)
    p2 = jnp.max(eprob2, axis=1, keepdims=True)
    i2 = first_lane(eprob2 == p2)
    den = p1 + p2
    gate1 = g_weight * (p1 / den)
    gate2 = g_weight * (p2 / den)
    id1 = (i1 - N_GROUPS).astype(F32)
    id2 = (i2 - N_GROUPS).astype(F32)
    o_ref[...] = jnp.where(lane == 0, id1,
                           jnp.where(lane == 1, id2,
                                     jnp.where(lane == 2, gate1,
                                               jnp.where(lane == 3, gate2, 0.0))))


def _router(h, whi, wlo, bias, tm):
    s = h.shape[0]
    fixed = lambda i: (0, 0)
    return pl.pallas_call(
        _router_kernel,
        grid=(s // tm,),
        in_specs=[pl.BlockSpec((tm, D_MODEL), lambda i: (i, 0)),
                  pl.BlockSpec((D_MODEL, LANES), fixed), pl.BlockSpec((D_MODEL, LANES), fixed),
                  pl.BlockSpec((1, LANES), fixed)],
        out_specs=pl.BlockSpec((tm, LANES), lambda i: (i, 0)),
        out_shape=jax.ShapeDtypeStruct((s, LANES), F32),
        compiler_params=_cparams(("parallel",)),
        name="moe_router",
    )(h, whi, wlo, bias)


SLAB_ROWS = D_MODEL // LANES
SLAB_PITCH = SLAB_ROWS + 4


def _store_slabs(o_ref, x):
    rows = x.shape[0]
    for c in range(SLAB_ROWS):
        o_ref[pl.ds(c, rows, stride=SLAB_ROWS), :] = x[:, c * LANES:(c + 1) * LANES]


def _load_slabs(buf_ref, rows):
    return jnp.concatenate([buf_ref[pl.ds(c, rows, stride=SLAB_PITCH), :]
                            for c in range(SLAB_ROWS)], axis=1)


GATHER_GROUP = 8


def _issue_slab_gather(slab_row0_of, ngroups, src_hbm, dst_ref, sem, both_queues=False):
    def body(gi, carry):
        for u in range(GATHER_GROUP):
            r = gi * GATHER_GROUP + u
            src0 = pl.multiple_of(slab_row0_of(r), SLAB_ROWS)
            dst0 = pl.multiple_of(r * SLAB_PITCH, 4)
            pltpu.make_async_copy(src_hbm.at[pl.ds(src0, SLAB_ROWS), :],
                                  dst_ref.at[pl.ds(dst0, SLAB_ROWS), :],
                                  sem).start(priority=u % 2 if both_queues else 0)
        return carry
    lax.fori_loop(0, ngroups, body, 0)


def _wait_slab_gather(ngroups, buf_ref, sem):
    part = buf_ref.at[pl.ds(0, ngroups * (GATHER_GROUP * SLAB_ROWS)), :]
    pltpu.make_async_copy(part, part, sem).wait()


def _moe_kernel(te_ref, nt_ref, start_ref, cnt_ref, tok_ref, wfirst_ref, wslot_ref, wnext_ref,
                h_hbm, wg_hbm, wu_hbm, wd_hbm, y_ref, xbuf, wgb, wub, wdb, sem, wsem, *, tm):
    t = pl.program_id(0)
    nt = nt_ref[0]
    slot = t % 2
    last = tok_ref.shape[0] - 1

    def groups(tile):
        return (cnt_ref[tile] + (GATHER_GROUP - 1)) // GATHER_GROUP

    def issue(tile, sl):
        s0 = start_ref[tile]
        _issue_slab_gather(lambda r: tok_ref[jnp.minimum(s0 + r, last)], groups(tile), h_hbm,
                           xbuf.at[sl], sem.at[sl])

    def weight_copies(e, ws):
        return (pltpu.make_async_copy(wg_hbm.at[e], wgb.at[ws], wsem.at[ws]),
                pltpu.make_async_copy(wu_hbm.at[e], wub.at[ws], wsem.at[ws]),
                pltpu.make_async_copy(wd_hbm.at[e], wdb.at[ws], wsem.at[ws]))

    @pl.when(t == 0)
    def _():
        xbuf[...] = jnp.zeros_like(xbuf)
        issue(0, 0)
        for c in weight_copies(te_ref[0], wslot_ref[0]):
            c.start(priority=1)

    @pl.when(t + 1 < nt)
    def _():
        issue(t + 1, 1 - slot)

    @pl.when(t < nt)
    def _():
        ws = wslot_ref[t]

        @pl.when(wfirst_ref[t] == 1)
        def _():
            for c in weight_copies(te_ref[t], ws):
                c.wait()

            @pl.when(wnext_ref[t] >= 0)
            def _():
                for c in weight_copies(wnext_ref[t], 1 - ws):
                    c.start(priority=1)

        _wait_slab_gather(groups(t), xbuf.at[slot], sem.at[slot])
        x = _load_slabs(xbuf.at[slot], tm).astype(BF16)
        hg = jnp.dot(x, wgb[ws].astype(BF16), preferred_element_type=F32)
        hu = jnp.dot(x, wub[ws].astype(BF16), preferred_element_type=F32)
        act = (hg * jax.nn.sigmoid(hg)) * hu
        y = jnp.dot(act.astype(BF16), wdb[ws].astype(BF16), preferred_element_type=F32)
        _store_slabs(y_ref, y)

    @pl.when(t >= nt)
    def _():
        y_ref[...] = jnp.zeros_like(y_ref)


def _moe_grouped(plan, tok_sorted, h_slab, wg, wu, wd, tm):
    tile_expert, ntiles, tile_start, tile_count, wfirst, wslot, wnext = plan
    nt_max = tile_expert.shape[0]
    grid_spec = pltpu.PrefetchScalarGridSpec(
        num_scalar_prefetch=8,
        grid=(nt_max,),
        in_specs=[pl.BlockSpec(memory_space=pl.ANY)] * 4,
        out_specs=pl.BlockSpec((tm * SLAB_ROWS, LANES), lambda t, *_: (t, 0)),
        scratch_shapes=[pltpu.VMEM((2, tm * SLAB_PITCH, LANES), F32),
                        pltpu.VMEM((2, D_MODEL, D_FF_EXPERT), F32),
                        pltpu.VMEM((2, D_MODEL, D_FF_EXPERT), F32),
                        pltpu.VMEM((2, D_FF_EXPERT, D_MODEL), F32),
                        pltpu.SemaphoreType.DMA((2,)), pltpu.SemaphoreType.DMA((2,))],
    )
    return pl.pallas_call(
        functools.partial(_moe_kernel, tm=tm),
        grid_spec=grid_spec,
        out_shape=jax.ShapeDtypeStruct((nt_max * tm * SLAB_ROWS, LANES), F32),
        compiler_params=_cparams(("arbitrary",)),
        name="moe_experts",
    )(tile_expert, ntiles, tile_start, tile_count, tok_sorted, wfirst, wslot, wnext,
      h_slab, wg, wu, wd)


def _combine_ln_kernel(pos_ref, y_hbm, h_ref, meta_ref, g_ref, b_ref, o32_ref, o16_ref,
                       ybuf, sem, *, tm):
    t = pl.program_id(0)
    nt = pl.num_programs(0)
    slot = t % 2
    ngroups = tm // GATHER_GROUP

    def issue(tile, sl):
        for j in range(2):
            base = j * (nt * tm) + tile * tm
            _issue_slab_gather(lambda r, base=base: pos_ref[base + r], ngroups, y_hbm,
                               ybuf.at[sl, j], sem.at[sl], both_queues=True)

    @pl.when(t == 0)
    def _():
        issue(0, 0)

    @pl.when(t + 1 < nt)
    def _():
        issue(t + 1, 1 - slot)

    _wait_slab_gather(ngroups, ybuf.at[slot, 0], sem.at[slot])
    _wait_slab_gather(ngroups, ybuf.at[slot, 1], sem.at[slot])
    meta = meta_ref[...]
    ffn = (meta[:, 2:3] * _load_slabs(ybuf.at[slot, 0], tm)
           + meta[:, 3:4] * _load_slabs(ybuf.at[slot, 1], tm))
    o = _layer_norm_rows(ALPHA * h_ref[...] + ffn, g_ref[...], b_ref[...])
    o32_ref[...] = o
    o16_ref[...] = o.astype(BF16)


def _combine_ln(pos, y_slab, h, meta, g, b, tm):
    s = h.shape[0]
    row = lambda t, pos: (t, 0)
    fixed = lambda t, pos: (0, 0)
    grid_spec = pltpu.PrefetchScalarGridSpec(
        num_scalar_prefetch=1,
        grid=(s // tm,),
        in_specs=[pl.BlockSpec(memory_space=pl.ANY),
                  pl.BlockSpec((tm, D_MODEL), row),
                  pl.BlockSpec((tm, LANES), row),
                  pl.BlockSpec((1, D_MODEL), fixed), pl.BlockSpec((1, D_MODEL), fixed)],
        out_specs=[pl.BlockSpec((tm, D_MODEL), row), pl.BlockSpec((tm, D_MODEL), row)],
        scratch_shapes=[pltpu.VMEM((2, 2, tm * SLAB_PITCH, LANES), F32),
                        pltpu.SemaphoreType.DMA((2,))],
    )
    return pl.pallas_call(
        functools.partial(_combine_ln_kernel, tm=tm),
        grid_spec=grid_spec,
        out_shape=[jax.ShapeDtypeStruct((s, D_MODEL), F32),
                   jax.ShapeDtypeStruct((s, D_MODEL), BF16)],
        compiler_params=_cparams(("arbitrary",)),
        name="combine_ln2",
    )(pos, y_slab, h, meta, g, b)


def _dispatch_plan(meta, tm, layer):
    t = meta.shape[0]
    eid = meta[:, 0:2].astype(jnp.int32).reshape(-1)
    onehot = (eid[:, None] == jnp.arange(N_EXPERTS, dtype=jnp.int32)[None, :]).astype(jnp.int32)
    csum = jnp.cumsum(onehot, axis=0)
    counts = csum[-1]
    padded = ((counts + tm - 1) // tm) * tm
    pend = jnp.cumsum(padded)
    poff = pend - padded
    off = jnp.cumsum(counts) - counts
    pos = jnp.sum(onehot * (csum + poff[None, :]), axis=1) - 1
    p_max = 2 * t + N_EXPERTS * tm
    nt_max = p_max // tm
    ntiles = (pend[-1] // tm).astype(jnp.int32)
    tile_ids = jnp.minimum(jnp.arange(nt_max, dtype=jnp.int32), ntiles - 1)
    tile_expert = jnp.sum((pend[None, :] // tm <= tile_ids[:, None]).astype(jnp.int32), axis=1)
    tile_expert = jnp.minimum(tile_expert, N_EXPERTS - 1).astype(jnp.int32)
    tok_sorted = ((jnp.argsort(eid, stable=True) // 2) * SLAB_ROWS).astype(jnp.int32)
    k0 = jnp.arange(nt_max, dtype=jnp.int32) * tm - poff[tile_expert]
    tile_start = jnp.clip(off[tile_expert] + k0, 0, 2 * t - 1).astype(jnp.int32)
    tile_count = jnp.clip(counts[tile_expert] - k0, 0, tm).astype(jnp.int32)
    ids = jnp.arange(N_EXPERTS, dtype=jnp.int32)
    has = counts > 0
    rank = jnp.cumsum(has.astype(jnp.int32)) - 1
    later = jnp.logical_and(has[None, :], ids[None, :] > ids[:, None])
    next_e = jnp.min(jnp.where(later, ids[None, :], N_EXPERTS), axis=1)
    next_e = jnp.where(next_e < N_EXPERTS, next_e + layer * N_EXPERTS, -1)
    wfirst = (k0 == 0).astype(jnp.int32)
    wslot = (rank[tile_expert] % 2).astype(jnp.int32)
    wnext = next_e[tile_expert].astype(jnp.int32)
    plan = (tile_expert + layer * N_EXPERTS, ntiles.reshape(1), tile_start, tile_count,
            wfirst, wslot, wnext)
    pos_rows = (pos * SLAB_ROWS).astype(jnp.int32).reshape(t, 2).T.reshape(-1)
    return plan, tok_sorted, pos_rows


def _pad_lanes(v, width=LANES):
    v = v.reshape(1, -1).astype(F32)
    return jnp.pad(v, ((0, 0), (0, width - v.shape[1])))


TM_PROJ, TN_PROJ = 4096, 256
TB_PREP = 512
TQ_FOX, TK_FOX = 1024, 512
TQ_DIFF, TK_DIFF = 1024, 512
TB_LRU, SUB_LRU = 1024, 256
TB_ROPE = 512
TM_OPROJ = 512
TM_ROUTER = 512
TM_MOE = 256
TM_COMBINE = 256


def _layer(l, h32, h16, pos_f, invf, p, stacked):
    s = h32.shape[0]
    wt = _prep_w_in(stacked["w_in"], l)
    s16 = jnp.concatenate([jnp.full((1, FOX_W), FOX_HD ** -0.5 * LOG2E, F32),
                           jnp.ones((1, G16_W - FOX_W), F32)], axis=1)
    s32 = jnp.ones((1, G32_W), F32)
    tm_proj = min(s, TM_PROJ)
    g16 = _matmul_nt(h16, wt, 0, G16_W, s16, BF16, tm_proj, TN_PROJ, "in_proj_bf16")
    g32 = _matmul_nt(h16, wt, G16_W, G32_W, s32, F32, tm_proj, TN_PROJ, "in_proj_f32")

    tb_prep = min(s, TB_PREP)
    qaux, kaux, stat = _fox_prep(g32, g16, _pad_lanes(p["b_f"]), tb_prep)
    tk_fox = min(s, TK_FOX)
    tq_fox = TQ_FOX if s % TQ_FOX == 0 else tk_fox
    jmin = _fox_skip_table(stat, tq_fox // tb_prep, tk_fox // tb_prep)
    out_a = _fox_attention(jmin, g16, qaux, kaux, tq_fox, tk_fox)

    out_b = _lru_branch(g32, p["conv_w"], p["conv_b"].reshape(1, -1),
                        p["w_a"].astype(BF16), p["b_a"].reshape(1, -1),
                        p["w_i"].astype(BF16), p["b_i"].reshape(1, -1),
                        p["lru_lambda"].reshape(1, -1), min(s, TB_LRU), min(s, SUB_LRU))

    qk16 = _rope(g32, pos_f, invf, min(s, TB_ROPE))
    lamv = jnp.concatenate([_pad_lanes(p["lam_q1"]), _pad_lanes(p["lam_k1"]),
                            _pad_lanes(p["lam_q2"]), _pad_lanes(p["lam_k2"])], axis=0)
    lam_init = 0.8 - 0.6 * math.exp(-0.3 * l)
    tk_diff = min(s, TK_DIFF)
    tq_diff = TQ_DIFF if s % TQ_DIFF == 0 else tk_diff
    out_c = _diff_attention(qk16, g16, lamv, p["subln_g"].reshape(1, -1), lam_init,
                            tq_diff, tk_diff)

    w_o = p["w_o"].astype(BF16)
    h1_32, h1_slab = _oproj_ln(out_a, out_b, out_c, w_o[:FOX_W], w_o[FOX_W:FOX_W + LRU_W],
                               w_o[FOX_W + LRU_W:], h32, p["ln1_g"].reshape(1, -1),
                               p["ln1_b"].reshape(1, -1), min(s, TM_OPROJ))

    w_rt = jnp.concatenate([p["w_group"], p["w_router"],
                            jnp.zeros((D_MODEL, LANES - N_GROUPS - N_EXPERTS), F32)], axis=1)
    w_rt_hi = w_rt.astype(BF16)
    w_rt_lo = (w_rt - w_rt_hi.astype(F32)).astype(BF16)
    b_rt = _pad_lanes(jnp.concatenate([p["b_group"], p["b_router"]]))
    meta = _router(h1_32, w_rt_hi, w_rt_lo, b_rt, min(s, TM_ROUTER))

    tm_moe = min(s, TM_MOE)
    plan, tok_sorted, pos = _dispatch_plan(meta, tm_moe, l)
    y_slab = _moe_grouped(plan, tok_sorted, h1_slab, stacked["w_gate"], stacked["w_up"],
                          stacked["w_down"], tm_moe)
    return _combine_ln(pos, y_slab, h1_32, meta, p["ln2_g"].reshape(1, -1),
                       p["ln2_b"].reshape(1, -1), min(s, TM_COMBINE))


def _forward(x, positions, params):
    bsz, s, d = x.shape
    assert bsz == 1 and d == D_MODEL
    h32 = x.reshape(s, d)
    h16 = h32.astype(BF16)
    half = ROT_DIM // 2
    inv_freq = ROPE_THETA ** (-jnp.arange(half, dtype=F32) * 2.0 / ROT_DIM)
    dlane = jnp.arange(LANES) % DIFF_HD
    invf = jnp.where(dlane < ROT_DIM, inv_freq[dlane % half], 0.0).reshape(1, LANES).astype(F32)
    pos_f = jnp.broadcast_to(positions.astype(F32).reshape(s, 1), (s, LANES))
    big = ("w_in", "w_gate", "w_up", "w_down")
    w_view = jnp.transpose(params["w_in"].reshape(DEPTH, K_TILES, LANES, N_IN), (3, 1, 0, 2))
    stacked = {
        "w_in": w_view.reshape(N_IN * ROWS_PER_COL, LANES),
        "w_gate": params["w_gate"].reshape(DEPTH * N_EXPERTS, D_MODEL, D_FF_EXPERT),
        "w_up": params["w_up"].reshape(DEPTH * N_EXPERTS, D_MODEL, D_FF_EXPERT),
        "w_down": params["w_down"].reshape(DEPTH * N_EXPERTS, D_FF_EXPERT, D_MODEL),
    }
    for l in range(DEPTH):
        p = {k: v[l] for k, v in params.items() if k not in big}
        h32, h16 = _layer(l, h32, h16, pos_f, invf, p, stacked)
    return h32.reshape(bsz, s, d)


def kernel(x, positions, w_in, b_f, conv_w, conv_b, w_a, b_a, w_i, b_i, lru_lambda, lam_q1, lam_k1, lam_q2, lam_k2, subln_g, w_o, ln1_g, ln1_b, w_group, b_group, w_router, b_router, w_gate, w_up, w_down, ln2_g, ln2_b):
    params = dict(w_in=w_in, b_f=b_f, conv_w=conv_w, conv_b=conv_b, w_a=w_a, b_a=b_a, w_i=w_i,
                  b_i=b_i, lru_lambda=lru_lambda, lam_q1=lam_q1, lam_k1=lam_k1, lam_q2=lam_q2,
                  lam_k2=lam_k2, subln_g=subln_g, w_o=w_o, ln1_g=ln1_g, ln1_b=ln1_b,
                  w_group=w_group, b_group=b_group, w_router=w_router, b_router=b_router,
                  w_gate=w_gate, w_up=w_up, w_down=w_down, ln2_g=ln2_g, ln2_b=ln2_b)
    return _forward(x, positions, params)
```

```python
import functools
import math

import jax
import jax.numpy as jnp
from jax import lax
from jax.experimental import pallas as pl
from jax.experimental.pallas import tpu as pltpu

F32 = jnp.float32
BF16 = jnp.bfloat16

D_MODEL = 2048
DEPTH = 2
CHUNK = 64
FOX_HEADS = 6
FOX_HD = 128
FOX_W = FOX_HEADS * FOX_HD
LRU_W = 768
LRU_BLOCKS = 6
LRU_BW = LRU_W // LRU_BLOCKS
CONV_W = 4
LRU_C = 8.0
DIFF_HEADS = 4
DIFF_HD = 64
DIFF_VD = 2 * DIFF_HD
DIFF_W = DIFF_HEADS * DIFF_VD
ROT_DIM = DIFF_HD // 4
ROPE_THETA = 500000.0
N_GROUPS = 4
EXPERTS_PER_GROUP = 8
N_EXPERTS = N_GROUPS * EXPERTS_PER_GROUP
D_FF_EXPERT = 512
ALPHA = (2.0 * DEPTH) ** 0.25
LN_EPS = 1e-5

LANES = 128
NEG_BIG = -1e30
LOG2E = math.log2(math.e)
SKIP_BITS = 72.0
VMEM_LIMIT = 56 * 1024 * 1024

G16_W = 3 * FOX_W + DIFF_W
G32_W = 2 * DIFF_W + 2 * LRU_W + 2 * LANES
G16_QA, G16_KA, G16_VA, G16_VC = 0, 6, 12, 18
G32_XR, G32_GR, G32_FA = 8, 14, 20


def _cparams(sem, vmem=VMEM_LIMIT):
    return pltpu.CompilerParams(dimension_semantics=sem, vmem_limit_bytes=vmem)


def _mm_kernel(x_ref, wt_ref, s_ref, o_ref):
    acc = lax.dot_general(x_ref[...], wt_ref[...], (((1,), (1,)), ((), ())),
                          preferred_element_type=F32)
    o_ref[...] = (acc * s_ref[...]).astype(o_ref.dtype)


def _matmul_nt(x, wt, row0, n, s, out_dtype, tm, tn, name):
    m, k = x.shape
    off = row0 // tn
    return pl.pallas_call(
        _mm_kernel,
        grid=(m // tm, n // tn),
        in_specs=[pl.BlockSpec((tm, k), lambda i, j: (i, 0)),
                  pl.BlockSpec((tn, k), lambda i, j: (j + off, 0)),
                  pl.BlockSpec((1, tn), lambda i, j: (0, j))],
        out_specs=pl.BlockSpec((tm, tn), lambda i, j: (i, j)),
        out_shape=jax.ShapeDtypeStruct((m, n), out_dtype),
        compiler_params=_cparams(("parallel", "parallel")),
        name=name,
    )(x, wt, s)


_IN_QA = 0
_IN_KA = _IN_QA + FOX_W
_IN_VA = _IN_KA + FOX_W
_IN_FA = _IN_VA + FOX_W
_IN_XR = _IN_FA + FOX_HEADS
_IN_GR = _IN_XR + LRU_W
_IN_QC = _IN_GR + LRU_W
_IN_KC = _IN_QC + DIFF_W
_IN_VC = _IN_KC + DIFF_W
N_IN = _IN_VC + DIFF_W


K_TILES = D_MODEL // LANES
ROWS_PER_COL = K_TILES * DEPTH
WCHUNK = 256
_SEGMENTS = ((_IN_QA, FOX_W), (_IN_KA, FOX_W), (_IN_VA, FOX_W), (_IN_VC, DIFF_W),
             (_IN_QC, DIFF_W), (_IN_KC, DIFF_W), (_IN_XR, LRU_W), (_IN_GR, LRU_W),
             (_IN_FA, FOX_HEADS))
_CHUNK_COL0 = [c0 + d for c0, width in _SEGMENTS for d in range(0, width, WCHUNK)]
_CHUNK_COLS = [min(WCHUNK, width - d) for c0, width in _SEGMENTS for d in range(0, width, WCHUNK)]
assert len(_CHUNK_COL0) * WCHUNK == G16_W + G32_W and max(_CHUNK_COL0) + WCHUNK <= N_IN


def _wprep_kernel(col0_ref, cols_ref, w_hbm, o_ref, buf, sem, *, layer):
    c = pl.program_id(0)
    slot = c % 2

    def chunk_copy(ci, sl):
        r0 = pl.multiple_of(col0_ref[ci] * ROWS_PER_COL, ROWS_PER_COL)
        return pltpu.make_async_copy(w_hbm.at[pl.ds(r0, WCHUNK * ROWS_PER_COL), :], buf.at[sl],
                                     sem.at[sl])

    @pl.when(c == 0)
    def _():
        chunk_copy(0, 0).start()

    @pl.when(c + 1 < pl.num_programs(0))
    def _():
        chunk_copy(c + 1, 1 - slot).start()

    chunk_copy(c, slot).wait()
    keep = lax.broadcasted_iota(jnp.int32, (WCHUNK, LANES), 0) < cols_ref[c]
    for kt in range(K_TILES):
        piece = buf[slot, pl.ds(kt * DEPTH + layer, WCHUNK, stride=ROWS_PER_COL), :]
        o_ref[:, kt * LANES:(kt + 1) * LANES] = jnp.where(keep, piece, 0.0).astype(BF16)


def _prep_w_in(w_view, layer):
    nchunks = len(_CHUNK_COL0)
    grid_spec = pltpu.PrefetchScalarGridSpec(
        num_scalar_prefetch=2,
        grid=(nchunks,),
        in_specs=[pl.BlockSpec(memory_space=pl.ANY)],
        out_specs=pl.BlockSpec((WCHUNK, D_MODEL), lambda c, col0, cols: (c, 0)),
        scratch_shapes=[pltpu.VMEM((2, WCHUNK * ROWS_PER_COL, LANES), F32),
                        pltpu.SemaphoreType.DMA((2,))],
    )
    return pl.pallas_call(
        functools.partial(_wprep_kernel, layer=layer),
        grid_spec=grid_spec,
        out_shape=jax.ShapeDtypeStruct((nchunks * WCHUNK, D_MODEL), BF16),
        compiler_params=_cparams(("arbitrary",)),
        name="w_in_regroup",
    )(jnp.asarray(_CHUNK_COL0, jnp.int32), jnp.asarray(_CHUNK_COLS, jnp.int32), w_view)


def _split3(x):
    hi = x.astype(BF16)
    r1 = x - hi.astype(F32)
    mid = r1.astype(BF16)
    lo = (r1 - mid.astype(F32)).astype(BF16)
    return hi, mid, lo


def _fox_prep_kernel(fa_ref, bf_ref, q_ref, k_ref, qaux_ref, kaux_ref, stat_ref, carry_ref):
    @pl.when(pl.program_id(0) == 0)
    def _():
        carry_ref[...] = jnp.zeros_like(carry_ref)

    z = fa_ref[...] + bf_ref[...]
    lf = (jnp.minimum(z, 0.0) - jnp.log1p(jnp.exp(-jnp.abs(z)))) * LOG2E
    t = z.shape[0]
    row = lax.broadcasted_iota(jnp.int32, (t, t), 0)
    col = lax.broadcasted_iota(jnp.int32, (t, t), 1)
    tri = jnp.where(row >= col, 1.0, 0.0).astype(BF16)
    hi, mid, lo = _split3(lf)
    cs = (jnp.dot(tri, hi, preferred_element_type=F32)
          + jnp.dot(tri, mid, preferred_element_type=F32)
          + jnp.dot(tri, lo, preferred_element_type=F32)) + carry_ref[...]
    carry_ref[...] = cs[t - 1:t, :]

    lane = lax.broadcasted_iota(jnp.int32, (t, LANES), 1)
    lane1 = lax.broadcasted_iota(jnp.int32, (1, LANES), 1)
    qn = jnp.zeros((1, LANES), F32)
    kn = jnp.zeros((1, LANES), F32)
    for h in range(FOX_HEADS):
        c = jnp.broadcast_to(jnp.sum(jnp.where(lane == h, cs, 0.0), axis=1, keepdims=True),
                             (t, LANES))
        c_hi = c.astype(BF16).astype(F32)
        r1 = c - c_hi
        c_mid = r1.astype(BF16).astype(F32)
        c_lo = (r1 - c_mid).astype(BF16).astype(F32)
        kaux = jnp.where(lane == 0, c_hi, jnp.where(lane == 1, c_mid, jnp.where(
            lane == 2, c_lo, jnp.where(lane < 6, 1.0, 0.0))))
        qaux = jnp.where(lane < 3, -1.0, jnp.where(lane == 3, c_hi, jnp.where(
            lane == 4, c_mid, jnp.where(lane == 5, c_lo, 0.0))))
        kaux_ref[h] = kaux.astype(BF16)
        qaux_ref[h] = qaux.astype(BF16)
        qh = q_ref[:, h * FOX_HD:(h + 1) * FOX_HD].astype(F32)
        kh = k_ref[:, h * FOX_HD:(h + 1) * FOX_HD].astype(F32)
        q2 = jnp.max(jnp.sum(qh * qh, axis=1, keepdims=True), axis=0, keepdims=True)
        k2 = jnp.max(jnp.sum(kh * kh, axis=1, keepdims=True), axis=0, keepdims=True)
        qn = jnp.where(lane1 == h, q2, qn)
        kn = jnp.where(lane1 == h, k2, kn)
    stat_ref[0] = jnp.concatenate([qn, kn, cs[0:1, :], cs[t - 1:t, :],
                                   jnp.zeros((4, LANES), F32)], axis=0)


def _fox_prep(g32, g16, bf_pad, tb):
    s = g32.shape[0]
    qk_blocks = FOX_W // LANES
    return pl.pallas_call(
        _fox_prep_kernel,
        grid=(s // tb,),
        in_specs=[pl.BlockSpec((tb, LANES), lambda i: (i, G32_FA)),
                  pl.BlockSpec((1, LANES), lambda i: (0, 0)),
                  pl.BlockSpec((tb, FOX_W), lambda i: (i, G16_QA // qk_blocks)),
                  pl.BlockSpec((tb, FOX_W), lambda i: (i, G16_KA // qk_blocks))],
        out_specs=[pl.BlockSpec((FOX_HEADS, tb, LANES), lambda i: (0, i, 0)),
                   pl.BlockSpec((FOX_HEADS, tb, LANES), lambda i: (0, i, 0)),
                   pl.BlockSpec((1, 8, LANES), lambda i: (i, 0, 0))],
        out_shape=[jax.ShapeDtypeStruct((FOX_HEADS, s, LANES), BF16),
                   jax.ShapeDtypeStruct((FOX_HEADS, s, LANES), BF16),
                   jax.ShapeDtypeStruct((s // tb, 8, LANES), F32)],
        scratch_shapes=[pltpu.VMEM((1, LANES), F32)],
        compiler_params=_cparams(("arbitrary",)),
        name="fox_prep",
    )(g32, bf_pad, g16, g16)


def _fox_skip_table(stat, per_q, per_k):
    nq = stat.shape[0] // per_q
    nk = stat.shape[0] // per_k
    sq = stat.reshape(nq, per_q, 8, LANES)
    sk = stat.reshape(nk, per_k, 8, LANES)
    qn = jnp.sqrt(jnp.max(sq[:, :, 0, :FOX_HEADS], axis=1))
    kn_own = jnp.sqrt(jnp.max(sq[:, :, 1, :FOX_HEADS], axis=1))
    kn = jnp.sqrt(jnp.max(sk[:, :, 1, :FOX_HEADS], axis=1))
    c_first = sq[:, 0, 2, :FOX_HEADS]
    c_last = sk[:, per_k - 1, 3, :FOX_HEADS]
    kmax = jnp.max(kn, axis=0, keepdims=True)
    bound = 1.01 * qn * (kmax + kn_own) + c_first
    skip = (bound[:, None, :] - c_last[None, :, :]) < -SKIP_BITS
    before = jnp.arange(nk)[None, :] < (jnp.arange(nq) * (per_q // per_k))[:, None]
    skip = jnp.logical_and(skip, before[:, :, None])
    lead = jnp.cumsum(1 - skip.astype(jnp.int32), axis=1) == 0
    return jnp.sum(lead.astype(jnp.int32), axis=1).T.reshape(-1)


def _rope_kernel(x_ref, pos_ref, invf_ref, o_ref):
    ang = pos_ref[...] * invf_ref[...]
    c = jnp.cos(ang)
    s = jnp.sin(ang)
    half = ROT_DIM // 2
    d = lax.broadcasted_iota(jnp.int32, (1, LANES), 1) % DIFF_HD
    sa = jnp.where(d >= half, s, 0.0)
    sb = jnp.where(d < half, -s, 0.0)
    nblk = x_ref.shape[1] // LANES
    for j in range(nblk):
        t = x_ref[:, j * LANES:(j + 1) * LANES]
        r = t * c + pltpu.roll(t, half, 1) * sa + pltpu.roll(t, LANES - half, 1) * sb
        if j < nblk // 2:
            r = r * (DIFF_HD ** -0.5 * LOG2E)
        o_ref[:, j * LANES:(j + 1) * LANES] = r.astype(o_ref.dtype)


def _rope(g32, pos_f, invf, tb):
    s = g32.shape[0]
    w = 2 * DIFF_W
    return pl.pallas_call(
        _rope_kernel,
        grid=(s // tb,),
        in_specs=[pl.BlockSpec((tb, w), lambda i: (i, 0)),
                  pl.BlockSpec((tb, LANES), lambda i: (i, 0)),
                  pl.BlockSpec((1, LANES), lambda i: (0, 0))],
        out_specs=pl.BlockSpec((tb, w), lambda i: (i, 0)),
        out_shape=jax.ShapeDtypeStruct((s, w), BF16),
        compiler_params=_cparams(("parallel",)),
        name="diff_rope",
    )(g32, pos_f, invf)


def _flash_step(v, s, m_ref, l_ref, acc_ref):
    nq = s.shape[1]
    cw = min(nq, FLASH_COLS)
    for c0 in range(0, nq, cw):
        cols = slice(c0, c0 + cw)
        sc = s[:, cols]
        m_prev = m_ref[:, cols]
        m_new = jnp.maximum(m_prev, jnp.max(sc, axis=0, keepdims=True))
        a = jnp.exp2(m_prev - m_new)
        p = jnp.exp2(sc - m_new)
        l_ref[:, cols] = a * l_ref[:, cols] + jnp.sum(p, axis=0, keepdims=True)
        pv = lax.dot_general(v, p.astype(v.dtype), (((0,), (0,)), ((), ())),
                             preferred_element_type=F32)
        acc_ref[:, cols] = a * acc_ref[:, cols] + pv
        m_ref[:, cols] = m_new


def _flash_pipeline(j0, i, qk, sm):
    npairs = (i - j0) // 2
    qk(j0, 0)

    def body(p, carry):
        j = j0 + 2 * p
        qk(j + 1, 1)
        sm(j, 0, False)
        qk(j + 2, 0)
        sm(j + 1, 1, False)
        return carry

    lax.fori_loop(0, npairs, body, 0)
    jn = j0 + 2 * npairs

    @pl.when(jn == i)
    def _():
        sm(i, 0, True)

    @pl.when(jn != i)
    def _():
        qk(i, 1)
        sm(jn, 0, False)
        sm(i, 1, True)


def _init_state(m_ref, l_ref, acc_ref):
    m_ref[...] = jnp.full(m_ref.shape, NEG_BIG, F32)
    l_ref[...] = jnp.zeros(l_ref.shape, F32)
    acc_ref[...] = jnp.zeros(acc_ref.shape, F32)


HEADS_PER_STEP = 2
FLASH_COLS = 256


def _head_lanes(g):
    return slice(g * LANES, (g + 1) * LANES)


def _fox_kernel(jmin_ref, q_ref, qaux_ref, k_ref, kaux_ref, v_ref, o_ref,
                m_ref, l_ref, acc_ref, sa_ref, sb_ref, *, tq, tk):
    hp = pl.program_id(0)
    i = pl.program_id(1)
    heads = range(HEADS_PER_STEP)
    qf = [jnp.concatenate([q_ref[:, _head_lanes(g)], qaux_ref[g]], axis=1)
          for g in heads]
    _init_state(m_ref, l_ref, acc_ref)
    s_refs = (sa_ref, sb_ref)

    def qk(j, slot):
        k0 = pl.multiple_of(j * tk, tk)
        for g in heads:
            kf = jnp.concatenate([k_ref[pl.ds(k0, tk), _head_lanes(g)],
                                  kaux_ref[g, pl.ds(k0, tk), :]], axis=1)
            s_refs[slot][g] = lax.dot_general(kf, qf[g], (((1,), (1,)), ((), ())),
                                              preferred_element_type=F32)

    def sm(j, slot, diag):
        k0 = pl.multiple_of(j * tk, tk)
        for g in heads:
            s = s_refs[slot][g]
            if diag is not None:
                kk = lax.broadcasted_iota(jnp.int32, (tk, tq), 0) + diag * tk
                qq = lax.broadcasted_iota(jnp.int32, (tk, tq), 1)
                s = jnp.where(kk <= qq, s, NEG_BIG)
            v = v_ref[pl.ds(k0, tk), _head_lanes(g)]
            _flash_step(v, s, m_ref.at[g], l_ref.at[g], acc_ref.at[g])

    nq = pl.num_programs(1)
    j0 = jmin_ref[hp * HEADS_PER_STEP * nq + i]
    for g in range(1, HEADS_PER_STEP):
        j0 = jnp.minimum(j0, jmin_ref[(hp * HEADS_PER_STEP + g) * nq + i])

    if tq == tk:
        _flash_pipeline(j0, i, qk, lambda j, slot, masked: sm(j, slot, 0 if masked else None))
    else:
        d0 = 2 * i
        npairs = (d0 - j0) // 2
        qk(j0, 0)

        def body(p, carry):
            j = j0 + 2 * p
            qk(j + 1, 1)
            sm(j, 0, None)
            qk(j + 2, 0)
            sm(j + 1, 1, None)
            return carry

        lax.fori_loop(0, npairs, body, 0)
        jn = j0 + 2 * npairs

        @pl.when(jn == d0)
        def _():
            qk(d0 + 1, 1)
            sm(d0, 0, 0)
            sm(d0 + 1, 1, 1)

        @pl.when(jn != d0)
        def _():
            qk(d0, 1)
            sm(jn, 0, None)
            qk(d0 + 1, 0)
            sm(d0, 1, 0)
            sm(d0 + 1, 0, 1)

    for g in heads:
        o = acc_ref[g] / l_ref[g]
        o_ref[:, _head_lanes(g)] = o.T.astype(o_ref.dtype)


def _fox_attention(jmin, g16, qaux, kaux, tq, tk):
    s = g16.shape[0]
    assert tq in (tk, 2 * tk)
    hps = HEADS_PER_STEP
    wide = hps * LANES
    grid_spec = pltpu.PrefetchScalarGridSpec(
        num_scalar_prefetch=1,
        grid=(FOX_HEADS // hps, s // tq),
        in_specs=[pl.BlockSpec((tq, wide), lambda h, i, jm: (i, G16_QA // hps + h)),
                  pl.BlockSpec((hps, tq, LANES), lambda h, i, jm: (h, i, 0)),
                  pl.BlockSpec((s, wide), lambda h, i, jm: (0, G16_KA // hps + h)),
                  pl.BlockSpec((hps, s, LANES), lambda h, i, jm: (h, 0, 0)),
                  pl.BlockSpec((s, wide), lambda h, i, jm: (0, G16_VA // hps + h))],
        out_specs=pl.BlockSpec((tq, wide), lambda h, i, jm: (i, h)),
        scratch_shapes=[pltpu.VMEM((hps, 1, tq), F32), pltpu.VMEM((hps, 1, tq), F32),
                        pltpu.VMEM((hps, FOX_HD, tq), F32),
                        pltpu.VMEM((hps, tk, tq), F32), pltpu.VMEM((hps, tk, tq), F32)],
    )
    return pl.pallas_call(
        functools.partial(_fox_kernel, tq=tq, tk=tk),
        grid_spec=grid_spec,
        out_shape=jax.ShapeDtypeStruct((s, FOX_W), BF16),
        compiler_params=_cparams(("parallel", "parallel")),
        name="fox_attention",
    )(jmin, g16, qaux, g16, kaux, g16)


def _diff_kernel(q_ref, k_ref, v_ref, lamv_ref, g_ref, o_ref, m_ref, l_ref, acc_ref,
                 sa_ref, sb_ref, *, tq, tk, lam_init):
    i = pl.program_id(1)
    heads = range(HEADS_PER_STEP)
    lane = lax.broadcasted_iota(jnp.int32, (tq, LANES), 1)
    qq2 = []
    for g in heads:
        q = q_ref[:, _head_lanes(g)].astype(F32)
        qq2.append(jnp.concatenate([jnp.where(lane < DIFF_HD, q, 0.0),
                                    jnp.where(lane >= DIFF_HD, q, 0.0)], axis=0).astype(BF16))
    _init_state(m_ref, l_ref, acc_ref)
    s_refs = (sa_ref, sb_ref)

    def qk(j, slot):
        k0 = pl.multiple_of(j * tk, tk)
        for g in heads:
            k = k_ref[pl.ds(k0, tk), _head_lanes(g)]
            s_refs[slot][g] = lax.dot_general(k, qq2[g], (((1,), (1,)), ((), ())),
                                              preferred_element_type=F32)

    def sm(j, slot, diag):
        k0 = pl.multiple_of(j * tk, tk)
        for g in heads:
            s = s_refs[slot][g]
            if diag is not None:
                kc = (lax.broadcasted_iota(jnp.int32, (tk, 2 * tq), 0) + diag * tk) // CHUNK
                qc = (lax.broadcasted_iota(jnp.int32, (tk, 2 * tq), 1) % tq) // CHUNK
                s = jnp.where(kc <= qc, s, NEG_BIG)
            v = v_ref[pl.ds(k0, tk), _head_lanes(g)]
            _flash_step(v, s, m_ref.at[g], l_ref.at[g], acc_ref.at[g])

    if tq == tk:
        _flash_pipeline(0, i, qk, lambda j, slot, masked: sm(j, slot, 0 if masked else None))
    else:
        qk(0, 0)

        def body(p, carry):
            j = 2 * p
            qk(j + 1, 1)
            sm(j, 0, None)
            qk(j + 2, 0)
            sm(j + 1, 1, None)
            return carry

        lax.fori_loop(0, i, body, 0)
        qk(2 * i + 1, 1)
        sm(2 * i, 0, 0)
        sm(2 * i + 1, 1, 1)

    lv = lamv_ref[...]
    lam = (jnp.exp(jnp.sum(lv[0:1] * lv[1:2], axis=1, keepdims=True))
           - jnp.exp(jnp.sum(lv[2:3] * lv[3:4], axis=1, keepdims=True)) + lam_init)
    for g in heads:
        on = acc_ref[g] / l_ref[g]
        o = (on[:, :tq] - lam * on[:, tq:]).T
        ms = jnp.mean(o * o, axis=-1, keepdims=True)
        o = o * lax.rsqrt(ms + LN_EPS) * g_ref[...] * (1.0 - lam_init)
        o_ref[:, _head_lanes(g)] = o.astype(o_ref.dtype)


def _diff_attention(qk16, g16, lamv, subg, lam_init, tq, tk):
    s = g16.shape[0]
    assert tq in (tk, 2 * tk)
    hps = HEADS_PER_STEP
    wide = hps * LANES
    return pl.pallas_call(
        functools.partial(_diff_kernel, tq=tq, tk=tk, lam_init=lam_init),
        grid=(DIFF_HEADS // hps, s // tq),
        in_specs=[pl.BlockSpec((tq, wide), lambda h, i: (i, h)),
                  pl.BlockSpec((s, wide), lambda h, i: (0, DIFF_HEADS // hps + h)),
                  pl.BlockSpec((s, wide), lambda h, i: (0, G16_VC // hps + h)),
                  pl.BlockSpec((4, LANES), lambda h, i: (0, 0)),
                  pl.BlockSpec((1, LANES), lambda h, i: (0, 0))],
        out_specs=pl.BlockSpec((tq, wide), lambda h, i: (i, h)),
        out_shape=jax.ShapeDtypeStruct((s, DIFF_W), BF16),
        scratch_shapes=[pltpu.VMEM((hps, 1, 2 * tq), F32), pltpu.VMEM((hps, 1, 2 * tq), F32),
                        pltpu.VMEM((hps, DIFF_VD, 2 * tq), F32),
                        pltpu.VMEM((hps, tk, 2 * tq), F32), pltpu.VMEM((hps, tk, 2 * tq), F32)],
        compiler_params=_cparams(("parallel", "parallel")),
        name="diff_attention",
    )(qk16, qk16, g16, lamv, subg)


def _shift_rows(x, d, fill):
    if d % 8 == 0:
        return jnp.concatenate([jnp.full((d, x.shape[1]), fill, x.dtype), x[:x.shape[0] - d]],
                               axis=0)
    rows = lax.broadcasted_iota(jnp.int32, x.shape, 0)
    return jnp.where(rows >= d, pltpu.roll(x, d, 0), fill)


def _lru_kernel(xr_ref, gr_ref, cw_ref, cb_ref, wa_ref, ba_ref, wi_ref, bi_ref, lam_ref,
                o_ref, halo_ref, h_ref, *, tb, sub):
    @pl.when(pl.program_id(1) == 0)
    def _():
        halo_ref[...] = jnp.zeros_like(halo_ref)
        h_ref[...] = jnp.zeros_like(h_ref)

    x = xr_ref[...]
    xe = jnp.concatenate([halo_ref[...], x], axis=0)
    halo_ref[...] = x[tb - 8:tb, :]
    cw = cw_ref[...]
    xc = cb_ref[...] + cw[CONV_W - 1:CONV_W, :] * x
    for j in range(CONV_W - 1):
        sh = CONV_W - 1 - j
        xc = xc + cw[j:j + 1, :] * pltpu.roll(xe, sh, 0)[8:8 + tb, :]

    xcb = xc.astype(BF16)
    r = jax.nn.sigmoid(jnp.dot(xcb, wa_ref[0], preferred_element_type=F32) + ba_ref[...])
    ig = jax.nn.sigmoid(jnp.dot(xcb, wi_ref[0], preferred_element_type=F32) + bi_ref[...])
    lam = lam_ref[...]
    ls = jnp.minimum(lam, 0.0) - jnp.log1p(jnp.exp(-jnp.abs(lam)))
    log_a = LRU_C * r * ls
    a = jnp.exp(log_a)
    z2 = 2.0 * log_a
    e2 = jnp.exp(z2)
    small = jnp.where(e2 == 1.0, -z2, (1.0 - e2) * z2 / jnp.log(e2))
    neg_expm1 = jnp.where(z2 < -1.0, 1.0 - e2, small)
    u = jnp.sqrt(neg_expm1) * (ig * xc)

    h = h_ref[...]
    for c in range(tb // sub):
        ac = a[c * sub:(c + 1) * sub, :]
        uc = u[c * sub:(c + 1) * sub, :]
        d = 1
        while d < sub:
            uc = ac * _shift_rows(uc, d, 0.0) + uc
            ac = ac * _shift_rows(ac, d, 1.0)
            d *= 2
        hc = uc + ac * h
        h = hc[sub - 1:sub, :]
        g = gr_ref[c * sub:(c + 1) * sub, :]
        gelu = 0.5 * g * (1.0 + jnp.tanh(math.sqrt(2.0 / math.pi) * (g + 0.044715 * (g * g * g))))
        o_ref[c * sub:(c + 1) * sub, :] = (gelu * hc).astype(o_ref.dtype)
    h_ref[...] = h


def _lru_branch(g32, cw, cb, wa, ba, wi, bi, lam, tb, sub):
    s = g32.shape[0]
    vec = lambda c, i: (0, c)
    return pl.pallas_call(
        functools.partial(_lru_kernel, tb=tb, sub=sub),
        grid=(LRU_BLOCKS, s // tb),
        in_specs=[pl.BlockSpec((tb, LANES), lambda c, i: (i, G32_XR + c)),
                  pl.BlockSpec((tb, LANES), lambda c, i: (i, G32_GR + c)),
                  pl.BlockSpec((CONV_W, LANES), vec),
                  pl.BlockSpec((1, LANES), vec),
                  pl.BlockSpec((1, LRU_BW, LRU_BW), lambda c, i: (c, 0, 0)),
                  pl.BlockSpec((1, LANES), vec),
                  pl.BlockSpec((1, LRU_BW, LRU_BW), lambda c, i: (c, 0, 0)),
                  pl.BlockSpec((1, LANES), vec),
                  pl.BlockSpec((1, LANES), vec)],
        out_specs=pl.BlockSpec((tb, LANES), lambda c, i: (i, c)),
        out_shape=jax.ShapeDtypeStruct((s, LRU_W), BF16),
        scratch_shapes=[pltpu.VMEM((8, LANES), F32), pltpu.VMEM((1, LANES), F32)],
        compiler_params=_cparams(("parallel", "arbitrary")),
        name="rg_lru",
    )(g32, g32, cw, cb, wa, ba, wi, bi, lam)


def _layer_norm_rows(y, g, b):
    mu = jnp.mean(y, axis=-1, keepdims=True)
    yc = y - mu
    var = jnp.mean(yc * yc, axis=-1, keepdims=True)
    return yc * lax.rsqrt(var + LN_EPS) * g + b


def _oproj_ln_kernel(xa_ref, xb_ref, xc_ref, wa_ref, wb_ref, wc_ref, h_ref, g_ref, b_ref,
                     o32_ref, oslab_ref):
    half = h_ref.shape[0] // 2
    for part in range(2):
        rows = slice(part * half, (part + 1) * half)
        mix = (jnp.dot(xa_ref[rows, :], wa_ref[...], preferred_element_type=F32)
               + jnp.dot(xb_ref[rows, :], wb_ref[...], preferred_element_type=F32)
               + jnp.dot(xc_ref[rows, :], wc_ref[...], preferred_element_type=F32))
        o = _layer_norm_rows(ALPHA * h_ref[rows, :] + mix, g_ref[...], b_ref[...])
        o32_ref[rows, :] = o
        _store_slabs(oslab_ref.at[pl.ds(part * half * SLAB_ROWS, half * SLAB_ROWS), :], o)


def _oproj_ln(xa, xb, xc, wa, wb, wc, h, g, b, tm):
    s = h.shape[0]
    row = lambda i: (i, 0)
    fixed = lambda i: (0, 0)
    return pl.pallas_call(
        _oproj_ln_kernel,
        grid=(s // tm,),
        in_specs=[pl.BlockSpec((tm, FOX_W), row), pl.BlockSpec((tm, LRU_W), row),
                  pl.BlockSpec((tm, DIFF_W), row),
                  pl.BlockSpec((FOX_W, D_MODEL), fixed), pl.BlockSpec((LRU_W, D_MODEL), fixed),
                  pl.BlockSpec((DIFF_W, D_MODEL), fixed),
                  pl.BlockSpec((tm, D_MODEL), row),
                  pl.BlockSpec((1, D_MODEL), fixed), pl.BlockSpec((1, D_MODEL), fixed)],
        out_specs=[pl.BlockSpec((tm, D_MODEL), row), pl.BlockSpec((tm * SLAB_ROWS, LANES), row)],
        out_shape=[jax.ShapeDtypeStruct((s, D_MODEL), F32),
                   jax.ShapeDtypeStruct((s * SLAB_ROWS, LANES), F32)],
        compiler_params=_cparams(("parallel",)),
        name="oproj_ln1",
    )(xa, xb, xc, wa, wb, wc, h, g, b)


def _router_kernel(h_ref, whi_ref, wlo_ref, b_ref, o_ref):
    h = h_ref[...]
    hi = h.astype(BF16)
    lo = (h - hi.astype(F32)).astype(BF16)
    whi = whi_ref[...]
    both = jnp.dot(hi, jnp.concatenate([whi, wlo_ref[...]], axis=1), preferred_element_type=F32)
    logits = (both[:, :LANES] + both[:, LANES:]
              + jnp.dot(lo, whi, preferred_element_type=F32)) + b_ref[...]
    lane = lax.broadcasted_iota(jnp.int32, logits.shape, 1)
    big = jnp.int32(1 << 20)

    def first_lane(cond):
        return jnp.min(jnp.where(cond, lane, big), axis=1, keepdims=True)

    gm = lane < N_GROUPS
    gl = jnp.where(gm, logits, -jnp.inf)
    gmax = jnp.max(gl, axis=1, keepdims=True)
    gexp = jnp.where(gm, jnp.exp(logits - gmax), 0.0)
    gprob = gexp / jnp.sum(gexp, axis=1, keepdims=True)
    gidx = first_lane(gl == gmax)
    g_weight = jnp.sum(jnp.where(lane == gidx, gprob, 0.0), axis=1, keepdims=True)

    e0 = N_GROUPS + EXPERTS_PER_GROUP * gidx
    em = jnp.logical_and(lane >= e0, lane < e0 + EXPERTS_PER_GROUP)
    el = jnp.where(em, logits, -jnp.inf)
    emax = jnp.max(el, axis=1, keepdims=True)
    eexp = jnp.where(em, jnp.exp(logits - emax), 0.0)
    eprob = jnp.where(em, eexp / jnp.sum(eexp, axis=1, keepdims=True), -1.0)
    p1 = jnp.max(eprob, axis=1, keepdims=True)
    i1 = first_lane(eprob == p1)
    eprob2 = jnp.where(lane == i1, -1.0, eprob)
    p2 = jnp.max(eprob2, axis=1, keepdims=True)
    i2 = first_lane(eprob2 == p2)
    den = p1 + p2
    gate1 = g_weight * (p1 / den)
    gate2 = g_weight * (p2 / den)
    id1 = (i1 - N_GROUPS).astype(F32)
    id2 = (i2 - N_GROUPS).astype(F32)
    o_ref[...] = jnp.where(lane == 0, id1,
                           jnp.where(lane == 1, id2,
                                     jnp.where(lane == 2, gate1,
                                               jnp.where(lane == 3, gate2, 0.0))))


def _router(h, whi, wlo, bias, tm):
    s = h.shape[0]
    fixed = lambda i: (0, 0)
    return pl.pallas_call(
        _router_kernel,
        grid=(s // tm,),
        in_specs=[pl.BlockSpec((tm, D_MODEL), lambda i: (i, 0)),
                  pl.BlockSpec((D_MODEL, LANES), fixed), pl.BlockSpec((D_MODEL, LANES), fixed),
                  pl.BlockSpec((1, LANES), fixed)],
        out_specs=pl.BlockSpec((tm, LANES), lambda i: (i, 0)),
        out_shape=jax.ShapeDtypeStruct((s, LANES), F32),
        compiler_params=_cparams(("parallel",)),
        name="moe_router",
    )(h, whi, wlo, bias)


SLAB_ROWS = D_MODEL // LANES
SLAB_PITCH = SLAB_ROWS + 4


def _store_slabs(o_ref, x):
    rows = x.shape[0]
    for c in range(SLAB_ROWS):
        o_ref[pl.ds(c, rows, stride=SLAB_ROWS), :] = x[:, c * LANES:(c + 1) * LANES]


def _load_slabs(buf_ref, rows):
    return jnp.concatenate([buf_ref[pl.ds(c, rows, stride=SLAB_PITCH), :]
                            for c in range(SLAB_ROWS)], axis=1)


GATHER_GROUP = 8


def _issue_slab_gather(slab_row0_of, ngroups, src_hbm, dst_ref, sem, both_queues=False):
    def body(gi, carry):
        for u in range(GATHER_GROUP):
            r = gi * GATHER_GROUP + u
            src0 = pl.multiple_of(slab_row0_of(r), SLAB_ROWS)
            dst0 = pl.multiple_of(r * SLAB_PITCH, 4)
            pltpu.make_async_copy(src_hbm.at[pl.ds(src0, SLAB_ROWS), :],
                                  dst_ref.at[pl.ds(dst0, SLAB_ROWS), :],
                                  sem).start(priority=u % 2 if both_queues else 0)
        return carry
    lax.fori_loop(0, ngroups, body, 0)


def _wait_slab_gather(ngroups, buf_ref, sem):
    part = buf_ref.at[pl.ds(0, ngroups * (GATHER_GROUP * SLAB_ROWS)), :]
    pltpu.make_async_copy(part, part, sem).wait()


def _moe_kernel(te_ref, nt_ref, start_ref, cnt_ref, tok_ref, wfirst_ref, wslot_ref, wnext_ref,
                h_hbm, wg_hbm, wu_hbm, wd_hbm, y_ref, xbuf, wgb, wub, wdb, sem, wsem, *, tm):
    t = pl.program_id(0)
    nt = nt_ref[0]
    slot = t % 2
    last = tok_ref.shape[0] - 1

    def groups(tile):
        return (cnt_ref[tile] + (GATHER_GROUP - 1)) // GATHER_GROUP

    def issue(tile, sl):
        s0 = start_ref[tile]
        _issue_slab_gather(lambda r: tok_ref[jnp.minimum(s0 + r, last)], groups(tile), h_hbm,
                           xbuf.at[sl], sem.at[sl])

    def weight_copies(e, ws):
        return (pltpu.make_async_copy(wg_hbm.at[e], wgb.at[ws], wsem.at[ws]),
                pltpu.make_async_copy(wu_hbm.at[e], wub.at[ws], wsem.at[ws]),
                pltpu.make_async_copy(wd_hbm.at[e], wdb.at[ws], wsem.at[ws]))

    @pl.when(t == 0)
    def _():
        xbuf[...] = jnp.zeros_like(xbuf)
        issue(0, 0)
        for c in weight_copies(te_ref[0], wslot_ref[0]):
            c.start(priority=1)

    @pl.when(t + 1 < nt)
    def _():
        issue(t + 1, 1 - slot)

    @pl.when(t < nt)
    def _():
        ws = wslot_ref[t]

        @pl.when(wfirst_ref[t] == 1)
        def _():
            for c in weight_copies(te_ref[t], ws):
                c.wait()

            @pl.when(wnext_ref[t] >= 0)
            def _():
                for c in weight_copies(wnext_ref[t], 1 - ws):
                    c.start(priority=1)

        _wait_slab_gather(groups(t), xbuf.at[slot], sem.at[slot])
        x = _load_slabs(xbuf.at[slot], tm).astype(BF16)
        hg = jnp.dot(x, wgb[ws].astype(BF16), preferred_element_type=F32)
        hu = jnp.dot(x, wub[ws].astype(BF16), preferred_element_type=F32)
        act = (hg * jax.nn.sigmoid(hg)) * hu
        y = jnp.dot(act.astype(BF16), wdb[ws].astype(BF16), preferred_element_type=F32)
        _store_slabs(y_ref, y)

    @pl.when(t >= nt)
    def _():
        y_ref[...] = jnp.zeros_like(y_ref)


def _moe_grouped(plan, tok_sorted, h_slab, wg, wu, wd, tm):
    tile_expert, ntiles, tile_start, tile_count, wfirst, wslot, wnext = plan
    nt_max = tile_expert.shape[0]
    grid_spec = pltpu.PrefetchScalarGridSpec(
        num_scalar_prefetch=8,
        grid=(nt_max,),
        in_specs=[pl.BlockSpec(memory_space=pl.ANY)] * 4,
        out_specs=pl.BlockSpec((tm * SLAB_ROWS, LANES), lambda t, *_: (t, 0)),
        scratch_shapes=[pltpu.VMEM((2, tm * SLAB_PITCH, LANES), F32),
                        pltpu.VMEM((2, D_MODEL, D_FF_EXPERT), F32),
                        pltpu.VMEM((2, D_MODEL, D_FF_EXPERT), F32),
                        pltpu.VMEM((2, D_FF_EXPERT, D_MODEL), F32),
                        pltpu.SemaphoreType.DMA((2,)), pltpu.SemaphoreType.DMA((2,))],
    )
    return pl.pallas_call(
        functools.partial(_moe_kernel, tm=tm),
        grid_spec=grid_spec,
        out_shape=jax.ShapeDtypeStruct((nt_max * tm * SLAB_ROWS, LANES), F32),
        compiler_params=_cparams(("arbitrary",)),
        name="moe_experts",
    )(tile_expert, ntiles, tile_start, tile_count, tok_sorted, wfirst, wslot, wnext,
      h_slab, wg, wu, wd)


def _combine_ln_kernel(pos_ref, y_hbm, h_ref, meta_ref, g_ref, b_ref, o32_ref, o16_ref,
                       ybuf, sem, *, tm):
    t = pl.program_id(0)
    nt = pl.num_programs(0)
    slot = t % 2
    ngroups = tm // GATHER_GROUP

    def issue(tile, sl):
        for j in range(2):
            base = j * (nt * tm) + tile * tm
            _issue_slab_gather(lambda r, base=base: pos_ref[base + r], ngroups, y_hbm,
                               ybuf.at[sl, j], sem.at[sl], both_queues=True)

    @pl.when(t == 0)
    def _():
        issue(0, 0)

    @pl.when(t + 1 < nt)
    def _():
        issue(t + 1, 1 - slot)

    _wait_slab_gather(ngroups, ybuf.at[slot, 0], sem.at[slot])
    _wait_slab_gather(ngroups, ybuf.at[slot, 1], sem.at[slot])
    meta = meta_ref[...]
    ffn = (meta[:, 2:3] * _load_slabs(ybuf.at[slot, 0], tm)
           + meta[:, 3:4] * _load_slabs(ybuf.at[slot, 1], tm))
    o = _layer_norm_rows(ALPHA * h_ref[...] + ffn, g_ref[...], b_ref[...])
    o32_ref[...] = o
    o16_ref[...] = o.astype(BF16)


def _combine_ln(pos, y_slab, h, meta, g, b, tm):
    s = h.shape[0]
    row = lambda t, pos: (t, 0)
    fixed = lambda t, pos: (0, 0)
    grid_spec = pltpu.PrefetchScalarGridSpec(
        num_scalar_prefetch=1,
        grid=(s // tm,),
        in_specs=[pl.BlockSpec(memory_space=pl.ANY),
                  pl.BlockSpec((tm, D_MODEL), row),
                  pl.BlockSpec((tm, LANES), row),
                  pl.BlockSpec((1, D_MODEL), fixed), pl.BlockSpec((1, D_MODEL), fixed)],
        out_specs=[pl.BlockSpec((tm, D_MODEL), row), pl.BlockSpec((tm, D_MODEL), row)],
        scratch_shapes=[pltpu.VMEM((2, 2, tm * SLAB_PITCH, LANES), F32),
                        pltpu.SemaphoreType.DMA((2,))],
    )
    return pl.pallas_call(
        functools.partial(_combine_ln_kernel, tm=tm),
        grid_spec=grid_spec,
        out_shape=[jax.ShapeDtypeStruct((s, D_MODEL), F32),
                   jax.ShapeDtypeStruct((s, D_MODEL), BF16)],
        compiler_params=_cparams(("arbitrary",)),
        name="combine_ln2",
    )(pos, y_slab, h, meta, g, b)


def _dispatch_plan(meta, tm, layer):
    t = meta.shape[0]
    eid = meta[:, 0:2].astype(jnp.int32).reshape(-1)
    onehot = (eid[:, None] == jnp.arange(N_EXPERTS, dtype=jnp.int32)[None, :]).astype(jnp.int32)
    csum = jnp.cumsum(onehot, axis=0)
    counts = csum[-1]
    padded = ((counts + tm - 1) // tm) * tm
    pend = jnp.cumsum(padded)
    poff = pend - padded
    off = jnp.cumsum(counts) - counts
    pos = jnp.sum(onehot * (csum + poff[None, :]), axis=1) - 1
    p_max = 2 * t + N_EXPERTS * tm
    nt_max = p_max // tm
    ntiles = (pend[-1] // tm).astype(jnp.int32)
    tile_ids = jnp.minimum(jnp.arange(nt_max, dtype=jnp.int32), ntiles - 1)
    tile_expert = jnp.sum((pend[None, :] // tm <= tile_ids[:, None]).astype(jnp.int32), axis=1)
    tile_expert = jnp.minimum(tile_expert, N_EXPERTS - 1).astype(jnp.int32)
    tok_sorted = ((jnp.argsort(eid, stable=True) // 2) * SLAB_ROWS).astype(jnp.int32)
    k0 = jnp.arange(nt_max, dtype=jnp.int32) * tm - poff[tile_expert]
    tile_start = jnp.clip(off[tile_expert] + k0, 0, 2 * t - 1).astype(jnp.int32)
    tile_count = jnp.clip(counts[tile_expert] - k0, 0, tm).astype(jnp.int32)
    ids = jnp.arange(N_EXPERTS, dtype=jnp.int32)
    has = counts > 0
    rank = jnp.cumsum(has.astype(jnp.int32)) - 1
    later = jnp.logical_and(has[None, :], ids[None, :] > ids[:, None])
    next_e = jnp.min(jnp.where(later, ids[None, :], N_EXPERTS), axis=1)
    next_e = jnp.where(next_e < N_EXPERTS, next_e + layer * N_EXPERTS, -1)
    wfirst = (k0 == 0).astype(jnp.int32)
    wslot = (rank[tile_expert] % 2).astype(jnp.int32)
    wnext = next_e[tile_expert].astype(jnp.int32)
    plan = (tile_expert + layer * N_EXPERTS, ntiles.reshape(1), tile_start, tile_count,
            wfirst, wslot, wnext)
    pos_rows = (pos * SLAB_ROWS).astype(jnp.int32).reshape(t, 2).T.reshape(-1)
    return plan, tok_sorted, pos_rows


def _pad_lanes(v, width=LANES):
    v = v.reshape(1, -1).astype(F32)
    return jnp.pad(v, ((0, 0), (0, width - v.shape[1])))


TM_PROJ, TN_PROJ = 4096, 256
TB_PREP = 512
TQ_FOX, TK_FOX = 1024, 512
TQ_DIFF, TK_DIFF = 1024, 512
TB_LRU, SUB_LRU = 1024, 256
TB_ROPE = 1024
TM_OPROJ = 512
TM_ROUTER = 512
TM_MOE = 256
TM_COMBINE = 512


def _layer(l, h32, h16, pos_f, invf, p, stacked):
    s = h32.shape[0]
    wt = _prep_w_in(stacked["w_in"], l)
    s16 = jnp.concatenate([jnp.full((1, FOX_W), FOX_HD ** -0.5 * LOG2E, F32),
                           jnp.ones((1, G16_W - FOX_W), F32)], axis=1)
    s32 = jnp.ones((1, G32_W), F32)
    tm_proj = min(s, TM_PROJ)
    g16 = _matmul_nt(h16, wt, 0, G16_W, s16, BF16, tm_proj, TN_PROJ, "in_proj_bf16")
    g32 = _matmul_nt(h16, wt, G16_W, G32_W, s32, F32, tm_proj, TN_PROJ, "in_proj_f32")

    tb_prep = min(s, TB_PREP)
    qaux, kaux, stat = _fox_prep(g32, g16, _pad_lanes(p["b_f"]), tb_prep)
    tk_fox = min(s, TK_FOX)
    tq_fox = TQ_FOX if s % TQ_FOX == 0 else tk_fox
    jmin = _fox_skip_table(stat, tq_fox // tb_prep, tk_fox // tb_prep)
    out_a = _fox_attention(jmin, g16, qaux, kaux, tq_fox, tk_fox)

    out_b = _lru_branch(g32, p["conv_w"], p["conv_b"].reshape(1, -1),
                        p["w_a"].astype(BF16), p["b_a"].reshape(1, -1),
                        p["w_i"].astype(BF16), p["b_i"].reshape(1, -1),
                        p["lru_lambda"].reshape(1, -1), min(s, TB_LRU), min(s, SUB_LRU))

    qk16 = _rope(g32, pos_f, invf, min(s, TB_ROPE))
    lamv = jnp.concatenate([_pad_lanes(p["lam_q1"]), _pad_lanes(p["lam_k1"]),
                            _pad_lanes(p["lam_q2"]), _pad_lanes(p["lam_k2"])], axis=0)
    lam_init = 0.8 - 0.6 * math.exp(-0.3 * l)
    tk_diff = min(s, TK_DIFF)
    tq_diff = TQ_DIFF if s % TQ_DIFF == 0 else tk_diff
    out_c = _diff_attention(qk16, g16, lamv, p["subln_g"].reshape(1, -1), lam_init,
                            tq_diff, tk_diff)

    w_o = p["w_o"].astype(BF16)
    h1_32, h1_slab = _oproj_ln(out_a, out_b, out_c, w_o[:FOX_W], w_o[FOX_W:FOX_W + LRU_W],
                               w_o[FOX_W + LRU_W:], h32, p["ln1_g"].reshape(1, -1),
                               p["ln1_b"].reshape(1, -1), min(s, TM_OPROJ))

    w_rt = jnp.concatenate([p["w_group"], p["w_router"],
                            jnp.zeros((D_MODEL, LANES - N_GROUPS - N_EXPERTS), F32)], axis=1)
    w_rt_hi = w_rt.astype(BF16)
    w_rt_lo = (w_rt - w_rt_hi.astype(F32)).astype(BF16)
    b_rt = _pad_lanes(jnp.concatenate([p["b_group"], p["b_router"]]))
    meta = _router(h1_32, w_rt_hi, w_rt_lo, b_rt, min(s, TM_ROUTER))

    tm_moe = min(s, TM_MOE)
    plan, tok_sorted, pos = _dispatch_plan(meta, tm_moe, l)
    y_slab = _moe_grouped(plan, tok_sorted, h1_slab, stacked["w_gate"], stacked["w_up"],
                          stacked["w_down"], tm_moe)
    return _combine_ln(pos, y_slab, h1_32, meta, p["ln2_g"].reshape(1, -1),
                       p["ln2_b"].reshape(1, -1), min(s, TM_COMBINE))


def _forward(x, positions, params):
    bsz, s, d = x.shape
    assert bsz == 1 and d == D_MODEL
    h32 = x.reshape(s, d)
    h16 = h32.astype(BF16)
    half = ROT_DIM // 2
    inv_freq = ROPE_THETA ** (-jnp.arange(half, dtype=F32) * 2.0 / ROT_DIM)
    dlane = jnp.arange(LANES) % DIFF_HD
    invf = jnp.where(dlane < ROT_DIM, inv_freq[dlane % half], 0.0).reshape(1, LANES).astype(F32)
    pos_f = jnp.broadcast_to(positions.astype(F32).reshape(s, 1), (s, LANES))
    big = ("w_in", "w_gate", "w_up", "w_down")
    w_view = jnp.transpose(params["w_in"].reshape(DEPTH, K_TILES, LANES, N_IN), (3, 1, 0, 2))
    stacked = {
        "w_in": w_view.reshape(N_IN * ROWS_PER_COL, LANES),
        "w_gate": params["w_gate"].reshape(DEPTH * N_EXPERTS, D_MODEL, D_FF_EXPERT),
        "w_up": params["w_up"].reshape(DEPTH * N_EXPERTS, D_MODEL, D_FF_EXPERT),
        "w_down": params["w_down"].reshape(DEPTH * N_EXPERTS, D_FF_EXPERT, D_MODEL),
    }
    for l in range(DEPTH):
        p = {k: v[l] for k, v in params.items() if k not in big}
        h32, h16 = _layer(l, h32, h16, pos_f, invf, p, stacked)
    return h32.reshape(bsz, s, d)


def kernel(x, positions, w_in, b_f, conv_w, conv_b, w_a, b_a, w_i, b_i, lru_lambda, lam_q1, lam_k1, lam_q2, lam_k2, subln_g, w_o, ln1_g, ln1_b, w_group, b_group, w_router, b_router, w_gate, w_up, w_down, ln2_g, ln2_b):
    params = dict(w_in=w_in, b_f=b_f, conv_w=conv_w, conv_b=conv_b, w_a=w_a, b_a=b_a, w_i=w_i,
                  b_i=b_i, lru_lambda=lru_lambda, lam_q1=lam_q1, lam_k1=lam_k1, lam_q2=lam_q2,
                  lam_k2=lam_k2, subln_g=subln_g, w_o=w_o, ln1_g=ln1_g, ln1_b=ln1_b,
                  w_group=w_group, b_group=b_group, w_router=w_router, b_router=b_router,
                  w_gate=w_gate, w_up=w_up, w_down=w_down, ln2_g=ln2_g, ln2_b=ln2_b)
    return _forward(x, positions, params)
```
